```python
import math
import jax, jax.numpy as jnp
from jax import lax
import numpy as np

D_MODEL = 1024
BATCH = 32
SEQ = 256
DEPTH = 2
DEC_BATCH = 2
DEC_SEQ = 4096
PAST_LEN = 512

GRID_W = 64
HEAD_DIM = 64
H_NA = 6
H_DN = 4
H_DF = 6
W_NA = H_NA * HEAD_DIM
W_DN = H_DN * HEAD_DIM
W_DF = H_DF * HEAD_DIM
DF_QK = HEAD_DIM // 2
WIN_R = 8
WIN_C = 16
QCB = 16
KCB = 32
CONV_K = 5
CHUNK = 64
Q_BLOCK = 128
D_FF = 2816
N_EXPERTS = 8
TOP_K = 2
MOE_FF = 1408
N_DENSE = (DEPTH + 1) // 2
N_MOE = DEPTH // 2
ALPHA = (2 * DEPTH) ** 0.25
BETA_INIT = (8 * DEPTH) ** -0.25
LN_EPS = 1e-5
RMS_EPS = 1e-6
ROPE_BASE = 10000.0
NEG_INF = -1e30
PROJ_SIZES = (W_NA, W_NA, W_NA, 3 * W_DN, W_DN, 2 * H_DN, 2 * H_DN, W_DF, W_DF, W_DF)
P_IN = 3 * W_NA + 4 * W_DN + 4 * H_DN + 3 * W_DF

kernel_name = 'hybrid_natten_deltanet_diffattn_step'


def layer_norm(x, g, b):
    xf = x.astype(jnp.float32)
    mu = jnp.mean(xf, axis=-1, keepdims=True)
    var = jnp.mean(jnp.square(xf - mu), axis=-1, keepdims=True)
    return ((xf - mu) * lax.rsqrt(var + LN_EPS) * g + b).astype(x.dtype)


def rms_norm(x, g):
    xf = x.astype(jnp.float32)
    return (xf * lax.rsqrt(jnp.mean(xf * xf, axis=-1, keepdims=True) + RMS_EPS) * g).astype(x.dtype)


def l2norm(x):
    return x * lax.rsqrt(jnp.sum(x * x, axis=-1, keepdims=True) + RMS_EPS)


def adaln(cond, w, b):
    m = jax.nn.silu(cond) @ w + b
    return [t[:, None, :] for t in jnp.split(m, 6, axis=-1)]


def modulate(x, shift, scale):
    return x * (1.0 + scale) + shift


def split_proj(p):
    B, L, _ = p.shape
    offs, acc = [], 0
    for s in PROJ_SIZES[:-1]:
        acc += s
        offs.append(acc)
    qa, ka, va, qkv_b, gate_b, a_b, b_b, qc, kc, vc = jnp.split(p, offs, axis=-1)
    na = lambda t: t.reshape(B, L, H_NA, HEAD_DIM)
    df = lambda t: t.reshape(B, L, H_DF, 2, DF_QK)
    return (na(qa), na(ka), na(va), qkv_b, gate_b,
            a_b.reshape(B, L, 2, H_DN), b_b.reshape(B, L, 2, H_DN),
            df(qc), df(kc), vc.reshape(B, L, H_DF, HEAD_DIM))


def map_query_blocks(fn, q):
    B, L = q.shape[:2]
    qb = jnp.moveaxis(q.reshape(B, L // Q_BLOCK, Q_BLOCK, *q.shape[2:]), 1, 0)
    out = lax.map(fn, qb)
    return jnp.moveaxis(out, 0, 1).reshape(B, L, *out.shape[3:])


def softmax_attend(qi, k, v):
    s = jnp.einsum('bqhd,bkhd->bhqk', qi, k, preferred_element_type=jnp.float32) * HEAD_DIM ** -0.5
    p = jax.nn.softmax(s, axis=-1).astype(v.dtype)
    return jnp.einsum('bhqk,bkhd->bqhd', p, v)


def diff_lambda(lam_p, layer):
    lam_init = 0.8 - 0.6 * math.exp(-0.3 * layer)
    lp = lam_p.astype(jnp.float32)
    lam = jnp.exp(jnp.sum(lp[0] * lp[1])) - jnp.exp(jnp.sum(lp[2] * lp[3])) + lam_init
    return lam, lam_init


def diff_attend(qi, k, v, lam):
    s = jnp.einsum('bqhid,bkhid->bhiqk', qi, k, preferred_element_type=jnp.float32) * DF_QK ** -0.5
    p = jax.nn.softmax(s, axis=-1)
    a = (p[:, :, 0] - lam * p[:, :, 1]).astype(v.dtype)
    return jnp.einsum('bhqk,bkhd->bqhd', a, v)


def axial_rope(x):
    N, R = x.shape[1], x.shape[-1]
    nf = R // 4
    inv = ROPE_BASE ** (-jnp.arange(nf, dtype=jnp.float32) / nf)
    t = jnp.arange(N)
    pos = jnp.stack([t // GRID_W, t % GRID_W], axis=-1).astype(jnp.float32)
    ang = pos[:, :, None] * inv
    cos, sin = jnp.cos(ang)[:, None], jnp.sin(ang)[:, None]
    xr = x.astype(jnp.float32).reshape(*x.shape[:-1], 2, 2, nf)
    x1, x2 = xr[..., 0, :], xr[..., 1, :]
    out = jnp.stack([x1 * cos - x2 * sin, x1 * sin + x2 * cos], axis=-2)
    return out.reshape(x.shape).astype(x.dtype)


def neighbourhood_attend(q, k, v, kc, vc, rpb):
    B, N, H, Dh = q.shape
    rows = N // GRID_W
    wr = min(WIN_R, rows)
    ncb = GRID_W // QCB
    r = jnp.arange(rows)
    row_idx = jnp.clip(r - wr // 2, 0, rows - wr)[:, None] + jnp.arange(wr)
    ks = jnp.clip(jnp.arange(ncb) * QCB - WIN_C // 2, 0, GRID_W - KCB)
    col_idx = ks[:, None] + jnp.arange(KCB)
    qcol = (jnp.arange(ncb) * QCB)[:, None] + jnp.arange(QCB)
    cs = jnp.clip(qcol - WIN_C // 2, 0, GRID_W - WIN_C)
    kcol = col_idx[:, None, :]
    valid = (kcol >= cs[..., None]) & (kcol < cs[..., None] + WIN_C)
    dr = row_idx - r[:, None] + (WIN_R - 1)
    dc = jnp.clip(kcol - qcol[..., None] + (WIN_C - 1), 0, 2 * WIN_C - 2)
    bias = rpb[:, dr[:, None, None, :, None], dc[None, :, :, None, :]].astype(jnp.float32)
    ri, ci = row_idx[:, :, None, None], col_idx[None, None]
    kb = k.reshape(B, rows, GRID_W, H, Dh)[:, ri, ci]
    vb = v.reshape(B, rows, GRID_W, H, Dh)[:, ri, ci]
    qb = q.reshape(B, rows, ncb, QCB, H, Dh)
    scale = Dh ** -0.5
    s_loc = jnp.einsum('brjqhd,brwjkhd->bhrjqwk', qb, kb, preferred_element_type=jnp.float32) * scale + bias
    s_loc = jnp.where(valid[:, :, None, :], s_loc, NEG_INF)
    s_ctx = jnp.einsum('brjqhd,bmhd->bhrjqm', qb, kc, preferred_element_type=jnp.float32) * scale
    nl = wr * KCB
    s = jnp.concatenate([s_loc.reshape(*s_loc.shape[:5], nl), s_ctx], axis=-1)
    p = jax.nn.softmax(s, axis=-1).astype(v.dtype)
    p_loc = p[..., :nl].reshape(s_loc.shape)
    o = (jnp.einsum('bhrjqwk,brwjkhd->brjqhd', p_loc, vb)
         + jnp.einsum('bhrjqm,bmhd->brjqhd', p[..., nl:], vc))
    return o.reshape(B, N, H * Dh)


def short_conv(x, w):
    return lax.conv_general_dilated(x, w[:, None, :], window_strides=(1,),
                                    padding=[(CONV_K // 2, CONV_K // 2)],
                                    dimension_numbers=('NWC', 'WIO', 'NWC'),
                                    feature_group_count=x.shape[-1])


def gated_delta_chunked(q, k, v, g, beta, s0):
    B, L, H, Dk = q.shape
    Dv = v.shape[-1]
    n = L // CHUNK

    def chunks(t):
        t = t.reshape(B, n, CHUNK, H, *t.shape[3:])
        return jnp.moveaxis(jnp.moveaxis(t, 1, 0), 3, 2)

    q, k, v, g, beta = (chunks(t) for t in (q, k, v, g, beta))
    gc = jnp.cumsum(g, axis=-1)
    idx = jnp.arange(CHUNK)
    incl = idx[:, None] >= idx[None, :]
    strict = idx[:, None] > idx[None, :]
    decay = jnp.exp(jnp.where(incl, gc[..., :, None] - gc[..., None, :], -jnp.inf))
    kb = k * beta[..., None]
    a_mat = jnp.where(strict, jnp.einsum('nbhck,nbhsk->nbhcs', kb, k) * decay, 0.0)
    rhs = jnp.concatenate([v * beta[..., None], kb * jnp.exp(gc)[..., None]], axis=-1)
    sol = lax.linalg.triangular_solve(a_mat, rhs, left_side=True, lower=True, unit_diagonal=True)
    value, k_cum = sol[..., :Dv], sol[..., Dv:]
    attn_q = jnp.einsum('nbhck,nbhsk->nbhcs', q, k) * decay
    q_dec = q * jnp.exp(gc)[..., None]
    k_tail = k * jnp.exp(gc[..., -1:] - gc)[..., None]
    c_dec = jnp.exp(gc[..., -1])

    def step(S, xs):
        qd, kcd, val, aq, kt, cd = xs
        v_new = val - jnp.einsum('bhck,bhkv->bhcv', kcd, S)
        o = jnp.einsum('bhck,bhkv->bhcv', qd, S) + jnp.einsum('bhcs,bhsv->bhcv', aq, v_new)
        S = S * cd[..., None, None] + jnp.einsum('bhck,bhcv->bhkv', kt, v_new)
        return S, o

    S, o = lax.scan(step, s0, (q_dec, k_cum, value, attn_q, k_tail, c_dec))
    o = jnp.moveaxis(jnp.moveaxis(o, 2, 3), 0, 1).reshape(B, L, H, Dv)
    return o, S


def delta_mixer(qkv, gate, a, b, conv_w, a_log, dt_bias, onorm, s0):
    B, L, _ = qkv.shape
    u = jax.nn.silu(short_conv(qkv, conv_w)).astype(jnp.float32)
    q, k, v = jnp.split(u, 3, axis=-1)
    q = l2norm(q.reshape(B, L, H_DN, HEAD_DIM)) * HEAD_DIM ** -0.5
    k = l2norm(k.reshape(B, L, H_DN, HEAD_DIM))
    v = v.reshape(B, L, H_DN, HEAD_DIM)
    g = -jnp.exp(a_log.astype(jnp.float32)) * jax.nn.softplus(a.astype(jnp.float32) + dt_bias.astype(jnp.float32))
    beta = jax.nn.sigmoid(b.astype(jnp.float32))
    s0 = s0.astype(jnp.float32)
    o_f, s_f = gated_delta_chunked(q, k, v, g[:, :, 0], beta[:, :, 0], s0[:, 0])
    rev = lambda t: jnp.flip(t, axis=1)
    o_b, s_b = gated_delta_chunked(rev(q), rev(k), rev(v), rev(g[:, :, 1]), rev(beta[:, :, 1]), s0[:, 1])
    o = o_f + rev(o_b)
    o = rms_norm(o, onorm) * jax.nn.silu(gate.reshape(B, L, H_DN, HEAD_DIM).astype(jnp.float32))
    return o.reshape(B, L, W_DN).astype(qkv.dtype), jnp.stack([s_f, s_b], axis=1)


def context_mixer(h, w_in_l, conv_l, alog_l, dtb_l, onorm_l, lam_l, subln_l, w_out_l, layer):
    B, L, _ = h.shape
    qa, ka, va, qkv_b, gate_b, a_b, b_b, qc, kc, vc = split_proj(h @ w_in_l)
    oa = map_query_blocks(lambda qi: softmax_attend(qi, ka, va), qa)
    s0 = jnp.zeros((B, 2, H_DN, HEAD_DIM, HEAD_DIM), jnp.float32)
    ob, s_dn = delta_mixer(qkv_b, gate_b, a_b, b_b, conv_l, alog_l, dtb_l, onorm_l, s0)
    lam, lam_init = diff_lambda(lam_l, layer)
    oc = map_query_blocks(lambda qi: diff_attend(qi, kc, vc, lam), qc)
    oc = rms_norm(oc, subln_l) * (1.0 - lam_init)
    o = jnp.concatenate([oa.reshape(B, L, W_NA), ob, oc.reshape(B, L, W_DF)], axis=-1) @ w_out_l
    return o, (ka, va, kc.reshape(B, L, H_DF, HEAD_DIM), vc, s_dn.astype(h.dtype))


def latent_mixer(h, ck_na, cv_na, ck_df, cv_df, s_dn, w_in_l, rpb_l, conv_l, alog_l, dtb_l, onorm_l,
                 lam_l, subln_l, w_out_l, layer):
    B, N, _ = h.shape
    Lc = ck_df.shape[1]
    qa, ka, va, qkv_b, gate_b, a_b, b_b, qc, kc, vc = split_proj(h @ w_in_l)
    oa = neighbourhood_attend(qa, ka, va, ck_na, cv_na, rpb_l)
    ob, _ = delta_mixer(qkv_b, gate_b, a_b, b_b, conv_l, alog_l, dtb_l, onorm_l, s_dn)
    qc = axial_rope(qc.reshape(B, N, 2 * H_DF, DF_QK)).reshape(B, N, H_DF, 2, DF_QK)
    kc = axial_rope(kc.reshape(B, N, 2 * H_DF, DF_QK)).reshape(B, N, H_DF, 2, DF_QK)
    k_all = jnp.concatenate([kc, ck_df.reshape(B, Lc, H_DF, 2, DF_QK)], axis=1)
    v_all = jnp.concatenate([vc, cv_df], axis=1)
    lam, lam_init = diff_lambda(lam_l, layer)
    oc = map_query_blocks(lambda qi: diff_attend(qi, k_all, v_all, lam), qc)
    oc = rms_norm(oc, subln_l) * (1.0 - lam_init)
    return jnp.concatenate([oa, ob, oc.reshape(B, N, W_DF)], axis=-1) @ w_out_l


def swiglu(h, wg, wu, wd):
    return (jax.nn.silu(h @ wg) * (h @ wu)) @ wd


def moe_swiglu(h, router_w, router_b, wg, wu, wd):
    B, L, D = h.shape
    t = h.reshape(B * L, D)
    logits = (t @ router_w + router_b).astype(jnp.float32)
    top_v, top_i = lax.top_k(logits, TOP_K)
    w = jax.nn.softmax(top_v, axis=-1)
    gates = jnp.sum(jax.nn.one_hot(top_i, N_EXPERTS, dtype=jnp.float32) * w[..., None], axis=1).astype(t.dtype)
    y = jnp.zeros_like(t)
    for e in range(N_EXPERTS):
        y = y + gates[:, e:e + 1] * swiglu(t, wg[e], wu[e], wd[e])
    return y.reshape(B, L, D)


def setup_inputs(seed: int = 0) -> dict:
    ks = jax.random.split(jax.random.key(seed), 32)

    def nrm(i, shape, scale):
        return jax.random.normal(ks[i], shape, jnp.float32) * scale

    def gain(i, shape):
        return 1.0 + nrm(i, shape, 0.02)

    D = D_MODEL
    s_in = D ** -0.5
    dt = jnp.exp(jax.random.uniform(ks[14], (DEPTH, 2, H_DN), jnp.float32, math.log(1e-3), math.log(1e-1)))
    return {
        'x_prompt': nrm(0, (BATCH, SEQ, D), 1.0),
        'x_sample': nrm(1, (DEC_BATCH, DEC_SEQ, D), 1.0),
        'cache_na_k': nrm(2, (DEC_BATCH, DEPTH, PAST_LEN, H_NA, HEAD_DIM), 1.0),
        'cache_na_v': nrm(3, (DEC_BATCH, DEPTH, PAST_LEN, H_NA, HEAD_DIM), 1.0),
        'cache_df_k': nrm(4, (DEC_BATCH, DEPTH, PAST_LEN, H_DF, HEAD_DIM), 1.0),
        'cache_df_v': nrm(5, (DEC_BATCH, DEPTH, PAST_LEN, H_DF, HEAD_DIM), 1.0),
        'state_dn': nrm(6, (DEC_BATCH, DEPTH, 2, H_DN, HEAD_DIM, HEAD_DIM), 0.2),
        'c': nrm(7, (DEC_BATCH, D), 1.0),
        'c_ctx': nrm(8, (D,), 1.0),
        'ada_w': nrm(9, (DEPTH, D, 6 * D), 0.5 * s_in),
        'ada_b': nrm(10, (DEPTH, 6 * D), 0.02),
        'w_in': nrm(11, (DEPTH, D, P_IN), s_in),
        'conv_dn': nrm(12, (DEPTH, CONV_K, 3 * W_DN), CONV_K ** -0.5),
        'a_log_dn': jnp.log(jax.random.uniform(ks[13], (DEPTH, 2, H_DN), jnp.float32, 1.0, 16.0)),
        'dt_bias_dn': dt + jnp.log(-jnp.expm1(-dt)),
        'onorm_dn': gain(15, (DEPTH, HEAD_DIM)),
        'rpb_na': nrm(16, (DEPTH, H_NA, 2 * WIN_R - 1, 2 * WIN_C - 1), 0.05),
        'lambda_df': nrm(17, (DEPTH, 4, DF_QK), 0.1),
        'subln_df': gain(18, (DEPTH, HEAD_DIM)),
        'w_out': nrm(19, (DEPTH, D, D), s_in * BETA_INIT),
        'ln1_g': gain(20, (DEPTH, D)),
        'ln1_b': nrm(21, (DEPTH, D), 0.02),
        'ln2_g': gain(22, (DEPTH, D)),
        'ln2_b': nrm(23, (DEPTH, D), 0.02),
        'ffn_w_gate': nrm(24, (N_DENSE, D, D_FF), s_in),
        'ffn_w_up': nrm(25, (N_DENSE, D, D_FF), s_in),
        'ffn_w_down': nrm(26, (N_DENSE, D_FF, D), D_FF ** -0.5 * BETA_INIT),
        'router_w': nrm(27, (N_MOE, D, N_EXPERTS), s_in),
        'router_b': nrm(28, (N_MOE, N_EXPERTS), 0.01),
        'moe_w_gate': nrm(29, (N_MOE, N_EXPERTS, D, MOE_FF), s_in),
        'moe_w_up': nrm(30, (N_MOE, N_EXPERTS, D, MOE_FF), s_in),
        'moe_w_down': nrm(31, (N_MOE, N_EXPERTS, MOE_FF, D), MOE_FF ** -0.5 * BETA_INIT),
    }


def reference(x_prompt, x_sample, cache_na_k, cache_na_v, cache_df_k, cache_df_v, state_dn, c,
              c_ctx, ada_w, ada_b, w_in, conv_dn, a_log_dn, dt_bias_dn, onorm_dn, rpb_na, lambda_df,
              subln_df, w_out, ln1_g, ln1_b, ln2_g, ln2_b, ffn_w_gate, ffn_w_up, ffn_w_down,
              router_w, router_b, moe_w_gate, moe_w_up, moe_w_down):
    def channel_mixer(h, layer):
        i = layer // 2
        if layer % 2 == 0:
            return swiglu(h, ffn_w_gate[i], ffn_w_up[i], ffn_w_down[i])
        return moe_swiglu(h, router_w[i], router_b[i], moe_w_gate[i], moe_w_up[i], moe_w_down[i])

    y_prompt = x_prompt
    ctx_tensors = []
    for l in range(DEPTH):
        sh1, sc1, g1, sh2, sc2, g2 = adaln(c_ctx[None, :], ada_w[l], ada_b[l])
        o, tensors = context_mixer(modulate(y_prompt, sh1, sc1), w_in[l], conv_dn[l], a_log_dn[l],
                                   dt_bias_dn[l], onorm_dn[l], lambda_df[l], subln_df[l], w_out[l], l)
        y_prompt = layer_norm(ALPHA * y_prompt + g1 * o, ln1_g[l], ln1_b[l])
        y_prompt = layer_norm(ALPHA * y_prompt + g2 * channel_mixer(modulate(y_prompt, sh2, sc2), l),
                              ln2_g[l], ln2_b[l])
        ctx_tensors.append(tensors)
    new_na_k = jnp.stack([t[0] for t in ctx_tensors], axis=1)
    new_na_v = jnp.stack([t[1] for t in ctx_tensors], axis=1)
    new_df_k = jnp.stack([t[2] for t in ctx_tensors], axis=1)
    new_df_v = jnp.stack([t[3] for t in ctx_tensors], axis=1)
    new_dn_state = jnp.stack([t[4] for t in ctx_tensors], axis=1)

    y_sample = x_sample
    for l in range(DEPTH):
        sh1, sc1, g1, sh2, sc2, g2 = adaln(c, ada_w[l], ada_b[l])
        o = latent_mixer(modulate(y_sample, sh1, sc1), cache_na_k[:, l], cache_na_v[:, l],
                         cache_df_k[:, l], cache_df_v[:, l], state_dn[:, l], w_in[l], rpb_na[l],
                         conv_dn[l], a_log_dn[l], dt_bias_dn[l], onorm_dn[l], lambda_df[l], subln_df[l],
                         w_out[l], l)
        y_sample = layer_norm(ALPHA * y_sample + g1 * o, ln1_g[l], ln1_b[l])
        y_sample = layer_norm(ALPHA * y_sample + g2 * channel_mixer(modulate(y_sample, sh2, sc2), l),
                              ln2_g[l], ln2_b[l])

    return (y_prompt, y_sample, new_na_k, new_na_v, new_df_k, new_df_v, new_dn_state)
```

```python
import functools
import math

import jax
import jax.numpy as jnp
import numpy as np
from jax import lax
from jax.experimental import pallas as pl
from jax.experimental.pallas import tpu as pltpu

F32 = jnp.float32
BF16 = jnp.bfloat16
HIGHEST = lax.Precision.HIGHEST

D_MODEL = 1024
BATCH = 32
SEQ = 256
DEPTH = 2
DEC_BATCH = 2
DEC_SEQ = 4096
PAST_LEN = 512
GRID_W = 64
GRID_H = DEC_SEQ // GRID_W
HEAD_DIM = 64
H_NA = 6
H_DN = 4
H_DF = 6
W_NA = H_NA * HEAD_DIM
W_DN = H_DN * HEAD_DIM
W_DF = H_DF * HEAD_DIM
DF_QK = HEAD_DIM // 2
WIN_R = 8
WIN_C = 16
CONV_K = 5
CHUNK = 64
D_FF = 2816
N_EXPERTS = 8
MOE_FF = 1408
ALPHA = (2 * DEPTH) ** 0.25
LN_EPS = 1e-5
RMS_EPS = 1e-6
ROPE_BASE = 10000.0
NEG_INF = -1e30
PROJ_SIZES = (W_NA, W_NA, W_NA, 3 * W_DN, W_DN, 2 * H_DN, 2 * H_DN, W_DF, W_DF, W_DF)

T_CTX = BATCH * SEQ
T_LAT = DEC_BATCH * DEC_SEQ
N_COND = 8
LANES = 128
AB_PAD = LANES
SEG_NA = (0, 3 * W_NA)
SEG_DN = (SEG_NA[1], SEG_NA[1] + 3 * W_DN)
SEG_GATE = (SEG_DN[1], SEG_DN[1] + W_DN)
SEG_DF = (SEG_GATE[1], SEG_GATE[1] + 3 * W_DF)
SEG_AB = (SEG_DF[1], SEG_DF[1] + AB_PAD)
P_PAD = SEG_AB[1]
SEGS = (SEG_NA, SEG_DN, SEG_GATE, SEG_DF, SEG_AB)
BD = H_DN * CHUNK
VMEM_LIMIT = 56 * 1024 * 1024


def _params(*sem):
    return pltpu.CompilerParams(dimension_semantics=sem, vmem_limit_bytes=VMEM_LIMIT)


def _dot(a, b):
    return jnp.dot(a.astype(BF16), b.astype(BF16), preferred_element_type=F32)


def _dot_nt(a, b):
    return lax.dot_general(a.astype(BF16), b.astype(BF16), (((1,), (1,)), ((), ())),
                           preferred_element_type=F32)


def _dot_exact(a, b):
    return jnp.dot(a, b, precision=HIGHEST, preferred_element_type=F32)


def _silu(x):
    return x * jax.nn.sigmoid(x)


def _layer_norm(x, g, b):
    mu = jnp.mean(x, axis=-1, keepdims=True)
    xc = x - mu
    var = jnp.mean(xc * xc, axis=-1, keepdims=True)
    return xc * lax.rsqrt(var + LN_EPS) * g + b


def _ada_kernel(c_ref, w_ref, b_ref, o_ref):
    o_ref[...] = _dot_exact(_silu(c_ref[...]), w_ref[...]) + b_ref[...]


def _ada_table(conds, ada_w, ada_b):
    out = pl.pallas_call(
        _ada_kernel,
        grid=(DEPTH, 6),
        in_specs=[pl.BlockSpec((N_COND, D_MODEL), lambda l, k: (0, 0)),
                  pl.BlockSpec((None, D_MODEL, D_MODEL), lambda l, k: (l, 0, k)),
                  pl.BlockSpec((None, None, 1, D_MODEL), lambda l, k: (l, k, 0, 0))],
        out_specs=pl.BlockSpec((None, None, N_COND, D_MODEL), lambda l, k: (l, k, 0, 0)),
        out_shape=jax.ShapeDtypeStruct((DEPTH, 6, N_COND, D_MODEL), F32),
        compiler_params=_params("parallel", "parallel"),
        name="ada_table",
    )(conds, ada_w, ada_b.reshape(DEPTH, 6, 1, D_MODEL))
    return out.reshape(DEPTH, 6, N_COND, 1, D_MODEL)


def _mod_spec(k, cond_fn):
    return pl.BlockSpec((None, None, 1, D_MODEL), lambda i, *_: (k, cond_fn(i), 0, 0))


def _inproj_kernel(x_ref, sh_ref, sc_ref, w_ref, *o_refs):
    h = (x_ref[...] * (1.0 + sc_ref[...]) + sh_ref[...]).astype(BF16)
    for o_ref, (a, b) in zip(o_refs, SEGS):
        o_ref[...] = jnp.dot(h, w_ref[:, a:b], preferred_element_type=F32)


def _inproj(x, mod, w, cond_fn, tm=512):
    t = x.shape[0]
    return pl.pallas_call(
        _inproj_kernel,
        grid=(t // tm,),
        in_specs=[pl.BlockSpec((tm, D_MODEL), lambda i: (i, 0)),
                  _mod_spec(0, cond_fn), _mod_spec(1, cond_fn),
                  pl.BlockSpec((D_MODEL, P_PAD), lambda i: (0, 0))],
        out_specs=[pl.BlockSpec((tm, b - a), lambda i: (i, 0)) for a, b in SEGS],
        out_shape=[jax.ShapeDtypeStruct((t, b - a), F32) for a, b in SEGS],
        compiler_params=_params("parallel"),
        name="inproj",
    )(x, mod, mod, w)


def _dn_prep_kernel(x_ref, prev_ref, next_ref, ab_ref, w_ref, alog_ref, dtb_ref, grp_ref,
                    u_ref, gb_ref, *, blocks_per_seq, rows):
    i = pl.program_id(0)
    j = i % blocks_per_seq
    prev = jnp.where(j != 0, prev_ref[...], 0.0)
    nxt = jnp.where(j != blocks_per_seq - 1, next_ref[...], 0.0)
    xe = jnp.concatenate([prev, x_ref[...], nxt], axis=0)
    w = w_ref[...]
    base = 8 - CONV_K // 2
    acc = w[0:1, :] * xe[base:base + rows, :]
    for t in range(1, CONV_K):
        acc = acc + w[t:t + 1, :] * xe[base + t:base + t + rows, :]
    u = _silu(acc)
    qk = u[:, :2 * W_DN]
    ss = _dot_exact(qk * qk, grp_ref[...])
    qk = qk * lax.rsqrt(ss + RMS_EPS)
    u_ref[:, :W_DN] = qk[:, :W_DN] * HEAD_DIM ** -0.5
    u_ref[:, W_DN:2 * W_DN] = qk[:, W_DN:]
    u_ref[:, 2 * W_DN:] = u[:, 2 * W_DN:]
    ab = ab_ref[...]
    z = ab + dtb_ref[...]
    softplus = jnp.maximum(z, 0.0) + jnp.log(1.0 + jnp.exp(-jnp.abs(z)))
    g = -jnp.exp(alog_ref[...]) * softplus
    lane = lax.broadcasted_iota(jnp.int32, ab.shape, 1)
    gb_ref[...] = jnp.where(lane < 2 * H_DN, g, jax.nn.sigmoid(ab))


def _dn_prep(qkv, ab, conv_w, alog_row, dtb_row, grp, seq_len, rows=256):
    t = qkv.shape[0]
    bps = seq_len // rows
    r8 = rows // 8
    last8 = t // 8 - 1
    kern = functools.partial(_dn_prep_kernel, blocks_per_seq=bps, rows=rows)
    return pl.pallas_call(
        kern,
        grid=(t // rows,),
        in_specs=[pl.BlockSpec((rows, 3 * W_DN), lambda i: (i, 0)),
                  pl.BlockSpec((8, 3 * W_DN), lambda i: (jnp.maximum(i * r8 - 1, 0), 0)),
                  pl.BlockSpec((8, 3 * W_DN), lambda i: (jnp.minimum((i + 1) * r8, last8), 0)),
                  pl.BlockSpec((rows, AB_PAD), lambda i: (i, 0)),
                  pl.BlockSpec((8, 3 * W_DN), lambda i: (0, 0)),
                  pl.BlockSpec((1, AB_PAD), lambda i: (0, 0)),
                  pl.BlockSpec((1, AB_PAD), lambda i: (0, 0)),
                  pl.BlockSpec((2 * W_DN, 2 * W_DN), lambda i: (0, 0))],
        out_specs=[pl.BlockSpec((rows, 3 * W_DN), lambda i: (i, 0)),
                   pl.BlockSpec((rows, AB_PAD), lambda i: (i, 0))],
        out_shape=[jax.ShapeDtypeStruct((t, 3 * W_DN), F32),
                   jax.ShapeDtypeStruct((t, AB_PAD), F32)],
        compiler_params=_params("parallel"),
        name="dn_prep",
    )(qkv, qkv, qkv, ab, conv_w, alog_row, dtb_row, grp)


def _dn_kernel(u_ref, gb_ref, e_ref, s0_ref, o_ref, sf_ref, s_scr, *, n_chunks):
    d = pl.program_id(1)
    j = pl.program_id(2)
    sign = 1 - 2 * d
    r = lax.broadcasted_iota(jnp.int32, (BD, BD), 0)
    c = lax.broadcasted_iota(jnp.int32, (BD, BD), 1)
    same = (r // CHUNK) == (c // CHUNK)
    dt = ((r % CHUNK) - (c % CHUNK)) * sign
    same_f = jnp.where(same, 1.0, 0.0)
    incl_f = jnp.where(jnp.logical_and(same, dt >= 0), 1.0, 0.0)
    strict_f = jnp.where(jnp.logical_and(same, dt > 0), 1.0, 0.0)
    eye_f = jnp.where(r == c, 1.0, 0.0)
    r64 = lax.broadcasted_iota(jnp.int32, (CHUNK, CHUNK), 0)
    c64 = lax.broadcasted_iota(jnp.int32, (CHUNK, CHUNK), 1)
    cum_f = jnp.where((r64 - c64) * sign >= 0, 1.0, 0.0)

    def lift(x):
        return jnp.where(same_f > 0.5, jnp.concatenate([x, x, x, x], axis=0), 0.0)

    @pl.when(j == 0)
    def _():
        s_scr[...] = s0_ref[...]

    def body(ci, carry):
        cc = jnp.where(d == 0, ci, n_chunks - 1 - ci)
        off = pl.multiple_of(cc * CHUNK, CHUNK)
        u = u_ref[pl.ds(off, CHUNK), :]
        q, k, v = u[:, :W_DN], u[:, W_DN:2 * W_DN], u[:, 2 * W_DN:]
        gbx = _dot_exact(gb_ref[pl.ds(off, CHUNK), :], e_ref[...])
        g, beta = gbx[:, :W_DN], gbx[:, W_DN:]
        gc = _dot_exact(cum_f, g)
        gl = jnp.where(d == 0, gc[CHUNK - 1:CHUNK, :], gc[0:1, :])
        eg = jnp.exp(gc)
        kb = k * beta
        gcol = lift(gc)
        diff = gcol - gcol.T
        decay = jnp.exp(jnp.where(incl_f > 0.5, diff, NEG_INF))
        k_bd = lift(k).astype(BF16)
        attn = _dot_nt(lift(q), k_bd) * decay
        n_mat = -(_dot_nt(lift(kb), k_bd) * decay) * strict_f
        t_inv = eye_f + n_mat
        n_hi = n_mat.astype(BF16)
        m_b = n_hi
        for _ in range(int(math.log2(CHUNK)) - 1):
            m = jnp.dot(m_b, m_b, preferred_element_type=F32)
            m_b = m.astype(BF16)
            t_inv = t_inv + jnp.dot(t_inv.astype(BF16), m_b, preferred_element_type=F32)
        n_lo = (n_mat - n_hi.astype(F32)).astype(BF16)
        x_hi = t_inv.astype(BF16)
        x_lo = (t_inv - x_hi.astype(F32)).astype(BF16)
        nx = (jnp.dot(n_hi, x_hi, preferred_element_type=F32)
              + jnp.dot(n_lo, x_hi, preferred_element_type=F32)
              + jnp.dot(n_hi, x_lo, preferred_element_type=F32))
        resid = eye_f - t_inv + nx
        t_inv = t_inv + jnp.dot(x_hi, resid.astype(BF16), preferred_element_type=F32)
        t_b = t_inv.astype(BF16)
        value = jnp.dot(t_b, lift(v * beta).astype(BF16), preferred_element_type=F32)
        k_cum = jnp.dot(t_b, lift(kb * eg).astype(BF16), preferred_element_type=F32)
        s = s_scr[...]
        s_b = s.astype(BF16)
        v_new = value - jnp.dot(k_cum.astype(BF16), s_b, preferred_element_type=F32)
        v_new_b = v_new.astype(BF16)
        o_bd = (jnp.dot(lift(q * eg).astype(BF16), s_b, preferred_element_type=F32)
                + jnp.dot(attn.astype(BF16), v_new_b, preferred_element_type=F32))
        o_tm = (o_bd[0:CHUNK] + o_bd[CHUNK:2 * CHUNK]
                + o_bd[2 * CHUNK:3 * CHUNK] + o_bd[3 * CHUNK:])
        k_tail = lift(k * jnp.exp(gl - gc))
        s_scr[...] = s * jnp.exp(gl) + jnp.dot(k_tail.T.astype(BF16), v_new_b,
                                               preferred_element_type=F32)

        o_ref[pl.ds(off, CHUNK), :] = o_tm
        return carry

    lax.fori_loop(0, n_chunks, body, 0)

    @pl.when(j == pl.num_programs(2) - 1)
    def _():
        sf_ref[...] = s_scr[...]


def _deltanet(u, gb, e_mat, s0_bd, seq_len, rows):
    t = u.shape[0]
    n_seq = t // seq_len
    nb = seq_len // rows
    kern = functools.partial(_dn_kernel, n_chunks=rows // CHUNK)
    blk = lambda s, d, j: s * nb + jnp.where(d == 0, j, nb - 1 - j)
    return pl.pallas_call(
        kern,
        grid=(n_seq, 2, nb),
        in_specs=[pl.BlockSpec((rows, 3 * W_DN), lambda s, d, j: (blk(s, d, j), 0)),
                  pl.BlockSpec((rows, AB_PAD), lambda s, d, j: (blk(s, d, j), 0)),
                  pl.BlockSpec((None, AB_PAD, 2 * W_DN), lambda s, d, j: (d, 0, 0)),
                  pl.BlockSpec((None, None, BD, BD), lambda s, d, j: (s, d, 0, 0))],
        out_specs=[pl.BlockSpec((None, rows, W_DN), lambda s, d, j: (d, blk(s, d, j), 0)),
                   pl.BlockSpec((None, None, BD, BD), lambda s, d, j: (s, d, 0, 0))],
        out_shape=[jax.ShapeDtypeStruct((2, t, W_DN), F32),
                   jax.ShapeDtypeStruct((n_seq, 2, BD, BD), F32)],
        scratch_shapes=[pltpu.VMEM((BD, BD), F32)],
        compiler_params=_params("parallel", "arbitrary", "arbitrary"),
        name="deltanet",
    )(u, gb, e_mat, s0_bd)


def _diff_lambda(lam_ref, lam_init):
    lp = lam_ref[...]
    return (jnp.exp(jnp.sum(lp[0:1] * lp[1:2], axis=1, keepdims=True))
            - jnp.exp(jnp.sum(lp[2:3] * lp[3:4], axis=1, keepdims=True)) + lam_init)


def _sub_rms(o, subln, lam_init):
    ms = jnp.mean(o * o, axis=-1, keepdims=True)
    return o * lax.rsqrt(ms + RMS_EPS) * subln * (1.0 - lam_init)


def _ctx_attn_kernel(na_ref, df_ref, lam_ref, subln_ref, oa_ref, oc_ref, *, lam_init):
    na = na_ref[...]
    df = df_ref[...]
    lam = _diff_lambda(lam_ref, lam_init)
    subln = subln_ref[...]
    for h in range(H_NA):
        a = h * HEAD_DIM
        q, k, v = (na[:, a + o:a + o + HEAD_DIM] for o in (0, W_NA, 2 * W_NA))
        s = _dot_nt(q, k) * HEAD_DIM ** -0.5
        p = jnp.exp(s - jnp.max(s, axis=-1, keepdims=True))
        oa_ref[:, a:a + HEAD_DIM] = _dot(p, v) / jnp.sum(p, axis=-1, keepdims=True)
    for h in range(H_DF):
        a = h * HEAD_DIM
        ps = []
        for i in range(2):
            b = a + i * DF_QK
            s = _dot_nt(df[:, b:b + DF_QK], df[:, W_DF + b:W_DF + b + DF_QK]) * DF_QK ** -0.5
            p = jnp.exp(s - jnp.max(s, axis=-1, keepdims=True))
            ps.append(p / jnp.sum(p, axis=-1, keepdims=True))
        o = _dot(ps[0] - lam * ps[1], df[:, 2 * W_DF + a:2 * W_DF + a + HEAD_DIM])
        oc_ref[:, a:a + HEAD_DIM] = _sub_rms(o, subln, lam_init)


def _ctx_attn(qkv_na, qkv_df, lam_p, subln_row, layer):
    lam_init = 0.8 - 0.6 * math.exp(-0.3 * layer)
    kern = functools.partial(_ctx_attn_kernel, lam_init=lam_init)
    return pl.pallas_call(
        kern,
        grid=(BATCH,),
        in_specs=[pl.BlockSpec((SEQ, 3 * W_NA), lambda b: (b, 0)),
                  pl.BlockSpec((SEQ, 3 * W_DF), lambda b: (b, 0)),
                  pl.BlockSpec((4, DF_QK), lambda b: (0, 0)),
                  pl.BlockSpec((1, HEAD_DIM), lambda b: (0, 0))],
        out_specs=[pl.BlockSpec((SEQ, W_NA), lambda b: (b, 0)),
                   pl.BlockSpec((SEQ, W_DF), lambda b: (b, 0))],
        out_shape=[jax.ShapeDtypeStruct((T_CTX, W_NA), F32),
                   jax.ShapeDtypeStruct((T_CTX, W_DF), F32)],
        compiler_params=_params("parallel"),
        name="ctx_attn",
    )(qkv_na, qkv_df, lam_p, subln_row)


def _na_bias_table(rpb):
    qc = np.arange(GRID_W)[:, None]
    kc = np.arange(GRID_W)[None, :]
    cs = np.clip(qc - WIN_C // 2, 0, GRID_W - WIN_C)
    valid = (kc >= cs) & (kc < cs + WIN_C)
    dc = np.clip(kc - qc + (WIN_C - 1), 0, 2 * WIN_C - 2)
    dr = np.arange(WIN_R)[:, None] + np.arange(WIN_R)[None, :]
    tab = rpb[:, dr[:, :, None, None], dc[None, None, :, :]]
    tab = jnp.where(valid[None, None, None], tab.astype(F32), NEG_INF)
    return jnp.transpose(tab, (0, 1, 3, 2, 4)).reshape(H_NA, WIN_R, GRID_W, WIN_R * GRID_W)


def _row_window_start(r):
    return jnp.clip(r - WIN_R // 2, 0, GRID_H - WIN_R)


def _lat_na_kernel(q_ref, k_ref, v_ref, kc_ref, vc_ref, bias_ref, o_ref):
    r = pl.program_id(1)
    start = pl.multiple_of(_row_window_start(r) * GRID_W, GRID_W)
    q = q_ref[...]
    kw = k_ref[pl.ds(start, WIN_R * GRID_W), :]
    vw = v_ref[pl.ds(start, WIN_R * GRID_W), :]
    kc = kc_ref[...]
    vc = vc_ref[...]
    scale = HEAD_DIM ** -0.5
    for h in range(H_NA):
        sl = slice(h * HEAD_DIM, (h + 1) * HEAD_DIM)
        qh = q[:, sl].astype(BF16)
        s_loc = _dot_nt(qh, kw[:, sl]) * scale + bias_ref[h]
        s_ctx = _dot_nt(qh, kc[:, sl]) * scale
        m = jnp.maximum(jnp.max(s_loc, axis=-1, keepdims=True), jnp.max(s_ctx, axis=-1, keepdims=True))
        p_loc = jnp.exp(s_loc - m)
        p_ctx = jnp.exp(s_ctx - m)
        den = jnp.sum(p_loc, axis=-1, keepdims=True) + jnp.sum(p_ctx, axis=-1, keepdims=True)
        o_ref[:, sl] = (_dot(p_loc, vw[:, sl]) + _dot(p_ctx, vc[:, sl])) / den


def _lat_na(qkv_na, ck, cv, bias_tab):
    rows_per_b = DEC_SEQ // GRID_W
    return pl.pallas_call(
        _lat_na_kernel,
        grid=(DEC_BATCH, rows_per_b),
        in_specs=[pl.BlockSpec((GRID_W, W_NA), lambda b, r: (b * rows_per_b + r, 0)),
                  pl.BlockSpec((DEC_SEQ, W_NA), lambda b, r: (b, 1)),
                  pl.BlockSpec((DEC_SEQ, W_NA), lambda b, r: (b, 2)),
                  pl.BlockSpec((None, PAST_LEN, W_NA), lambda b, r: (b, 0, 0)),
                  pl.BlockSpec((None, PAST_LEN, W_NA), lambda b, r: (b, 0, 0)),
                  pl.BlockSpec((H_NA, None, GRID_W, WIN_R * GRID_W),
                               lambda b, r: (0, _row_window_start(r) - r + WIN_R - 1, 0, 0))],
        out_specs=pl.BlockSpec((GRID_W, W_NA), lambda b, r: (b * rows_per_b + r, 0)),
        out_shape=jax.ShapeDtypeStruct((T_LAT, W_NA), F32),
        compiler_params=_params("parallel", "arbitrary"),
        name="lat_na",
    )(qkv_na, qkv_na, qkv_na, ck, cv, bias_tab)


def _rope_tables():
    nf = DF_QK // 4
    inv = ROPE_BASE ** (-np.arange(nf, dtype=np.float32) / nf)
    t = np.arange(DEC_SEQ)
    pos = np.stack([t // GRID_W, t % GRID_W], axis=-1).astype(np.float32)
    ang = jnp.asarray(pos[:, :, None] * inv)
    cos, sin = jnp.cos(ang), jnp.sin(ang)
    cos32 = jnp.concatenate([cos, cos], axis=-1).reshape(DEC_SEQ, DF_QK)
    sin32 = jnp.concatenate([-sin, sin], axis=-1).reshape(DEC_SEQ, DF_QK)
    reps = W_DF // DF_QK
    rot = np.zeros((W_DF, W_DF), np.float32)
    for dd in range(W_DF):
        rot[dd + nf if dd % (2 * nf) < nf else dd - nf, dd] = 1.0
    return jnp.tile(cos32, (1, reps)), jnp.tile(sin32, (1, reps)), jnp.asarray(rot)


def _rope_kernel(df_ref, cos_ref, sin_ref, rot_ref, q_ref, kt_ref, v_ref):
    cos, sin, rot = cos_ref[...], sin_ref[...], rot_ref[...]
    q = df_ref[:, :W_DF]
    k = df_ref[:, W_DF:2 * W_DF]
    q_ref[...] = (q * cos + _dot_exact(q, rot) * sin).astype(BF16)
    kt_ref[...] = (k * cos + _dot_exact(k, rot) * sin).T.astype(BF16)
    v_ref[...] = df_ref[:, 2 * W_DF:].astype(BF16)


def _rope(qkv_df, cos, sin, rot, tm=512):
    nb = DEC_SEQ // tm
    return pl.pallas_call(
        _rope_kernel,
        grid=(DEC_BATCH, nb),
        in_specs=[pl.BlockSpec((tm, 3 * W_DF), lambda b, i: (b * nb + i, 0)),
                  pl.BlockSpec((tm, W_DF), lambda b, i: (i, 0)),
                  pl.BlockSpec((tm, W_DF), lambda b, i: (i, 0)),
                  pl.BlockSpec((W_DF, W_DF), lambda b, i: (0, 0))],
        out_specs=[pl.BlockSpec((tm, W_DF), lambda b, i: (b * nb + i, 0)),
                   pl.BlockSpec((None, W_DF, tm), lambda b, i: (b, 0, i)),
                   pl.BlockSpec((tm, W_DF), lambda b, i: (b * nb + i, 0))],
        out_shape=[jax.ShapeDtypeStruct((T_LAT, W_DF), BF16),
                   jax.ShapeDtypeStruct((DEC_BATCH, W_DF, DEC_SEQ), BF16),
                   jax.ShapeDtypeStruct((T_LAT, W_DF), BF16)],
        compiler_params=_params("parallel", "parallel"),
        name="rope",
    )(qkv_df, cos, sin, rot)


def _lat_df_kernel(q_ref, kt_ref, ktc_ref, v_ref, vc_ref, lam_ref, subln_ref, o_ref, *, lam_init):
    q = q_ref[...]
    lam = _diff_lambda(lam_ref, lam_init)
    subln = subln_ref[...]
    scale = DF_QK ** -0.5
    for h in range(H_DF):
        sl = slice(h * HEAD_DIM, (h + 1) * HEAD_DIM)
        ps = []
        for i in range(2):
            a = h * HEAD_DIM + i * DF_QK
            qi = q[:, a:a + DF_QK]
            s_l = jnp.dot(qi, kt_ref[a:a + DF_QK, :], preferred_element_type=F32) * scale
            s_c = jnp.dot(qi, ktc_ref[a:a + DF_QK, :], preferred_element_type=F32) * scale
            m = jnp.maximum(jnp.max(s_l, axis=-1, keepdims=True), jnp.max(s_c, axis=-1, keepdims=True))
            e_l = jnp.exp(s_l - m)
            e_c = jnp.exp(s_c - m)
            inv = 1.0 / (jnp.sum(e_l, axis=-1, keepdims=True) + jnp.sum(e_c, axis=-1, keepdims=True))
            ps.append((e_l * inv, e_c * inv))
        a_l = (ps[0][0] - lam * ps[1][0]).astype(BF16)
        a_c = (ps[0][1] - lam * ps[1][1]).astype(BF16)
        o = (jnp.dot(a_l, v_ref[:, sl], preferred_element_type=F32)
             + jnp.dot(a_c, vc_ref[:, sl], preferred_element_type=F32))
        o_ref[:, sl] = _sub_rms(o, subln, lam_init)


def _lat_df(q_r, kt, ktc, v, vc, lam_p, subln_row, layer, tq=128):
    lam_init = 0.8 - 0.6 * math.exp(-0.3 * layer)
    nb = DEC_SEQ // tq
    kern = functools.partial(_lat_df_kernel, lam_init=lam_init)
    return pl.pallas_call(
        kern,
        grid=(DEC_BATCH, nb),
        in_specs=[pl.BlockSpec((tq, W_DF), lambda b, i: (b * nb + i, 0)),
                  pl.BlockSpec((None, W_DF, DEC_SEQ), lambda b, i: (b, 0, 0)),
                  pl.BlockSpec((None, W_DF, PAST_LEN), lambda b, i: (b, 0, 0)),
                  pl.BlockSpec((DEC_SEQ, W_DF), lambda b, i: (b, 0)),
                  pl.BlockSpec((None, PAST_LEN, W_DF), lambda b, i: (b, 0, 0)),
                  pl.BlockSpec((4, DF_QK), lambda b, i: (0, 0)),
                  pl.BlockSpec((1, HEAD_DIM), lambda b, i: (0, 0))],
        out_specs=pl.BlockSpec((tq, W_DF), lambda b, i: (b * nb + i, 0)),
        out_shape=jax.ShapeDtypeStruct((T_LAT, W_DF), F32),
        compiler_params=_params("parallel", "arbitrary"),
        name="lat_df",
    )(q_r, kt, ktc, v, vc, lam_p, subln_row)


def _outproj_kernel(oa_ref, of_ref, obk_ref, gate_ref, onorm_ref, grp_ref, oc_ref, y_ref, g1_ref, w_ref,
                    lg_ref, lb_ref, o_ref):
    ob = of_ref[...] + obk_ref[...]
    ms = _dot_exact(ob * ob, grp_ref[...]) * (1.0 / HEAD_DIM)
    ob = ob * lax.rsqrt(ms + RMS_EPS) * onorm_ref[...] * _silu(gate_ref[...])
    o = (jnp.dot(oa_ref[...].astype(BF16), w_ref[:W_NA, :], preferred_element_type=F32)
         + jnp.dot(ob.astype(BF16), w_ref[W_NA:W_NA + W_DN, :], preferred_element_type=F32)
         + jnp.dot(oc_ref[...].astype(BF16), w_ref[W_NA + W_DN:, :], preferred_element_type=F32))
    o_ref[...] = _layer_norm(ALPHA * y_ref[...] + g1_ref[...] * o, lg_ref[...], lb_ref[...])


def _outproj(oa, o_dn, gate, onorm_row, grp, oc, y, mod, w, ln_g, ln_b, cond_fn, tm=512):
    t = y.shape[0]
    row = lambda n: pl.BlockSpec((tm, n), lambda i: (i, 0))
    vec = pl.BlockSpec((1, D_MODEL), lambda i: (0, 0))
    return pl.pallas_call(
        _outproj_kernel,
        grid=(t // tm,),
        in_specs=[row(W_NA),
                  pl.BlockSpec((None, tm, W_DN), lambda i: (0, i, 0)),
                  pl.BlockSpec((None, tm, W_DN), lambda i: (1, i, 0)),
                  row(W_DN), pl.BlockSpec((1, W_DN), lambda i: (0, 0)),
                  pl.BlockSpec((W_DN, W_DN), lambda i: (0, 0)),
                  row(W_DF), row(D_MODEL), _mod_spec(2, cond_fn),
                  pl.BlockSpec((D_MODEL, D_MODEL), lambda i: (0, 0)), vec, vec],
        out_specs=row(D_MODEL),
        out_shape=jax.ShapeDtypeStruct((t, D_MODEL), F32),
        compiler_params=_params("parallel"),
        name="outproj_ln",
    )(oa, o_dn, o_dn, gate, onorm_row, grp, oc, y, mod, w, ln_g, ln_b)


def _router_kernel(y_ref, sh_ref, sc_ref, w_ref, b_ref, g_ref):
    h = y_ref[...] * (1.0 + sc_ref[...]) + sh_ref[...]
    logits = _dot_exact(h, w_ref[...]) + b_ref[...]
    lane = lax.broadcasted_iota(jnp.int32, logits.shape, 1).astype(F32)
    logits = jnp.where(lane < N_EXPERTS, logits, -jnp.inf)
    m1 = jnp.max(logits, axis=-1, keepdims=True)
    i1 = jnp.min(jnp.where(logits == m1, lane, float(LANES)), axis=-1, keepdims=True)
    rest = jnp.where(lane == i1, -jnp.inf, logits)
    m2 = jnp.max(rest, axis=-1, keepdims=True)
    i2 = jnp.min(jnp.where(rest == m2, lane, float(LANES)), axis=-1, keepdims=True)
    e2 = jnp.exp(m2 - m1)
    w1 = 1.0 / (1.0 + e2)
    g_ref[...] = jnp.where(lane == i1, w1, 0.0) + jnp.where(lane == i2, e2 * w1, 0.0)


def _router(y, mod, w_pad, b_pad, cond_fn, tm=512):
    t = y.shape[0]
    return pl.pallas_call(
        _router_kernel,
        grid=(t // tm,),
        in_specs=[pl.BlockSpec((tm, D_MODEL), lambda i: (i, 0)),
                  _mod_spec(3, cond_fn), _mod_spec(4, cond_fn),
                  pl.BlockSpec((D_MODEL, LANES), lambda i: (0, 0)),
                  pl.BlockSpec((1, LANES), lambda i: (0, 0))],
        out_specs=pl.BlockSpec((tm, LANES), lambda i: (i, 0)),
        out_shape=jax.ShapeDtypeStruct((t, LANES), F32),
        compiler_params=_params("parallel"),
        name="router",
    )(y, mod, mod, w_pad, b_pad)


def _ffn_kernel(y_ref, sh_ref, sc_ref, g2_ref, gates_ref, wg_ref, wu_ref, wd_ref, lg_ref, lb_ref,
                o_ref, h_scr, acc_scr, *, n_blocks):
    e = pl.program_id(1)

    @pl.when(e == 0)
    def _():
        h_scr[...] = (y_ref[...] * (1.0 + sc_ref[...]) + sh_ref[...]).astype(BF16)
        acc_scr[...] = jnp.zeros_like(acc_scr)

    h = h_scr[...]
    a = jnp.dot(h, wg_ref[...], preferred_element_type=F32)
    b = jnp.dot(h, wu_ref[...], preferred_element_type=F32)
    f = jnp.dot((_silu(a) * b).astype(BF16), wd_ref[...], preferred_element_type=F32)
    lane = lax.broadcasted_iota(jnp.int32, gates_ref.shape, 1)
    gate = jnp.sum(jnp.where(lane == e, gates_ref[...], 0.0), axis=-1, keepdims=True)
    acc_scr[...] += gate * f

    @pl.when(e == n_blocks - 1)
    def _():
        o_ref[...] = _layer_norm(ALPHA * y_ref[...] + g2_ref[...] * acc_scr[...],
                                 lg_ref[...], lb_ref[...])


def _ffn(y, mod, gates, wg, wu, wd, ln_g, ln_b, cond_fn, tm=512):
    t = y.shape[0]
    n_blocks, _, ff = wg.shape
    vec = pl.BlockSpec((1, D_MODEL), lambda i, e: (0, 0))
    kern = functools.partial(_ffn_kernel, n_blocks=n_blocks)
    return pl.pallas_call(
        kern,
        grid=(t // tm, n_blocks),
        in_specs=[pl.BlockSpec((tm, D_MODEL), lambda i, e: (i, 0)),
                  _mod_spec(3, cond_fn), _mod_spec(4, cond_fn), _mod_spec(5, cond_fn),
                  pl.BlockSpec((tm, LANES), lambda i, e: (i, 0)),
                  pl.BlockSpec((None, D_MODEL, ff), lambda i, e: (e, 0, 0)),
                  pl.BlockSpec((None, D_MODEL, ff), lambda i, e: (e, 0, 0)),
                  pl.BlockSpec((None, ff, D_MODEL), lambda i, e: (e, 0, 0)),
                  vec, vec],
        out_specs=pl.BlockSpec((tm, D_MODEL), lambda i, e: (i, 0)),
        out_shape=jax.ShapeDtypeStruct((t, D_MODEL), F32),
        scratch_shapes=[pltpu.VMEM((tm, D_MODEL), BF16), pltpu.VMEM((tm, D_MODEL), F32)],
        compiler_params=_params("parallel", "arbitrary"),
        name="ffn_ln",
    )(y, mod, mod, mod, gates, wg, wu, wd, ln_g, ln_b)


def _permute_w_in(w):
    offs = np.cumsum((0,) + PROJ_SIZES)
    qa, ka, va, qkv, gate, a, b, qc, kc, vc = (w[:, offs[i]:offs[i + 1]] for i in range(10))
    pad = jnp.zeros((D_MODEL, AB_PAD - 4 * H_DN), w.dtype)
    return jnp.concatenate([qa, ka, va, qkv, gate, qc, kc, vc, a, b, pad], axis=1).astype(BF16)


def _pad_row(v, n=AB_PAD):
    v = v.reshape(1, -1)
    return jnp.pad(v, ((0, 0), (0, n - v.shape[1])))


def _group_matrix(n):
    idx = np.arange(n) // HEAD_DIM
    return jnp.asarray((idx[:, None] == idx[None, :]).astype(np.float32))


def _expand_matrix():
    e = np.zeros((2, AB_PAD, 2 * W_DN), np.float32)
    for d in range(2):
        for h in range(H_DN):
            e[d, d * H_DN + h, h * HEAD_DIM:(h + 1) * HEAD_DIM] = 1.0
            e[d, 2 * H_DN + d * H_DN + h, W_DN + h * HEAD_DIM:W_DN + (h + 1) * HEAD_DIM] = 1.0
    return jnp.asarray(e)


def _state_to_bd(s):
    eye = jnp.eye(H_DN, dtype=s.dtype)
    return jnp.einsum('sdhkv,hg->sdhkgv', s, eye).reshape(s.shape[0], 2, BD, BD)


def _bd_to_state(s):
    n = s.shape[0]
    return jnp.einsum('sdhkhv->sdhkv', s.reshape(n, 2, H_DN, HEAD_DIM, H_DN, HEAD_DIM))


def kernel(x_prompt, x_sample, cache_na_k, cache_na_v, cache_df_k, cache_df_v, state_dn, c, c_ctx,
           ada_w, ada_b, w_in, conv_dn, a_log_dn, dt_bias_dn, onorm_dn, rpb_na, lambda_df, subln_df,
           w_out, ln1_g, ln1_b, ln2_g, ln2_b, ffn_w_gate, ffn_w_up, ffn_w_down, router_w, router_b,
           moe_w_gate, moe_w_up, moe_w_down):
    conds = jnp.concatenate([c_ctx[None, :], c, jnp.zeros((N_COND - 1 - DEC_BATCH, D_MODEL), F32)], axis=0)
    mods = _ada_table(conds, ada_w, ada_b)
    ctx_cond = lambda i: 0
    lat_cond = lambda i: 1 + (i * 512) // DEC_SEQ

    grp512, grp256 = _group_matrix(2 * W_DN), _group_matrix(W_DN)
    e_mat = _expand_matrix()
    cos, sin, rot = _rope_tables()
    ones_gates = jnp.ones((T_CTX, LANES), F32)

    y_ctx = x_prompt.reshape(T_CTX, D_MODEL)
    y_lat = x_sample.reshape(T_LAT, D_MODEL)
    ctx_out = []
    for l in range(DEPTH):
        mod = mods[l]
        w_in_l = _permute_w_in(w_in[l])
        w_out_l = w_out[l].astype(BF16)
        conv_w = jnp.pad(conv_dn[l], ((0, 8 - CONV_K), (0, 0)))
        alog_row = _pad_row(a_log_dn[l])
        dtb_row = _pad_row(dt_bias_dn[l])
        onorm_row = jnp.tile(onorm_dn[l], H_DN).reshape(1, W_DN)
        subln_row = subln_df[l].reshape(1, HEAD_DIM)
        lg1, lb1 = ln1_g[l].reshape(1, D_MODEL), ln1_b[l].reshape(1, D_MODEL)
        lg2, lb2 = ln2_g[l].reshape(1, D_MODEL), ln2_b[l].reshape(1, D_MODEL)

        na, dn, gate, df, ab = _inproj(y_ctx, mod, w_in_l, ctx_cond)
        u, gb = _dn_prep(dn, ab, conv_w, alog_row, dtb_row, grp512, SEQ)
        s0 = jnp.zeros((BATCH, 2, BD, BD), F32)
        ob, s_fin = _deltanet(u, gb, e_mat, s0, SEQ, SEQ)
        oa, oc = _ctx_attn(na, df, lambda_df[l], subln_row, l)
        y1_ctx = _outproj(oa, ob, gate, onorm_row, grp256, oc, y_ctx, mod, w_out_l, lg1, lb1, ctx_cond)
        ctx_out.append((na[:, W_NA:2 * W_NA].reshape(BATCH, SEQ, H_NA, HEAD_DIM),
                        na[:, 2 * W_NA:].reshape(BATCH, SEQ, H_NA, HEAD_DIM),
                        df[:, W_DF:2 * W_DF].reshape(BATCH, SEQ, H_DF, HEAD_DIM),
                        df[:, 2 * W_DF:].reshape(BATCH, SEQ, H_DF, HEAD_DIM),
                        _bd_to_state(s_fin)))

        na, dn, gate, df, ab = _inproj(y_lat, mod, w_in_l, lat_cond)
        u, gb = _dn_prep(dn, ab, conv_w, alog_row, dtb_row, grp512, DEC_SEQ)
        ob, _ = _deltanet(u, gb, e_mat, _state_to_bd(state_dn[:, l]), DEC_SEQ, 512)
        oa = _lat_na(na, cache_na_k[:, l].reshape(DEC_BATCH, PAST_LEN, W_NA),
                     cache_na_v[:, l].reshape(DEC_BATCH, PAST_LEN, W_NA), _na_bias_table(rpb_na[l]))
        q_r, kt, v_b = _rope(df, cos, sin, rot)
        ktc = jnp.swapaxes(cache_df_k[:, l].reshape(DEC_BATCH, PAST_LEN, W_DF), 1, 2).astype(BF16)
        vc = cache_df_v[:, l].reshape(DEC_BATCH, PAST_LEN, W_DF).astype(BF16)
        oc = _lat_df(q_r, kt, ktc, v_b, vc, lambda_df[l], subln_row, l)
        y1_lat = _outproj(oa, ob, gate, onorm_row, grp256, oc, y_lat, mod, w_out_l, lg1, lb1, lat_cond)

        i = l // 2
        if l % 2 == 0:
            split = lambda w: jnp.transpose(w.reshape(D_MODEL, D_FF // MOE_FF, MOE_FF), (1, 0, 2))
            wg, wu = split(ffn_w_gate[i]).astype(BF16), split(ffn_w_up[i]).astype(BF16)
            wd = ffn_w_down[i].reshape(D_FF // MOE_FF, MOE_FF, D_MODEL).astype(BF16)
            g_ctx = g_lat = ones_gates
        else:
            wg, wu, wd = moe_w_gate[i].astype(BF16), moe_w_up[i].astype(BF16), moe_w_down[i].astype(BF16)
            rw = jnp.pad(router_w[i], ((0, 0), (0, LANES - N_EXPERTS)))
            rb = _pad_row(router_b[i], LANES)
            g_ctx = _router(y1_ctx, mod, rw, rb, ctx_cond)
            g_lat = _router(y1_lat, mod, rw, rb, lat_cond)
        y_ctx = _ffn(y1_ctx, mod, g_ctx, wg, wu, wd, lg2, lb2, ctx_cond)
        y_lat = _ffn(y1_lat, mod, g_lat, wg, wu, wd, lg2, lb2, lat_cond)

    stack = lambda j: jnp.stack([t[j] for t in ctx_out], axis=1)
    return (y_ctx.reshape(BATCH, SEQ, D_MODEL), y_lat.reshape(DEC_BATCH, DEC_SEQ, D_MODEL),
            stack(0), stack(1), stack(2), stack(3), stack(4))
```

```python
import functools
import math

import jax
import jax.numpy as jnp
import numpy as np
from jax import lax
from jax.experimental import pallas as pl
from jax.experimental.pallas import tpu as pltpu

F32 = jnp.float32
BF16 = jnp.bfloat16
HIGHEST = lax.Precision.HIGHEST

D_MODEL = 1024
BATCH = 32
SEQ = 256
DEPTH = 2
DEC_BATCH = 2
DEC_SEQ = 4096
PAST_LEN = 512
GRID_W = 64
GRID_H = DEC_SEQ // GRID_W
HEAD_DIM = 64
H_NA = 6
H_DN = 4
H_DF = 6
W_NA = H_NA * HEAD_DIM
W_DN = H_DN * HEAD_DIM
W_DF = H_DF * HEAD_DIM
DF_QK = HEAD_DIM // 2
WIN_R = 8
WIN_C = 16
CONV_K = 5
CHUNK = 64
D_FF = 2816
N_EXPERTS = 8
MOE_FF = 1408
ALPHA = (2 * DEPTH) ** 0.25
LN_EPS = 1e-5
RMS_EPS = 1e-6
ROPE_BASE = 10000.0
NEG_INF = -1e30
PROJ_SIZES = (W_NA, W_NA, W_NA, 3 * W_DN, W_DN, 2 * H_DN, 2 * H_DN, W_DF, W_DF, W_DF)

T_CTX = BATCH * SEQ
T_LAT = DEC_BATCH * DEC_SEQ
N_COND = 8
LANES = 128
AB_PAD = LANES
SEG_NA = (0, 3 * W_NA)
SEG_DN = (SEG_NA[1], SEG_NA[1] + 3 * W_DN)
SEG_GATE = (SEG_DN[1], SEG_DN[1] + W_DN)
SEG_DF = (SEG_GATE[1], SEG_GATE[1] + 3 * W_DF)
SEG_AB = (SEG_DF[1], SEG_DF[1] + AB_PAD)
P_PAD = SEG_AB[1]
SEGS = (SEG_NA, SEG_DN, SEG_GATE, SEG_DF, SEG_AB)
BD = H_DN * CHUNK
VMEM_LIMIT = 56 * 1024 * 1024


def _params(*sem):
    return pltpu.CompilerParams(dimension_semantics=sem, vmem_limit_bytes=VMEM_LIMIT)


def _dot(a, b):
    return jnp.dot(a.astype(BF16), b.astype(BF16), preferred_element_type=F32)


def _dot_nt(a, b):
    return lax.dot_general(a.astype(BF16), b.astype(BF16), (((1,), (1,)), ((), ())),
                           preferred_element_type=F32)


def _dot_exact(a, b):
    return jnp.dot(a, b, precision=HIGHEST, preferred_element_type=F32)


def _silu(x):
    return x * jax.nn.sigmoid(x)


def _layer_norm(x, g, b):
    mu = jnp.mean(x, axis=-1, keepdims=True)
    xc = x - mu
    var = jnp.mean(xc * xc, axis=-1, keepdims=True)
    return xc * lax.rsqrt(var + LN_EPS) * g + b


def _ada_kernel(c_ref, w_ref, b_ref, o_ref):
    o_ref[...] = _dot_exact(_silu(c_ref[...]), w_ref[...]) + b_ref[...]


def _ada_table(conds, ada_w, ada_b):
    out = pl.pallas_call(
        _ada_kernel,
        grid=(DEPTH, 6),
        in_specs=[pl.BlockSpec((N_COND, D_MODEL), lambda l, k: (0, 0)),
                  pl.BlockSpec((None, D_MODEL, D_MODEL), lambda l, k: (l, 0, k)),
                  pl.BlockSpec((None, None, 1, D_MODEL), lambda l, k: (l, k, 0, 0))],
        out_specs=pl.BlockSpec((None, None, N_COND, D_MODEL), lambda l, k: (l, k, 0, 0)),
        out_shape=jax.ShapeDtypeStruct((DEPTH, 6, N_COND, D_MODEL), F32),
        compiler_params=_params("parallel", "parallel"),
        name="ada_table",
    )(conds, ada_w, ada_b.reshape(DEPTH, 6, 1, D_MODEL))
    return out.reshape(DEPTH, 6, N_COND, 1, D_MODEL)


def _mod_spec(k, cond_fn):
    return pl.BlockSpec((None, None, 1, D_MODEL), lambda i, *_: (k, cond_fn(i), 0, 0))


def _inproj_kernel(x_ref, sh_ref, sc_ref, w_ref, *o_refs):
    h = (x_ref[...] * (1.0 + sc_ref[...]) + sh_ref[...]).astype(BF16)
    for o_ref, (a, b) in zip(o_refs, SEGS):
        o_ref[...] = jnp.dot(h, w_ref[:, a:b], preferred_element_type=F32)


def _inproj(x, mod, w, cond_fn, tm=512):
    t = x.shape[0]
    return pl.pallas_call(
        _inproj_kernel,
        grid=(t // tm,),
        in_specs=[pl.BlockSpec((tm, D_MODEL), lambda i: (i, 0)),
                  _mod_spec(0, cond_fn), _mod_spec(1, cond_fn),
                  pl.BlockSpec((D_MODEL, P_PAD), lambda i: (0, 0))],
        out_specs=[pl.BlockSpec((tm, b - a), lambda i: (i, 0)) for a, b in SEGS],
        out_shape=[jax.ShapeDtypeStruct((t, b - a), F32) for a, b in SEGS],
        compiler_params=_params("parallel"),
        name="inproj",
    )(x, mod, mod, w)


def _dn_prep_kernel(x_ref, prev_ref, next_ref, ab_ref, w_ref, alog_ref, dtb_ref, grp_ref,
                    u_ref, gb_ref, *, blocks_per_seq, rows):
    i = pl.program_id(0)
    j = i % blocks_per_seq
    prev = jnp.where(j != 0, prev_ref[...], 0.0)
    nxt = jnp.where(j != blocks_per_seq - 1, next_ref[...], 0.0)
    xe = jnp.concatenate([prev, x_ref[...], nxt], axis=0)
    w = w_ref[...]
    base = 8 - CONV_K // 2
    acc = w[0:1, :] * xe[base:base + rows, :]
    for t in range(1, CONV_K):
        acc = acc + w[t:t + 1, :] * xe[base + t:base + t + rows, :]
    u = _silu(acc)
    qk = u[:, :2 * W_DN]
    ss = _dot_exact(qk * qk, grp_ref[...])
    qk = qk * lax.rsqrt(ss + RMS_EPS)
    u_ref[:, :W_DN] = qk[:, :W_DN] * HEAD_DIM ** -0.5
    u_ref[:, W_DN:2 * W_DN] = qk[:, W_DN:]
    u_ref[:, 2 * W_DN:] = u[:, 2 * W_DN:]
    ab = ab_ref[...]
    z = ab + dtb_ref[...]
    softplus = jnp.maximum(z, 0.0) + jnp.log(1.0 + jnp.exp(-jnp.abs(z)))
    g = -jnp.exp(alog_ref[...]) * softplus
    lane = lax.broadcasted_iota(jnp.int32, ab.shape, 1)
    gb_ref[...] = jnp.where(lane < 2 * H_DN, g, jax.nn.sigmoid(ab))


def _dn_prep(qkv, ab, conv_w, alog_row, dtb_row, grp, seq_len, rows=256):
    t = qkv.shape[0]
    bps = seq_len // rows
    r8 = rows // 8
    last8 = t // 8 - 1
    kern = functools.partial(_dn_prep_kernel, blocks_per_seq=bps, rows=rows)
    return pl.pallas_call(
        kern,
        grid=(t // rows,),
        in_specs=[pl.BlockSpec((rows, 3 * W_DN), lambda i: (i, 0)),
                  pl.BlockSpec((8, 3 * W_DN), lambda i: (jnp.maximum(i * r8 - 1, 0), 0)),
                  pl.BlockSpec((8, 3 * W_DN), lambda i: (jnp.minimum((i + 1) * r8, last8), 0)),
                  pl.BlockSpec((rows, AB_PAD), lambda i: (i, 0)),
                  pl.BlockSpec((8, 3 * W_DN), lambda i: (0, 0)),
                  pl.BlockSpec((1, AB_PAD), lambda i: (0, 0)),
                  pl.BlockSpec((1, AB_PAD), lambda i: (0, 0)),
                  pl.BlockSpec((2 * W_DN, 2 * W_DN), lambda i: (0, 0))],
        out_specs=[pl.BlockSpec((rows, 3 * W_DN), lambda i: (i, 0)),
                   pl.BlockSpec((rows, AB_PAD), lambda i: (i, 0))],
        out_shape=[jax.ShapeDtypeStruct((t, 3 * W_DN), F32),
                   jax.ShapeDtypeStruct((t, AB_PAD), F32)],
        compiler_params=_params("parallel"),
        name="dn_prep",
    )(qkv, qkv, qkv, ab, conv_w, alog_row, dtb_row, grp)


def _dn_kernel(uf_ref, ub_ref, gbf_ref, gbb_ref, e_ref, s0_ref, of_ref, ob_ref, sf_ref, s_scr,
               *, n_chunks, n_sub):
    j = pl.program_id(1)
    r = lax.broadcasted_iota(jnp.int32, (BD, BD), 0)
    c = lax.broadcasted_iota(jnp.int32, (BD, BD), 1)
    same = (r // CHUNK) == (c // CHUNK)
    dt = (r % CHUNK) - (c % CHUNK)
    same_f = jnp.where(same, 1.0, 0.0)
    same_b = same_f.astype(BF16)
    eye_f = jnp.where(r == c, 1.0, 0.0)
    r64 = lax.broadcasted_iota(jnp.int32, (CHUNK, CHUNK), 0)
    c64 = lax.broadcasted_iota(jnp.int32, (CHUNK, CHUNK), 1)
    dirs = []
    for sign, last in ((1, CHUNK - 1), (-1, 0)):
        dirs.append((jnp.where(jnp.logical_and(same, dt * sign >= 0), 1.0, 0.0),
                     jnp.where(jnp.logical_and(same, dt * sign > 0), 1.0, 0.0),
                     jnp.where((r64 - c64) * sign >= 0, 1.0, 0.0), last))

    def lift(x):
        return jnp.concatenate([x, x, x, x], axis=0) * same_f

    def lift_b(x):
        xb = x.astype(BF16)
        return jnp.concatenate([xb, xb, xb, xb], axis=0) * same_b

    @pl.when(j == 0)
    def _():
        s_scr[...] = s0_ref[...]

    def each(f, *xs):
        return [f(*a) for a in zip(*xs)]

    def mm(a, b):
        return jnp.dot(a, b, preferred_element_type=F32)

    def chunks(us, gbs, es, ss, incls, stricts, cums, lasts):
        q = [u[:, :W_DN] for u in us]
        k = [u[:, W_DN:2 * W_DN] for u in us]
        v = [u[:, 2 * W_DN:] for u in us]
        gbx = each(_dot_exact, gbs, es)
        beta = [x[:, W_DN:] for x in gbx]
        gc = each(lambda c, x: _dot_exact(c, x[:, :W_DN]), cums, gbx)
        gl = each(lambda x, last: x[last:last + 1, :], gc, lasts)
        eg = [jnp.exp(x) for x in gc]
        kb = each(jnp.multiply, k, beta)
        gcol = [lift(x) for x in gc]
        decay = each(lambda x, m: jnp.exp(jnp.where(m > 0.5, x - x.T, NEG_INF)), gcol, incls)
        qk = each(lambda a, b, kk: lax.dot_general(jnp.concatenate([lift_b(a), lift_b(b)], axis=0),
                                                   lift_b(kk), (((1,), (1,)), ((), ())),
                                                   preferred_element_type=F32), q, kb, k)
        attn = each(lambda x, d: (x[:BD] * d).astype(BF16), qk, decay)
        n_mat = each(lambda x, d, m: -(x[BD:] * d) * m, qk, decay, stricts)
        t_inv = [eye_f + x for x in n_mat]
        n_hi = [x.astype(BF16) for x in n_mat]
        m_b = n_hi
        for _ in range(int(math.log2(CHUNK)) - 1):
            m_b = [mm(x, x).astype(BF16) for x in m_b]
            t_inv = each(lambda t, m: t + mm(t.astype(BF16), m), t_inv, m_b)
        n_lo = each(lambda x, h: (x - h.astype(F32)).astype(BF16), n_mat, n_hi)
        x_hi = [t.astype(BF16) for t in t_inv]
        x_lo = each(lambda t, h: (t - h.astype(F32)).astype(BF16), t_inv, x_hi)
        nx = each(lambda h, lo, x: mm(jnp.concatenate([h, lo], axis=0), x), n_hi, n_lo, x_hi)
        nxl = each(mm, n_hi, x_lo)
        resid = each(lambda t, a, b: (eye_f - t + a[:BD] + a[BD:] + b).astype(BF16), t_inv, nx, nxl)
        t_b = each(lambda t, h, rr: (t + mm(h, rr)).astype(BF16), t_inv, x_hi, resid)
        rhs = each(lambda vv, b, kk, e: jnp.concatenate([lift_b(vv * b), lift_b(kk * e)], axis=1),
                   v, beta, kb, eg)
        sol = each(mm, t_b, rhs)
        s_b = [s.astype(BF16) for s in ss]
        ps = each(lambda so, qq, e, sb: mm(jnp.concatenate([so[:, BD:].astype(BF16), lift_b(qq * e)],
                                                           axis=0), sb), sol, q, eg, s_b)
        v_new = each(lambda so, p: (so[:, :BD] - p[:BD]).astype(BF16), sol, ps)
        o_bd = each(lambda p, a, vn: p[BD:] + mm(a, vn), ps, attn, v_new)
        o_tm = [x[0:CHUNK] + x[CHUNK:2 * CHUNK] + x[2 * CHUNK:3 * CHUNK] + x[3 * CHUNK:] for x in o_bd]
        k_tail = each(lambda kk, a, b: lift(kk * jnp.exp(a - b)).T.astype(BF16), k, gl, gc)
        s_new = each(lambda s, a, kt, vn: s * jnp.exp(a) + mm(kt, vn), ss, gl, k_tail, v_new)
        return o_tm, s_new

    chains = [(si, di) for si in range(n_sub) for di in range(2)]

    def body(ci, carry):
        offs = (pl.multiple_of(ci * CHUNK, CHUNK), pl.multiple_of((n_chunks - 1 - ci) * CHUNK, CHUNK))
        u_refs, gb_refs, o_refs = (uf_ref, ub_ref), (gbf_ref, gbb_ref), (of_ref, ob_ref)
        o_tm, s_new = chunks(
            [u_refs[di][si, pl.ds(offs[di], CHUNK), :] for si, di in chains],
            [gb_refs[di][si, pl.ds(offs[di], CHUNK), :] for si, di in chains],
            [e_ref[di] for si, di in chains],
            [s_scr[si, di] for si, di in chains],
            *[[dirs[di][n] for si, di in chains] for n in range(4)])
        for (si, di), o, s in zip(chains, o_tm, s_new):
            o_refs[di][si, pl.ds(offs[di], CHUNK), :] = o
            s_scr[si, di] = s
        return carry

    lax.fori_loop(0, n_chunks, body, 0)

    @pl.when(j == pl.num_programs(1) - 1)
    def _():
        sf_ref[...] = s_scr[...]


def _deltanet(u, gb, e_mat, s0_bd, seq_len, rows, n_sub=2):
    t = u.shape[0]
    n_seq = t // seq_len
    nb = seq_len // rows
    kern = functools.partial(_dn_kernel, n_chunks=rows // CHUNK, n_sub=n_sub)
    fwd = lambda s, j: (s, j, 0)
    bwd = lambda s, j: (s, nb - 1 - j, 0)
    state = pl.BlockSpec((n_sub, 2, BD, BD), lambda s, j: (s, 0, 0, 0))
    o_f, o_b, s_fin = pl.pallas_call(
        kern,
        grid=(n_seq // n_sub, nb),
        in_specs=[pl.BlockSpec((n_sub, rows, 3 * W_DN), fwd), pl.BlockSpec((n_sub, rows, 3 * W_DN), bwd),
                  pl.BlockSpec((n_sub, rows, AB_PAD), fwd), pl.BlockSpec((n_sub, rows, AB_PAD), bwd),
                  pl.BlockSpec((2, AB_PAD, 2 * W_DN), lambda s, j: (0, 0, 0)), state],
        out_specs=[pl.BlockSpec((n_sub, rows, W_DN), fwd), pl.BlockSpec((n_sub, rows, W_DN), bwd), state],
        out_shape=[jax.ShapeDtypeStruct((n_seq, seq_len, W_DN), F32),
                   jax.ShapeDtypeStruct((n_seq, seq_len, W_DN), F32),
                   jax.ShapeDtypeStruct((n_seq, 2, BD, BD), F32)],
        scratch_shapes=[pltpu.VMEM((n_sub, 2, BD, BD), F32)],
        compiler_params=_params("parallel", "arbitrary"),
        name="deltanet",
    )(*[x.reshape(n_seq, seq_len, -1) for x in (u, u, gb, gb)], e_mat, s0_bd)
    return o_f.reshape(t, W_DN), o_b.reshape(t, W_DN), s_fin


def _diff_lambda(lam_ref, lam_init):
    lp = lam_ref[...]
    return (jnp.exp(jnp.sum(lp[0:1] * lp[1:2], axis=1, keepdims=True))
            - jnp.exp(jnp.sum(lp[2:3] * lp[3:4], axis=1, keepdims=True)) + lam_init)


def _sub_rms(o, subln, lam_init):
    ms = jnp.mean(o * o, axis=-1, keepdims=True)
    return o * lax.rsqrt(ms + RMS_EPS) * subln * (1.0 - lam_init)


def _ctx_attn_kernel(na_ref, df_ref, lam_ref, subln_ref, oa_ref, oc_ref, *, lam_init):
    na = na_ref[...]
    df = df_ref[...]
    lam = _diff_lambda(lam_ref, lam_init)
    subln = subln_ref[...]
    for h in range(H_NA):
        a = h * HEAD_DIM
        q, k, v = (na[:, a + o:a + o + HEAD_DIM] for o in (0, W_NA, 2 * W_NA))
        s = _dot_nt(q, k) * HEAD_DIM ** -0.5
        p = jnp.exp(s - jnp.max(s, axis=-1, keepdims=True))
        oa_ref[:, a:a + HEAD_DIM] = _dot(p, v) / jnp.sum(p, axis=-1, keepdims=True)
    for h in range(H_DF):
        a = h * HEAD_DIM
        ps = []
        for i in range(2):
            b = a + i * DF_QK
            s = _dot_nt(df[:, b:b + DF_QK], df[:, W_DF + b:W_DF + b + DF_QK]) * DF_QK ** -0.5
            p = jnp.exp(s - jnp.max(s, axis=-1, keepdims=True))
            ps.append(p / jnp.sum(p, axis=-1, keepdims=True))
        o = _dot(ps[0] - lam * ps[1], df[:, 2 * W_DF + a:2 * W_DF + a + HEAD_DIM])
        oc_ref[:, a:a + HEAD_DIM] = _sub_rms(o, subln, lam_init)


def _ctx_attn(qkv_na, qkv_df, lam_p, subln_row, layer):
    lam_init = 0.8 - 0.6 * math.exp(-0.3 * layer)
    kern = functools.partial(_ctx_attn_kernel, lam_init=lam_init)
    return pl.pallas_call(
        kern,
        grid=(BATCH,),
        in_specs=[pl.BlockSpec((SEQ, 3 * W_NA), lambda b: (b, 0)),
                  pl.BlockSpec((SEQ, 3 * W_DF), lambda b: (b, 0)),
                  pl.BlockSpec((4, DF_QK), lambda b: (0, 0)),
                  pl.BlockSpec((1, HEAD_DIM), lambda b: (0, 0))],
        out_specs=[pl.BlockSpec((SEQ, W_NA), lambda b: (b, 0)),
                   pl.BlockSpec((SEQ, W_DF), lambda b: (b, 0))],
        out_shape=[jax.ShapeDtypeStruct((T_CTX, W_NA), F32),
                   jax.ShapeDtypeStruct((T_CTX, W_DF), F32)],
        compiler_params=_params("parallel"),
        name="ctx_attn",
    )(qkv_na, qkv_df, lam_p, subln_row)


def _na_bias_table(rpb):
    qc = np.arange(GRID_W)[:, None]
    kc = np.arange(GRID_W)[None, :]
    cs = np.clip(qc - WIN_C // 2, 0, GRID_W - WIN_C)
    valid = (kc >= cs) & (kc < cs + WIN_C)
    dc = np.clip(kc - qc + (WIN_C - 1), 0, 2 * WIN_C - 2)
    onehot = (dc[None] == np.arange(2 * WIN_C - 1)[:, None, None]).astype(np.float32)
    x = jnp.einsum('hrd,dqk->hrqk', rpb.astype(F32), jnp.asarray(onehot), precision=HIGHEST)
    x = jnp.where(valid[None, None], x, NEG_INF)
    tab = jnp.stack([x[:, d:d + WIN_R] for d in range(WIN_R)], axis=1)
    return jnp.transpose(tab, (0, 1, 3, 2, 4)).reshape(H_NA, WIN_R, GRID_W, WIN_R * GRID_W)


def _row_window_start(r):
    return jnp.clip(r - WIN_R // 2, 0, GRID_H - WIN_R)


def _lat_na_kernel(q_ref, k_ref, v_ref, kc_ref, vc_ref, bias_ref, o_ref):
    r = pl.program_id(1)
    start = pl.multiple_of(_row_window_start(r) * GRID_W, GRID_W)
    q = q_ref[...]
    kw = k_ref[pl.ds(start, WIN_R * GRID_W), :]
    vw = v_ref[pl.ds(start, WIN_R * GRID_W), :]
    kc = kc_ref[...]
    vc = vc_ref[...]
    scale = HEAD_DIM ** -0.5
    for h in range(H_NA):
        sl = slice(h * HEAD_DIM, (h + 1) * HEAD_DIM)
        qh = q[:, sl].astype(BF16)
        s_loc = _dot_nt(qh, kw[:, sl]) * scale + bias_ref[h]
        s_ctx = _dot_nt(qh, kc[:, sl]) * scale
        m = jnp.maximum(jnp.max(s_loc, axis=-1, keepdims=True), jnp.max(s_ctx, axis=-1, keepdims=True))
        p_loc = jnp.exp(s_loc - m)
        p_ctx = jnp.exp(s_ctx - m)
        den = jnp.sum(p_loc, axis=-1, keepdims=True) + jnp.sum(p_ctx, axis=-1, keepdims=True)
        o_ref[:, sl] = (_dot(p_loc, vw[:, sl]) + _dot(p_ctx, vc[:, sl])) / den


def _lat_na(qkv_na, ck, cv, bias_tab):
    rows_per_b = DEC_SEQ // GRID_W
    return pl.pallas_call(
        _lat_na_kernel,
        grid=(DEC_BATCH, rows_per_b),
        in_specs=[pl.BlockSpec((GRID_W, W_NA), lambda b, r: (b * rows_per_b + r, 0)),
                  pl.BlockSpec((DEC_SEQ, W_NA), lambda b, r: (b, 1)),
                  pl.BlockSpec((DEC_SEQ, W_NA), lambda b, r: (b, 2)),
                  pl.BlockSpec((None, PAST_LEN, W_NA), lambda b, r: (b, 0, 0)),
                  pl.BlockSpec((None, PAST_LEN, W_NA), lambda b, r: (b, 0, 0)),
                  pl.BlockSpec((H_NA, None, GRID_W, WIN_R * GRID_W),
                               lambda b, r: (0, _row_window_start(r) - r + WIN_R - 1, 0, 0))],
        out_specs=pl.BlockSpec((GRID_W, W_NA), lambda b, r: (b * rows_per_b + r, 0)),
        out_shape=jax.ShapeDtypeStruct((T_LAT, W_NA), F32),
        compiler_params=_params("parallel", "arbitrary"),
        name="lat_na",
    )(qkv_na, qkv_na, qkv_na, ck, cv, bias_tab)


def _rope_tables():
    nf = DF_QK // 4
    inv = ROPE_BASE ** (-np.arange(nf, dtype=np.float32) / nf)
    t = np.arange(DEC_SEQ)
    pos = np.stack([t // GRID_W, t % GRID_W], axis=-1).astype(np.float32)
    ang = jnp.asarray(pos[:, :, None] * inv)
    cos, sin = jnp.cos(ang), jnp.sin(ang)
    cos32 = jnp.concatenate([cos, cos], axis=-1).reshape(DEC_SEQ, DF_QK)
    sin32 = jnp.concatenate([-sin, sin], axis=-1).reshape(DEC_SEQ, DF_QK)
    reps = W_DF // DF_QK
    rot = np.zeros((W_DF, W_DF), np.float32)
    for dd in range(W_DF):
        rot[dd + nf if dd % (2 * nf) < nf else dd - nf, dd] = 1.0
    return jnp.tile(cos32, (1, reps)), jnp.tile(sin32, (1, reps)), jnp.asarray(rot)


def _rope_kernel(df_ref, cos_ref, sin_ref, rot_ref, q_ref, kt_ref, v_ref):
    cos, sin, rot = cos_ref[...], sin_ref[...], rot_ref[...]
    q = df_ref[:, :W_DF]
    k = df_ref[:, W_DF:2 * W_DF]
    q_ref[...] = (q * cos + _dot_exact(q, rot) * sin).astype(BF16)
    kt_ref[...] = (k * cos + _dot_exact(k, rot) * sin).T.astype(BF16)
    v_ref[...] = df_ref[:, 2 * W_DF:].astype(BF16)


def _rope(qkv_df, cos, sin, rot, tm=512):
    nb = DEC_SEQ // tm
    return pl.pallas_call(
        _rope_kernel,
        grid=(DEC_BATCH, nb),
        in_specs=[pl.BlockSpec((tm, 3 * W_DF), lambda b, i: (b * nb + i, 0)),
                  pl.BlockSpec((tm, W_DF), lambda b, i: (i, 0)),
                  pl.BlockSpec((tm, W_DF), lambda b, i: (i, 0)),
                  pl.BlockSpec((W_DF, W_DF), lambda b, i: (0, 0))],
        out_specs=[pl.BlockSpec((tm, W_DF), lambda b, i: (b * nb + i, 0)),
                   pl.BlockSpec((None, W_DF, tm), lambda b, i: (b, 0, i)),
                   pl.BlockSpec((tm, W_DF), lambda b, i: (b * nb + i, 0))],
        out_shape=[jax.ShapeDtypeStruct((T_LAT, W_DF), BF16),
                   jax.ShapeDtypeStruct((DEC_BATCH, W_DF, DEC_SEQ), BF16),
                   jax.ShapeDtypeStruct((T_LAT, W_DF), BF16)],
        compiler_params=_params("parallel", "parallel"),
        name="rope",
    )(qkv_df, cos, sin, rot)


def _lat_df_kernel(q_ref, kt_ref, ktc_ref, v_ref, vc_ref, lam_ref, subln_ref, o_ref, *, lam_init):
    q = q_ref[...]
    lam = _diff_lambda(lam_ref, lam_init)
    subln = subln_ref[...]
    scale = DF_QK ** -0.5
    for h in range(H_DF):
        sl = slice(h * HEAD_DIM, (h + 1) * HEAD_DIM)
        ps = []
        for i in range(2):
            a = h * HEAD_DIM + i * DF_QK
            qi = q[:, a:a + DF_QK]
            s_l = jnp.dot(qi, kt_ref[a:a + DF_QK, :], preferred_element_type=F32) * scale
            s_c = jnp.dot(qi, ktc_ref[a:a + DF_QK, :], preferred_element_type=F32) * scale
            m = jnp.maximum(jnp.max(s_l, axis=-1, keepdims=True), jnp.max(s_c, axis=-1, keepdims=True))
            e_l = jnp.exp(s_l - m)
            e_c = jnp.exp(s_c - m)
            inv = 1.0 / (jnp.sum(e_l, axis=-1, keepdims=True) + jnp.sum(e_c, axis=-1, keepdims=True))
            ps.append((e_l * inv, e_c * inv))
        a_l = (ps[0][0] - lam * ps[1][0]).astype(BF16)
        a_c = (ps[0][1] - lam * ps[1][1]).astype(BF16)
        o = (jnp.dot(a_l, v_ref[:, sl], preferred_element_type=F32)
             + jnp.dot(a_c, vc_ref[:, sl], preferred_element_type=F32))
        o_ref[:, sl] = _sub_rms(o, subln, lam_init)


def _lat_df(q_r, kt, ktc, v, vc, lam_p, subln_row, layer, tq=128):
    lam_init = 0.8 - 0.6 * math.exp(-0.3 * layer)
    nb = DEC_SEQ // tq
    kern = functools.partial(_lat_df_kernel, lam_init=lam_init)
    return pl.pallas_call(
        kern,
        grid=(DEC_BATCH, nb),
        in_specs=[pl.BlockSpec((tq, W_DF), lambda b, i: (b * nb + i, 0)),
                  pl.BlockSpec((None, W_DF, DEC_SEQ), lambda b, i: (b, 0, 0)),
                  pl.BlockSpec((None, W_DF, PAST_LEN), lambda b, i: (b, 0, 0)),
                  pl.BlockSpec((DEC_SEQ, W_DF), lambda b, i: (b, 0)),
                  pl.BlockSpec((None, PAST_LEN, W_DF), lambda b, i: (b, 0, 0)),
                  pl.BlockSpec((4, DF_QK), lambda b, i: (0, 0)),
                  pl.BlockSpec((1, HEAD_DIM), lambda b, i: (0, 0))],
        out_specs=pl.BlockSpec((tq, W_DF), lambda b, i: (b * nb + i, 0)),
        out_shape=jax.ShapeDtypeStruct((T_LAT, W_DF), F32),
        compiler_params=_params("parallel", "arbitrary"),
        name="lat_df",
    )(q_r, kt, ktc, v, vc, lam_p, subln_row)


def _outproj_kernel(oa_ref, of_ref, obk_ref, gate_ref, onorm_ref, grp_ref, oc_ref, y_ref, g1_ref, w_ref,
                    lg_ref, lb_ref, o_ref):
    ob = of_ref[...] + obk_ref[...]
    ms = _dot_exact(ob * ob, grp_ref[...]) * (1.0 / HEAD_DIM)
    ob = ob * lax.rsqrt(ms + RMS_EPS) * onorm_ref[...] * _silu(gate_ref[...])
    o = (jnp.dot(oa_ref[...].astype(BF16), w_ref[:W_NA, :], preferred_element_type=F32)
         + jnp.dot(ob.astype(BF16), w_ref[W_NA:W_NA + W_DN, :], preferred_element_type=F32)
         + jnp.dot(oc_ref[...].astype(BF16), w_ref[W_NA + W_DN:, :], preferred_element_type=F32))
    o_ref[...] = _layer_norm(ALPHA * y_ref[...] + g1_ref[...] * o, lg_ref[...], lb_ref[...])


def _outproj(oa, o_fwd, o_bwd, gate, onorm_row, grp, oc, y, mod, w, ln_g, ln_b, cond_fn, tm=512):
    t = y.shape[0]
    row = lambda n: pl.BlockSpec((tm, n), lambda i: (i, 0))
    vec = pl.BlockSpec((1, D_MODEL), lambda i: (0, 0))
    return pl.pallas_call(
        _outproj_kernel,
        grid=(t // tm,),
        in_specs=[row(W_NA), row(W_DN), row(W_DN),
                  row(W_DN), pl.BlockSpec((1, W_DN), lambda i: (0, 0)),
                  pl.BlockSpec((W_DN, W_DN), lambda i: (0, 0)),
                  row(W_DF), row(D_MODEL), _mod_spec(2, cond_fn),
                  pl.BlockSpec((D_MODEL, D_MODEL), lambda i: (0, 0)), vec, vec],
        out_specs=row(D_MODEL),
        out_shape=jax.ShapeDtypeStruct((t, D_MODEL), F32),
        compiler_params=_params("parallel"),
        name="outproj_ln",
    )(oa, o_fwd, o_bwd, gate, onorm_row, grp, oc, y, mod, w, ln_g, ln_b)


def _router_kernel(y_ref, sh_ref, sc_ref, w_ref, b_ref, g_ref):
    h = y_ref[...] * (1.0 + sc_ref[...]) + sh_ref[...]
    logits = _dot_exact(h, w_ref[...]) + b_ref[...]
    lane = lax.broadcasted_iota(jnp.int32, logits.shape, 1).astype(F32)
    logits = jnp.where(lane < N_EXPERTS, logits, -jnp.inf)
    m1 = jnp.max(logits, axis=-1, keepdims=True)
    i1 = jnp.min(jnp.where(logits == m1, lane, float(LANES)), axis=-1, keepdims=True)
    rest = jnp.where(lane == i1, -jnp.inf, logits)
    m2 = jnp.max(rest, axis=-1, keepdims=True)
    i2 = jnp.min(jnp.where(rest == m2, lane, float(LANES)), axis=-1, keepdims=True)
    e2 = jnp.exp(m2 - m1)
    w1 = 1.0 / (1.0 + e2)
    g_ref[...] = jnp.where(lane == i1, w1, 0.0) + jnp.where(lane == i2, e2 * w1, 0.0)


def _router(y, mod, w_pad, b_pad, cond_fn, tm=512):
    t = y.shape[0]
    return pl.pallas_call(
        _router_kernel,
        grid=(t // tm,),
        in_specs=[pl.BlockSpec((tm, D_MODEL), lambda i: (i, 0)),
                  _mod_spec(3, cond_fn), _mod_spec(4, cond_fn),
                  pl.BlockSpec((D_MODEL, LANES), lambda i: (0, 0)),
                  pl.BlockSpec((1, LANES), lambda i: (0, 0))],
        out_specs=pl.BlockSpec((tm, LANES), lambda i: (i, 0)),
        out_shape=jax.ShapeDtypeStruct((t, LANES), F32),
        compiler_params=_params("parallel"),
        name="router",
    )(y, mod, mod, w_pad, b_pad)


def _ffn_kernel(y_ref, sh_ref, sc_ref, g2_ref, gates_ref, wg_ref, wu_ref, wd_ref, lg_ref, lb_ref,
                o_ref, h_scr, acc_scr, *, n_blocks):
    e = pl.program_id(1)

    @pl.when(e == 0)
    def _():
        h_scr[...] = (y_ref[...] * (1.0 + sc_ref[...]) + sh_ref[...]).astype(BF16)
        acc_scr[...] = jnp.zeros_like(acc_scr)

    h = h_scr[...]
    a = jnp.dot(h, wg_ref[...], preferred_element_type=F32)
    b = jnp.dot(h, wu_ref[...], preferred_element_type=F32)
    f = jnp.dot((_silu(a) * b).astype(BF16), wd_ref[...], preferred_element_type=F32)
    lane = lax.broadcasted_iota(jnp.int32, gates_ref.shape, 1)
    gate = jnp.sum(jnp.where(lane == e, gates_ref[...], 0.0), axis=-1, keepdims=True)
    acc_scr[...] += gate * f

    @pl.when(e == n_blocks - 1)
    def _():
        o_ref[...] = _layer_norm(ALPHA * y_ref[...] + g2_ref[...] * acc_scr[...],
                                 lg_ref[...], lb_ref[...])


def _ffn(y, mod, gates, wg, wu, wd, ln_g, ln_b, cond_fn, tm=512):
    t = y.shape[0]
    n_blocks, _, ff = wg.shape
    vec = pl.BlockSpec((1, D_MODEL), lambda i, e: (0, 0))
    kern = functools.partial(_ffn_kernel, n_blocks=n_blocks)
    return pl.pallas_call(
        kern,
        grid=(t // tm, n_blocks),
        in_specs=[pl.BlockSpec((tm, D_MODEL), lambda i, e: (i, 0)),
                  _mod_spec(3, cond_fn), _mod_spec(4, cond_fn), _mod_spec(5, cond_fn),
                  pl.BlockSpec((tm, LANES), lambda i, e: (i, 0)),
                  pl.BlockSpec((None, D_MODEL, ff), lambda i, e: (e, 0, 0)),
                  pl.BlockSpec((None, D_MODEL, ff), lambda i, e: (e, 0, 0)),
                  pl.BlockSpec((None, ff, D_MODEL), lambda i, e: (e, 0, 0)),
                  vec, vec],
        out_specs=pl.BlockSpec((tm, D_MODEL), lambda i, e: (i, 0)),
        out_shape=jax.ShapeDtypeStruct((t, D_MODEL), F32),
        scratch_shapes=[pltpu.VMEM((tm, D_MODEL), BF16), pltpu.VMEM((tm, D_MODEL), F32)],
        compiler_params=_params("parallel", "arbitrary"),
        name="ffn_ln",
    )(y, mod, mod, mod, gates, wg, wu, wd, ln_g, ln_b)


def _permute_w_in(w):
    offs = np.cumsum((0,) + PROJ_SIZES)
    qa, ka, va, qkv, gate, a, b, qc, kc, vc = (w[:, offs[i]:offs[i + 1]] for i in range(10))
    pad = jnp.zeros((D_MODEL, AB_PAD - 4 * H_DN), w.dtype)
    return jnp.concatenate([qa, ka, va, qkv, gate, qc, kc, vc, a, b, pad], axis=1).astype(BF16)


def _pad_row(v, n=AB_PAD):
    v = v.reshape(1, -1)
    return jnp.pad(v, ((0, 0), (0, n - v.shape[1])))


def _group_matrix(n):
    idx = np.arange(n) // HEAD_DIM
    return jnp.asarray((idx[:, None] == idx[None, :]).astype(np.float32))


def _expand_matrix():
    e = np.zeros((2, AB_PAD, 2 * W_DN), np.float32)
    for d in range(2):
        for h in range(H_DN):
            e[d, d * H_DN + h, h * HEAD_DIM:(h + 1) * HEAD_DIM] = 1.0
            e[d, 2 * H_DN + d * H_DN + h, W_DN + h * HEAD_DIM:W_DN + (h + 1) * HEAD_DIM] = 1.0
    return jnp.asarray(e)


def _state_to_bd(s):
    eye = jnp.eye(H_DN, dtype=s.dtype)
    return jnp.einsum('sdhkv,hg->sdhkgv', s, eye).reshape(s.shape[0], 2, BD, BD)


def _bd_to_state(s):
    n = s.shape[0]
    return jnp.einsum('sdhkhv->sdhkv', s.reshape(n, 2, H_DN, HEAD_DIM, H_DN, HEAD_DIM))


def kernel(x_prompt, x_sample, cache_na_k, cache_na_v, cache_df_k, cache_df_v, state_dn, c, c_ctx,
           ada_w, ada_b, w_in, conv_dn, a_log_dn, dt_bias_dn, onorm_dn, rpb_na, lambda_df, subln_df,
           w_out, ln1_g, ln1_b, ln2_g, ln2_b, ffn_w_gate, ffn_w_up, ffn_w_down, router_w, router_b,
           moe_w_gate, moe_w_up, moe_w_down):
    conds = jnp.concatenate([c_ctx[None, :], c, jnp.zeros((N_COND - 1 - DEC_BATCH, D_MODEL), F32)], axis=0)
    mods = _ada_table(conds, ada_w, ada_b)
    ctx_cond = lambda i: 0
    lat_cond = lambda i: 1 + (i * 512) // DEC_SEQ

    grp512, grp256 = _group_matrix(2 * W_DN), _group_matrix(W_DN)
    e_mat = _expand_matrix()
    cos, sin, rot = _rope_tables()
    ones_gates = jnp.ones((T_CTX, LANES), F32)

    y_ctx = x_prompt.reshape(T_CTX, D_MODEL)
    y_lat = x_sample.reshape(T_LAT, D_MODEL)
    ctx_out = []
    for l in range(DEPTH):
        mod = mods[l]
        w_in_l = _permute_w_in(w_in[l])
        w_out_l = w_out[l].astype(BF16)
        conv_w = jnp.pad(conv_dn[l], ((0, 8 - CONV_K), (0, 0)))
        alog_row = _pad_row(a_log_dn[l])
        dtb_row = _pad_row(dt_bias_dn[l])
        onorm_row = jnp.tile(onorm_dn[l], H_DN).reshape(1, W_DN)
        subln_row = subln_df[l].reshape(1, HEAD_DIM)
        lg1, lb1 = ln1_g[l].reshape(1, D_MODEL), ln1_b[l].reshape(1, D_MODEL)
        lg2, lb2 = ln2_g[l].reshape(1, D_MODEL), ln2_b[l].reshape(1, D_MODEL)

        na, dn, gate, df, ab = _inproj(y_ctx, mod, w_in_l, ctx_cond)
        u, gb = _dn_prep(dn, ab, conv_w, alog_row, dtb_row, grp512, SEQ)
        s0 = jnp.zeros((BATCH, 2, BD, BD), F32)
        o_f, o_b, s_fin = _deltanet(u, gb, e_mat, s0, SEQ, SEQ)
        oa, oc = _ctx_attn(na, df, lambda_df[l], subln_row, l)
        y1_ctx = _outproj(oa, o_f, o_b, gate, onorm_row, grp256, oc, y_ctx, mod, w_out_l, lg1, lb1, ctx_cond)
        ctx_out.append((na[:, W_NA:2 * W_NA].reshape(BATCH, SEQ, H_NA, HEAD_DIM),
                        na[:, 2 * W_NA:].reshape(BATCH, SEQ, H_NA, HEAD_DIM),
                        df[:, W_DF:2 * W_DF].reshape(BATCH, SEQ, H_DF, HEAD_DIM),
                        df[:, 2 * W_DF:].reshape(BATCH, SEQ, H_DF, HEAD_DIM),
                        _bd_to_state(s_fin)))

        na, dn, gate, df, ab = _inproj(y_lat, mod, w_in_l, lat_cond)
        u, gb = _dn_prep(dn, ab, conv_w, alog_row, dtb_row, grp512, DEC_SEQ)
        o_f, o_b, _ = _deltanet(u, gb, e_mat, _state_to_bd(state_dn[:, l]), DEC_SEQ, 512)
        oa = _lat_na(na, cache_na_k[:, l].reshape(DEC_BATCH, PAST_LEN, W_NA),
                     cache_na_v[:, l].reshape(DEC_BATCH, PAST_LEN, W_NA), _na_bias_table(rpb_na[l]))
        q_r, kt, v_b = _rope(df, cos, sin, rot)
        ktc = jnp.swapaxes(cache_df_k[:, l].reshape(DEC_BATCH, PAST_LEN, W_DF), 1, 2).astype(BF16)
        vc = cache_df_v[:, l].reshape(DEC_BATCH, PAST_LEN, W_DF).astype(BF16)
        oc = _lat_df(q_r, kt, ktc, v_b, vc, lambda_df[l], subln_row, l)
        y1_lat = _outproj(oa, o_f, o_b, gate, onorm_row, grp256, oc, y_lat, mod, w_out_l, lg1, lb1, lat_cond)

        i = l // 2
        if l % 2 == 0:
            split = lambda w: jnp.transpose(w.reshape(D_MODEL, D_FF // MOE_FF, MOE_FF), (1, 0, 2))
            wg, wu = split(ffn_w_gate[i]).astype(BF16), split(ffn_w_up[i]).astype(BF16)
            wd = ffn_w_down[i].reshape(D_FF // MOE_FF, MOE_FF, D_MODEL).astype(BF16)
            g_ctx = g_lat = ones_gates
        else:
            wg, wu, wd = moe_w_gate[i].astype(BF16), moe_w_up[i].astype(BF16), moe_w_down[i].astype(BF16)
            rw = jnp.pad(router_w[i], ((0, 0), (0, LANES - N_EXPERTS)))
            rb = _pad_row(router_b[i], LANES)
            g_ctx = _router(y1_ctx, mod, rw, rb, ctx_cond)
            g_lat = _router(y1_lat, mod, rw, rb, lat_cond)
        y_ctx = _ffn(y1_ctx, mod, g_ctx, wg, wu, wd, lg2, lb2, ctx_cond)
        y_lat = _ffn(y1_lat, mod, g_lat, wg, wu, wd, lg2, lb2, lat_cond)

    stack = lambda j: jnp.stack([t[j] for t in ctx_out], axis=1)
    return (y_ctx.reshape(BATCH, SEQ, D_MODEL), y_lat.reshape(DEC_BATCH, DEC_SEQ, D_MODEL),
            stack(0), stack(1), stack(2), stack(3), stack(4))
```

```python
import functools
import math

import jax
import jax.numpy as jnp
import numpy as np
from jax import lax
from jax.experimental import pallas as pl
from jax.experimental.pallas import tpu as pltpu

F32 = jnp.float32
BF16 = jnp.bfloat16
HIGHEST = lax.Precision.HIGHEST

D_MODEL = 1024
BATCH = 32
SEQ = 256
DEPTH = 2
DEC_BATCH = 2
DEC_SEQ = 4096
PAST_LEN = 512
GRID_W = 64
GRID_H = DEC_SEQ // GRID_W
HEAD_DIM = 64
H_NA = 6
H_DN = 4
H_DF = 6
W_NA = H_NA * HEAD_DIM
W_DN = H_DN * HEAD_DIM
W_DF = H_DF * HEAD_DIM
DF_QK = HEAD_DIM // 2
WIN_R = 8
WIN_C = 16
CONV_K = 5
CHUNK = 64
D_FF = 2816
N_EXPERTS = 8
MOE_FF = 1408
ALPHA = (2 * DEPTH) ** 0.25
LN_EPS = 1e-5
RMS_EPS = 1e-6
ROPE_BASE = 10000.0
NEG_INF = -1e30
PROJ_SIZES = (W_NA, W_NA, W_NA, 3 * W_DN, W_DN, 2 * H_DN, 2 * H_DN, W_DF, W_DF, W_DF)

T_CTX = BATCH * SEQ
T_LAT = DEC_BATCH * DEC_SEQ
N_COND = 8
LANES = 128
AB_PAD = LANES
SEG_NA = (0, 3 * W_NA)
SEG_DN = (SEG_NA[1], SEG_NA[1] + 3 * W_DN)
SEG_GATE = (SEG_DN[1], SEG_DN[1] + W_DN)
SEG_DF = (SEG_GATE[1], SEG_GATE[1] + 3 * W_DF)
SEG_AB = (SEG_DF[1], SEG_DF[1] + AB_PAD)
P_PAD = SEG_AB[1]
SEGS = (SEG_NA, SEG_DN, SEG_GATE, SEG_DF, SEG_AB)
BD = H_DN * CHUNK
V_AUG = LANES
DF_Q_SCALE = DF_QK ** -0.5 * math.log2(math.e)
VMEM_LIMIT = 56 * 1024 * 1024


def _params(*sem):
    return pltpu.CompilerParams(dimension_semantics=sem, vmem_limit_bytes=VMEM_LIMIT)


def _dot(a, b):
    return jnp.dot(a.astype(BF16), b.astype(BF16), preferred_element_type=F32)


def _dot_nt(a, b):
    return lax.dot_general(a.astype(BF16), b.astype(BF16), (((1,), (1,)), ((), ())),
                           preferred_element_type=F32)


def _dot_exact(a, b):
    return jnp.dot(a, b, precision=HIGHEST, preferred_element_type=F32)


def _silu(x):
    return x * jax.nn.sigmoid(x)


def _layer_norm(x, g, b):
    mu = jnp.mean(x, axis=-1, keepdims=True)
    xc = x - mu
    var = jnp.mean(xc * xc, axis=-1, keepdims=True)
    return xc * lax.rsqrt(var + LN_EPS) * g + b


def _ada_kernel(c_ref, w_ref, b_ref, o_ref):
    o_ref[...] = _dot_exact(_silu(c_ref[...]), w_ref[...]) + b_ref[...]


def _ada_table(conds, ada_w, ada_b):
    out = pl.pallas_call(
        _ada_kernel,
        grid=(DEPTH, 6),
        in_specs=[pl.BlockSpec((N_COND, D_MODEL), lambda l, k: (0, 0)),
                  pl.BlockSpec((None, D_MODEL, D_MODEL), lambda l, k: (l, 0, k)),
                  pl.BlockSpec((None, None, 1, D_MODEL), lambda l, k: (l, k, 0, 0))],
        out_specs=pl.BlockSpec((None, None, N_COND, D_MODEL), lambda l, k: (l, k, 0, 0)),
        out_shape=jax.ShapeDtypeStruct((DEPTH, 6, N_COND, D_MODEL), F32),
        compiler_params=_params("parallel", "parallel"),
        name="ada_table",
    )(conds, ada_w, ada_b.reshape(DEPTH, 6, 1, D_MODEL))
    return out.reshape(DEPTH, 6, N_COND, 1, D_MODEL)


def _mod_spec(k, cond_fn):
    return pl.BlockSpec((None, None, 1, D_MODEL), lambda i, *_: (k, cond_fn(i), 0, 0))


def _inproj_kernel(x_ref, sh_ref, sc_ref, w_ref, *o_refs):
    h = (x_ref[...] * (1.0 + sc_ref[...]) + sh_ref[...]).astype(BF16)
    for o_ref, (a, b) in zip(o_refs, SEGS):
        o_ref[...] = jnp.dot(h, w_ref[:, a:b], preferred_element_type=F32)


def _inproj(x, mod, w, cond_fn, tm=512):
    t = x.shape[0]
    return pl.pallas_call(
        _inproj_kernel,
        grid=(t // tm,),
        in_specs=[pl.BlockSpec((tm, D_MODEL), lambda i: (i, 0)),
                  _mod_spec(0, cond_fn), _mod_spec(1, cond_fn),
                  pl.BlockSpec((D_MODEL, P_PAD), lambda i: (0, 0))],
        out_specs=[pl.BlockSpec((tm, b - a), lambda i: (i, 0)) for a, b in SEGS],
        out_shape=[jax.ShapeDtypeStruct((t, b - a), F32) for a, b in SEGS],
        compiler_params=_params("parallel"),
        name="inproj",
    )(x, mod, mod, w)


def _dn_prep_kernel(x_ref, prev_ref, next_ref, ab_ref, w_ref, alog_ref, dtb_ref, grp_ref,
                    u_ref, gb_ref, *, blocks_per_seq, rows):
    i = pl.program_id(0)
    j = i % blocks_per_seq
    prev = jnp.where(j != 0, prev_ref[...], 0.0)
    nxt = jnp.where(j != blocks_per_seq - 1, next_ref[...], 0.0)
    xe = jnp.concatenate([prev, x_ref[...], nxt], axis=0)
    w = w_ref[...]
    base = 8 - CONV_K // 2
    acc = w[0:1, :] * xe[base:base + rows, :]
    for t in range(1, CONV_K):
        acc = acc + w[t:t + 1, :] * xe[base + t:base + t + rows, :]
    u = _silu(acc)
    qk = u[:, :2 * W_DN]
    ss = _dot_exact(qk * qk, grp_ref[...])
    qk = qk * lax.rsqrt(ss + RMS_EPS)
    u_ref[:, :W_DN] = qk[:, :W_DN] * HEAD_DIM ** -0.5
    u_ref[:, W_DN:2 * W_DN] = qk[:, W_DN:]
    u_ref[:, 2 * W_DN:] = u[:, 2 * W_DN:]
    ab = ab_ref[...]
    z = ab + dtb_ref[...]
    softplus = jnp.maximum(z, 0.0) + jnp.log(1.0 + jnp.exp(-jnp.abs(z)))
    g = -jnp.exp(alog_ref[...]) * softplus
    lane = lax.broadcasted_iota(jnp.int32, ab.shape, 1)
    gb_ref[...] = jnp.where(lane < 2 * H_DN, g, jax.nn.sigmoid(ab))


def _dn_prep(qkv, ab, conv_w, alog_row, dtb_row, grp, seq_len, rows=256):
    t = qkv.shape[0]
    bps = seq_len // rows
    r8 = rows // 8
    last8 = t // 8 - 1
    kern = functools.partial(_dn_prep_kernel, blocks_per_seq=bps, rows=rows)
    return pl.pallas_call(
        kern,
        grid=(t // rows,),
        in_specs=[pl.BlockSpec((rows, 3 * W_DN), lambda i: (i, 0)),
                  pl.BlockSpec((8, 3 * W_DN), lambda i: (jnp.maximum(i * r8 - 1, 0), 0)),
                  pl.BlockSpec((8, 3 * W_DN), lambda i: (jnp.minimum((i + 1) * r8, last8), 0)),
                  pl.BlockSpec((rows, AB_PAD), lambda i: (i, 0)),
                  pl.BlockSpec((8, 3 * W_DN), lambda i: (0, 0)),
                  pl.BlockSpec((1, AB_PAD), lambda i: (0, 0)),
                  pl.BlockSpec((1, AB_PAD), lambda i: (0, 0)),
                  pl.BlockSpec((2 * W_DN, 2 * W_DN), lambda i: (0, 0))],
        out_specs=[pl.BlockSpec((rows, 3 * W_DN), lambda i: (i, 0)),
                   pl.BlockSpec((rows, AB_PAD), lambda i: (i, 0))],
        out_shape=[jax.ShapeDtypeStruct((t, 3 * W_DN), F32),
                   jax.ShapeDtypeStruct((t, AB_PAD), F32)],
        compiler_params=_params("parallel"),
        name="dn_prep",
    )(qkv, qkv, qkv, ab, conv_w, alog_row, dtb_row, grp)


def _dn_kernel(uf_ref, ub_ref, gbf_ref, gbb_ref, e_ref, s0_ref, of_ref, ob_ref, sf_ref, s_scr,
               *, n_chunks, n_sub):
    j = pl.program_id(1)
    r = lax.broadcasted_iota(jnp.int32, (BD, BD), 0)
    c = lax.broadcasted_iota(jnp.int32, (BD, BD), 1)
    same = (r // CHUNK) == (c // CHUNK)
    dt = (r % CHUNK) - (c % CHUNK)
    same_f = jnp.where(same, 1.0, 0.0)
    same_b = same_f.astype(BF16)
    eye_f = jnp.where(r == c, 1.0, 0.0)
    r64 = lax.broadcasted_iota(jnp.int32, (CHUNK, CHUNK), 0)
    c64 = lax.broadcasted_iota(jnp.int32, (CHUNK, CHUNK), 1)
    dirs = []
    for sign, last in ((1, CHUNK - 1), (-1, 0)):
        dirs.append((jnp.where(jnp.logical_and(same, dt * sign >= 0), 1.0, 0.0),
                     jnp.where(jnp.logical_and(same, dt * sign > 0), 1.0, 0.0),
                     jnp.where((r64 - c64) * sign >= 0, 1.0, 0.0), last))

    def lift(x):
        return jnp.concatenate([x, x, x, x], axis=0) * same_f

    def lift_b(x):
        xb = x.astype(BF16)
        return jnp.concatenate([xb, xb, xb, xb], axis=0) * same_b

    @pl.when(j == 0)
    def _():
        s_scr[...] = s0_ref[...]

    def each(f, *xs):
        return [f(*a) for a in zip(*xs)]

    def mm(a, b):
        return jnp.dot(a, b, preferred_element_type=F32)

    def chunks(us, gbs, es, ss, incls, stricts, cums, lasts):
        q = [u[:, :W_DN] for u in us]
        k = [u[:, W_DN:2 * W_DN] for u in us]
        v = [u[:, 2 * W_DN:] for u in us]
        gbx = each(_dot_exact, gbs, es)
        beta = [x[:, W_DN:] for x in gbx]
        gc = each(lambda c, x: _dot_exact(c, x[:, :W_DN]), cums, gbx)
        gl = each(lambda x, last: x[last:last + 1, :], gc, lasts)
        eg = [jnp.exp(x) for x in gc]
        kb = each(jnp.multiply, k, beta)
        gcol = [lift(x) for x in gc]
        decay = each(lambda x, m: jnp.exp(jnp.where(m > 0.5, x - x.T, NEG_INF)), gcol, incls)
        qk = each(lambda a, b, kk: lax.dot_general(jnp.concatenate([lift_b(a), lift_b(b)], axis=0),
                                                   lift_b(kk), (((1,), (1,)), ((), ())),
                                                   preferred_element_type=F32), q, kb, k)
        attn = each(lambda x, d: (x[:BD] * d).astype(BF16), qk, decay)
        n_mat = each(lambda x, d, m: -(x[BD:] * d) * m, qk, decay, stricts)
        t_inv = [eye_f + x for x in n_mat]
        n_hi = [x.astype(BF16) for x in n_mat]
        m_b = n_hi
        for _ in range(int(math.log2(CHUNK)) - 1):
            m_b = [mm(x, x).astype(BF16) for x in m_b]
            t_inv = each(lambda t, m: t + mm(t.astype(BF16), m), t_inv, m_b)
        n_lo = each(lambda x, h: (x - h.astype(F32)).astype(BF16), n_mat, n_hi)
        x_hi = [t.astype(BF16) for t in t_inv]
        x_lo = each(lambda t, h: (t - h.astype(F32)).astype(BF16), t_inv, x_hi)
        nx = each(lambda h, lo, x: mm(jnp.concatenate([h, lo], axis=0), x), n_hi, n_lo, x_hi)
        nxl = each(mm, n_hi, x_lo)
        resid = each(lambda t, a, b: (eye_f - t + a[:BD] + a[BD:] + b).astype(BF16), t_inv, nx, nxl)
        t_b = each(lambda t, h, rr: (t + mm(h, rr)).astype(BF16), t_inv, x_hi, resid)
        rhs = each(lambda vv, b, kk, e: jnp.concatenate([lift_b(vv * b), lift_b(kk * e)], axis=1),
                   v, beta, kb, eg)
        sol = each(mm, t_b, rhs)
        s_b = [s.astype(BF16) for s in ss]
        ps = each(lambda so, qq, e, sb: mm(jnp.concatenate([so[:, BD:].astype(BF16), lift_b(qq * e)],
                                                           axis=0), sb), sol, q, eg, s_b)
        v_new = each(lambda so, p: (so[:, :BD] - p[:BD]).astype(BF16), sol, ps)
        o_bd = each(lambda p, a, vn: p[BD:] + mm(a, vn), ps, attn, v_new)
        o_tm = [x[0:CHUNK] + x[CHUNK:2 * CHUNK] + x[2 * CHUNK:3 * CHUNK] + x[3 * CHUNK:] for x in o_bd]
        k_tail = each(lambda kk, a, b: lift(kk * jnp.exp(a - b)).T.astype(BF16), k, gl, gc)
        s_new = each(lambda s, a, kt, vn: s * jnp.exp(a) + mm(kt, vn), ss, gl, k_tail, v_new)
        return o_tm, s_new

    chains = [(si, di) for si in range(n_sub) for di in range(2)]

    def body(ci, carry):
        offs = (pl.multiple_of(ci * CHUNK, CHUNK), pl.multiple_of((n_chunks - 1 - ci) * CHUNK, CHUNK))
        u_refs, gb_refs, o_refs = (uf_ref, ub_ref), (gbf_ref, gbb_ref), (of_ref, ob_ref)
        o_tm, s_new = chunks(
            [u_refs[di][si, pl.ds(offs[di], CHUNK), :] for si, di in chains],
            [gb_refs[di][si, pl.ds(offs[di], CHUNK), :] for si, di in chains],
            [e_ref[di] for si, di in chains],
            [s_scr[si, di] for si, di in chains],
            *[[dirs[di][n] for si, di in chains] for n in range(4)])
        for (si, di), o, s in zip(chains, o_tm, s_new):
            o_refs[di][si, pl.ds(offs[di], CHUNK), :] = o
            s_scr[si, di] = s
        return carry

    lax.fori_loop(0, n_chunks, body, 0)

    @pl.when(j == pl.num_programs(1) - 1)
    def _():
        sf_ref[...] = s_scr[...]


def _deltanet(u, gb, e_mat, s0_bd, seq_len, rows, n_sub=2):
    t = u.shape[0]
    n_seq = t // seq_len
    nb = seq_len // rows
    kern = functools.partial(_dn_kernel, n_chunks=rows // CHUNK, n_sub=n_sub)
    fwd = lambda s, j: (s, j, 0)
    bwd = lambda s, j: (s, nb - 1 - j, 0)
    state = pl.BlockSpec((n_sub, 2, BD, BD), lambda s, j: (s, 0, 0, 0))
    o_f, o_b, s_fin = pl.pallas_call(
        kern,
        grid=(n_seq // n_sub, nb),
        in_specs=[pl.BlockSpec((n_sub, rows, 3 * W_DN), fwd), pl.BlockSpec((n_sub, rows, 3 * W_DN), bwd),
                  pl.BlockSpec((n_sub, rows, AB_PAD), fwd), pl.BlockSpec((n_sub, rows, AB_PAD), bwd),
                  pl.BlockSpec((2, AB_PAD, 2 * W_DN), lambda s, j: (0, 0, 0)), state],
        out_specs=[pl.BlockSpec((n_sub, rows, W_DN), fwd), pl.BlockSpec((n_sub, rows, W_DN), bwd), state],
        out_shape=[jax.ShapeDtypeStruct((n_seq, seq_len, W_DN), F32),
                   jax.ShapeDtypeStruct((n_seq, seq_len, W_DN), F32),
                   jax.ShapeDtypeStruct((n_seq, 2, BD, BD), F32)],
        scratch_shapes=[pltpu.VMEM((n_sub, 2, BD, BD), F32)],
        compiler_params=_params("parallel", "arbitrary"),
        name="deltanet",
    )(*[x.reshape(n_seq, seq_len, -1) for x in (u, u, gb, gb)], e_mat, s0_bd)
    return o_f.reshape(t, W_DN), o_b.reshape(t, W_DN), s_fin


def _diff_lambda(lam_ref, lam_init):
    lp = lam_ref[...]
    return (jnp.exp(jnp.sum(lp[0:1] * lp[1:2], axis=1, keepdims=True))
            - jnp.exp(jnp.sum(lp[2:3] * lp[3:4], axis=1, keepdims=True)) + lam_init)


def _sub_rms(o, subln, lam_init):
    ms = jnp.mean(o * o, axis=-1, keepdims=True)
    return o * lax.rsqrt(ms + RMS_EPS) * subln * (1.0 - lam_init)


def _ctx_attn_kernel(na_ref, df_ref, lam_ref, subln_ref, oa_ref, oc_ref, *, lam_init):
    lam = _diff_lambda(lam_ref, lam_init)
    subln = subln_ref[...]
    n = na_ref.shape[0]
    qa = (na_ref[:, :W_NA] * NA_Q_SCALE).astype(BF16)
    ka = na_ref[:, W_NA:2 * W_NA].astype(BF16)
    va = na_ref[:, 2 * W_NA:].astype(BF16)
    for h in range(H_NA):
        sl = slice(h * HEAD_DIM, (h + 1) * HEAD_DIM)
        s = _dot_nt(qa[:, sl], ka[:, sl])
        p = jnp.exp2(s - jnp.max(s, axis=-1, keepdims=True))
        oa_ref[:, sl] = _dot(p, va[:, sl]) / jnp.sum(p, axis=-1, keepdims=True)
    qc = (df_ref[:, :W_DF] * DF_Q_SCALE).astype(BF16)
    kc = df_ref[:, W_DF:2 * W_DF].astype(BF16)
    vc = df_ref[:, 2 * W_DF:].astype(BF16)
    for h in range(H_DF):
        sl = slice(h * HEAD_DIM, (h + 1) * HEAD_DIM)
        es, dens = [], []
        for i in range(2):
            b = h * HEAD_DIM + i * DF_QK
            s = _dot_nt(qc[:, b:b + DF_QK], kc[:, b:b + DF_QK])
            e = jnp.exp2(s - jnp.max(s, axis=-1, keepdims=True))
            es.append(e.astype(BF16))
            dens.append(jnp.sum(e, axis=-1, keepdims=True))
        ov = _dot(jnp.concatenate(es, axis=0), vc[:, sl])
        o = ov[:n] / dens[0] - lam * (ov[n:] / dens[1])
        oc_ref[:, sl] = _sub_rms(o, subln, lam_init)


def _ctx_attn(qkv_na, qkv_df, lam_p, subln_row, layer):
    lam_init = 0.8 - 0.6 * math.exp(-0.3 * layer)
    kern = functools.partial(_ctx_attn_kernel, lam_init=lam_init)
    return pl.pallas_call(
        kern,
        grid=(BATCH,),
        in_specs=[pl.BlockSpec((SEQ, 3 * W_NA), lambda b: (b, 0)),
                  pl.BlockSpec((SEQ, 3 * W_DF), lambda b: (b, 0)),
                  pl.BlockSpec((4, DF_QK), lambda b: (0, 0)),
                  pl.BlockSpec((1, HEAD_DIM), lambda b: (0, 0))],
        out_specs=[pl.BlockSpec((SEQ, W_NA), lambda b: (b, 0)),
                   pl.BlockSpec((SEQ, W_DF), lambda b: (b, 0))],
        out_shape=[jax.ShapeDtypeStruct((T_CTX, W_NA), F32),
                   jax.ShapeDtypeStruct((T_CTX, W_DF), F32)],
        compiler_params=_params("parallel"),
        name="ctx_attn",
    )(qkv_na, qkv_df, lam_p, subln_row)


NA_R = 4
NA_U = NA_R + WIN_R - 1
NA_GROUPS = GRID_H // NA_R
NA_Q_SCALE = HEAD_DIM ** -0.5 * math.log2(math.e)


def _na_key_start(g):
    return jnp.clip(g * NA_R - WIN_R // 2, 0, GRID_H - NA_U)


def _na_bias_table(rpb):
    qc = np.arange(GRID_W)[:, None]
    kc = np.arange(GRID_W)[None, :]
    cs = np.clip(qc - WIN_C // 2, 0, GRID_W - WIN_C)
    valid = (kc >= cs) & (kc < cs + WIN_C)
    dc = np.clip(kc - qc + (WIN_C - 1), 0, 2 * WIN_C - 2)
    onehot = (dc[None] == np.arange(2 * WIN_C - 1)[:, None, None]).astype(np.float32)
    x = jnp.einsum('hrd,dqk->hrqk', rpb.astype(F32), jnp.asarray(onehot), precision=HIGHEST)
    x = jnp.where(valid[None, None], x * math.log2(math.e), NEG_INF)
    masked = jnp.full((H_NA, GRID_W, GRID_W), NEG_INF, F32)
    classes = []
    for g in (0, 1, NA_GROUPS - 1):
        u0 = int(np.clip(g * NA_R - WIN_R // 2, 0, GRID_H - NA_U))
        rows = []
        for i in range(NA_R):
            r = g * NA_R + i
            w0 = int(np.clip(r - WIN_R // 2, 0, GRID_H - WIN_R))
            blocks = [x[:, u0 + u - r + WIN_R - 1] if w0 <= u0 + u < w0 + WIN_R else masked
                      for u in range(NA_U)]
            rows.append(jnp.stack(blocks, axis=2))
        classes.append(jnp.stack(rows, axis=1))
    return jnp.stack(classes, axis=1).reshape(H_NA, 3, NA_R * GRID_W, NA_U * GRID_W)


def _lat_na_kernel(q_ref, k_ref, v_ref, kc_ref, vc_ref, bias_ref, o_ref):
    g = pl.program_id(1)
    start = pl.multiple_of(_na_key_start(g) * GRID_W, GRID_W)
    q = (q_ref[...] * NA_Q_SCALE).astype(BF16)
    kw = k_ref[pl.ds(start, NA_U * GRID_W), :].astype(BF16)
    vw = v_ref[pl.ds(start, NA_U * GRID_W), :].astype(BF16)
    kc = kc_ref[...]
    vc = vc_ref[...]
    for h in range(H_NA):
        sl = slice(h * HEAD_DIM, (h + 1) * HEAD_DIM)
        qh = q[:, sl]
        s_loc = _dot_nt(qh, kw[:, sl]) + bias_ref[h]
        s_ctx = _dot_nt(qh, kc[:, sl])
        m = jnp.maximum(jnp.max(s_loc, axis=-1, keepdims=True), jnp.max(s_ctx, axis=-1, keepdims=True))
        p_loc = jnp.exp2(s_loc - m)
        p_ctx = jnp.exp2(s_ctx - m)
        den = jnp.sum(p_loc, axis=-1, keepdims=True) + jnp.sum(p_ctx, axis=-1, keepdims=True)
        o_ref[:, sl] = (_dot(p_loc, vw[:, sl]) + _dot(p_ctx, vc[:, sl])) / den


def _lat_na(qkv_na, ck, cv, bias_tab):
    rows = NA_R * GRID_W
    cls = lambda g: jnp.where(g == 0, 0, jnp.where(g == NA_GROUPS - 1, 2, 1))
    return pl.pallas_call(
        _lat_na_kernel,
        grid=(DEC_BATCH, NA_GROUPS),
        in_specs=[pl.BlockSpec((rows, W_NA), lambda b, g: (b * NA_GROUPS + g, 0)),
                  pl.BlockSpec((DEC_SEQ, W_NA), lambda b, g: (b, 1)),
                  pl.BlockSpec((DEC_SEQ, W_NA), lambda b, g: (b, 2)),
                  pl.BlockSpec((None, PAST_LEN, W_NA), lambda b, g: (b, 0, 0)),
                  pl.BlockSpec((None, PAST_LEN, W_NA), lambda b, g: (b, 0, 0)),
                  pl.BlockSpec((H_NA, None, rows, NA_U * GRID_W), lambda b, g: (0, cls(g), 0, 0))],
        out_specs=pl.BlockSpec((rows, W_NA), lambda b, g: (b * NA_GROUPS + g, 0)),
        out_shape=jax.ShapeDtypeStruct((T_LAT, W_NA), F32),
        compiler_params=_params("parallel", "arbitrary"),
        name="lat_na",
    )(qkv_na, qkv_na, qkv_na, ck, cv, bias_tab)


def _rope_tables():
    nf = DF_QK // 4
    inv = ROPE_BASE ** (-np.arange(nf, dtype=np.float32) / nf)
    t = np.arange(DEC_SEQ)
    pos = np.stack([t // GRID_W, t % GRID_W], axis=-1).astype(np.float32)
    ang = jnp.asarray(pos[:, :, None] * inv)
    cos, sin = jnp.cos(ang), jnp.sin(ang)
    cos32 = jnp.concatenate([cos, cos], axis=-1).reshape(DEC_SEQ, DF_QK)
    sin32 = jnp.concatenate([-sin, sin], axis=-1).reshape(DEC_SEQ, DF_QK)
    reps = W_DF // DF_QK
    rot = np.zeros((W_DF, W_DF), np.float32)
    for dd in range(W_DF):
        rot[dd + nf if dd % (2 * nf) < nf else dd - nf, dd] = 1.0
    return jnp.tile(cos32, (1, reps)), jnp.tile(sin32, (1, reps)), jnp.asarray(rot)


def _rope_kernel(df_ref, cos_ref, sin_ref, rot_ref, q_ref, kt_ref, v_ref):
    cos, sin, rot = cos_ref[...], sin_ref[...], rot_ref[...]
    q = df_ref[:, :W_DF]
    k = df_ref[:, W_DF:2 * W_DF]
    q_ref[...] = ((q * cos + _dot_exact(q, rot) * sin) * DF_Q_SCALE).astype(BF16)
    kt_ref[...] = (k * cos + _dot_exact(k, rot) * sin).T.astype(BF16)
    lane = lax.broadcasted_iota(jnp.int32, (q.shape[0], V_AUG - HEAD_DIM), 1)
    one_col = jnp.where(lane == 0, 1.0, 0.0).astype(BF16)
    for h in range(H_DF):
        a = 2 * W_DF + h * HEAD_DIM
        v_ref[:, h * V_AUG:h * V_AUG + HEAD_DIM] = df_ref[:, a:a + HEAD_DIM].astype(BF16)
        v_ref[:, h * V_AUG + HEAD_DIM:(h + 1) * V_AUG] = one_col


def _rope(qkv_df, cos, sin, rot, tm=512):
    nb = DEC_SEQ // tm
    return pl.pallas_call(
        _rope_kernel,
        grid=(DEC_BATCH, nb),
        in_specs=[pl.BlockSpec((tm, 3 * W_DF), lambda b, i: (b * nb + i, 0)),
                  pl.BlockSpec((tm, W_DF), lambda b, i: (i, 0)),
                  pl.BlockSpec((tm, W_DF), lambda b, i: (i, 0)),
                  pl.BlockSpec((W_DF, W_DF), lambda b, i: (0, 0))],
        out_specs=[pl.BlockSpec((tm, W_DF), lambda b, i: (b * nb + i, 0)),
                   pl.BlockSpec((None, W_DF, tm), lambda b, i: (b, 0, i)),
                   pl.BlockSpec((tm, H_DF * V_AUG), lambda b, i: (b * nb + i, 0))],
        out_shape=[jax.ShapeDtypeStruct((T_LAT, W_DF), BF16),
                   jax.ShapeDtypeStruct((DEC_BATCH, W_DF, DEC_SEQ), BF16),
                   jax.ShapeDtypeStruct((T_LAT, H_DF * V_AUG), BF16)],
        compiler_params=_params("parallel", "parallel"),
        name="rope",
    )(qkv_df, cos, sin, rot)


def _lat_df_kernel(q_ref, kt_ref, v_ref, lam_ref, subln_ref, o_ref, *, lam_init, tq):
    q = q_ref[...]
    lam = _diff_lambda(lam_ref, lam_init)
    subln = subln_ref[...]
    for h in range(H_DF):
        es = []
        for i in range(2):
            a = h * HEAD_DIM + i * DF_QK
            s = jnp.dot(q[:, a:a + DF_QK], kt_ref[a:a + DF_QK, :], preferred_element_type=F32)
            es.append(jnp.exp2(s - jnp.max(s, axis=-1, keepdims=True)).astype(BF16))
        ov = jnp.dot(jnp.concatenate(es, axis=0), v_ref[:, h * V_AUG:(h + 1) * V_AUG],
                     preferred_element_type=F32)
        o = (ov[:tq, :HEAD_DIM] / ov[:tq, HEAD_DIM:HEAD_DIM + 1]
             - lam * (ov[tq:, :HEAD_DIM] / ov[tq:, HEAD_DIM:HEAD_DIM + 1]))
        o_ref[:, h * HEAD_DIM:(h + 1) * HEAD_DIM] = _sub_rms(o, subln, lam_init)


def _lat_df(q_r, kt_all, v_all, lam_p, subln_row, layer, tq=256):
    lam_init = 0.8 - 0.6 * math.exp(-0.3 * layer)
    nb = DEC_SEQ // tq
    n_keys = kt_all.shape[2]
    kern = functools.partial(_lat_df_kernel, lam_init=lam_init, tq=tq)
    return pl.pallas_call(
        kern,
        grid=(DEC_BATCH, nb),
        in_specs=[pl.BlockSpec((tq, W_DF), lambda b, i: (b * nb + i, 0)),
                  pl.BlockSpec((None, W_DF, n_keys), lambda b, i: (b, 0, 0)),
                  pl.BlockSpec((None, n_keys, H_DF * V_AUG), lambda b, i: (b, 0, 0)),
                  pl.BlockSpec((4, DF_QK), lambda b, i: (0, 0)),
                  pl.BlockSpec((1, HEAD_DIM), lambda b, i: (0, 0))],
        out_specs=pl.BlockSpec((tq, W_DF), lambda b, i: (b * nb + i, 0)),
        out_shape=jax.ShapeDtypeStruct((T_LAT, W_DF), F32),
        compiler_params=_params("parallel", "arbitrary"),
        name="lat_df",
    )(q_r, kt_all, v_all, lam_p, subln_row)


def _outproj_kernel(oa_ref, of_ref, obk_ref, gate_ref, onorm_ref, grp_ref, oc_ref, y_ref, g1_ref, w_ref,
                    lg_ref, lb_ref, o_ref):
    ob = of_ref[...] + obk_ref[...]
    ms = _dot_exact(ob * ob, grp_ref[...]) * (1.0 / HEAD_DIM)
    ob = ob * lax.rsqrt(ms + RMS_EPS) * onorm_ref[...] * _silu(gate_ref[...])
    o = (jnp.dot(oa_ref[...].astype(BF16), w_ref[:W_NA, :], preferred_element_type=F32)
         + jnp.dot(ob.astype(BF16), w_ref[W_NA:W_NA + W_DN, :], preferred_element_type=F32)
         + jnp.dot(oc_ref[...].astype(BF16), w_ref[W_NA + W_DN:, :], preferred_element_type=F32))
    o_ref[...] = _layer_norm(ALPHA * y_ref[...] + g1_ref[...] * o, lg_ref[...], lb_ref[...])


def _outproj(oa, o_fwd, o_bwd, gate, onorm_row, grp, oc, y, mod, w, ln_g, ln_b, cond_fn, tm=512):
    t = y.shape[0]
    row = lambda n: pl.BlockSpec((tm, n), lambda i: (i, 0))
    vec = pl.BlockSpec((1, D_MODEL), lambda i: (0, 0))
    return pl.pallas_call(
        _outproj_kernel,
        grid=(t // tm,),
        in_specs=[row(W_NA), row(W_DN), row(W_DN),
                  row(W_DN), pl.BlockSpec((1, W_DN), lambda i: (0, 0)),
                  pl.BlockSpec((W_DN, W_DN), lambda i: (0, 0)),
                  row(W_DF), row(D_MODEL), _mod_spec(2, cond_fn),
                  pl.BlockSpec((D_MODEL, D_MODEL), lambda i: (0, 0)), vec, vec],
        out_specs=row(D_MODEL),
        out_shape=jax.ShapeDtypeStruct((t, D_MODEL), F32),
        compiler_params=_params("parallel"),
        name="outproj_ln",
    )(oa, o_fwd, o_bwd, gate, onorm_row, grp, oc, y, mod, w, ln_g, ln_b)


def _router_kernel(y_ref, sh_ref, sc_ref, w_ref, b_ref, g_ref):
    h = y_ref[...] * (1.0 + sc_ref[...]) + sh_ref[...]
    logits = _dot_exact(h, w_ref[...]) + b_ref[...]
    lane = lax.broadcasted_iota(jnp.int32, logits.shape, 1).astype(F32)
    logits = jnp.where(lane < N_EXPERTS, logits, -jnp.inf)
    m1 = jnp.max(logits, axis=-1, keepdims=True)
    i1 = jnp.min(jnp.where(logits == m1, lane, float(LANES)), axis=-1, keepdims=True)
    rest = jnp.where(lane == i1, -jnp.inf, logits)
    m2 = jnp.max(rest, axis=-1, keepdims=True)
    i2 = jnp.min(jnp.where(rest == m2, lane, float(LANES)), axis=-1, keepdims=True)
    e2 = jnp.exp(m2 - m1)
    w1 = 1.0 / (1.0 + e2)
    g_ref[...] = jnp.where(lane == i1, w1, 0.0) + jnp.where(lane == i2, e2 * w1, 0.0)


def _router(y, mod, w_pad, b_pad, cond_fn, tm=512):
    t = y.shape[0]
    return pl.pallas_call(
        _router_kernel,
        grid=(t // tm,),
        in_specs=[pl.BlockSpec((tm, D_MODEL), lambda i: (i, 0)),
                  _mod_spec(3, cond_fn), _mod_spec(4, cond_fn),
                  pl.BlockSpec((D_MODEL, LANES), lambda i: (0, 0)),
                  pl.BlockSpec((1, LANES), lambda i: (0, 0))],
        out_specs=pl.BlockSpec((tm, LANES), lambda i: (i, 0)),
        out_shape=jax.ShapeDtypeStruct((t, LANES), F32),
        compiler_params=_params("parallel"),
        name="router",
    )(y, mod, mod, w_pad, b_pad)


def _ffn_kernel(y_ref, sh_ref, sc_ref, g2_ref, gates_ref, wg_ref, wu_ref, wd_ref, lg_ref, lb_ref,
                o_ref, h_scr, acc_scr, *, n_blocks):
    e = pl.program_id(1)

    @pl.when(e == 0)
    def _():
        h_scr[...] = (y_ref[...] * (1.0 + sc_ref[...]) + sh_ref[...]).astype(BF16)
        acc_scr[...] = jnp.zeros_like(acc_scr)

    h = h_scr[...]
    a = jnp.dot(h, wg_ref[...], preferred_element_type=F32)
    b = jnp.dot(h, wu_ref[...], preferred_element_type=F32)
    f = jnp.dot((_silu(a) * b).astype(BF16), wd_ref[...], preferred_element_type=F32)
    lane = lax.broadcasted_iota(jnp.int32, gates_ref.shape, 1)
    gate = jnp.sum(jnp.where(lane == e, gates_ref[...], 0.0), axis=-1, keepdims=True)
    acc_scr[...] += gate * f

    @pl.when(e == n_blocks - 1)
    def _():
        o_ref[...] = _layer_norm(ALPHA * y_ref[...] + g2_ref[...] * acc_scr[...],
                                 lg_ref[...], lb_ref[...])


def _ffn(y, mod, gates, wg, wu, wd, ln_g, ln_b, cond_fn, tm=512):
    t = y.shape[0]
    n_blocks, _, ff = wg.shape
    vec = pl.BlockSpec((1, D_MODEL), lambda i, e: (0, 0))
    kern = functools.partial(_ffn_kernel, n_blocks=n_blocks)
    return pl.pallas_call(
        kern,
        grid=(t // tm, n_blocks),
        in_specs=[pl.BlockSpec((tm, D_MODEL), lambda i, e: (i, 0)),
                  _mod_spec(3, cond_fn), _mod_spec(4, cond_fn), _mod_spec(5, cond_fn),
                  pl.BlockSpec((tm, LANES), lambda i, e: (i, 0)),
                  pl.BlockSpec((None, D_MODEL, ff), lambda i, e: (e, 0, 0)),
                  pl.BlockSpec((None, D_MODEL, ff), lambda i, e: (e, 0, 0)),
                  pl.BlockSpec((None, ff, D_MODEL), lambda i, e: (e, 0, 0)),
                  vec, vec],
        out_specs=pl.BlockSpec((tm, D_MODEL), lambda i, e: (i, 0)),
        out_shape=jax.ShapeDtypeStruct((t, D_MODEL), F32),
        scratch_shapes=[pltpu.VMEM((tm, D_MODEL), BF16), pltpu.VMEM((tm, D_MODEL), F32)],
        compiler_params=_params("parallel", "arbitrary"),
        name="ffn_ln",
    )(y, mod, mod, mod, gates, wg, wu, wd, ln_g, ln_b)


def _permute_w_in(w):
    offs = np.cumsum((0,) + PROJ_SIZES)
    qa, ka, va, qkv, gate, a, b, qc, kc, vc = (w[:, offs[i]:offs[i + 1]] for i in range(10))
    pad = jnp.zeros((D_MODEL, AB_PAD - 4 * H_DN), w.dtype)
    return jnp.concatenate([qa, ka, va, qkv, gate, qc, kc, vc, a, b, pad], axis=1).astype(BF16)


def _pad_row(v, n=AB_PAD):
    v = v.reshape(1, -1)
    return jnp.pad(v, ((0, 0), (0, n - v.shape[1])))


def _group_matrix(n):
    idx = np.arange(n) // HEAD_DIM
    return jnp.asarray((idx[:, None] == idx[None, :]).astype(np.float32))


def _expand_matrix():
    e = np.zeros((2, AB_PAD, 2 * W_DN), np.float32)
    for d in range(2):
        for h in range(H_DN):
            e[d, d * H_DN + h, h * HEAD_DIM:(h + 1) * HEAD_DIM] = 1.0
            e[d, 2 * H_DN + d * H_DN + h, W_DN + h * HEAD_DIM:W_DN + (h + 1) * HEAD_DIM] = 1.0
    return jnp.asarray(e)


def _state_to_bd(s):
    eye = jnp.eye(H_DN, dtype=s.dtype)
    return jnp.einsum('sdhkv,hg->sdhkgv', s, eye).reshape(s.shape[0], 2, BD, BD)


def _bd_to_state(s):
    n = s.shape[0]
    return jnp.einsum('sdhkhv->sdhkv', s.reshape(n, 2, H_DN, HEAD_DIM, H_DN, HEAD_DIM))


def kernel(x_prompt, x_sample, cache_na_k, cache_na_v, cache_df_k, cache_df_v, state_dn, c, c_ctx,
           ada_w, ada_b, w_in, conv_dn, a_log_dn, dt_bias_dn, onorm_dn, rpb_na, lambda_df, subln_df,
           w_out, ln1_g, ln1_b, ln2_g, ln2_b, ffn_w_gate, ffn_w_up, ffn_w_down, router_w, router_b,
           moe_w_gate, moe_w_up, moe_w_down):
    conds = jnp.concatenate([c_ctx[None, :], c, jnp.zeros((N_COND - 1 - DEC_BATCH, D_MODEL), F32)], axis=0)
    mods = _ada_table(conds, ada_w, ada_b)
    ctx_cond = lambda i: 0
    lat_cond = lambda i: 1 + (i * 512) // DEC_SEQ

    grp512, grp256 = _group_matrix(2 * W_DN), _group_matrix(W_DN)
    e_mat = _expand_matrix()
    cos, sin, rot = _rope_tables()
    ones_gates = jnp.ones((T_CTX, LANES), F32)
    one_col = jnp.zeros((DEC_BATCH, PAST_LEN, H_DF, V_AUG - HEAD_DIM), BF16).at[..., 0].set(1.0)

    y_ctx = x_prompt.reshape(T_CTX, D_MODEL)
    y_lat = x_sample.reshape(T_LAT, D_MODEL)
    ctx_out = []
    for l in range(DEPTH):
        mod = mods[l]
        w_in_l = _permute_w_in(w_in[l])
        w_out_l = w_out[l].astype(BF16)
        conv_w = jnp.pad(conv_dn[l], ((0, 8 - CONV_K), (0, 0)))
        alog_row = _pad_row(a_log_dn[l])
        dtb_row = _pad_row(dt_bias_dn[l])
        onorm_row = jnp.tile(onorm_dn[l], H_DN).reshape(1, W_DN)
        subln_row = subln_df[l].reshape(1, HEAD_DIM)
        lg1, lb1 = ln1_g[l].reshape(1, D_MODEL), ln1_b[l].reshape(1, D_MODEL)
        lg2, lb2 = ln2_g[l].reshape(1, D_MODEL), ln2_b[l].reshape(1, D_MODEL)

        na, dn, gate, df, ab = _inproj(y_ctx, mod, w_in_l, ctx_cond)
        u, gb = _dn_prep(dn, ab, conv_w, alog_row, dtb_row, grp512, SEQ)
        s0 = jnp.zeros((BATCH, 2, BD, BD), F32)
        o_f, o_b, s_fin = _deltanet(u, gb, e_mat, s0, SEQ, SEQ)
        oa, oc = _ctx_attn(na, df, lambda_df[l], subln_row, l)
        y1_ctx = _outproj(oa, o_f, o_b, gate, onorm_row, grp256, oc, y_ctx, mod, w_out_l, lg1, lb1, ctx_cond)
        ctx_out.append((na[:, W_NA:2 * W_NA].reshape(BATCH, SEQ, H_NA, HEAD_DIM),
                        na[:, 2 * W_NA:].reshape(BATCH, SEQ, H_NA, HEAD_DIM),
                        df[:, W_DF:2 * W_DF].reshape(BATCH, SEQ, H_DF, HEAD_DIM),
                        df[:, 2 * W_DF:].reshape(BATCH, SEQ, H_DF, HEAD_DIM),
                        _bd_to_state(s_fin)))

        na, dn, gate, df, ab = _inproj(y_lat, mod, w_in_l, lat_cond)
        u, gb = _dn_prep(dn, ab, conv_w, alog_row, dtb_row, grp512, DEC_SEQ)
        o_f, o_b, _ = _deltanet(u, gb, e_mat, _state_to_bd(state_dn[:, l]), DEC_SEQ, 512)
        oa = _lat_na(na, cache_na_k[:, l].reshape(DEC_BATCH, PAST_LEN, W_NA).astype(BF16),
                     cache_na_v[:, l].reshape(DEC_BATCH, PAST_LEN, W_NA).astype(BF16),
                     _na_bias_table(rpb_na[l]))
        q_r, kt, v_b = _rope(df, cos, sin, rot)
        ktc = jnp.swapaxes(cache_df_k[:, l].reshape(DEC_BATCH, PAST_LEN, W_DF), 1, 2).astype(BF16)
        vc = jnp.concatenate([cache_df_v[:, l].astype(BF16), one_col], axis=-1)
        kt_all = jnp.concatenate([kt, ktc], axis=2)
        v_all = jnp.concatenate([v_b.reshape(DEC_BATCH, DEC_SEQ, H_DF * V_AUG),
                                 vc.reshape(DEC_BATCH, PAST_LEN, H_DF * V_AUG)], axis=1)
        oc = _lat_df(q_r, kt_all, v_all, lambda_df[l], subln_row, l)
        y1_lat = _outproj(oa, o_f, o_b, gate, onorm_row, grp256, oc, y_lat, mod, w_out_l, lg1, lb1, lat_cond)

        i = l // 2
        if l % 2 == 0:
            split = lambda w: jnp.transpose(w.reshape(D_MODEL, D_FF // MOE_FF, MOE_FF), (1, 0, 2))
            wg, wu = split(ffn_w_gate[i]).astype(BF16), split(ffn_w_up[i]).astype(BF16)
            wd = ffn_w_down[i].reshape(D_FF // MOE_FF, MOE_FF, D_MODEL).astype(BF16)
            g_ctx = g_lat = ones_gates
        else:
            wg, wu, wd = moe_w_gate[i].astype(BF16), moe_w_up[i].astype(BF16), moe_w_down[i].astype(BF16)
            rw = jnp.pad(router_w[i], ((0, 0), (0, LANES - N_EXPERTS)))
            rb = _pad_row(router_b[i], LANES)
            g_ctx = _router(y1_ctx, mod, rw, rb, ctx_cond)
            g_lat = _router(y1_lat, mod, rw, rb, lat_cond)
        y_ctx = _ffn(y1_ctx, mod, g_ctx, wg, wu, wd, lg2, lb2, ctx_cond)
        y_lat = _ffn(y1_lat, mod, g_lat, wg, wu, wd, lg2, lb2, lat_cond)

    stack = lambda j: jnp.stack([t[j] for t in ctx_out], axis=1)
    return (y_ctx.reshape(BATCH, SEQ, D_MODEL), y_lat.reshape(DEC_BATCH, DEC_SEQ, D_MODEL),
            stack(0), stack(1), stack(2), stack(3), stack(4))
```

```python
import functools
import math

import jax
import jax.numpy as jnp
import numpy as np
from jax import lax
from jax.experimental import pallas as pl
from jax.experimental.pallas import tpu as pltpu

F32 = jnp.float32
BF16 = jnp.bfloat16
HIGHEST = lax.Precision.HIGHEST

D_MODEL = 1024
BATCH = 32
SEQ = 256
DEPTH = 2
DEC_BATCH = 2
DEC_SEQ = 4096
PAST_LEN = 512
GRID_W = 64
GRID_H = DEC_SEQ // GRID_W
HEAD_DIM = 64
H_NA = 6
H_DN = 4
H_DF = 6
W_NA = H_NA * HEAD_DIM
W_DN = H_DN * HEAD_DIM
W_DF = H_DF * HEAD_DIM
DF_QK = HEAD_DIM // 2
WIN_R = 8
WIN_C = 16
CONV_K = 5
CHUNK = 64
D_FF = 2816
N_EXPERTS = 8
MOE_FF = 1408
ALPHA = (2 * DEPTH) ** 0.25
LN_EPS = 1e-5
RMS_EPS = 1e-6
ROPE_BASE = 10000.0
NEG_INF = -1e30
PROJ_SIZES = (W_NA, W_NA, W_NA, 3 * W_DN, W_DN, 2 * H_DN, 2 * H_DN, W_DF, W_DF, W_DF)

T_CTX = BATCH * SEQ
T_LAT = DEC_BATCH * DEC_SEQ
N_COND = 8
LANES = 128
AB_PAD = LANES
SEG_NA = (0, 3 * W_NA)
SEG_DN = (SEG_NA[1], SEG_NA[1] + 3 * W_DN)
SEG_GATE = (SEG_DN[1], SEG_DN[1] + W_DN)
SEG_DF = (SEG_GATE[1], SEG_GATE[1] + 3 * W_DF)
SEG_AB = (SEG_DF[1], SEG_DF[1] + AB_PAD)
P_PAD = SEG_AB[1]
SEGS = (SEG_NA, SEG_DN, SEG_GATE, SEG_DF, SEG_AB)
BD = H_DN * CHUNK
V_AUG = LANES
DF_Q_SCALE = DF_QK ** -0.5 * math.log2(math.e)
VMEM_LIMIT = 56 * 1024 * 1024


def _params(*sem):
    return pltpu.CompilerParams(dimension_semantics=sem, vmem_limit_bytes=VMEM_LIMIT)


def _dot(a, b):
    return jnp.dot(a.astype(BF16), b.astype(BF16), preferred_element_type=F32)


def _dot_nt(a, b):
    return lax.dot_general(a.astype(BF16), b.astype(BF16), (((1,), (1,)), ((), ())),
                           preferred_element_type=F32)


def _dot_exact(a, b):
    return jnp.dot(a, b, precision=HIGHEST, preferred_element_type=F32)


def _silu(x):
    return x * jax.nn.sigmoid(x)


def _layer_norm(x, g, b):
    mu = jnp.mean(x, axis=-1, keepdims=True)
    xc = x - mu
    var = jnp.mean(xc * xc, axis=-1, keepdims=True)
    return xc * lax.rsqrt(var + LN_EPS) * g + b


def _ada_kernel(c_ref, w_ref, b_ref, o_ref):
    o_ref[...] = _dot_exact(_silu(c_ref[...]), w_ref[...]) + b_ref[...]


def _ada_table(conds, ada_w, ada_b):
    out = pl.pallas_call(
        _ada_kernel,
        grid=(DEPTH, 6),
        in_specs=[pl.BlockSpec((N_COND, D_MODEL), lambda l, k: (0, 0)),
                  pl.BlockSpec((None, D_MODEL, D_MODEL), lambda l, k: (l, 0, k)),
                  pl.BlockSpec((None, None, 1, D_MODEL), lambda l, k: (l, k, 0, 0))],
        out_specs=pl.BlockSpec((None, None, N_COND, D_MODEL), lambda l, k: (l, k, 0, 0)),
        out_shape=jax.ShapeDtypeStruct((DEPTH, 6, N_COND, D_MODEL), F32),
        compiler_params=_params("parallel", "parallel"),
        name="ada_table",
    )(conds, ada_w, ada_b.reshape(DEPTH, 6, 1, D_MODEL))
    return out.reshape(DEPTH, 6, N_COND, 1, D_MODEL)


def _mod_spec(k, cond_fn):
    return pl.BlockSpec((None, None, 1, D_MODEL), lambda i, *_: (k, cond_fn(i), 0, 0))


def _inproj_kernel(x_ref, sh_ref, sc_ref, w_ref, *o_refs):
    h = (x_ref[...] * (1.0 + sc_ref[...]) + sh_ref[...]).astype(BF16)
    for o_ref, (a, b) in zip(o_refs, SEGS):
        o_ref[...] = jnp.dot(h, w_ref[:, a:b], preferred_element_type=F32)


def _inproj(x, mod, w, cond_fn, tm=512):
    t = x.shape[0]
    return pl.pallas_call(
        _inproj_kernel,
        grid=(t // tm,),
        in_specs=[pl.BlockSpec((tm, D_MODEL), lambda i: (i, 0)),
                  _mod_spec(0, cond_fn), _mod_spec(1, cond_fn),
                  pl.BlockSpec((D_MODEL, P_PAD), lambda i: (0, 0))],
        out_specs=[pl.BlockSpec((tm, b - a), lambda i: (i, 0)) for a, b in SEGS],
        out_shape=[jax.ShapeDtypeStruct((t, b - a), F32) for a, b in SEGS],
        compiler_params=_params("parallel"),
        name="inproj",
    )(x, mod, mod, w)


def _dn_prep_kernel(x_ref, prev_ref, next_ref, ab_ref, w_ref, alog_ref, dtb_ref, grp_ref,
                    u_ref, gb_ref, *, blocks_per_seq, rows):
    i = pl.program_id(0)
    j = i % blocks_per_seq
    prev = jnp.where(j != 0, prev_ref[...], 0.0)
    nxt = jnp.where(j != blocks_per_seq - 1, next_ref[...], 0.0)
    xe = jnp.concatenate([prev, x_ref[...], nxt], axis=0)
    w = w_ref[...]
    base = 8 - CONV_K // 2
    acc = w[0:1, :] * xe[base:base + rows, :]
    for t in range(1, CONV_K):
        acc = acc + w[t:t + 1, :] * xe[base + t:base + t + rows, :]
    u = _silu(acc)
    qk = u[:, :2 * W_DN]
    ss = _dot_exact(qk * qk, grp_ref[...])
    qk = qk * lax.rsqrt(ss + RMS_EPS)
    u_ref[:, :W_DN] = qk[:, :W_DN] * HEAD_DIM ** -0.5
    u_ref[:, W_DN:2 * W_DN] = qk[:, W_DN:]
    u_ref[:, 2 * W_DN:] = u[:, 2 * W_DN:]
    ab = ab_ref[...]
    z = ab + dtb_ref[...]
    softplus = jnp.maximum(z, 0.0) + jnp.log(1.0 + jnp.exp(-jnp.abs(z)))
    g = -jnp.exp(alog_ref[...]) * softplus
    lane = lax.broadcasted_iota(jnp.int32, ab.shape, 1)
    gb_ref[...] = jnp.where(lane < 2 * H_DN, g, jax.nn.sigmoid(ab))


def _dn_prep(qkv, ab, conv_w, alog_row, dtb_row, grp, seq_len, rows=256):
    t = qkv.shape[0]
    bps = seq_len // rows
    r8 = rows // 8
    last8 = t // 8 - 1
    kern = functools.partial(_dn_prep_kernel, blocks_per_seq=bps, rows=rows)
    return pl.pallas_call(
        kern,
        grid=(t // rows,),
        in_specs=[pl.BlockSpec((rows, 3 * W_DN), lambda i: (i, 0)),
                  pl.BlockSpec((8, 3 * W_DN), lambda i: (jnp.maximum(i * r8 - 1, 0), 0)),
                  pl.BlockSpec((8, 3 * W_DN), lambda i: (jnp.minimum((i + 1) * r8, last8), 0)),
                  pl.BlockSpec((rows, AB_PAD), lambda i: (i, 0)),
                  pl.BlockSpec((8, 3 * W_DN), lambda i: (0, 0)),
                  pl.BlockSpec((1, AB_PAD), lambda i: (0, 0)),
                  pl.BlockSpec((1, AB_PAD), lambda i: (0, 0)),
                  pl.BlockSpec((2 * W_DN, 2 * W_DN), lambda i: (0, 0))],
        out_specs=[pl.BlockSpec((rows, 3 * W_DN), lambda i: (i, 0)),
                   pl.BlockSpec((rows, AB_PAD), lambda i: (i, 0))],
        out_shape=[jax.ShapeDtypeStruct((t, 3 * W_DN), F32),
                   jax.ShapeDtypeStruct((t, AB_PAD), F32)],
        compiler_params=_params("parallel"),
        name="dn_prep",
    )(qkv, qkv, qkv, ab, conv_w, alog_row, dtb_row, grp)


def _dn_kernel(uf_ref, ub_ref, gbf_ref, gbb_ref, e_ref, s0_ref, of_ref, ob_ref, sf_ref, s_scr,
               *, n_chunks, n_sub):
    j = pl.program_id(1)
    r = lax.broadcasted_iota(jnp.int32, (BD, BD), 0)
    c = lax.broadcasted_iota(jnp.int32, (BD, BD), 1)
    same = (r // CHUNK) == (c // CHUNK)
    dt = (r % CHUNK) - (c % CHUNK)
    same_f = jnp.where(same, 1.0, 0.0)
    same_b = same_f.astype(BF16)
    eye_f = jnp.where(r == c, 1.0, 0.0)
    r64 = lax.broadcasted_iota(jnp.int32, (CHUNK, CHUNK), 0)
    c64 = lax.broadcasted_iota(jnp.int32, (CHUNK, CHUNK), 1)
    dirs = []
    for sign, last in ((1, CHUNK - 1), (-1, 0)):
        dirs.append((jnp.where(jnp.logical_and(same, dt * sign >= 0), 1.0, 0.0),
                     jnp.where(jnp.logical_and(same, dt * sign > 0), 1.0, 0.0),
                     jnp.where((r64 - c64) * sign >= 0, 1.0, 0.0), last))

    def lift(x):
        return jnp.concatenate([x, x, x, x], axis=0) * same_f

    def lift_b(x):
        xb = x.astype(BF16)
        return jnp.concatenate([xb, xb, xb, xb], axis=0) * same_b

    @pl.when(j == 0)
    def _():
        s_scr[...] = s0_ref[...]

    def each(f, *xs):
        return [f(*a) for a in zip(*xs)]

    def mm(a, b):
        return jnp.dot(a, b, preferred_element_type=F32)

    def chunks(us, gbs, es, ss, incls, stricts, cums, lasts):
        q = [u[:, :W_DN] for u in us]
        k = [u[:, W_DN:2 * W_DN] for u in us]
        v = [u[:, 2 * W_DN:] for u in us]
        gbx = each(_dot_exact, gbs, es)
        beta = [x[:, W_DN:] for x in gbx]
        gc = each(lambda c, x: _dot_exact(c, x[:, :W_DN]), cums, gbx)
        gl = each(lambda x, last: x[last:last + 1, :], gc, lasts)
        eg = [jnp.exp(x) for x in gc]
        kb = each(jnp.multiply, k, beta)
        gcol = [lift(x) for x in gc]
        decay = each(lambda x, m: jnp.exp(jnp.where(m > 0.5, x - x.T, NEG_INF)), gcol, incls)
        qk = each(lambda a, b, kk: lax.dot_general(jnp.concatenate([lift_b(a), lift_b(b)], axis=0),
                                                   lift_b(kk), (((1,), (1,)), ((), ())),
                                                   preferred_element_type=F32), q, kb, k)
        attn = each(lambda x, d: (x[:BD] * d).astype(BF16), qk, decay)
        n_mat = each(lambda x, d, m: -(x[BD:] * d) * m, qk, decay, stricts)
        t_inv = [eye_f + x for x in n_mat]
        n_hi = [x.astype(BF16) for x in n_mat]
        m_b = n_hi
        for _ in range(int(math.log2(CHUNK)) - 1):
            m_b = [mm(x, x).astype(BF16) for x in m_b]
            t_inv = each(lambda t, m: t + mm(t.astype(BF16), m), t_inv, m_b)
        n_lo = each(lambda x, h: (x - h.astype(F32)).astype(BF16), n_mat, n_hi)
        x_hi = [t.astype(BF16) for t in t_inv]
        x_lo = each(lambda t, h: (t - h.astype(F32)).astype(BF16), t_inv, x_hi)
        nx = each(lambda h, lo, x: mm(jnp.concatenate([h, lo], axis=0), x), n_hi, n_lo, x_hi)
        nxl = each(mm, n_hi, x_lo)
        resid = each(lambda t, a, b: (eye_f - t + a[:BD] + a[BD:] + b).astype(BF16), t_inv, nx, nxl)
        t_b = each(lambda t, h, rr: (t + mm(h, rr)).astype(BF16), t_inv, x_hi, resid)
        rhs = each(lambda vv, b, kk, e: jnp.concatenate([lift_b(vv * b), lift_b(kk * e)], axis=1),
                   v, beta, kb, eg)
        sol = each(mm, t_b, rhs)
        s_b = [s.astype(BF16) for s in ss]
        ps = each(lambda so, qq, e, sb: mm(jnp.concatenate([so[:, BD:].astype(BF16), lift_b(qq * e)],
                                                           axis=0), sb), sol, q, eg, s_b)
        v_new = each(lambda so, p: (so[:, :BD] - p[:BD]).astype(BF16), sol, ps)
        o_bd = each(lambda p, a, vn: p[BD:] + mm(a, vn), ps, attn, v_new)
        o_tm = [x[0:CHUNK] + x[CHUNK:2 * CHUNK] + x[2 * CHUNK:3 * CHUNK] + x[3 * CHUNK:] for x in o_bd]
        k_tail = each(lambda kk, a, b: lift(kk * jnp.exp(a - b)).T.astype(BF16), k, gl, gc)
        s_new = each(lambda s, a, kt, vn: s * jnp.exp(a) + mm(kt, vn), ss, gl, k_tail, v_new)
        return o_tm, s_new

    chains = [(si, di) for si in range(n_sub) for di in range(2)]

    def body(ci, carry):
        offs = (pl.multiple_of(ci * CHUNK, CHUNK), pl.multiple_of((n_chunks - 1 - ci) * CHUNK, CHUNK))
        u_refs, gb_refs, o_refs = (uf_ref, ub_ref), (gbf_ref, gbb_ref), (of_ref, ob_ref)
        o_tm, s_new = chunks(
            [u_refs[di][si, pl.ds(offs[di], CHUNK), :] for si, di in chains],
            [gb_refs[di][si, pl.ds(offs[di], CHUNK), :] for si, di in chains],
            [e_ref[di] for si, di in chains],
            [s_scr[si, di] for si, di in chains],
            *[[dirs[di][n] for si, di in chains] for n in range(4)])
        for (si, di), o, s in zip(chains, o_tm, s_new):
            o_refs[di][si, pl.ds(offs[di], CHUNK), :] = o
            s_scr[si, di] = s
        return carry

    lax.fori_loop(0, n_chunks, body, 0)

    @pl.when(j == pl.num_programs(1) - 1)
    def _():
        sf_ref[...] = s_scr[...]


def _deltanet(u, gb, e_mat, s0_bd, seq_len, rows, n_sub=2):
    t = u.shape[0]
    n_seq = t // seq_len
    nb = seq_len // rows
    kern = functools.partial(_dn_kernel, n_chunks=rows // CHUNK, n_sub=n_sub)
    fwd = lambda s, j: (s, j, 0)
    bwd = lambda s, j: (s, nb - 1 - j, 0)
    state = pl.BlockSpec((n_sub, 2, BD, BD), lambda s, j: (s, 0, 0, 0))
    o_f, o_b, s_fin = pl.pallas_call(
        kern,
        grid=(n_seq // n_sub, nb),
        in_specs=[pl.BlockSpec((n_sub, rows, 3 * W_DN), fwd), pl.BlockSpec((n_sub, rows, 3 * W_DN), bwd),
                  pl.BlockSpec((n_sub, rows, AB_PAD), fwd), pl.BlockSpec((n_sub, rows, AB_PAD), bwd),
                  pl.BlockSpec((2, AB_PAD, 2 * W_DN), lambda s, j: (0, 0, 0)), state],
        out_specs=[pl.BlockSpec((n_sub, rows, W_DN), fwd), pl.BlockSpec((n_sub, rows, W_DN), bwd), state],
        out_shape=[jax.ShapeDtypeStruct((n_seq, seq_len, W_DN), F32),
                   jax.ShapeDtypeStruct((n_seq, seq_len, W_DN), F32),
                   jax.ShapeDtypeStruct((n_seq, 2, BD, BD), F32)],
        scratch_shapes=[pltpu.VMEM((n_sub, 2, BD, BD), F32)],
        compiler_params=_params("parallel", "arbitrary"),
        name="deltanet",
    )(*[x.reshape(n_seq, seq_len, -1) for x in (u, u, gb, gb)], e_mat, s0_bd)
    return o_f.reshape(t, W_DN), o_b.reshape(t, W_DN), s_fin


def _diff_lambda(lam_ref, lam_init):
    lp = lam_ref[...]
    return (jnp.exp(jnp.sum(lp[0:1] * lp[1:2], axis=1, keepdims=True))
            - jnp.exp(jnp.sum(lp[2:3] * lp[3:4], axis=1, keepdims=True)) + lam_init)


def _sub_rms(o, subln, lam_init):
    ms = jnp.mean(o * o, axis=-1, keepdims=True)
    return o * lax.rsqrt(ms + RMS_EPS) * subln * (1.0 - lam_init)


def _ctx_attn_kernel(na_ref, df_ref, lam_ref, subln_ref, oa_ref, oc_ref, *, lam_init):
    lam = _diff_lambda(lam_ref, lam_init)
    subln = subln_ref[...]
    n = na_ref.shape[0]
    qa = (na_ref[:, :W_NA] * NA_Q_SCALE).astype(BF16)
    ka = na_ref[:, W_NA:2 * W_NA].astype(BF16)
    va = na_ref[:, 2 * W_NA:].astype(BF16)
    for h in range(H_NA):
        sl = slice(h * HEAD_DIM, (h + 1) * HEAD_DIM)
        s = _dot_nt(qa[:, sl], ka[:, sl])
        p = jnp.exp2(s - jnp.max(s, axis=-1, keepdims=True))
        oa_ref[:, sl] = _dot(p, va[:, sl]) / jnp.sum(p, axis=-1, keepdims=True)
    qc = (df_ref[:, :W_DF] * DF_Q_SCALE).astype(BF16)
    kc = df_ref[:, W_DF:2 * W_DF].astype(BF16)
    vc = df_ref[:, 2 * W_DF:].astype(BF16)
    for h in range(H_DF):
        sl = slice(h * HEAD_DIM, (h + 1) * HEAD_DIM)
        es, dens = [], []
        for i in range(2):
            b = h * HEAD_DIM + i * DF_QK
            s = _dot_nt(qc[:, b:b + DF_QK], kc[:, b:b + DF_QK])
            e = jnp.exp2(s - jnp.max(s, axis=-1, keepdims=True))
            es.append(e.astype(BF16))
            dens.append(jnp.sum(e, axis=-1, keepdims=True))
        ov = _dot(jnp.concatenate(es, axis=0), vc[:, sl])
        o = ov[:n] / dens[0] - lam * (ov[n:] / dens[1])
        oc_ref[:, sl] = _sub_rms(o, subln, lam_init)


def _ctx_attn(qkv_na, qkv_df, lam_p, subln_row, layer):
    lam_init = 0.8 - 0.6 * math.exp(-0.3 * layer)
    kern = functools.partial(_ctx_attn_kernel, lam_init=lam_init)
    return pl.pallas_call(
        kern,
        grid=(BATCH,),
        in_specs=[pl.BlockSpec((SEQ, 3 * W_NA), lambda b: (b, 0)),
                  pl.BlockSpec((SEQ, 3 * W_DF), lambda b: (b, 0)),
                  pl.BlockSpec((4, DF_QK), lambda b: (0, 0)),
                  pl.BlockSpec((1, HEAD_DIM), lambda b: (0, 0))],
        out_specs=[pl.BlockSpec((SEQ, W_NA), lambda b: (b, 0)),
                   pl.BlockSpec((SEQ, W_DF), lambda b: (b, 0))],
        out_shape=[jax.ShapeDtypeStruct((T_CTX, W_NA), F32),
                   jax.ShapeDtypeStruct((T_CTX, W_DF), F32)],
        compiler_params=_params("parallel"),
        name="ctx_attn",
    )(qkv_na, qkv_df, lam_p, subln_row)


NA_R = 4
NA_U = NA_R + WIN_R - 1
NA_GROUPS = GRID_H // NA_R
NA_Q_SCALE = HEAD_DIM ** -0.5 * math.log2(math.e)


def _na_key_start(g):
    return jnp.clip(g * NA_R - WIN_R // 2, 0, GRID_H - NA_U)


def _na_bias_table(rpb):
    qc = np.arange(GRID_W)[:, None]
    kc = np.arange(GRID_W)[None, :]
    cs = np.clip(qc - WIN_C // 2, 0, GRID_W - WIN_C)
    valid = (kc >= cs) & (kc < cs + WIN_C)
    dc = np.clip(kc - qc + (WIN_C - 1), 0, 2 * WIN_C - 2)
    onehot = (dc[None] == np.arange(2 * WIN_C - 1)[:, None, None]).astype(np.float32)
    x = jnp.einsum('hrd,dqk->hrqk', rpb.astype(F32), jnp.asarray(onehot), precision=HIGHEST)
    x = jnp.where(valid[None, None], x * math.log2(math.e), NEG_INF)
    masked = jnp.full((H_NA, GRID_W, GRID_W), NEG_INF, F32)
    classes = []
    for g in (0, 1, NA_GROUPS - 1):
        u0 = int(np.clip(g * NA_R - WIN_R // 2, 0, GRID_H - NA_U))
        rows = []
        for i in range(NA_R):
            r = g * NA_R + i
            w0 = int(np.clip(r - WIN_R // 2, 0, GRID_H - WIN_R))
            blocks = [x[:, u0 + u - r + WIN_R - 1] if w0 <= u0 + u < w0 + WIN_R else masked
                      for u in range(NA_U)]
            rows.append(jnp.stack(blocks, axis=2))
        classes.append(jnp.stack(rows, axis=1))
    return jnp.stack(classes, axis=1).reshape(H_NA, 3, NA_R * GRID_W, NA_U * GRID_W)


def _lat_na_kernel(q_ref, k_ref, v_ref, kc_ref, vc_ref, bias_ref, o_ref):
    g = pl.program_id(1)
    start = pl.multiple_of(_na_key_start(g) * GRID_W, GRID_W)
    q = (q_ref[...] * NA_Q_SCALE).astype(BF16)
    kw = k_ref[pl.ds(start, NA_U * GRID_W), :].astype(BF16)
    vw = v_ref[pl.ds(start, NA_U * GRID_W), :].astype(BF16)
    kc = kc_ref[...]
    vc = vc_ref[...]
    for h in range(H_NA):
        sl = slice(h * HEAD_DIM, (h + 1) * HEAD_DIM)
        qh = q[:, sl]
        s_loc = _dot_nt(qh, kw[:, sl]) + bias_ref[h]
        s_ctx = _dot_nt(qh, kc[:, sl])
        m = jnp.maximum(jnp.max(s_loc, axis=-1, keepdims=True), jnp.max(s_ctx, axis=-1, keepdims=True))
        p_loc = jnp.exp2(s_loc - m)
        p_ctx = jnp.exp2(s_ctx - m)
        den = jnp.sum(p_loc, axis=-1, keepdims=True) + jnp.sum(p_ctx, axis=-1, keepdims=True)
        o_ref[:, sl] = (_dot(p_loc, vw[:, sl]) + _dot(p_ctx, vc[:, sl])) / den


def _lat_na(qkv_na, ck, cv, bias_tab):
    rows = NA_R * GRID_W
    cls = lambda g: jnp.where(g == 0, 0, jnp.where(g == NA_GROUPS - 1, 2, 1))
    return pl.pallas_call(
        _lat_na_kernel,
        grid=(DEC_BATCH, NA_GROUPS),
        in_specs=[pl.BlockSpec((rows, W_NA), lambda b, g: (b * NA_GROUPS + g, 0)),
                  pl.BlockSpec((DEC_SEQ, W_NA), lambda b, g: (b, 1)),
                  pl.BlockSpec((DEC_SEQ, W_NA), lambda b, g: (b, 2)),
                  pl.BlockSpec((None, PAST_LEN, W_NA), lambda b, g: (b, 0, 0)),
                  pl.BlockSpec((None, PAST_LEN, W_NA), lambda b, g: (b, 0, 0)),
                  pl.BlockSpec((H_NA, None, rows, NA_U * GRID_W), lambda b, g: (0, cls(g), 0, 0))],
        out_specs=pl.BlockSpec((rows, W_NA), lambda b, g: (b * NA_GROUPS + g, 0)),
        out_shape=jax.ShapeDtypeStruct((T_LAT, W_NA), F32),
        compiler_params=_params("parallel", "arbitrary"),
        name="lat_na",
    )(qkv_na, qkv_na, qkv_na, ck, cv, bias_tab)


def _rope_tables():
    nf = DF_QK // 4
    inv = ROPE_BASE ** (-np.arange(nf, dtype=np.float32) / nf)
    t = np.arange(DEC_SEQ)
    pos = np.stack([t // GRID_W, t % GRID_W], axis=-1).astype(np.float32)
    ang = jnp.asarray(pos[:, :, None] * inv)
    cos, sin = jnp.cos(ang), jnp.sin(ang)
    cos32 = jnp.concatenate([cos, cos], axis=-1).reshape(DEC_SEQ, DF_QK)
    sin32 = jnp.concatenate([-sin, sin], axis=-1).reshape(DEC_SEQ, DF_QK)
    reps = W_DF // DF_QK
    rot = np.zeros((W_DF, W_DF), np.float32)
    for dd in range(W_DF):
        rot[dd + nf if dd % (2 * nf) < nf else dd - nf, dd] = 1.0
    return jnp.tile(cos32, (1, reps)), jnp.tile(sin32, (1, reps)), jnp.asarray(rot)


def _rope_kernel(df_ref, cos_ref, sin_ref, rot_ref, q_ref, kt_ref, v_ref):
    cos, sin, rot = cos_ref[...], sin_ref[...], rot_ref[...]
    q = df_ref[:, :W_DF]
    k = df_ref[:, W_DF:2 * W_DF]
    q_ref[...] = ((q * cos + _dot_exact(q, rot) * sin) * DF_Q_SCALE).astype(BF16)
    kt_ref[...] = (k * cos + _dot_exact(k, rot) * sin).T.astype(BF16)
    lane = lax.broadcasted_iota(jnp.int32, (q.shape[0], V_AUG - HEAD_DIM), 1)
    one_col = jnp.where(lane == 0, 1.0, 0.0).astype(BF16)
    for h in range(H_DF):
        a = 2 * W_DF + h * HEAD_DIM
        v_ref[:, h * V_AUG:h * V_AUG + HEAD_DIM] = df_ref[:, a:a + HEAD_DIM].astype(BF16)
        v_ref[:, h * V_AUG + HEAD_DIM:(h + 1) * V_AUG] = one_col


def _rope(qkv_df, cos, sin, rot, tm=512):
    nb = DEC_SEQ // tm
    return pl.pallas_call(
        _rope_kernel,
        grid=(DEC_BATCH, nb),
        in_specs=[pl.BlockSpec((tm, 3 * W_DF), lambda b, i: (b * nb + i, 0)),
                  pl.BlockSpec((tm, W_DF), lambda b, i: (i, 0)),
                  pl.BlockSpec((tm, W_DF), lambda b, i: (i, 0)),
                  pl.BlockSpec((W_DF, W_DF), lambda b, i: (0, 0))],
        out_specs=[pl.BlockSpec((tm, W_DF), lambda b, i: (b * nb + i, 0)),
                   pl.BlockSpec((None, W_DF, tm), lambda b, i: (b, 0, i)),
                   pl.BlockSpec((tm, H_DF * V_AUG), lambda b, i: (b * nb + i, 0))],
        out_shape=[jax.ShapeDtypeStruct((T_LAT, W_DF), BF16),
                   jax.ShapeDtypeStruct((DEC_BATCH, W_DF, DEC_SEQ), BF16),
                   jax.ShapeDtypeStruct((T_LAT, H_DF * V_AUG), BF16)],
        compiler_params=_params("parallel", "parallel"),
        name="rope",
    )(qkv_df, cos, sin, rot)


def _lat_df_kernel(q_ref, kt_ref, v_ref, lam_ref, subln_ref, o_ref, *, lam_init, tq):
    q = q_ref[...]
    lam = _diff_lambda(lam_ref, lam_init)
    subln = subln_ref[...]
    for h in range(H_DF):
        es = []
        for i in range(2):
            a = h * HEAD_DIM + i * DF_QK
            s = jnp.dot(q[:, a:a + DF_QK], kt_ref[a:a + DF_QK, :], preferred_element_type=F32)
            es.append(jnp.exp2(s - jnp.max(s, axis=-1, keepdims=True)).astype(BF16))
        ov = jnp.dot(jnp.concatenate(es, axis=0), v_ref[:, h * V_AUG:(h + 1) * V_AUG],
                     preferred_element_type=F32)
        o = (ov[:tq, :HEAD_DIM] / ov[:tq, HEAD_DIM:HEAD_DIM + 1]
             - lam * (ov[tq:, :HEAD_DIM] / ov[tq:, HEAD_DIM:HEAD_DIM + 1]))
        o_ref[:, h * HEAD_DIM:(h + 1) * HEAD_DIM] = _sub_rms(o, subln, lam_init)


def _lat_df(q_r, kt_all, v_all, lam_p, subln_row, layer, tq=256):
    lam_init = 0.8 - 0.6 * math.exp(-0.3 * layer)
    nb = DEC_SEQ // tq
    n_keys = kt_all.shape[2]
    kern = functools.partial(_lat_df_kernel, lam_init=lam_init, tq=tq)
    return pl.pallas_call(
        kern,
        grid=(DEC_BATCH, nb),
        in_specs=[pl.BlockSpec((tq, W_DF), lambda b, i: (b * nb + i, 0)),
                  pl.BlockSpec((None, W_DF, n_keys), lambda b, i: (b, 0, 0)),
                  pl.BlockSpec((None, n_keys, H_DF * V_AUG), lambda b, i: (b, 0, 0)),
                  pl.BlockSpec((4, DF_QK), lambda b, i: (0, 0)),
                  pl.BlockSpec((1, HEAD_DIM), lambda b, i: (0, 0))],
        out_specs=pl.BlockSpec((tq, W_DF), lambda b, i: (b * nb + i, 0)),
        out_shape=jax.ShapeDtypeStruct((T_LAT, W_DF), F32),
        compiler_params=_params("parallel", "arbitrary"),
        name="lat_df",
    )(q_r, kt_all, v_all, lam_p, subln_row)


def _outproj_kernel(oa_ref, of_ref, obk_ref, gate_ref, onorm_ref, grp_ref, oc_ref, y_ref, g1_ref, w_ref,
                    lg_ref, lb_ref, o_ref):
    ob = of_ref[...] + obk_ref[...]
    ms = _dot_exact(ob * ob, grp_ref[...]) * (1.0 / HEAD_DIM)
    ob = ob * lax.rsqrt(ms + RMS_EPS) * onorm_ref[...] * _silu(gate_ref[...])
    o = (jnp.dot(oa_ref[...].astype(BF16), w_ref[:W_NA, :], preferred_element_type=F32)
         + jnp.dot(ob.astype(BF16), w_ref[W_NA:W_NA + W_DN, :], preferred_element_type=F32)
         + jnp.dot(oc_ref[...].astype(BF16), w_ref[W_NA + W_DN:, :], preferred_element_type=F32))
    o_ref[...] = _layer_norm(ALPHA * y_ref[...] + g1_ref[...] * o, lg_ref[...], lb_ref[...])


def _outproj(oa, o_fwd, o_bwd, gate, onorm_row, grp, oc, y, mod, w, ln_g, ln_b, cond_fn, tm=512):
    t = y.shape[0]
    row = lambda n: pl.BlockSpec((tm, n), lambda i: (i, 0))
    vec = pl.BlockSpec((1, D_MODEL), lambda i: (0, 0))
    return pl.pallas_call(
        _outproj_kernel,
        grid=(t // tm,),
        in_specs=[row(W_NA), row(W_DN), row(W_DN),
                  row(W_DN), pl.BlockSpec((1, W_DN), lambda i: (0, 0)),
                  pl.BlockSpec((W_DN, W_DN), lambda i: (0, 0)),
                  row(W_DF), row(D_MODEL), _mod_spec(2, cond_fn),
                  pl.BlockSpec((D_MODEL, D_MODEL), lambda i: (0, 0)), vec, vec],
        out_specs=row(D_MODEL),
        out_shape=jax.ShapeDtypeStruct((t, D_MODEL), F32),
        compiler_params=_params("parallel"),
        name="outproj_ln",
    )(oa, o_fwd, o_bwd, gate, onorm_row, grp, oc, y, mod, w, ln_g, ln_b)


MOE_TM = 1024
MOE_CAP = 320


def _router_kernel(y_ref, sh_ref, sc_ref, w_ref, b_ref, tri_ref, g_ref, rk_ref, rkt_ref, cnt_ref):
    h = y_ref[...] * (1.0 + sc_ref[...]) + sh_ref[...]
    logits = _dot_exact(h, w_ref[...]) + b_ref[...]
    lane = lax.broadcasted_iota(jnp.int32, logits.shape, 1).astype(F32)
    logits = jnp.where(lane < N_EXPERTS, logits, -jnp.inf)
    m1 = jnp.max(logits, axis=-1, keepdims=True)
    i1 = jnp.min(jnp.where(logits == m1, lane, float(LANES)), axis=-1, keepdims=True)
    rest = jnp.where(lane == i1, -jnp.inf, logits)
    m2 = jnp.max(rest, axis=-1, keepdims=True)
    i2 = jnp.min(jnp.where(rest == m2, lane, float(LANES)), axis=-1, keepdims=True)
    e2 = jnp.exp(m2 - m1)
    w1 = 1.0 / (1.0 + e2)
    g_ref[...] = jnp.where(lane == i1, w1, 0.0) + jnp.where(lane == i2, e2 * w1, 0.0)
    routed = jnp.where(lane == i1, 1.0, 0.0) + jnp.where(lane == i2, 1.0, 0.0)
    before = jnp.dot(tri_ref[...], routed.astype(BF16), preferred_element_type=F32)
    rank = jnp.where(routed > 0.5, before, -1.0)
    rk_ref[...] = rank
    rkt_ref[...] = rank.T[:N_EXPERTS, :]
    cnt_ref[...] = jnp.sum(routed, axis=0, keepdims=True)


def _router(y, mod, w_pad, b_pad, tri, cond_fn, tm=MOE_TM):
    t = y.shape[0]
    nt = t // tm
    return pl.pallas_call(
        _router_kernel,
        grid=(nt,),
        in_specs=[pl.BlockSpec((tm, D_MODEL), lambda i: (i, 0)),
                  _mod_spec(3, cond_fn), _mod_spec(4, cond_fn),
                  pl.BlockSpec((D_MODEL, LANES), lambda i: (0, 0)),
                  pl.BlockSpec((1, LANES), lambda i: (0, 0)),
                  pl.BlockSpec((tm, tm), lambda i: (0, 0))],
        out_specs=[pl.BlockSpec((tm, LANES), lambda i: (i, 0)),
                   pl.BlockSpec((tm, LANES), lambda i: (i, 0)),
                   pl.BlockSpec((None, N_EXPERTS, tm), lambda i: (i, 0, 0)),
                   pl.BlockSpec((None, 1, LANES), lambda i: (i, 0, 0))],
        out_shape=[jax.ShapeDtypeStruct((t, LANES), F32),
                   jax.ShapeDtypeStruct((t, LANES), F32),
                   jax.ShapeDtypeStruct((nt, N_EXPERTS, tm), F32),
                   jax.ShapeDtypeStruct((nt, 1, LANES), F32)],
        compiler_params=_params("parallel"),
        name="router",
    )(y, mod, mod, w_pad, b_pad, tri)


def _moe_kernel(cnt_ref, y_ref, sh_ref, sc_ref, g2_ref, gates_ref, rk_ref, rkt_ref, wg_ref, wu_ref, wd_ref,
                lg_ref, lb_ref, o_ref, h_scr, acc_scr, *, tm, cap):
    i = pl.program_id(0)
    e = pl.program_id(1)

    @pl.when(e == 0)
    def _():
        h_scr[...] = (y_ref[...] * (1.0 + sc_ref[...]) + sh_ref[...]).astype(BF16)
        acc_scr[...] = jnp.zeros_like(acc_scr)

    lane = lax.broadcasted_iota(jnp.int32, (tm, LANES), 1)
    gate_col = jnp.sum(jnp.where(lane == e, gates_ref[...], 0.0), axis=-1, keepdims=True)
    rank_col = jnp.sum(jnp.where(lane == e, rk_ref[...], 0.0), axis=-1, keepdims=True)
    rank_row = rkt_ref[pl.ds(e, 1), :]
    n_pass = (cnt_ref[i * N_EXPERTS + e] + cap - 1) // cap

    def body(ps, carry):
        base = (ps * cap).astype(F32)
        slot_r = lax.broadcasted_iota(jnp.int32, (cap, tm), 0).astype(F32) + base
        sel = jnp.where(rank_row == slot_r, 1.0, 0.0).astype(BF16)
        xe = jnp.dot(sel, h_scr[...], preferred_element_type=F32).astype(BF16)
        a = jnp.dot(xe, wg_ref[...], preferred_element_type=F32)
        b = jnp.dot(xe, wu_ref[...], preferred_element_type=F32)
        f = jnp.dot((_silu(a) * b).astype(BF16), wd_ref[...], preferred_element_type=F32).astype(BF16)
        slot_c = lax.broadcasted_iota(jnp.int32, (tm, cap), 1).astype(F32) + base
        sel_t = jnp.where(rank_col == slot_c, 1.0, 0.0).astype(BF16)
        acc_scr[...] += gate_col * jnp.dot(sel_t, f, preferred_element_type=F32)
        return carry

    lax.fori_loop(0, n_pass, body, 0)

    @pl.when(e == N_EXPERTS - 1)
    def _():
        o_ref[...] = _layer_norm(ALPHA * y_ref[...] + g2_ref[...] * acc_scr[...],
                                 lg_ref[...], lb_ref[...])


def _moe(y, mod, gates, rank, rank_t, counts, wg, wu, wd, ln_g, ln_b, cond_fn, tm=MOE_TM, cap=MOE_CAP):
    t = y.shape[0]
    ff = wg.shape[2]
    vec = pl.BlockSpec((1, D_MODEL), lambda i, e, c: (0, 0))
    tok = lambda n: pl.BlockSpec((tm, n), lambda i, e, c: (i, 0))
    kern = functools.partial(_moe_kernel, tm=tm, cap=cap)
    return pl.pallas_call(
        kern,
        grid_spec=pltpu.PrefetchScalarGridSpec(
            num_scalar_prefetch=1,
            grid=(t // tm, N_EXPERTS),
            in_specs=[tok(D_MODEL), _mod_spec(3, cond_fn), _mod_spec(4, cond_fn), _mod_spec(5, cond_fn),
                      tok(LANES), tok(LANES),
                      pl.BlockSpec((None, N_EXPERTS, tm), lambda i, e, c: (i, 0, 0)),
                      pl.BlockSpec((None, D_MODEL, ff), lambda i, e, c: (e, 0, 0)),
                      pl.BlockSpec((None, D_MODEL, ff), lambda i, e, c: (e, 0, 0)),
                      pl.BlockSpec((None, ff, D_MODEL), lambda i, e, c: (e, 0, 0)),
                      vec, vec],
            out_specs=tok(D_MODEL),
            scratch_shapes=[pltpu.VMEM((tm, D_MODEL), BF16), pltpu.VMEM((tm, D_MODEL), F32)]),
        out_shape=jax.ShapeDtypeStruct((t, D_MODEL), F32),
        compiler_params=_params("parallel", "arbitrary"),
        name="moe_ln",
    )(counts, y, mod, mod, mod, gates, rank, rank_t, wg, wu, wd, ln_g, ln_b)


def _ffn_kernel(y_ref, sh_ref, sc_ref, g2_ref, gates_ref, wg_ref, wu_ref, wd_ref, lg_ref, lb_ref,
                o_ref, h_scr, acc_scr, *, n_blocks):
    e = pl.program_id(1)

    @pl.when(e == 0)
    def _():
        h_scr[...] = (y_ref[...] * (1.0 + sc_ref[...]) + sh_ref[...]).astype(BF16)
        acc_scr[...] = jnp.zeros_like(acc_scr)

    h = h_scr[...]
    a = jnp.dot(h, wg_ref[...], preferred_element_type=F32)
    b = jnp.dot(h, wu_ref[...], preferred_element_type=F32)
    f = jnp.dot((_silu(a) * b).astype(BF16), wd_ref[...], preferred_element_type=F32)
    lane = lax.broadcasted_iota(jnp.int32, gates_ref.shape, 1)
    gate = jnp.sum(jnp.where(lane == e, gates_ref[...], 0.0), axis=-1, keepdims=True)
    acc_scr[...] += gate * f

    @pl.when(e == n_blocks - 1)
    def _():
        o_ref[...] = _layer_norm(ALPHA * y_ref[...] + g2_ref[...] * acc_scr[...],
                                 lg_ref[...], lb_ref[...])


def _ffn(y, mod, gates, wg, wu, wd, ln_g, ln_b, cond_fn, tm=512):
    t = y.shape[0]
    n_blocks, _, ff = wg.shape
    vec = pl.BlockSpec((1, D_MODEL), lambda i, e: (0, 0))
    kern = functools.partial(_ffn_kernel, n_blocks=n_blocks)
    return pl.pallas_call(
        kern,
        grid=(t // tm, n_blocks),
        in_specs=[pl.BlockSpec((tm, D_MODEL), lambda i, e: (i, 0)),
                  _mod_spec(3, cond_fn), _mod_spec(4, cond_fn), _mod_spec(5, cond_fn),
                  pl.BlockSpec((tm, LANES), lambda i, e: (i, 0)),
                  pl.BlockSpec((None, D_MODEL, ff), lambda i, e: (e, 0, 0)),
                  pl.BlockSpec((None, D_MODEL, ff), lambda i, e: (e, 0, 0)),
                  pl.BlockSpec((None, ff, D_MODEL), lambda i, e: (e, 0, 0)),
                  vec, vec],
        out_specs=pl.BlockSpec((tm, D_MODEL), lambda i, e: (i, 0)),
        out_shape=jax.ShapeDtypeStruct((t, D_MODEL), F32),
        scratch_shapes=[pltpu.VMEM((tm, D_MODEL), BF16), pltpu.VMEM((tm, D_MODEL), F32)],
        compiler_params=_params("parallel", "arbitrary"),
        name="ffn_ln",
    )(y, mod, mod, mod, gates, wg, wu, wd, ln_g, ln_b)


def _permute_w_in(w):
    offs = np.cumsum((0,) + PROJ_SIZES)
    qa, ka, va, qkv, gate, a, b, qc, kc, vc = (w[:, offs[i]:offs[i + 1]] for i in range(10))
    pad = jnp.zeros((D_MODEL, AB_PAD - 4 * H_DN), w.dtype)
    return jnp.concatenate([qa, ka, va, qkv, gate, qc, kc, vc, a, b, pad], axis=1).astype(BF16)


def _pad_row(v, n=AB_PAD):
    v = v.reshape(1, -1)
    return jnp.pad(v, ((0, 0), (0, n - v.shape[1])))


def _group_matrix(n):
    idx = np.arange(n) // HEAD_DIM
    return jnp.asarray((idx[:, None] == idx[None, :]).astype(np.float32))


def _expand_matrix():
    e = np.zeros((2, AB_PAD, 2 * W_DN), np.float32)
    for d in range(2):
        for h in range(H_DN):
            e[d, d * H_DN + h, h * HEAD_DIM:(h + 1) * HEAD_DIM] = 1.0
            e[d, 2 * H_DN + d * H_DN + h, W_DN + h * HEAD_DIM:W_DN + (h + 1) * HEAD_DIM] = 1.0
    return jnp.asarray(e)


def _state_to_bd(s):
    eye = jnp.eye(H_DN, dtype=s.dtype)
    return jnp.einsum('sdhkv,hg->sdhkgv', s, eye).reshape(s.shape[0], 2, BD, BD)


def _bd_to_state(s):
    n = s.shape[0]
    return jnp.einsum('sdhkhv->sdhkv', s.reshape(n, 2, H_DN, HEAD_DIM, H_DN, HEAD_DIM))


def kernel(x_prompt, x_sample, cache_na_k, cache_na_v, cache_df_k, cache_df_v, state_dn, c, c_ctx,
           ada_w, ada_b, w_in, conv_dn, a_log_dn, dt_bias_dn, onorm_dn, rpb_na, lambda_df, subln_df,
           w_out, ln1_g, ln1_b, ln2_g, ln2_b, ffn_w_gate, ffn_w_up, ffn_w_down, router_w, router_b,
           moe_w_gate, moe_w_up, moe_w_down):
    conds = jnp.concatenate([c_ctx[None, :], c, jnp.zeros((N_COND - 1 - DEC_BATCH, D_MODEL), F32)], axis=0)
    mods = _ada_table(conds, ada_w, ada_b)
    ctx_cond = lambda i: 0
    lat_cond = lambda i: 1 + (i * 512) // DEC_SEQ
    lat_cond_moe = lambda i: 1 + (i * MOE_TM) // DEC_SEQ
    tri = jnp.asarray(np.tril(np.ones((MOE_TM, MOE_TM), np.float32), -1), dtype=BF16)

    grp512, grp256 = _group_matrix(2 * W_DN), _group_matrix(W_DN)
    e_mat = _expand_matrix()
    cos, sin, rot = _rope_tables()
    ones_gates = jnp.ones((T_CTX, LANES), F32)
    one_col = jnp.zeros((DEC_BATCH, PAST_LEN, H_DF, V_AUG - HEAD_DIM), BF16).at[..., 0].set(1.0)

    y_ctx = x_prompt.reshape(T_CTX, D_MODEL)
    y_lat = x_sample.reshape(T_LAT, D_MODEL)
    ctx_out = []
    for l in range(DEPTH):
        mod = mods[l]
        w_in_l = _permute_w_in(w_in[l])
        w_out_l = w_out[l].astype(BF16)
        conv_w = jnp.pad(conv_dn[l], ((0, 8 - CONV_K), (0, 0)))
        alog_row = _pad_row(a_log_dn[l])
        dtb_row = _pad_row(dt_bias_dn[l])
        onorm_row = jnp.tile(onorm_dn[l], H_DN).reshape(1, W_DN)
        subln_row = subln_df[l].reshape(1, HEAD_DIM)
        lg1, lb1 = ln1_g[l].reshape(1, D_MODEL), ln1_b[l].reshape(1, D_MODEL)
        lg2, lb2 = ln2_g[l].reshape(1, D_MODEL), ln2_b[l].reshape(1, D_MODEL)

        na, dn, gate, df, ab = _inproj(y_ctx, mod, w_in_l, ctx_cond)
        u, gb = _dn_prep(dn, ab, conv_w, alog_row, dtb_row, grp512, SEQ)
        s0 = jnp.zeros((BATCH, 2, BD, BD), F32)
        o_f, o_b, s_fin = _deltanet(u, gb, e_mat, s0, SEQ, SEQ)
        oa, oc = _ctx_attn(na, df, lambda_df[l], subln_row, l)
        y1_ctx = _outproj(oa, o_f, o_b, gate, onorm_row, grp256, oc, y_ctx, mod, w_out_l, lg1, lb1, ctx_cond)
        ctx_out.append((na[:, W_NA:2 * W_NA].reshape(BATCH, SEQ, H_NA, HEAD_DIM),
                        na[:, 2 * W_NA:].reshape(BATCH, SEQ, H_NA, HEAD_DIM),
                        df[:, W_DF:2 * W_DF].reshape(BATCH, SEQ, H_DF, HEAD_DIM),
                        df[:, 2 * W_DF:].reshape(BATCH, SEQ, H_DF, HEAD_DIM),
                        _bd_to_state(s_fin)))

        na, dn, gate, df, ab = _inproj(y_lat, mod, w_in_l, lat_cond)
        u, gb = _dn_prep(dn, ab, conv_w, alog_row, dtb_row, grp512, DEC_SEQ)
        o_f, o_b, _ = _deltanet(u, gb, e_mat, _state_to_bd(state_dn[:, l]), DEC_SEQ, 512)
        oa = _lat_na(na, cache_na_k[:, l].reshape(DEC_BATCH, PAST_LEN, W_NA).astype(BF16),
                     cache_na_v[:, l].reshape(DEC_BATCH, PAST_LEN, W_NA).astype(BF16),
                     _na_bias_table(rpb_na[l]))
        q_r, kt, v_b = _rope(df, cos, sin, rot)
        ktc = jnp.swapaxes(cache_df_k[:, l].reshape(DEC_BATCH, PAST_LEN, W_DF), 1, 2).astype(BF16)
        vc = jnp.concatenate([cache_df_v[:, l].astype(BF16), one_col], axis=-1)
        kt_all = jnp.concatenate([kt, ktc], axis=2)
        v_all = jnp.concatenate([v_b.reshape(DEC_BATCH, DEC_SEQ, H_DF * V_AUG),
                                 vc.reshape(DEC_BATCH, PAST_LEN, H_DF * V_AUG)], axis=1)
        oc = _lat_df(q_r, kt_all, v_all, lambda_df[l], subln_row, l)
        y1_lat = _outproj(oa, o_f, o_b, gate, onorm_row, grp256, oc, y_lat, mod, w_out_l, lg1, lb1, lat_cond)

        i = l // 2
        if l % 2 == 0:
            split = lambda w: jnp.transpose(w.reshape(D_MODEL, D_FF // MOE_FF, MOE_FF), (1, 0, 2))
            wg, wu = split(ffn_w_gate[i]).astype(BF16), split(ffn_w_up[i]).astype(BF16)
            wd = ffn_w_down[i].reshape(D_FF // MOE_FF, MOE_FF, D_MODEL).astype(BF16)
            y_ctx = _ffn(y1_ctx, mod, ones_gates, wg, wu, wd, lg2, lb2, ctx_cond)
            y_lat = _ffn(y1_lat, mod, ones_gates, wg, wu, wd, lg2, lb2, lat_cond)
        else:
            wg, wu, wd = moe_w_gate[i].astype(BF16), moe_w_up[i].astype(BF16), moe_w_down[i].astype(BF16)
            rw = jnp.pad(router_w[i], ((0, 0), (0, LANES - N_EXPERTS)))
            rb = _pad_row(router_b[i], LANES)
            ys = []
            for y1, cond in ((y1_ctx, ctx_cond), (y1_lat, lat_cond_moe)):
                gates, rank, rank_t, cnt = _router(y1, mod, rw, rb, tri, cond)
                counts = cnt[:, 0, :N_EXPERTS].astype(jnp.int32).reshape(-1)
                ys.append(_moe(y1, mod, gates, rank, rank_t, counts, wg, wu, wd, lg2, lb2, cond))
            y_ctx, y_lat = ys

    stack = lambda j: jnp.stack([t[j] for t in ctx_out], axis=1)
    return (y_ctx.reshape(BATCH, SEQ, D_MODEL), y_lat.reshape(DEC_BATCH, DEC_SEQ, D_MODEL),
            stack(0), stack(1), stack(2), stack(3), stack(4))
```

```python
import functools
import math

import jax
import jax.numpy as jnp
import numpy as np
from jax import lax
from jax.experimental import pallas as pl
from jax.experimental.pallas import tpu as pltpu

F32 = jnp.float32
BF16 = jnp.bfloat16
HIGHEST = lax.Precision.HIGHEST

D_MODEL = 1024
BATCH = 32
SEQ = 256
DEPTH = 2
DEC_BATCH = 2
DEC_SEQ = 4096
PAST_LEN = 512
GRID_W = 64
GRID_H = DEC_SEQ // GRID_W
HEAD_DIM = 64
H_NA = 6
H_DN = 4
H_DF = 6
W_NA = H_NA * HEAD_DIM
W_DN = H_DN * HEAD_DIM
W_DF = H_DF * HEAD_DIM
DF_QK = HEAD_DIM // 2
WIN_R = 8
WIN_C = 16
CONV_K = 5
CHUNK = 64
D_FF = 2816
N_EXPERTS = 8
MOE_FF = 1408
ALPHA = (2 * DEPTH) ** 0.25
LN_EPS = 1e-5
RMS_EPS = 1e-6
ROPE_BASE = 10000.0
NEG_INF = -1e30
PROJ_SIZES = (W_NA, W_NA, W_NA, 3 * W_DN, W_DN, 2 * H_DN, 2 * H_DN, W_DF, W_DF, W_DF)

T_CTX = BATCH * SEQ
T_LAT = DEC_BATCH * DEC_SEQ
N_COND = 8
LANES = 128
AB_PAD = LANES
SEG_NA = (0, 3 * W_NA)
SEG_DN = (SEG_NA[1], SEG_NA[1] + 3 * W_DN)
SEG_GATE = (SEG_DN[1], SEG_DN[1] + W_DN)
SEG_DF = (SEG_GATE[1], SEG_GATE[1] + 3 * W_DF)
SEG_AB = (SEG_DF[1], SEG_DF[1] + AB_PAD)
P_PAD = SEG_AB[1]
SEGS = (SEG_NA, SEG_DN, SEG_GATE, SEG_DF, SEG_AB)
BD = H_DN * CHUNK
V_AUG = LANES
DF_Q_SCALE = DF_QK ** -0.5 * math.log2(math.e)
VMEM_LIMIT = 56 * 1024 * 1024


def _params(*sem):
    return pltpu.CompilerParams(dimension_semantics=sem, vmem_limit_bytes=VMEM_LIMIT)


def _dot(a, b):
    return jnp.dot(a.astype(BF16), b.astype(BF16), preferred_element_type=F32)


def _dot_nt(a, b):
    return lax.dot_general(a.astype(BF16), b.astype(BF16), (((1,), (1,)), ((), ())),
                           preferred_element_type=F32)


def _split3(x):
    x1 = x.astype(BF16)
    r1 = x - x1.astype(F32)
    x2 = r1.astype(BF16)
    return x1, x2, (r1 - x2.astype(F32)).astype(BF16)


def _dot_sel(x, sel):
    m = x.shape[0]
    y = jnp.dot(jnp.concatenate(_split3(x), axis=0), sel, preferred_element_type=F32)
    return y[:m] + y[m:2 * m] + y[2 * m:]


def _sel_dot(sel, x):
    n = x.shape[1]
    y = jnp.dot(sel, jnp.concatenate(_split3(x), axis=1), preferred_element_type=F32)
    return y[:, :n] + y[:, n:2 * n] + y[:, 2 * n:]


def _dot_exact(a, b):
    return jnp.dot(a, b, precision=HIGHEST, preferred_element_type=F32)


def _silu(x):
    return x * jax.nn.sigmoid(x)


def _layer_norm(x, g, b):
    mu = jnp.mean(x, axis=-1, keepdims=True)
    xc = x - mu
    var = jnp.mean(xc * xc, axis=-1, keepdims=True)
    return xc * lax.rsqrt(var + LN_EPS) * g + b


def _ada_kernel(c_ref, w_ref, b_ref, o_ref):
    o_ref[...] = _dot_exact(_silu(c_ref[...]), w_ref[...]) + b_ref[...]


def _ada_table(conds, ada_w, ada_b):
    out = pl.pallas_call(
        _ada_kernel,
        grid=(DEPTH, 6),
        in_specs=[pl.BlockSpec((N_COND, D_MODEL), lambda l, k: (0, 0)),
                  pl.BlockSpec((None, D_MODEL, D_MODEL), lambda l, k: (l, 0, k)),
                  pl.BlockSpec((None, None, 1, D_MODEL), lambda l, k: (l, k, 0, 0))],
        out_specs=pl.BlockSpec((None, None, N_COND, D_MODEL), lambda l, k: (l, k, 0, 0)),
        out_shape=jax.ShapeDtypeStruct((DEPTH, 6, N_COND, D_MODEL), F32),
        compiler_params=_params("parallel", "parallel"),
        name="ada_table",
    )(conds, ada_w, ada_b.reshape(DEPTH, 6, 1, D_MODEL))
    return out.reshape(DEPTH, 6, N_COND, 1, D_MODEL)


def _mod_spec(k, cond_fn):
    return pl.BlockSpec((None, None, 1, D_MODEL), lambda i, *_: (k, cond_fn(i), 0, 0))


def _inproj_kernel(x_ref, sh_ref, sc_ref, w_ref, *o_refs):
    h = (x_ref[...] * (1.0 + sc_ref[...]) + sh_ref[...]).astype(BF16)
    for o_ref, (a, b) in zip(o_refs, SEGS):
        o_ref[...] = jnp.dot(h, w_ref[:, a:b], preferred_element_type=F32)


def _inproj(x, mod, w, cond_fn, tm=512):
    t = x.shape[0]
    return pl.pallas_call(
        _inproj_kernel,
        grid=(t // tm,),
        in_specs=[pl.BlockSpec((tm, D_MODEL), lambda i: (i, 0)),
                  _mod_spec(0, cond_fn), _mod_spec(1, cond_fn),
                  pl.BlockSpec((D_MODEL, P_PAD), lambda i: (0, 0))],
        out_specs=[pl.BlockSpec((tm, b - a), lambda i: (i, 0)) for a, b in SEGS],
        out_shape=[jax.ShapeDtypeStruct((t, b - a), F32) for a, b in SEGS],
        compiler_params=_params("parallel"),
        name="inproj",
    )(x, mod, mod, w)


def _dn_prep_kernel(x_ref, prev_ref, next_ref, ab_ref, w_ref, alog_ref, dtb_ref, grp_ref, e_ref, cum_ref,
                    u_ref, gcf_ref, gcb_ref, bf_ref, bb_ref, *, blocks_per_seq, rows):
    i = pl.program_id(0)
    j = i % blocks_per_seq
    prev = jnp.where(j != 0, prev_ref[...], 0.0)
    nxt = jnp.where(j != blocks_per_seq - 1, next_ref[...], 0.0)
    xe = jnp.concatenate([prev, x_ref[...], nxt], axis=0)
    w = w_ref[...]
    base = 8 - CONV_K // 2
    acc = w[0:1, :] * xe[base:base + rows, :]
    for t in range(1, CONV_K):
        acc = acc + w[t:t + 1, :] * xe[base + t:base + t + rows, :]
    u = _silu(acc)
    qk = u[:, :2 * W_DN]
    ss = _dot_sel(qk * qk, grp_ref[...])
    qk = qk * lax.rsqrt(ss + RMS_EPS)
    u_ref[:, :W_DN] = qk[:, :W_DN] * HEAD_DIM ** -0.5
    u_ref[:, W_DN:2 * W_DN] = qk[:, W_DN:]
    u_ref[:, 2 * W_DN:] = u[:, 2 * W_DN:]
    ab = ab_ref[...]
    z = ab + dtb_ref[...]
    softplus = jnp.maximum(z, 0.0) + jnp.log(1.0 + jnp.exp(-jnp.abs(z)))
    g = -jnp.exp(alog_ref[...]) * softplus
    lane = lax.broadcasted_iota(jnp.int32, ab.shape, 1)
    gb = jnp.where(lane < 2 * H_DN, g, jax.nn.sigmoid(ab))
    for d, (gc_ref, beta_ref) in enumerate(((gcf_ref, bf_ref), (gcb_ref, bb_ref))):
        gbx = _dot_sel(gb, e_ref[d])
        gc_ref[...] = _sel_dot(cum_ref[d], gbx[:, :W_DN])
        beta_ref[...] = gbx[:, W_DN:]


def _dn_prep(qkv, ab, conv_w, alog_row, dtb_row, grp, e_mat, cum, seq_len, rows=256):
    t = qkv.shape[0]
    bps = seq_len // rows
    r8 = rows // 8
    last8 = t // 8 - 1
    kern = functools.partial(_dn_prep_kernel, blocks_per_seq=bps, rows=rows)
    return pl.pallas_call(
        kern,
        grid=(t // rows,),
        in_specs=[pl.BlockSpec((rows, 3 * W_DN), lambda i: (i, 0)),
                  pl.BlockSpec((8, 3 * W_DN), lambda i: (jnp.maximum(i * r8 - 1, 0), 0)),
                  pl.BlockSpec((8, 3 * W_DN), lambda i: (jnp.minimum((i + 1) * r8, last8), 0)),
                  pl.BlockSpec((rows, AB_PAD), lambda i: (i, 0)),
                  pl.BlockSpec((8, 3 * W_DN), lambda i: (0, 0)),
                  pl.BlockSpec((1, AB_PAD), lambda i: (0, 0)),
                  pl.BlockSpec((1, AB_PAD), lambda i: (0, 0)),
                  pl.BlockSpec((2 * W_DN, 2 * W_DN), lambda i: (0, 0)),
                  pl.BlockSpec((2, AB_PAD, 2 * W_DN), lambda i: (0, 0, 0)),
                  pl.BlockSpec((2, rows, rows), lambda i: (0, 0, 0))],
        out_specs=[pl.BlockSpec((rows, 3 * W_DN), lambda i: (i, 0))]
        + [pl.BlockSpec((rows, W_DN), lambda i: (i, 0))] * 4,
        out_shape=[jax.ShapeDtypeStruct((t, 3 * W_DN), F32)]
        + [jax.ShapeDtypeStruct((t, W_DN), F32)] * 4,
        compiler_params=_params("parallel"),
        name="dn_prep",
    )(qkv, qkv, qkv, ab, conv_w, alog_row, dtb_row, grp, e_mat, cum)


def _dn_kernel(uf_ref, ub_ref, gcf_ref, gcb_ref, bf_ref, bb_ref, s0_ref, of_ref, ob_ref, sf_ref, s_scr,
               *, n_chunks, n_sub):
    j = pl.program_id(1)
    r = lax.broadcasted_iota(jnp.int32, (BD, BD), 0)
    c = lax.broadcasted_iota(jnp.int32, (BD, BD), 1)
    same = (r // CHUNK) == (c // CHUNK)
    dt = (r % CHUNK) - (c % CHUNK)
    same_f = jnp.where(same, 1.0, 0.0)
    same_b = same_f.astype(BF16)
    eye_f = jnp.where(r == c, 1.0, 0.0)
    dirs = []
    for sign, last in ((1, CHUNK - 1), (-1, 0)):
        dirs.append((jnp.where(jnp.logical_and(same, dt * sign >= 0), 1.0, 0.0),
                     jnp.where(jnp.logical_and(same, dt * sign > 0), 1.0, 0.0), last))

    def lift(x):
        return jnp.concatenate([x, x, x, x], axis=0) * same_f

    def lift_b(x):
        xb = x.astype(BF16)
        return jnp.concatenate([xb, xb, xb, xb], axis=0) * same_b

    @pl.when(j == 0)
    def _():
        for si in range(n_sub):
            for di in range(2):
                rows = [jnp.concatenate([s0_ref[si, di, h]] * H_DN, axis=1) for h in range(H_DN)]
                s_scr[si, di] = jnp.concatenate(rows, axis=0) * same_f

    def each(f, *xs):
        return [f(*a) for a in zip(*xs)]

    def mm(a, b):
        return jnp.dot(a, b, preferred_element_type=F32)

    def chunks(us, gc, beta, ss, incls, stricts, lasts):
        q = [u[:, :W_DN] for u in us]
        k = [u[:, W_DN:2 * W_DN] for u in us]
        v = [u[:, 2 * W_DN:] for u in us]
        gl = each(lambda x, last: x[last:last + 1, :], gc, lasts)
        eg = [jnp.exp(x) for x in gc]
        kb = each(jnp.multiply, k, beta)
        gcol = [lift(x) for x in gc]
        decay = each(lambda x, m: jnp.exp(jnp.where(m > 0.5, x - x.T, NEG_INF)), gcol, incls)
        qk = each(lambda a, b, kk: lax.dot_general(jnp.concatenate([lift_b(a), lift_b(b)], axis=0),
                                                   lift_b(kk), (((1,), (1,)), ((), ())),
                                                   preferred_element_type=F32), q, kb, k)
        attn = each(lambda x, d: (x[:BD] * d).astype(BF16), qk, decay)
        n_mat = each(lambda x, d, m: -(x[BD:] * d) * m, qk, decay, stricts)
        t_inv = [eye_f + x for x in n_mat]
        n_hi = [x.astype(BF16) for x in n_mat]
        m_b = n_hi
        for _ in range(int(math.log2(CHUNK)) - 1):
            m_b = [mm(x, x).astype(BF16) for x in m_b]
            t_inv = each(lambda t, m: t + mm(t.astype(BF16), m), t_inv, m_b)
        n_lo = each(lambda x, h: (x - h.astype(F32)).astype(BF16), n_mat, n_hi)
        x_hi = [t.astype(BF16) for t in t_inv]
        x_lo = each(lambda t, h: (t - h.astype(F32)).astype(BF16), t_inv, x_hi)
        nx = each(lambda h, lo, x: mm(jnp.concatenate([h, lo], axis=0), x), n_hi, n_lo, x_hi)
        nxl = each(mm, n_hi, x_lo)
        resid = each(lambda t, a, b: (eye_f - t + a[:BD] + a[BD:] + b).astype(BF16), t_inv, nx, nxl)
        t_b = each(lambda t, h, rr: (t + mm(h, rr)).astype(BF16), t_inv, x_hi, resid)
        rhs = each(lambda vv, b, kk, e: jnp.concatenate([lift_b(vv * b), lift_b(kk * e)], axis=1),
                   v, beta, kb, eg)
        sol = each(mm, t_b, rhs)
        s_b = [s.astype(BF16) for s in ss]
        ps = each(lambda so, qq, e, sb: mm(jnp.concatenate([so[:, BD:].astype(BF16), lift_b(qq * e)],
                                                           axis=0), sb), sol, q, eg, s_b)
        v_new = each(lambda so, p: (so[:, :BD] - p[:BD]).astype(BF16), sol, ps)
        o_bd = each(lambda p, a, vn: p[BD:] + mm(a, vn), ps, attn, v_new)
        o_tm = [x[0:CHUNK] + x[CHUNK:2 * CHUNK] + x[2 * CHUNK:3 * CHUNK] + x[3 * CHUNK:] for x in o_bd]
        k_tail = each(lambda kk, a, b: lift(kk * jnp.exp(a - b)).T.astype(BF16), k, gl, gc)
        s_new = each(lambda s, a, kt, vn: s * jnp.exp(a) + mm(kt, vn), ss, gl, k_tail, v_new)
        return o_tm, s_new

    chains = [(si, di) for si in range(n_sub) for di in range(2)]

    def body(ci, carry):
        offs = (pl.multiple_of(ci * CHUNK, CHUNK), pl.multiple_of((n_chunks - 1 - ci) * CHUNK, CHUNK))
        u_refs, gc_refs, b_refs, o_refs = (uf_ref, ub_ref), (gcf_ref, gcb_ref), (bf_ref, bb_ref), (of_ref, ob_ref)
        o_tm, s_new = chunks(
            [u_refs[di][si, pl.ds(offs[di], CHUNK), :] for si, di in chains],
            [gc_refs[di][si, pl.ds(offs[di], CHUNK), :] for si, di in chains],
            [b_refs[di][si, pl.ds(offs[di], CHUNK), :] for si, di in chains],
            [s_scr[si, di] for si, di in chains],
            *[[dirs[di][n] for si, di in chains] for n in range(3)])
        for (si, di), o, s in zip(chains, o_tm, s_new):
            o_refs[di][si, pl.ds(offs[di], CHUNK), :] = o
            s_scr[si, di] = s
        return carry

    lax.fori_loop(0, n_chunks, body, 0)

    @pl.when(j == pl.num_programs(1) - 1)
    def _():
        for si in range(n_sub):
            for di in range(2):
                for h in range(H_DN):
                    sl = slice(h * HEAD_DIM, (h + 1) * HEAD_DIM)
                    sf_ref[si, di, h] = s_scr[si, di, sl, sl]


def _deltanet(u, gc_f, gc_b, beta_f, beta_b, s0_bd, seq_len, rows, n_sub=2):
    t = u.shape[0]
    n_seq = t // seq_len
    nb = seq_len // rows
    kern = functools.partial(_dn_kernel, n_chunks=rows // CHUNK, n_sub=n_sub)
    fwd = lambda s, j: (s, j, 0)
    bwd = lambda s, j: (s, nb - 1 - j, 0)
    state = pl.BlockSpec((n_sub, 2, H_DN, HEAD_DIM, HEAD_DIM), lambda s, j: (s, 0, 0, 0, 0))
    o_f, o_b, s_fin = pl.pallas_call(
        kern,
        grid=(n_seq // n_sub, nb),
        in_specs=[pl.BlockSpec((n_sub, rows, 3 * W_DN), fwd), pl.BlockSpec((n_sub, rows, 3 * W_DN), bwd),
                  pl.BlockSpec((n_sub, rows, W_DN), fwd), pl.BlockSpec((n_sub, rows, W_DN), bwd),
                  pl.BlockSpec((n_sub, rows, W_DN), fwd), pl.BlockSpec((n_sub, rows, W_DN), bwd), state],
        out_specs=[pl.BlockSpec((n_sub, rows, W_DN), fwd), pl.BlockSpec((n_sub, rows, W_DN), bwd), state],
        out_shape=[jax.ShapeDtypeStruct((n_seq, seq_len, W_DN), F32),
                   jax.ShapeDtypeStruct((n_seq, seq_len, W_DN), F32),
                   jax.ShapeDtypeStruct((n_seq, 2, H_DN, HEAD_DIM, HEAD_DIM), F32)],
        scratch_shapes=[pltpu.VMEM((n_sub, 2, BD, BD), F32)],
        compiler_params=_params("parallel", "arbitrary"),
        name="deltanet",
    )(*[x.reshape(n_seq, seq_len, -1) for x in (u, u, gc_f, gc_b, beta_f, beta_b)], s0_bd)
    return o_f.reshape(t, W_DN), o_b.reshape(t, W_DN), s_fin


def _diff_lambda(lam_ref, lam_init):
    lp = lam_ref[...]
    return (jnp.exp(jnp.sum(lp[0:1] * lp[1:2], axis=1, keepdims=True))
            - jnp.exp(jnp.sum(lp[2:3] * lp[3:4], axis=1, keepdims=True)) + lam_init)


def _sub_rms(o, subln, lam_init):
    ms = jnp.mean(o * o, axis=-1, keepdims=True)
    return o * lax.rsqrt(ms + RMS_EPS) * subln * (1.0 - lam_init)


def _ctx_attn_kernel(na_ref, df_ref, lam_ref, subln_ref, oa_ref, oc_ref, *, lam_init):
    lam = _diff_lambda(lam_ref, lam_init)
    subln = subln_ref[...]
    n = na_ref.shape[0]
    qa = (na_ref[:, :W_NA] * NA_Q_SCALE).astype(BF16)
    ka = na_ref[:, W_NA:2 * W_NA].astype(BF16)
    va = na_ref[:, 2 * W_NA:].astype(BF16)
    qc = (df_ref[:, :W_DF] * DF_Q_SCALE).astype(BF16)
    kc = df_ref[:, W_DF:2 * W_DF].astype(BF16)
    vc = df_ref[:, 2 * W_DF:].astype(BF16)
    heads = [slice(h * HEAD_DIM, (h + 1) * HEAD_DIM) for h in range(H_NA)]
    maps = [slice(m * DF_QK, (m + 1) * DF_QK) for m in range(2 * H_DF)]
    s_na = [_dot_nt(qa[:, sl], ka[:, sl]) for sl in heads]
    s_df = [_dot_nt(qc[:, sl], kc[:, sl]) for sl in maps]
    p_na = [jnp.exp2(s - jnp.max(s, axis=-1, keepdims=True)) for s in s_na]
    p_df = [jnp.exp2(s - jnp.max(s, axis=-1, keepdims=True)) for s in s_df]
    d_na = [jnp.sum(p, axis=-1, keepdims=True) for p in p_na]
    d_df = [jnp.sum(p, axis=-1, keepdims=True) for p in p_df]
    o_na = [_dot(p, va[:, sl]) for p, sl in zip(p_na, heads)]
    o_df = [_dot(jnp.concatenate([p_df[2 * h].astype(BF16), p_df[2 * h + 1].astype(BF16)], axis=0),
                 vc[:, heads[h]]) for h in range(H_DF)]
    for h in range(H_NA):
        oa_ref[:, heads[h]] = o_na[h] / d_na[h]
    for h in range(H_DF):
        o = o_df[h][:n] / d_df[2 * h] - lam * (o_df[h][n:] / d_df[2 * h + 1])
        oc_ref[:, heads[h]] = _sub_rms(o, subln, lam_init)


def _ctx_attn(qkv_na, qkv_df, lam_p, subln_row, layer):
    lam_init = 0.8 - 0.6 * math.exp(-0.3 * layer)
    kern = functools.partial(_ctx_attn_kernel, lam_init=lam_init)
    return pl.pallas_call(
        kern,
        grid=(BATCH,),
        in_specs=[pl.BlockSpec((SEQ, 3 * W_NA), lambda b: (b, 0)),
                  pl.BlockSpec((SEQ, 3 * W_DF), lambda b: (b, 0)),
                  pl.BlockSpec((4, DF_QK), lambda b: (0, 0)),
                  pl.BlockSpec((1, HEAD_DIM), lambda b: (0, 0))],
        out_specs=[pl.BlockSpec((SEQ, W_NA), lambda b: (b, 0)),
                   pl.BlockSpec((SEQ, W_DF), lambda b: (b, 0))],
        out_shape=[jax.ShapeDtypeStruct((T_CTX, W_NA), F32),
                   jax.ShapeDtypeStruct((T_CTX, W_DF), F32)],
        compiler_params=_params("parallel"),
        name="ctx_attn",
    )(qkv_na, qkv_df, lam_p, subln_row)


NA_R = 4
NA_U = NA_R + WIN_R - 1
NA_GROUPS = GRID_H // NA_R
NA_Q_SCALE = HEAD_DIM ** -0.5 * math.log2(math.e)


def _na_key_start(g):
    return jnp.clip(g * NA_R - WIN_R // 2, 0, GRID_H - NA_U)


def _na_bias_table(rpb):
    qc = np.arange(GRID_W)[:, None]
    kc = np.arange(GRID_W)[None, :]
    cs = np.clip(qc - WIN_C // 2, 0, GRID_W - WIN_C)
    valid = (kc >= cs) & (kc < cs + WIN_C)
    dc = np.clip(kc - qc + (WIN_C - 1), 0, 2 * WIN_C - 2)
    onehot = (dc[None] == np.arange(2 * WIN_C - 1)[:, None, None]).astype(np.float32)
    x = jnp.einsum('hrd,dqk->hrqk', rpb.astype(F32), jnp.asarray(onehot), precision=HIGHEST)
    x = jnp.where(valid[None, None], x * math.log2(math.e), NEG_INF)
    masked = jnp.full((H_NA, GRID_W, GRID_W), NEG_INF, F32)
    classes = []
    for g in (0, 1, NA_GROUPS - 1):
        u0 = int(np.clip(g * NA_R - WIN_R // 2, 0, GRID_H - NA_U))
        rows = []
        for i in range(NA_R):
            r = g * NA_R + i
            w0 = int(np.clip(r - WIN_R // 2, 0, GRID_H - WIN_R))
            blocks = [x[:, u0 + u - r + WIN_R - 1] if w0 <= u0 + u < w0 + WIN_R else masked
                      for u in range(NA_U)]
            rows.append(jnp.stack(blocks, axis=2))
        classes.append(jnp.stack(rows, axis=1))
    return jnp.stack(classes, axis=1).reshape(H_NA, 3, NA_R * GRID_W, NA_U * GRID_W)


def _lat_na_kernel(q_ref, k_ref, v_ref, kc_ref, vc_ref, bias_ref, o_ref):
    g = pl.program_id(1)
    start = pl.multiple_of(_na_key_start(g) * GRID_W, GRID_W)
    q = (q_ref[...] * NA_Q_SCALE).astype(BF16)
    kw = k_ref[pl.ds(start, NA_U * GRID_W), :].astype(BF16)
    vw = v_ref[pl.ds(start, NA_U * GRID_W), :].astype(BF16)
    kc = kc_ref[...]
    vc = vc_ref[...]
    heads = [slice(h * HEAD_DIM, (h + 1) * HEAD_DIM) for h in range(H_NA)]
    s_loc = [_dot_nt(q[:, sl], kw[:, sl]) + bias_ref[h] for h, sl in enumerate(heads)]
    s_ctx = [_dot_nt(q[:, sl], kc[:, sl]) for sl in heads]
    m = [jnp.maximum(jnp.max(a, axis=-1, keepdims=True), jnp.max(b, axis=-1, keepdims=True))
         for a, b in zip(s_loc, s_ctx)]
    p_loc = [jnp.exp2(a - mm) for a, mm in zip(s_loc, m)]
    p_ctx = [jnp.exp2(b - mm) for b, mm in zip(s_ctx, m)]
    den = [jnp.sum(a, axis=-1, keepdims=True) + jnp.sum(b, axis=-1, keepdims=True)
           for a, b in zip(p_loc, p_ctx)]
    o = [_dot(a, vw[:, sl]) + _dot(b, vc[:, sl]) for a, b, sl in zip(p_loc, p_ctx, heads)]
    for h, sl in enumerate(heads):
        o_ref[:, sl] = o[h] / den[h]


def _lat_na(qkv_na, ck, cv, bias_tab):
    rows = NA_R * GRID_W
    cls = lambda g: jnp.where(g == 0, 0, jnp.where(g == NA_GROUPS - 1, 2, 1))
    return pl.pallas_call(
        _lat_na_kernel,
        grid=(DEC_BATCH, NA_GROUPS),
        in_specs=[pl.BlockSpec((rows, W_NA), lambda b, g: (b * NA_GROUPS + g, 0)),
                  pl.BlockSpec((DEC_SEQ, W_NA), lambda b, g: (b, 1)),
                  pl.BlockSpec((DEC_SEQ, W_NA), lambda b, g: (b, 2)),
                  pl.BlockSpec((None, PAST_LEN, W_NA), lambda b, g: (b, 0, 0)),
                  pl.BlockSpec((None, PAST_LEN, W_NA), lambda b, g: (b, 0, 0)),
                  pl.BlockSpec((H_NA, None, rows, NA_U * GRID_W), lambda b, g: (0, cls(g), 0, 0))],
        out_specs=pl.BlockSpec((rows, W_NA), lambda b, g: (b * NA_GROUPS + g, 0)),
        out_shape=jax.ShapeDtypeStruct((T_LAT, W_NA), F32),
        compiler_params=_params("parallel", "arbitrary"),
        name="lat_na",
    )(qkv_na, qkv_na, qkv_na, ck, cv, bias_tab)


def _rope_tables():
    nf = DF_QK // 4
    inv = ROPE_BASE ** (-np.arange(nf, dtype=np.float32) / nf)
    t = np.arange(DEC_SEQ)
    pos = np.stack([t // GRID_W, t % GRID_W], axis=-1).astype(np.float32)
    ang = jnp.asarray(pos[:, :, None] * inv)
    cos, sin = jnp.cos(ang), jnp.sin(ang)
    cos32 = jnp.concatenate([cos, cos], axis=-1).reshape(DEC_SEQ, DF_QK)
    sin32 = jnp.concatenate([-sin, sin], axis=-1).reshape(DEC_SEQ, DF_QK)
    reps = W_DF // DF_QK
    rot = np.zeros((W_DF, W_DF), np.float32)
    for dd in range(W_DF):
        rot[dd + nf if dd % (2 * nf) < nf else dd - nf, dd] = 1.0
    return jnp.tile(cos32, (1, reps)), jnp.tile(sin32, (1, reps)), jnp.asarray(rot, dtype=BF16)


def _rope_kernel(df_ref, cos_ref, sin_ref, rot_ref, q_ref, kt_ref, v_ref):
    cos, sin, rot = cos_ref[...], sin_ref[...], rot_ref[...]
    q = df_ref[:, :W_DF]
    k = df_ref[:, W_DF:2 * W_DF]
    q_ref[...] = ((q * cos + _dot_sel(q, rot) * sin) * DF_Q_SCALE).astype(BF16)
    kt_ref[...] = (k * cos + _dot_sel(k, rot) * sin).T.astype(BF16)
    lane = lax.broadcasted_iota(jnp.int32, (q.shape[0], V_AUG - HEAD_DIM), 1)
    one_col = jnp.where(lane == 0, 1.0, 0.0).astype(BF16)
    for h in range(H_DF):
        a = 2 * W_DF + h * HEAD_DIM
        v_ref[:, h * V_AUG:h * V_AUG + HEAD_DIM] = df_ref[:, a:a + HEAD_DIM].astype(BF16)
        v_ref[:, h * V_AUG + HEAD_DIM:(h + 1) * V_AUG] = one_col


def _rope(qkv_df, cos, sin, rot, tm=512):
    nb = DEC_SEQ // tm
    return pl.pallas_call(
        _rope_kernel,
        grid=(DEC_BATCH, nb),
        in_specs=[pl.BlockSpec((tm, 3 * W_DF), lambda b, i: (b * nb + i, 0)),
                  pl.BlockSpec((tm, W_DF), lambda b, i: (i, 0)),
                  pl.BlockSpec((tm, W_DF), lambda b, i: (i, 0)),
                  pl.BlockSpec((W_DF, W_DF), lambda b, i: (0, 0))],
        out_specs=[pl.BlockSpec((tm, W_DF), lambda b, i: (b * nb + i, 0)),
                   pl.BlockSpec((None, W_DF, tm), lambda b, i: (b, 0, i)),
                   pl.BlockSpec((tm, H_DF * V_AUG), lambda b, i: (b * nb + i, 0))],
        out_shape=[jax.ShapeDtypeStruct((T_LAT, W_DF), BF16),
                   jax.ShapeDtypeStruct((DEC_BATCH, W_DF, DEC_SEQ), BF16),
                   jax.ShapeDtypeStruct((T_LAT, H_DF * V_AUG), BF16)],
        compiler_params=_params("parallel", "parallel"),
        name="rope",
    )(qkv_df, cos, sin, rot)


def _lat_df_kernel(q_ref, kt_ref, v_ref, lam_ref, subln_ref, o_ref, *, lam_init, tq):
    q = q_ref[...]
    lam = _diff_lambda(lam_ref, lam_init)
    subln = subln_ref[...]
    for h in range(H_DF):
        es = []
        for i in range(2):
            a = h * HEAD_DIM + i * DF_QK
            s = jnp.dot(q[:, a:a + DF_QK], kt_ref[a:a + DF_QK, :], preferred_element_type=F32)
            es.append(jnp.exp2(s - jnp.max(s, axis=-1, keepdims=True)).astype(BF16))
        ov = jnp.dot(jnp.concatenate(es, axis=0), v_ref[:, h * V_AUG:(h + 1) * V_AUG],
                     preferred_element_type=F32)
        o = (ov[:tq, :HEAD_DIM] / ov[:tq, HEAD_DIM:HEAD_DIM + 1]
             - lam * (ov[tq:, :HEAD_DIM] / ov[tq:, HEAD_DIM:HEAD_DIM + 1]))
        o_ref[:, h * HEAD_DIM:(h + 1) * HEAD_DIM] = _sub_rms(o, subln, lam_init)


def _lat_df(q_r, kt_all, v_all, lam_p, subln_row, layer, tq=256):
    lam_init = 0.8 - 0.6 * math.exp(-0.3 * layer)
    nb = DEC_SEQ // tq
    n_keys = kt_all.shape[2]
    kern = functools.partial(_lat_df_kernel, lam_init=lam_init, tq=tq)
    return pl.pallas_call(
        kern,
        grid=(DEC_BATCH, nb),
        in_specs=[pl.BlockSpec((tq, W_DF), lambda b, i: (b * nb + i, 0)),
                  pl.BlockSpec((None, W_DF, n_keys), lambda b, i: (b, 0, 0)),
                  pl.BlockSpec((None, n_keys, H_DF * V_AUG), lambda b, i: (b, 0, 0)),
                  pl.BlockSpec((4, DF_QK), lambda b, i: (0, 0)),
                  pl.BlockSpec((1, HEAD_DIM), lambda b, i: (0, 0))],
        out_specs=pl.BlockSpec((tq, W_DF), lambda b, i: (b * nb + i, 0)),
        out_shape=jax.ShapeDtypeStruct((T_LAT, W_DF), F32),
        compiler_params=_params("parallel", "arbitrary"),
        name="lat_df",
    )(q_r, kt_all, v_all, lam_p, subln_row)


def _outproj_kernel(oa_ref, of_ref, obk_ref, gate_ref, onorm_ref, grp_ref, oc_ref, y_ref, g1_ref, w_ref,
                    lg_ref, lb_ref, o_ref):
    ob = of_ref[...] + obk_ref[...]
    ms = _dot_sel(ob * ob, grp_ref[...]) * (1.0 / HEAD_DIM)
    ob = ob * lax.rsqrt(ms + RMS_EPS) * onorm_ref[...] * _silu(gate_ref[...])
    o = (jnp.dot(oa_ref[...].astype(BF16), w_ref[:W_NA, :], preferred_element_type=F32)
         + jnp.dot(ob.astype(BF16), w_ref[W_NA:W_NA + W_DN, :], preferred_element_type=F32)
         + jnp.dot(oc_ref[...].astype(BF16), w_ref[W_NA + W_DN:, :], preferred_element_type=F32))
    o_ref[...] = _layer_norm(ALPHA * y_ref[...] + g1_ref[...] * o, lg_ref[...], lb_ref[...])


def _outproj(oa, o_fwd, o_bwd, gate, onorm_row, grp, oc, y, mod, w, ln_g, ln_b, cond_fn, tm=512):
    t = y.shape[0]
    row = lambda n: pl.BlockSpec((tm, n), lambda i: (i, 0))
    vec = pl.BlockSpec((1, D_MODEL), lambda i: (0, 0))
    return pl.pallas_call(
        _outproj_kernel,
        grid=(t // tm,),
        in_specs=[row(W_NA), row(W_DN), row(W_DN),
                  row(W_DN), pl.BlockSpec((1, W_DN), lambda i: (0, 0)),
                  pl.BlockSpec((W_DN, W_DN), lambda i: (0, 0)),
                  row(W_DF), row(D_MODEL), _mod_spec(2, cond_fn),
                  pl.BlockSpec((D_MODEL, D_MODEL), lambda i: (0, 0)), vec, vec],
        out_specs=row(D_MODEL),
        out_shape=jax.ShapeDtypeStruct((t, D_MODEL), F32),
        compiler_params=_params("parallel"),
        name="outproj_ln",
    )(oa, o_fwd, o_bwd, gate, onorm_row, grp, oc, y, mod, w, ln_g, ln_b)


MOE_TM = 1024
MOE_CAP = 320


def _router_kernel(y_ref, sh_ref, sc_ref, w_ref, b_ref, tri_ref, g_ref, rk_ref, rkt_ref, cnt_ref):
    h = y_ref[...] * (1.0 + sc_ref[...]) + sh_ref[...]
    logits = _dot_exact(h, w_ref[...]) + b_ref[...]
    lane = lax.broadcasted_iota(jnp.int32, logits.shape, 1).astype(F32)
    logits = jnp.where(lane < N_EXPERTS, logits, -jnp.inf)
    m1 = jnp.max(logits, axis=-1, keepdims=True)
    i1 = jnp.min(jnp.where(logits == m1, lane, float(LANES)), axis=-1, keepdims=True)
    rest = jnp.where(lane == i1, -jnp.inf, logits)
    m2 = jnp.max(rest, axis=-1, keepdims=True)
    i2 = jnp.min(jnp.where(rest == m2, lane, float(LANES)), axis=-1, keepdims=True)
    e2 = jnp.exp(m2 - m1)
    w1 = 1.0 / (1.0 + e2)
    g_ref[...] = jnp.where(lane == i1, w1, 0.0) + jnp.where(lane == i2, e2 * w1, 0.0)
    routed = jnp.where(lane == i1, 1.0, 0.0) + jnp.where(lane == i2, 1.0, 0.0)
    before = jnp.dot(tri_ref[...], routed.astype(BF16), preferred_element_type=F32)
    rank = jnp.where(routed > 0.5, before, -1.0)
    rk_ref[...] = rank
    rkt_ref[...] = rank.T[:N_EXPERTS, :]
    cnt_ref[...] = jnp.sum(routed, axis=0, keepdims=True)


def _router(y, mod, w_pad, b_pad, tri, cond_fn, tm=MOE_TM):
    t = y.shape[0]
    nt = t // tm
    return pl.pallas_call(
        _router_kernel,
        grid=(nt,),
        in_specs=[pl.BlockSpec((tm, D_MODEL), lambda i: (i, 0)),
                  _mod_spec(3, cond_fn), _mod_spec(4, cond_fn),
                  pl.BlockSpec((D_MODEL, LANES), lambda i: (0, 0)),
                  pl.BlockSpec((1, LANES), lambda i: (0, 0)),
                  pl.BlockSpec((tm, tm), lambda i: (0, 0))],
        out_specs=[pl.BlockSpec((tm, LANES), lambda i: (i, 0)),
                   pl.BlockSpec((tm, LANES), lambda i: (i, 0)),
                   pl.BlockSpec((None, N_EXPERTS, tm), lambda i: (i, 0, 0)),
                   pl.BlockSpec((None, 1, LANES), lambda i: (i, 0, 0))],
        out_shape=[jax.ShapeDtypeStruct((t, LANES), F32),
                   jax.ShapeDtypeStruct((t, LANES), F32),
                   jax.ShapeDtypeStruct((nt, N_EXPERTS, tm), F32),
                   jax.ShapeDtypeStruct((nt, 1, LANES), F32)],
        compiler_params=_params("parallel"),
        name="router",
    )(y, mod, mod, w_pad, b_pad, tri)


def _moe_kernel(cnt_ref, y_ref, sh_ref, sc_ref, g2_ref, gates_ref, rk_ref, rkt_ref, wg_ref, wu_ref, wd_ref,
                lg_ref, lb_ref, o_ref, h_scr, acc_scr, *, tm, cap):
    i = pl.program_id(0)
    e = pl.program_id(1)

    @pl.when(e == 0)
    def _():
        h_scr[...] = (y_ref[...] * (1.0 + sc_ref[...]) + sh_ref[...]).astype(BF16)
        acc_scr[...] = jnp.zeros_like(acc_scr)

    lane = lax.broadcasted_iota(jnp.int32, (tm, LANES), 1)
    gate_col = jnp.sum(jnp.where(lane == e, gates_ref[...], 0.0), axis=-1, keepdims=True)
    rank_col = jnp.sum(jnp.where(lane == e, rk_ref[...], 0.0), axis=-1, keepdims=True)
    rank_row = rkt_ref[pl.ds(e, 1), :]
    n_pass = (cnt_ref[i * N_EXPERTS + e] + cap - 1) // cap

    def body(ps, carry):
        base = (ps * cap).astype(F32)
        slot_r = lax.broadcasted_iota(jnp.int32, (cap, tm), 0).astype(F32) + base
        sel = jnp.where(rank_row == slot_r, 1.0, 0.0).astype(BF16)
        xe = jnp.dot(sel, h_scr[...], preferred_element_type=F32).astype(BF16)
        a = jnp.dot(xe, wg_ref[...], preferred_element_type=F32)
        b = jnp.dot(xe, wu_ref[...], preferred_element_type=F32)
        f = jnp.dot((_silu(a) * b).astype(BF16), wd_ref[...], preferred_element_type=F32).astype(BF16)
        slot_c = lax.broadcasted_iota(jnp.int32, (tm, cap), 1).astype(F32) + base
        sel_t = jnp.where(rank_col == slot_c, 1.0, 0.0).astype(BF16)
        acc_scr[...] += gate_col * jnp.dot(sel_t, f, preferred_element_type=F32)
        return carry

    lax.fori_loop(0, n_pass, body, 0)

    @pl.when(e == N_EXPERTS - 1)
    def _():
        o_ref[...] = _layer_norm(ALPHA * y_ref[...] + g2_ref[...] * acc_scr[...],
                                 lg_ref[...], lb_ref[...])


def _moe(y, mod, gates, rank, rank_t, counts, wg, wu, wd, ln_g, ln_b, cond_fn, tm=MOE_TM, cap=MOE_CAP):
    t = y.shape[0]
    ff = wg.shape[2]
    vec = pl.BlockSpec((1, D_MODEL), lambda i, e, c: (0, 0))
    tok = lambda n: pl.BlockSpec((tm, n), lambda i, e, c: (i, 0))
    kern = functools.partial(_moe_kernel, tm=tm, cap=cap)
    return pl.pallas_call(
        kern,
        grid_spec=pltpu.PrefetchScalarGridSpec(
            num_scalar_prefetch=1,
            grid=(t // tm, N_EXPERTS),
            in_specs=[tok(D_MODEL), _mod_spec(3, cond_fn), _mod_spec(4, cond_fn), _mod_spec(5, cond_fn),
                      tok(LANES), tok(LANES),
                      pl.BlockSpec((None, N_EXPERTS, tm), lambda i, e, c: (i, 0, 0)),
                      pl.BlockSpec((None, D_MODEL, ff), lambda i, e, c: (e, 0, 0)),
                      pl.BlockSpec((None, D_MODEL, ff), lambda i, e, c: (e, 0, 0)),
                      pl.BlockSpec((None, ff, D_MODEL), lambda i, e, c: (e, 0, 0)),
                      vec, vec],
            out_specs=tok(D_MODEL),
            scratch_shapes=[pltpu.VMEM((tm, D_MODEL), BF16), pltpu.VMEM((tm, D_MODEL), F32)]),
        out_shape=jax.ShapeDtypeStruct((t, D_MODEL), F32),
        compiler_params=_params("parallel", "arbitrary"),
        name="moe_ln",
    )(counts, y, mod, mod, mod, gates, rank, rank_t, wg, wu, wd, ln_g, ln_b)


def _ffn_kernel(y_ref, sh_ref, sc_ref, g2_ref, gates_ref, wg_ref, wu_ref, wd_ref, lg_ref, lb_ref,
                o_ref, h_scr, acc_scr, *, n_blocks):
    e = pl.program_id(1)

    @pl.when(e == 0)
    def _():
        h_scr[...] = (y_ref[...] * (1.0 + sc_ref[...]) + sh_ref[...]).astype(BF16)
        acc_scr[...] = jnp.zeros_like(acc_scr)

    h = h_scr[...]
    a = jnp.dot(h, wg_ref[...], preferred_element_type=F32)
    b = jnp.dot(h, wu_ref[...], preferred_element_type=F32)
    f = jnp.dot((_silu(a) * b).astype(BF16), wd_ref[...], preferred_element_type=F32)
    lane = lax.broadcasted_iota(jnp.int32, gates_ref.shape, 1)
    gate = jnp.sum(jnp.where(lane == e, gates_ref[...], 0.0), axis=-1, keepdims=True)
    acc_scr[...] += gate * f

    @pl.when(e == n_blocks - 1)
    def _():
        o_ref[...] = _layer_norm(ALPHA * y_ref[...] + g2_ref[...] * acc_scr[...],
                                 lg_ref[...], lb_ref[...])


def _ffn(y, mod, gates, wg, wu, wd, ln_g, ln_b, cond_fn, tm=512):
    t = y.shape[0]
    n_blocks, _, ff = wg.shape
    vec = pl.BlockSpec((1, D_MODEL), lambda i, e: (0, 0))
    kern = functools.partial(_ffn_kernel, n_blocks=n_blocks)
    return pl.pallas_call(
        kern,
        grid=(t // tm, n_blocks),
        in_specs=[pl.BlockSpec((tm, D_MODEL), lambda i, e: (i, 0)),
                  _mod_spec(3, cond_fn), _mod_spec(4, cond_fn), _mod_spec(5, cond_fn),
                  pl.BlockSpec((tm, LANES), lambda i, e: (i, 0)),
                  pl.BlockSpec((None, D_MODEL, ff), lambda i, e: (e, 0, 0)),
                  pl.BlockSpec((None, D_MODEL, ff), lambda i, e: (e, 0, 0)),
                  pl.BlockSpec((None, ff, D_MODEL), lambda i, e: (e, 0, 0)),
                  vec, vec],
        out_specs=pl.BlockSpec((tm, D_MODEL), lambda i, e: (i, 0)),
        out_shape=jax.ShapeDtypeStruct((t, D_MODEL), F32),
        scratch_shapes=[pltpu.VMEM((tm, D_MODEL), BF16), pltpu.VMEM((tm, D_MODEL), F32)],
        compiler_params=_params("parallel", "arbitrary"),
        name="ffn_ln",
    )(y, mod, mod, mod, gates, wg, wu, wd, ln_g, ln_b)


def _permute_w_in(w):
    offs = np.cumsum((0,) + PROJ_SIZES)
    qa, ka, va, qkv, gate, a, b, qc, kc, vc = (w[:, offs[i]:offs[i + 1]] for i in range(10))
    pad = jnp.zeros((D_MODEL, AB_PAD - 4 * H_DN), w.dtype)
    return jnp.concatenate([qa, ka, va, qkv, gate, qc, kc, vc, a, b, pad], axis=1).astype(BF16)


def _pad_row(v, n=AB_PAD):
    v = v.reshape(1, -1)
    return jnp.pad(v, ((0, 0), (0, n - v.shape[1])))


def _group_matrix(n):
    idx = np.arange(n) // HEAD_DIM
    return jnp.asarray((idx[:, None] == idx[None, :]).astype(np.float32), dtype=BF16)


def _expand_matrix():
    e = np.zeros((2, AB_PAD, 2 * W_DN), np.float32)
    for d in range(2):
        for h in range(H_DN):
            e[d, d * H_DN + h, h * HEAD_DIM:(h + 1) * HEAD_DIM] = 1.0
            e[d, 2 * H_DN + d * H_DN + h, W_DN + h * HEAD_DIM:W_DN + (h + 1) * HEAD_DIM] = 1.0
    return jnp.asarray(e, dtype=BF16)


def _chunk_cumsum_matrices(rows):
    r = np.arange(rows)
    same = (r[:, None] // CHUNK) == (r[None, :] // CHUNK)
    return jnp.asarray(np.stack([same & (r[:, None] >= r[None, :]),
                                 same & (r[:, None] <= r[None, :])]).astype(np.float32), dtype=BF16)


def kernel(x_prompt, x_sample, cache_na_k, cache_na_v, cache_df_k, cache_df_v, state_dn, c, c_ctx,
           ada_w, ada_b, w_in, conv_dn, a_log_dn, dt_bias_dn, onorm_dn, rpb_na, lambda_df, subln_df,
           w_out, ln1_g, ln1_b, ln2_g, ln2_b, ffn_w_gate, ffn_w_up, ffn_w_down, router_w, router_b,
           moe_w_gate, moe_w_up, moe_w_down):
    conds = jnp.concatenate([c_ctx[None, :], c, jnp.zeros((N_COND - 1 - DEC_BATCH, D_MODEL), F32)], axis=0)
    mods = _ada_table(conds, ada_w, ada_b)
    ctx_cond = lambda i: 0
    lat_cond = lambda i: 1 + (i * 512) // DEC_SEQ
    lat_cond_moe = lambda i: 1 + (i * MOE_TM) // DEC_SEQ
    tri = jnp.asarray(np.tril(np.ones((MOE_TM, MOE_TM), np.float32), -1), dtype=BF16)

    grp512, grp256 = _group_matrix(2 * W_DN), _group_matrix(W_DN)
    e_mat = _expand_matrix()
    cum = _chunk_cumsum_matrices(256)
    cos, sin, rot = _rope_tables()
    ones_gates = jnp.ones((T_CTX, LANES), F32)
    one_col = jnp.zeros((DEC_BATCH, PAST_LEN, H_DF, V_AUG - HEAD_DIM), BF16).at[..., 0].set(1.0)

    y_ctx = x_prompt.reshape(T_CTX, D_MODEL)
    y_lat = x_sample.reshape(T_LAT, D_MODEL)
    ctx_out = []
    for l in range(DEPTH):
        mod = mods[l]
        w_in_l = _permute_w_in(w_in[l])
        w_out_l = w_out[l].astype(BF16)
        conv_w = jnp.pad(conv_dn[l], ((0, 8 - CONV_K), (0, 0)))
        alog_row = _pad_row(a_log_dn[l])
        dtb_row = _pad_row(dt_bias_dn[l])
        onorm_row = jnp.tile(onorm_dn[l], H_DN).reshape(1, W_DN)
        subln_row = subln_df[l].reshape(1, HEAD_DIM)
        lg1, lb1 = ln1_g[l].reshape(1, D_MODEL), ln1_b[l].reshape(1, D_MODEL)
        lg2, lb2 = ln2_g[l].reshape(1, D_MODEL), ln2_b[l].reshape(1, D_MODEL)

        na, dn, gate, df, ab = _inproj(y_ctx, mod, w_in_l, ctx_cond)
        u, *gates_dn = _dn_prep(dn, ab, conv_w, alog_row, dtb_row, grp512, e_mat, cum, SEQ)
        s0 = jnp.zeros((BATCH, 2, H_DN, HEAD_DIM, HEAD_DIM), F32)
        o_f, o_b, s_fin = _deltanet(u, *gates_dn, s0, SEQ, SEQ)
        oa, oc = _ctx_attn(na, df, lambda_df[l], subln_row, l)
        y1_ctx = _outproj(oa, o_f, o_b, gate, onorm_row, grp256, oc, y_ctx, mod, w_out_l, lg1, lb1, ctx_cond)
        ctx_out.append((na[:, W_NA:2 * W_NA].reshape(BATCH, SEQ, H_NA, HEAD_DIM),
                        na[:, 2 * W_NA:].reshape(BATCH, SEQ, H_NA, HEAD_DIM),
                        df[:, W_DF:2 * W_DF].reshape(BATCH, SEQ, H_DF, HEAD_DIM),
                        df[:, 2 * W_DF:].reshape(BATCH, SEQ, H_DF, HEAD_DIM),
                        s_fin))

        na, dn, gate, df, ab = _inproj(y_lat, mod, w_in_l, lat_cond)
        u, *gates_dn = _dn_prep(dn, ab, conv_w, alog_row, dtb_row, grp512, e_mat, cum, DEC_SEQ)
        o_f, o_b, _ = _deltanet(u, *gates_dn, state_dn[:, l], DEC_SEQ, 512)
        oa = _lat_na(na, cache_na_k[:, l].reshape(DEC_BATCH, PAST_LEN, W_NA).astype(BF16),
                     cache_na_v[:, l].reshape(DEC_BATCH, PAST_LEN, W_NA).astype(BF16),
                     _na_bias_table(rpb_na[l]))
        q_r, kt, v_b = _rope(df, cos, sin, rot)
        ktc = jnp.swapaxes(cache_df_k[:, l].reshape(DEC_BATCH, PAST_LEN, W_DF), 1, 2).astype(BF16)
        vc = jnp.concatenate([cache_df_v[:, l].astype(BF16), one_col], axis=-1)
        kt_all = jnp.concatenate([kt, ktc], axis=2)
        v_all = jnp.concatenate([v_b.reshape(DEC_BATCH, DEC_SEQ, H_DF * V_AUG),
                                 vc.reshape(DEC_BATCH, PAST_LEN, H_DF * V_AUG)], axis=1)
        oc = _lat_df(q_r, kt_all, v_all, lambda_df[l], subln_row, l)
        y1_lat = _outproj(oa, o_f, o_b, gate, onorm_row, grp256, oc, y_lat, mod, w_out_l, lg1, lb1, lat_cond)

        i = l // 2
        if l % 2 == 0:
            split = lambda w: jnp.transpose(w.reshape(D_MODEL, D_FF // MOE_FF, MOE_FF), (1, 0, 2))
            wg, wu = split(ffn_w_gate[i]).astype(BF16), split(ffn_w_up[i]).astype(BF16)
            wd = ffn_w_down[i].reshape(D_FF // MOE_FF, MOE_FF, D_MODEL).astype(BF16)
            y_ctx = _ffn(y1_ctx, mod, ones_gates, wg, wu, wd, lg2, lb2, ctx_cond)
            y_lat = _ffn(y1_lat, mod, ones_gates, wg, wu, wd, lg2, lb2, lat_cond)
        else:
            wg, wu, wd = moe_w_gate[i].astype(BF16), moe_w_up[i].astype(BF16), moe_w_down[i].astype(BF16)
            rw = jnp.pad(router_w[i], ((0, 0), (0, LANES - N_EXPERTS)))
            rb = _pad_row(router_b[i], LANES)
            ys = []
            for y1, cond in ((y1_ctx, ctx_cond), (y1_lat, lat_cond_moe)):
                gates, rank, rank_t, cnt = _router(y1, mod, rw, rb, tri, cond)
                counts = cnt[:, 0, :N_EXPERTS].astype(jnp.int32).reshape(-1)
                ys.append(_moe(y1, mod, gates, rank, rank_t, counts, wg, wu, wd, lg2, lb2, cond))
            y_ctx, y_lat = ys

    stack = lambda j: jnp.stack([t[j] for t in ctx_out], axis=1)
    return (y_ctx.reshape(BATCH, SEQ, D_MODEL), y_lat.reshape(DEC_BATCH, DEC_SEQ, D_MODEL),
            stack(0), stack(1), stack(2), stack(3), stack(4))
```

```python
import functools
import math

import jax
import jax.numpy as jnp
import numpy as np
from jax import lax
from jax.experimental import pallas as pl
from jax.experimental.pallas import tpu as pltpu

F32 = jnp.float32
BF16 = jnp.bfloat16
HIGHEST = lax.Precision.HIGHEST

D_MODEL = 1024
BATCH = 32
SEQ = 256
DEPTH = 2
DEC_BATCH = 2
DEC_SEQ = 4096
PAST_LEN = 512
GRID_W = 64
GRID_H = DEC_SEQ // GRID_W
HEAD_DIM = 64
H_NA = 6
H_DN = 4
H_DF = 6
W_NA = H_NA * HEAD_DIM
W_DN = H_DN * HEAD_DIM
W_DF = H_DF * HEAD_DIM
DF_QK = HEAD_DIM // 2
WIN_R = 8
WIN_C = 16
CONV_K = 5
CHUNK = 64
D_FF = 2816
N_EXPERTS = 8
MOE_FF = 1408
ALPHA = (2 * DEPTH) ** 0.25
LN_EPS = 1e-5
RMS_EPS = 1e-6
ROPE_BASE = 10000.0
NEG_INF = -1e30
PROJ_SIZES = (W_NA, W_NA, W_NA, 3 * W_DN, W_DN, 2 * H_DN, 2 * H_DN, W_DF, W_DF, W_DF)

T_CTX = BATCH * SEQ
T_LAT = DEC_BATCH * DEC_SEQ
N_COND = 8
LANES = 128
AB_PAD = LANES
SEG_NA = (0, 3 * W_NA)
SEG_DN = (SEG_NA[1], SEG_NA[1] + 3 * W_DN)
SEG_GATE = (SEG_DN[1], SEG_DN[1] + W_DN)
SEG_DF = (SEG_GATE[1], SEG_GATE[1] + 3 * W_DF)
SEG_AB = (SEG_DF[1], SEG_DF[1] + AB_PAD)
P_PAD = SEG_AB[1]
SEGS = (SEG_NA, SEG_DN, SEG_GATE, SEG_DF, SEG_AB)
BD = H_DN * CHUNK
V_AUG = LANES
DF_Q_SCALE = DF_QK ** -0.5 * math.log2(math.e)
VMEM_LIMIT = 56 * 1024 * 1024


def _params(*sem):
    return pltpu.CompilerParams(dimension_semantics=sem, vmem_limit_bytes=VMEM_LIMIT)


def _dot(a, b):
    return jnp.dot(a.astype(BF16), b.astype(BF16), preferred_element_type=F32)


def _dot_nt(a, b):
    return lax.dot_general(a.astype(BF16), b.astype(BF16), (((1,), (1,)), ((), ())),
                           preferred_element_type=F32)


def _split3(x):
    x1 = x.astype(BF16)
    r1 = x - x1.astype(F32)
    x2 = r1.astype(BF16)
    return x1, x2, (r1 - x2.astype(F32)).astype(BF16)


def _dot_sel(x, sel):
    m = x.shape[0]
    y = jnp.dot(jnp.concatenate(_split3(x), axis=0), sel, preferred_element_type=F32)
    return y[:m] + y[m:2 * m] + y[2 * m:]


def _sel_dot(sel, x):
    n = x.shape[1]
    y = jnp.dot(sel, jnp.concatenate(_split3(x), axis=1), preferred_element_type=F32)
    return y[:, :n] + y[:, n:2 * n] + y[:, 2 * n:]


def _dot_exact(a, b):
    return jnp.dot(a, b, precision=HIGHEST, preferred_element_type=F32)


def _silu(x):
    return x * jax.nn.sigmoid(x)


def _layer_norm(x, g, b):
    mu = jnp.mean(x, axis=-1, keepdims=True)
    xc = x - mu
    var = jnp.mean(xc * xc, axis=-1, keepdims=True)
    return xc * lax.rsqrt(var + LN_EPS) * g + b


def _ada_kernel(c_ref, w_ref, b_ref, o_ref):
    o_ref[...] = _dot_exact(_silu(c_ref[...]), w_ref[...]) + b_ref[...]


def _ada_table(conds, ada_w, ada_b):
    out = pl.pallas_call(
        _ada_kernel,
        grid=(DEPTH, 6),
        in_specs=[pl.BlockSpec((N_COND, D_MODEL), lambda l, k: (0, 0)),
                  pl.BlockSpec((None, D_MODEL, D_MODEL), lambda l, k: (l, 0, k)),
                  pl.BlockSpec((None, None, 1, D_MODEL), lambda l, k: (l, k, 0, 0))],
        out_specs=pl.BlockSpec((None, None, N_COND, D_MODEL), lambda l, k: (l, k, 0, 0)),
        out_shape=jax.ShapeDtypeStruct((DEPTH, 6, N_COND, D_MODEL), F32),
        compiler_params=_params("parallel", "parallel"),
        name="ada_table",
    )(conds, ada_w, ada_b.reshape(DEPTH, 6, 1, D_MODEL))
    return out.reshape(DEPTH, 6, N_COND, 1, D_MODEL)


def _mod_spec(k, cond_fn):
    return pl.BlockSpec((None, None, 1, D_MODEL), lambda i, *_: (k, cond_fn(i), 0, 0))


def _inproj_kernel(x_ref, sh_ref, sc_ref, w_ref, *o_refs):
    h = (x_ref[...] * (1.0 + sc_ref[...]) + sh_ref[...]).astype(BF16)
    for o_ref, (a, b) in zip(o_refs, SEGS):
        o_ref[...] = jnp.dot(h, w_ref[:, a:b], preferred_element_type=F32)


def _inproj(x, mod, w, cond_fn, tm=512):
    t = x.shape[0]
    return pl.pallas_call(
        _inproj_kernel,
        grid=(t // tm,),
        in_specs=[pl.BlockSpec((tm, D_MODEL), lambda i: (i, 0)),
                  _mod_spec(0, cond_fn), _mod_spec(1, cond_fn),
                  pl.BlockSpec((D_MODEL, P_PAD), lambda i: (0, 0))],
        out_specs=[pl.BlockSpec((tm, b - a), lambda i: (i, 0)) for a, b in SEGS],
        out_shape=[jax.ShapeDtypeStruct((t, b - a), F32) for a, b in SEGS],
        compiler_params=_params("parallel"),
        name="inproj",
    )(x, mod, mod, w)


def _dn_prep_kernel(x_ref, prev_ref, next_ref, ab_ref, w_ref, alog_ref, dtb_ref, grp_ref, e_ref, cum_ref,
                    u_ref, gcf_ref, gcb_ref, bf_ref, bb_ref, *, blocks_per_seq, rows):
    i = pl.program_id(0)
    j = i % blocks_per_seq
    prev = jnp.where(j != 0, prev_ref[...], 0.0)
    nxt = jnp.where(j != blocks_per_seq - 1, next_ref[...], 0.0)
    xe = jnp.concatenate([prev, x_ref[...], nxt], axis=0)
    w = w_ref[...]
    base = 8 - CONV_K // 2
    acc = w[0:1, :] * xe[base:base + rows, :]
    for t in range(1, CONV_K):
        acc = acc + w[t:t + 1, :] * xe[base + t:base + t + rows, :]
    u = _silu(acc)
    qk = u[:, :2 * W_DN]
    ss = _dot_sel(qk * qk, grp_ref[...])
    qk = qk * lax.rsqrt(ss + RMS_EPS)
    u_ref[:, :W_DN] = qk[:, :W_DN] * HEAD_DIM ** -0.5
    u_ref[:, W_DN:2 * W_DN] = qk[:, W_DN:]
    u_ref[:, 2 * W_DN:] = u[:, 2 * W_DN:]
    ab = ab_ref[...]
    z = ab + dtb_ref[...]
    softplus = jnp.maximum(z, 0.0) + jnp.log(1.0 + jnp.exp(-jnp.abs(z)))
    g = -jnp.exp(alog_ref[...]) * softplus
    lane = lax.broadcasted_iota(jnp.int32, ab.shape, 1)
    gb = jnp.where(lane < 2 * H_DN, g, jax.nn.sigmoid(ab))
    for d, (gc_ref, beta_ref) in enumerate(((gcf_ref, bf_ref), (gcb_ref, bb_ref))):
        gbx = _dot_sel(gb, e_ref[d])
        gc_ref[...] = _sel_dot(cum_ref[d], gbx[:, :W_DN])
        beta_ref[...] = gbx[:, W_DN:]


def _dn_prep(qkv, ab, conv_w, alog_row, dtb_row, grp, e_mat, cum, seq_len, rows=256):
    t = qkv.shape[0]
    bps = seq_len // rows
    r8 = rows // 8
    last8 = t // 8 - 1
    kern = functools.partial(_dn_prep_kernel, blocks_per_seq=bps, rows=rows)
    return pl.pallas_call(
        kern,
        grid=(t // rows,),
        in_specs=[pl.BlockSpec((rows, 3 * W_DN), lambda i: (i, 0)),
                  pl.BlockSpec((8, 3 * W_DN), lambda i: (jnp.maximum(i * r8 - 1, 0), 0)),
                  pl.BlockSpec((8, 3 * W_DN), lambda i: (jnp.minimum((i + 1) * r8, last8), 0)),
                  pl.BlockSpec((rows, AB_PAD), lambda i: (i, 0)),
                  pl.BlockSpec((8, 3 * W_DN), lambda i: (0, 0)),
                  pl.BlockSpec((1, AB_PAD), lambda i: (0, 0)),
                  pl.BlockSpec((1, AB_PAD), lambda i: (0, 0)),
                  pl.BlockSpec((2 * W_DN, 2 * W_DN), lambda i: (0, 0)),
                  pl.BlockSpec((2, AB_PAD, 2 * W_DN), lambda i: (0, 0, 0)),
                  pl.BlockSpec((2, rows, rows), lambda i: (0, 0, 0))],
        out_specs=[pl.BlockSpec((rows, 3 * W_DN), lambda i: (i, 0))]
        + [pl.BlockSpec((rows, W_DN), lambda i: (i, 0))] * 4,
        out_shape=[jax.ShapeDtypeStruct((t, 3 * W_DN), F32)]
        + [jax.ShapeDtypeStruct((t, W_DN), F32)] * 4,
        compiler_params=_params("parallel"),
        name="dn_prep",
    )(qkv, qkv, qkv, ab, conv_w, alog_row, dtb_row, grp, e_mat, cum)


def _dn_kernel(uf_ref, ub_ref, gcf_ref, gcb_ref, bf_ref, bb_ref, s0_ref, of_ref, ob_ref, sf_ref, s_scr,
               *, n_chunks, n_sub):
    j = pl.program_id(1)
    r = lax.broadcasted_iota(jnp.int32, (BD, BD), 0)
    c = lax.broadcasted_iota(jnp.int32, (BD, BD), 1)
    same = (r // CHUNK) == (c // CHUNK)
    dt = (r % CHUNK) - (c % CHUNK)
    same_f = jnp.where(same, 1.0, 0.0)
    same_b = same_f.astype(BF16)
    eye_f = jnp.where(r == c, 1.0, 0.0)
    dirs = []
    for sign, last in ((1, CHUNK - 1), (-1, 0)):
        dirs.append((jnp.where(jnp.logical_and(same, dt * sign >= 0), 1.0, 0.0),
                     jnp.where(jnp.logical_and(same, dt * sign > 0), 1.0, 0.0), last))

    def lift(x):
        return jnp.concatenate([x, x, x, x], axis=0) * same_f

    def lift_b(x):
        xb = x.astype(BF16)
        return jnp.concatenate([xb, xb, xb, xb], axis=0) * same_b

    @pl.when(j == 0)
    def _():
        for si in range(n_sub):
            for di in range(2):
                rows = [jnp.concatenate([s0_ref[si, di, h]] * H_DN, axis=1) for h in range(H_DN)]
                s_scr[si, di] = jnp.concatenate(rows, axis=0) * same_f

    def each(f, *xs):
        return [f(*a) for a in zip(*xs)]

    def mm(a, b):
        return jnp.dot(a, b, preferred_element_type=F32)

    def chunks(us, gc, beta, ss, incls, stricts, lasts):
        q = [u[:, :W_DN] for u in us]
        k = [u[:, W_DN:2 * W_DN] for u in us]
        v = [u[:, 2 * W_DN:] for u in us]
        gl = each(lambda x, last: x[last:last + 1, :], gc, lasts)
        eg = [jnp.exp(x) for x in gc]
        kb = each(jnp.multiply, k, beta)
        gcol = [lift(x) for x in gc]
        decay = each(lambda x, m: jnp.exp(jnp.where(m > 0.5, x - x.T, NEG_INF)), gcol, incls)
        qk = each(lambda a, b, kk: lax.dot_general(jnp.concatenate([lift_b(a), lift_b(b)], axis=0),
                                                   lift_b(kk), (((1,), (1,)), ((), ())),
                                                   preferred_element_type=F32), q, kb, k)
        attn = each(lambda x, d: (x[:BD] * d).astype(BF16), qk, decay)
        n_mat = each(lambda x, d, m: -(x[BD:] * d) * m, qk, decay, stricts)
        t_inv = [eye_f + x for x in n_mat]
        n_hi = [x.astype(BF16) for x in n_mat]
        m_b = n_hi
        for _ in range(int(math.log2(CHUNK)) - 1):
            m_b = [mm(x, x).astype(BF16) for x in m_b]
            t_inv = each(lambda t, m: t + mm(t.astype(BF16), m), t_inv, m_b)
        n_lo = each(lambda x, h: (x - h.astype(F32)).astype(BF16), n_mat, n_hi)
        x_hi = [t.astype(BF16) for t in t_inv]
        x_lo = each(lambda t, h: (t - h.astype(F32)).astype(BF16), t_inv, x_hi)
        nx = each(lambda h, lo, x: mm(jnp.concatenate([h, lo], axis=0), x), n_hi, n_lo, x_hi)
        nxl = each(mm, n_hi, x_lo)
        resid = each(lambda t, a, b: (eye_f - t + a[:BD] + a[BD:] + b).astype(BF16), t_inv, nx, nxl)
        t_b = each(lambda t, h, rr: (t + mm(h, rr)).astype(BF16), t_inv, x_hi, resid)
        rhs = each(lambda vv, b, kk, e: jnp.concatenate([lift_b(vv * b), lift_b(kk * e)], axis=1),
                   v, beta, kb, eg)
        sol = each(mm, t_b, rhs)
        s_b = [s.astype(BF16) for s in ss]
        ps = each(lambda so, qq, e, sb: mm(jnp.concatenate([so[:, BD:].astype(BF16), lift_b(qq * e)],
                                                           axis=0), sb), sol, q, eg, s_b)
        v_new = each(lambda so, p: (so[:, :BD] - p[:BD]).astype(BF16), sol, ps)
        o_bd = each(lambda p, a, vn: p[BD:] + mm(a, vn), ps, attn, v_new)
        o_tm = [x[0:CHUNK] + x[CHUNK:2 * CHUNK] + x[2 * CHUNK:3 * CHUNK] + x[3 * CHUNK:] for x in o_bd]
        k_tail = each(lambda kk, a, b: lift(kk * jnp.exp(a - b)).T.astype(BF16), k, gl, gc)
        s_new = each(lambda s, a, kt, vn: s * jnp.exp(a) + mm(kt, vn), ss, gl, k_tail, v_new)
        return o_tm, s_new

    chains = [(si, di) for si in range(n_sub) for di in range(2)]

    def body(ci, carry):
        offs = (pl.multiple_of(ci * CHUNK, CHUNK), pl.multiple_of((n_chunks - 1 - ci) * CHUNK, CHUNK))
        u_refs, gc_refs, b_refs, o_refs = (uf_ref, ub_ref), (gcf_ref, gcb_ref), (bf_ref, bb_ref), (of_ref, ob_ref)
        o_tm, s_new = chunks(
            [u_refs[di][si, pl.ds(offs[di], CHUNK), :] for si, di in chains],
            [gc_refs[di][si, pl.ds(offs[di], CHUNK), :] for si, di in chains],
            [b_refs[di][si, pl.ds(offs[di], CHUNK), :] for si, di in chains],
            [s_scr[si, di] for si, di in chains],
            *[[dirs[di][n] for si, di in chains] for n in range(3)])
        for (si, di), o, s in zip(chains, o_tm, s_new):
            o_refs[di][si, pl.ds(offs[di], CHUNK), :] = o
            s_scr[si, di] = s
        return carry

    lax.fori_loop(0, n_chunks, body, 0)

    @pl.when(j == pl.num_programs(1) - 1)
    def _():
        for si in range(n_sub):
            for di in range(2):
                for h in range(H_DN):
                    sl = slice(h * HEAD_DIM, (h + 1) * HEAD_DIM)
                    sf_ref[si, di, h] = s_scr[si, di, sl, sl]


def _deltanet(u, gc_f, gc_b, beta_f, beta_b, s0_bd, seq_len, rows, n_sub=2):
    t = u.shape[0]
    n_seq = t // seq_len
    nb = seq_len // rows
    kern = functools.partial(_dn_kernel, n_chunks=rows // CHUNK, n_sub=n_sub)
    fwd = lambda s, j: (s, j, 0)
    bwd = lambda s, j: (s, nb - 1 - j, 0)
    state = pl.BlockSpec((n_sub, 2, H_DN, HEAD_DIM, HEAD_DIM), lambda s, j: (s, 0, 0, 0, 0))
    o_f, o_b, s_fin = pl.pallas_call(
        kern,
        grid=(n_seq // n_sub, nb),
        in_specs=[pl.BlockSpec((n_sub, rows, 3 * W_DN), fwd), pl.BlockSpec((n_sub, rows, 3 * W_DN), bwd),
                  pl.BlockSpec((n_sub, rows, W_DN), fwd), pl.BlockSpec((n_sub, rows, W_DN), bwd),
                  pl.BlockSpec((n_sub, rows, W_DN), fwd), pl.BlockSpec((n_sub, rows, W_DN), bwd), state],
        out_specs=[pl.BlockSpec((n_sub, rows, W_DN), fwd), pl.BlockSpec((n_sub, rows, W_DN), bwd), state],
        out_shape=[jax.ShapeDtypeStruct((n_seq, seq_len, W_DN), F32),
                   jax.ShapeDtypeStruct((n_seq, seq_len, W_DN), F32),
                   jax.ShapeDtypeStruct((n_seq, 2, H_DN, HEAD_DIM, HEAD_DIM), F32)],
        scratch_shapes=[pltpu.VMEM((n_sub, 2, BD, BD), F32)],
        compiler_params=_params("parallel", "arbitrary"),
        name="deltanet",
    )(*[x.reshape(n_seq, seq_len, -1) for x in (u, u, gc_f, gc_b, beta_f, beta_b)], s0_bd)
    return o_f.reshape(t, W_DN), o_b.reshape(t, W_DN), s_fin


def _diff_lambda(lam_ref, lam_init):
    lp = lam_ref[...]
    return (jnp.exp(jnp.sum(lp[0:1] * lp[1:2], axis=1, keepdims=True))
            - jnp.exp(jnp.sum(lp[2:3] * lp[3:4], axis=1, keepdims=True)) + lam_init)


def _sub_rms(o, subln, lam_init):
    ms = jnp.mean(o * o, axis=-1, keepdims=True)
    return o * lax.rsqrt(ms + RMS_EPS) * subln * (1.0 - lam_init)


def _ctx_attn_kernel(na_ref, df_ref, lam_ref, subln_ref, oa_ref, oc_ref, *, lam_init):
    lam = _diff_lambda(lam_ref, lam_init)
    subln = subln_ref[...]
    n = na_ref.shape[0]
    qa = (na_ref[:, :W_NA] * NA_Q_SCALE).astype(BF16)
    ka = na_ref[:, W_NA:2 * W_NA].astype(BF16)
    va = na_ref[:, 2 * W_NA:].astype(BF16)
    qc = (df_ref[:, :W_DF] * DF_Q_SCALE).astype(BF16)
    kc = df_ref[:, W_DF:2 * W_DF].astype(BF16)
    vc = df_ref[:, 2 * W_DF:].astype(BF16)
    heads = [slice(h * HEAD_DIM, (h + 1) * HEAD_DIM) for h in range(H_NA)]
    maps = [slice(m * DF_QK, (m + 1) * DF_QK) for m in range(2 * H_DF)]
    s_na = [_dot_nt(qa[:, sl], ka[:, sl]) for sl in heads]
    s_df = [_dot_nt(qc[:, sl], kc[:, sl]) for sl in maps]
    p_na = [jnp.exp2(s - jnp.max(s, axis=-1, keepdims=True)) for s in s_na]
    p_df = [jnp.exp2(s - jnp.max(s, axis=-1, keepdims=True)) for s in s_df]
    d_na = [jnp.sum(p, axis=-1, keepdims=True) for p in p_na]
    d_df = [jnp.sum(p, axis=-1, keepdims=True) for p in p_df]
    o_na = [_dot(p, va[:, sl]) for p, sl in zip(p_na, heads)]
    o_df = [_dot(jnp.concatenate([p_df[2 * h].astype(BF16), p_df[2 * h + 1].astype(BF16)], axis=0),
                 vc[:, heads[h]]) for h in range(H_DF)]
    for h in range(H_NA):
        oa_ref[:, heads[h]] = o_na[h] / d_na[h]
    for h in range(H_DF):
        o = o_df[h][:n] / d_df[2 * h] - lam * (o_df[h][n:] / d_df[2 * h + 1])
        oc_ref[:, heads[h]] = _sub_rms(o, subln, lam_init)


def _ctx_attn(qkv_na, qkv_df, lam_p, subln_row, layer):
    lam_init = 0.8 - 0.6 * math.exp(-0.3 * layer)
    kern = functools.partial(_ctx_attn_kernel, lam_init=lam_init)
    return pl.pallas_call(
        kern,
        grid=(BATCH,),
        in_specs=[pl.BlockSpec((SEQ, 3 * W_NA), lambda b: (b, 0)),
                  pl.BlockSpec((SEQ, 3 * W_DF), lambda b: (b, 0)),
                  pl.BlockSpec((4, DF_QK), lambda b: (0, 0)),
                  pl.BlockSpec((1, HEAD_DIM), lambda b: (0, 0))],
        out_specs=[pl.BlockSpec((SEQ, W_NA), lambda b: (b, 0)),
                   pl.BlockSpec((SEQ, W_DF), lambda b: (b, 0))],
        out_shape=[jax.ShapeDtypeStruct((T_CTX, W_NA), F32),
                   jax.ShapeDtypeStruct((T_CTX, W_DF), F32)],
        compiler_params=_params("parallel"),
        name="ctx_attn",
    )(qkv_na, qkv_df, lam_p, subln_row)


NA_R = 4
NA_U = NA_R + WIN_R - 1
NA_GROUPS = GRID_H // NA_R
NA_Q_SCALE = HEAD_DIM ** -0.5 * math.log2(math.e)


def _na_key_start(g):
    return jnp.clip(g * NA_R - WIN_R // 2, 0, GRID_H - NA_U)


def _na_bias_table(rpb):
    qc = np.arange(GRID_W)[:, None]
    kc = np.arange(GRID_W)[None, :]
    cs = np.clip(qc - WIN_C // 2, 0, GRID_W - WIN_C)
    valid = (kc >= cs) & (kc < cs + WIN_C)
    dc = np.clip(kc - qc + (WIN_C - 1), 0, 2 * WIN_C - 2)
    onehot = (dc[None] == np.arange(2 * WIN_C - 1)[:, None, None]).astype(np.float32)
    x = jnp.einsum('hrd,dqk->hrqk', rpb.astype(F32), jnp.asarray(onehot), precision=HIGHEST)
    x = jnp.where(valid[None, None], x * math.log2(math.e), NEG_INF)
    masked = jnp.full((H_NA, GRID_W, GRID_W), NEG_INF, F32)
    classes = []
    for g in (0, 1, NA_GROUPS - 1):
        u0 = int(np.clip(g * NA_R - WIN_R // 2, 0, GRID_H - NA_U))
        rows = []
        for i in range(NA_R):
            r = g * NA_R + i
            w0 = int(np.clip(r - WIN_R // 2, 0, GRID_H - WIN_R))
            blocks = [x[:, u0 + u - r + WIN_R - 1] if w0 <= u0 + u < w0 + WIN_R else masked
                      for u in range(NA_U)]
            rows.append(jnp.stack(blocks, axis=2))
        classes.append(jnp.stack(rows, axis=1))
    return jnp.stack(classes, axis=1).reshape(H_NA, 3, NA_R * GRID_W, NA_U * GRID_W)


def _lat_na_kernel(q_ref, k_ref, v_ref, kc_ref, vc_ref, bias_ref, o_ref):
    g = pl.program_id(1)
    start = pl.multiple_of(_na_key_start(g) * GRID_W, GRID_W)
    q = (q_ref[...] * NA_Q_SCALE).astype(BF16)
    kw = k_ref[pl.ds(start, NA_U * GRID_W), :].astype(BF16)
    vw = v_ref[pl.ds(start, NA_U * GRID_W), :].astype(BF16)
    kc = kc_ref[...]
    vc = vc_ref[...]
    heads = [slice(h * HEAD_DIM, (h + 1) * HEAD_DIM) for h in range(H_NA)]
    s_loc = [_dot_nt(q[:, sl], kw[:, sl]) + bias_ref[h] for h, sl in enumerate(heads)]
    s_ctx = [_dot_nt(q[:, sl], kc[:, sl]) for sl in heads]
    m = [jnp.maximum(jnp.max(a, axis=-1, keepdims=True), jnp.max(b, axis=-1, keepdims=True))
         for a, b in zip(s_loc, s_ctx)]
    p_loc = [jnp.exp2(a - mm) for a, mm in zip(s_loc, m)]
    p_ctx = [jnp.exp2(b - mm) for b, mm in zip(s_ctx, m)]
    den = [jnp.sum(a, axis=-1, keepdims=True) + jnp.sum(b, axis=-1, keepdims=True)
           for a, b in zip(p_loc, p_ctx)]
    o = [_dot(a, vw[:, sl]) + _dot(b, vc[:, sl]) for a, b, sl in zip(p_loc, p_ctx, heads)]
    for h, sl in enumerate(heads):
        o_ref[:, sl] = o[h] / den[h]


def _lat_na(qkv_na, ck, cv, bias_tab):
    rows = NA_R * GRID_W
    cls = lambda g: jnp.where(g == 0, 0, jnp.where(g == NA_GROUPS - 1, 2, 1))
    return pl.pallas_call(
        _lat_na_kernel,
        grid=(DEC_BATCH, NA_GROUPS),
        in_specs=[pl.BlockSpec((rows, W_NA), lambda b, g: (b * NA_GROUPS + g, 0)),
                  pl.BlockSpec((DEC_SEQ, W_NA), lambda b, g: (b, 1)),
                  pl.BlockSpec((DEC_SEQ, W_NA), lambda b, g: (b, 2)),
                  pl.BlockSpec((None, PAST_LEN, W_NA), lambda b, g: (b, 0, 0)),
                  pl.BlockSpec((None, PAST_LEN, W_NA), lambda b, g: (b, 0, 0)),
                  pl.BlockSpec((H_NA, None, rows, NA_U * GRID_W), lambda b, g: (0, cls(g), 0, 0))],
        out_specs=pl.BlockSpec((rows, W_NA), lambda b, g: (b * NA_GROUPS + g, 0)),
        out_shape=jax.ShapeDtypeStruct((T_LAT, W_NA), F32),
        compiler_params=_params("parallel", "arbitrary"),
        name="lat_na",
    )(qkv_na, qkv_na, qkv_na, ck, cv, bias_tab)


def _rope_tables():
    nf = DF_QK // 4
    inv = ROPE_BASE ** (-np.arange(nf, dtype=np.float32) / nf)
    t = np.arange(DEC_SEQ)
    pos = np.stack([t // GRID_W, t % GRID_W], axis=-1).astype(np.float32)
    ang = jnp.asarray(pos[:, :, None] * inv)
    cos, sin = jnp.cos(ang), jnp.sin(ang)
    cos32 = jnp.concatenate([cos, cos], axis=-1).reshape(DEC_SEQ, DF_QK)
    sin32 = jnp.concatenate([-sin, sin], axis=-1).reshape(DEC_SEQ, DF_QK)
    reps = W_DF // DF_QK
    rot = np.zeros((W_DF, W_DF), np.float32)
    for dd in range(W_DF):
        rot[dd + nf if dd % (2 * nf) < nf else dd - nf, dd] = 1.0
    return jnp.tile(cos32, (1, reps)), jnp.tile(sin32, (1, reps)), jnp.asarray(rot, dtype=BF16)


def _rope_kernel(df_ref, cos_ref, sin_ref, rot_ref, q_ref, kt_ref, v_ref):
    cos, sin, rot = cos_ref[...], sin_ref[...], rot_ref[...]
    q = df_ref[:, :W_DF]
    k = df_ref[:, W_DF:2 * W_DF]
    q_ref[...] = ((q * cos + _dot_sel(q, rot) * sin) * DF_Q_SCALE).astype(BF16)
    kt_ref[...] = (k * cos + _dot_sel(k, rot) * sin).T.astype(BF16)
    lane = lax.broadcasted_iota(jnp.int32, (q.shape[0], V_AUG - HEAD_DIM), 1)
    one_col = jnp.where(lane == 0, 1.0, 0.0).astype(BF16)
    for h in range(H_DF):
        a = 2 * W_DF + h * HEAD_DIM
        v_ref[:, h * V_AUG:h * V_AUG + HEAD_DIM] = df_ref[:, a:a + HEAD_DIM].astype(BF16)
        v_ref[:, h * V_AUG + HEAD_DIM:(h + 1) * V_AUG] = one_col


def _rope(qkv_df, cos, sin, rot, tm=512):
    nb = DEC_SEQ // tm
    return pl.pallas_call(
        _rope_kernel,
        grid=(DEC_BATCH, nb),
        in_specs=[pl.BlockSpec((tm, 3 * W_DF), lambda b, i: (b * nb + i, 0)),
                  pl.BlockSpec((tm, W_DF), lambda b, i: (i, 0)),
                  pl.BlockSpec((tm, W_DF), lambda b, i: (i, 0)),
                  pl.BlockSpec((W_DF, W_DF), lambda b, i: (0, 0))],
        out_specs=[pl.BlockSpec((tm, W_DF), lambda b, i: (b * nb + i, 0)),
                   pl.BlockSpec((None, W_DF, tm), lambda b, i: (b, 0, i)),
                   pl.BlockSpec((tm, H_DF * V_AUG), lambda b, i: (b * nb + i, 0))],
        out_shape=[jax.ShapeDtypeStruct((T_LAT, W_DF), BF16),
                   jax.ShapeDtypeStruct((DEC_BATCH, W_DF, DEC_SEQ), BF16),
                   jax.ShapeDtypeStruct((T_LAT, H_DF * V_AUG), BF16)],
        compiler_params=_params("parallel", "parallel"),
        name="rope",
    )(qkv_df, cos, sin, rot)


def _lat_df_kernel(q_ref, kt_ref, v_ref, lam_ref, subln_ref, o_ref, *, lam_init, tq):
    q = q_ref[...]
    lam = _diff_lambda(lam_ref, lam_init)
    subln = subln_ref[...]

    def scores(n):
        a = n * DF_QK
        return jnp.dot(q[:, a:a + DF_QK], kt_ref[a:a + DF_QK, :], preferred_element_type=F32)

    n_maps = 2 * H_DF
    es = []
    s_next = scores(0)
    for n in range(n_maps):
        s = s_next
        if n + 1 < n_maps:
            s_next = scores(n + 1)
        es.append(jnp.exp2(s - jnp.max(s, axis=-1, keepdims=True)).astype(BF16))
        if n % 2 == 1:
            h = n // 2
            ov = jnp.dot(jnp.concatenate(es, axis=0), v_ref[:, h * V_AUG:(h + 1) * V_AUG],
                         preferred_element_type=F32)
            o = (ov[:tq, :HEAD_DIM] / ov[:tq, HEAD_DIM:HEAD_DIM + 1]
                 - lam * (ov[tq:, :HEAD_DIM] / ov[tq:, HEAD_DIM:HEAD_DIM + 1]))
            o_ref[:, h * HEAD_DIM:(h + 1) * HEAD_DIM] = _sub_rms(o, subln, lam_init)
            es = []


def _lat_df(q_r, kt_all, v_all, lam_p, subln_row, layer, tq=256):
    lam_init = 0.8 - 0.6 * math.exp(-0.3 * layer)
    nb = DEC_SEQ // tq
    n_keys = kt_all.shape[2]
    kern = functools.partial(_lat_df_kernel, lam_init=lam_init, tq=tq)
    return pl.pallas_call(
        kern,
        grid=(DEC_BATCH, nb),
        in_specs=[pl.BlockSpec((tq, W_DF), lambda b, i: (b * nb + i, 0)),
                  pl.BlockSpec((None, W_DF, n_keys), lambda b, i: (b, 0, 0)),
                  pl.BlockSpec((None, n_keys, H_DF * V_AUG), lambda b, i: (b, 0, 0)),
                  pl.BlockSpec((4, DF_QK), lambda b, i: (0, 0)),
                  pl.BlockSpec((1, HEAD_DIM), lambda b, i: (0, 0))],
        out_specs=pl.BlockSpec((tq, W_DF), lambda b, i: (b * nb + i, 0)),
        out_shape=jax.ShapeDtypeStruct((T_LAT, W_DF), F32),
        compiler_params=_params("parallel", "arbitrary"),
        name="lat_df",
    )(q_r, kt_all, v_all, lam_p, subln_row)


def _outproj_kernel(oa_ref, of_ref, obk_ref, gate_ref, onorm_ref, grp_ref, oc_ref, y_ref, g1_ref, w_ref,
                    lg_ref, lb_ref, o_ref):
    ob = of_ref[...] + obk_ref[...]
    ms = _dot_sel(ob * ob, grp_ref[...]) * (1.0 / HEAD_DIM)
    ob = ob * lax.rsqrt(ms + RMS_EPS) * onorm_ref[...] * _silu(gate_ref[...])
    o = (jnp.dot(oa_ref[...].astype(BF16), w_ref[:W_NA, :], preferred_element_type=F32)
         + jnp.dot(ob.astype(BF16), w_ref[W_NA:W_NA + W_DN, :], preferred_element_type=F32)
         + jnp.dot(oc_ref[...].astype(BF16), w_ref[W_NA + W_DN:, :], preferred_element_type=F32))
    o_ref[...] = _layer_norm(ALPHA * y_ref[...] + g1_ref[...] * o, lg_ref[...], lb_ref[...])


def _outproj(oa, o_fwd, o_bwd, gate, onorm_row, grp, oc, y, mod, w, ln_g, ln_b, cond_fn, tm=512):
    t = y.shape[0]
    row = lambda n: pl.BlockSpec((tm, n), lambda i: (i, 0))
    vec = pl.BlockSpec((1, D_MODEL), lambda i: (0, 0))
    return pl.pallas_call(
        _outproj_kernel,
        grid=(t // tm,),
        in_specs=[row(W_NA), row(W_DN), row(W_DN),
                  row(W_DN), pl.BlockSpec((1, W_DN), lambda i: (0, 0)),
                  pl.BlockSpec((W_DN, W_DN), lambda i: (0, 0)),
                  row(W_DF), row(D_MODEL), _mod_spec(2, cond_fn),
                  pl.BlockSpec((D_MODEL, D_MODEL), lambda i: (0, 0)), vec, vec],
        out_specs=row(D_MODEL),
        out_shape=jax.ShapeDtypeStruct((t, D_MODEL), F32),
        compiler_params=_params("parallel"),
        name="outproj_ln",
    )(oa, o_fwd, o_bwd, gate, onorm_row, grp, oc, y, mod, w, ln_g, ln_b)


MOE_TM = 1024
MOE_CAP = 288


def _router_kernel(y_ref, sh_ref, sc_ref, w_ref, b_ref, tri_ref, g_ref, rk_ref, rkt_ref, cnt_ref):
    h = y_ref[...] * (1.0 + sc_ref[...]) + sh_ref[...]
    logits = _dot_exact(h, w_ref[...]) + b_ref[...]
    lane = lax.broadcasted_iota(jnp.int32, logits.shape, 1).astype(F32)
    logits = jnp.where(lane < N_EXPERTS, logits, -jnp.inf)
    m1 = jnp.max(logits, axis=-1, keepdims=True)
    i1 = jnp.min(jnp.where(logits == m1, lane, float(LANES)), axis=-1, keepdims=True)
    rest = jnp.where(lane == i1, -jnp.inf, logits)
    m2 = jnp.max(rest, axis=-1, keepdims=True)
    i2 = jnp.min(jnp.where(rest == m2, lane, float(LANES)), axis=-1, keepdims=True)
    e2 = jnp.exp(m2 - m1)
    w1 = 1.0 / (1.0 + e2)
    g_ref[...] = jnp.where(lane == i1, w1, 0.0) + jnp.where(lane == i2, e2 * w1, 0.0)
    routed = jnp.where(lane == i1, 1.0, 0.0) + jnp.where(lane == i2, 1.0, 0.0)
    before = jnp.dot(tri_ref[...], routed.astype(BF16), preferred_element_type=F32)
    rank = jnp.where(routed > 0.5, before, -1.0)
    rk_ref[...] = rank
    rkt_ref[...] = rank.T[:N_EXPERTS, :]
    cnt_ref[...] = jnp.sum(routed, axis=0, keepdims=True)


def _router(y, mod, w_pad, b_pad, tri, cond_fn, tm=MOE_TM):
    t = y.shape[0]
    nt = t // tm
    return pl.pallas_call(
        _router_kernel,
        grid=(nt,),
        in_specs=[pl.BlockSpec((tm, D_MODEL), lambda i: (i, 0)),
                  _mod_spec(3, cond_fn), _mod_spec(4, cond_fn),
                  pl.BlockSpec((D_MODEL, LANES), lambda i: (0, 0)),
                  pl.BlockSpec((1, LANES), lambda i: (0, 0)),
                  pl.BlockSpec((tm, tm), lambda i: (0, 0))],
        out_specs=[pl.BlockSpec((tm, LANES), lambda i: (i, 0)),
                   pl.BlockSpec((tm, LANES), lambda i: (i, 0)),
                   pl.BlockSpec((None, N_EXPERTS, tm), lambda i: (i, 0, 0)),
                   pl.BlockSpec((None, 1, LANES), lambda i: (i, 0, 0))],
        out_shape=[jax.ShapeDtypeStruct((t, LANES), F32),
                   jax.ShapeDtypeStruct((t, LANES), F32),
                   jax.ShapeDtypeStruct((nt, N_EXPERTS, tm), F32),
                   jax.ShapeDtypeStruct((nt, 1, LANES), F32)],
        compiler_params=_params("parallel"),
        name="router",
    )(y, mod, mod, w_pad, b_pad, tri)


def _moe_kernel(cnt_ref, y_ref, sh_ref, sc_ref, g2_ref, gates_ref, rk_ref, rkt_ref, wg_ref, wu_ref, wd_ref,
                lg_ref, lb_ref, o_ref, h_scr, acc_scr, *, tm, cap):
    i = pl.program_id(0)
    e = pl.program_id(1)

    @pl.when(e == 0)
    def _():
        h_scr[...] = (y_ref[...] * (1.0 + sc_ref[...]) + sh_ref[...]).astype(BF16)
        acc_scr[...] = jnp.zeros_like(acc_scr)

    lane = lax.broadcasted_iota(jnp.int32, (tm, LANES), 1)
    gate_col = jnp.sum(jnp.where(lane == e, gates_ref[...], 0.0), axis=-1, keepdims=True)
    rank_col = jnp.sum(jnp.where(lane == e, rk_ref[...], 0.0), axis=-1, keepdims=True)
    rank_row = rkt_ref[pl.ds(e, 1), :]
    n_pass = (cnt_ref[i * N_EXPERTS + e] + cap - 1) // cap

    def body(ps, carry):
        base = (ps * cap).astype(F32)
        slot_r = lax.broadcasted_iota(jnp.int32, (cap, tm), 0).astype(F32) + base
        sel = jnp.where(rank_row == slot_r, 1.0, 0.0).astype(BF16)
        xe = jnp.dot(sel, h_scr[...], preferred_element_type=F32).astype(BF16)
        f = _swiglu(xe, wg_ref, wu_ref, wd_ref).astype(BF16)
        slot_c = lax.broadcasted_iota(jnp.int32, (tm, cap), 1).astype(F32) + base
        sel_t = jnp.where(rank_col == slot_c, 1.0, 0.0).astype(BF16)
        acc_scr[...] += gate_col * jnp.dot(sel_t, f, preferred_element_type=F32)
        return carry

    lax.fori_loop(0, n_pass, body, 0)

    @pl.when(e == N_EXPERTS - 1)
    def _():
        o_ref[...] = _layer_norm(ALPHA * y_ref[...] + g2_ref[...] * acc_scr[...],
                                 lg_ref[...], lb_ref[...])


def _moe(y, mod, gates, rank, rank_t, counts, wg, wu, wd, ln_g, ln_b, cond_fn, tm=MOE_TM, cap=MOE_CAP):
    t = y.shape[0]
    ff = wg.shape[2]
    vec = pl.BlockSpec((1, D_MODEL), lambda i, e, c: (0, 0))
    tok = lambda n: pl.BlockSpec((tm, n), lambda i, e, c: (i, 0))
    kern = functools.partial(_moe_kernel, tm=tm, cap=cap)
    return pl.pallas_call(
        kern,
        grid_spec=pltpu.PrefetchScalarGridSpec(
            num_scalar_prefetch=1,
            grid=(t // tm, N_EXPERTS),
            in_specs=[tok(D_MODEL), _mod_spec(3, cond_fn), _mod_spec(4, cond_fn), _mod_spec(5, cond_fn),
                      tok(LANES), tok(LANES),
                      pl.BlockSpec((None, N_EXPERTS, tm), lambda i, e, c: (i, 0, 0)),
                      pl.BlockSpec((None, D_MODEL, ff), lambda i, e, c: (e, 0, 0)),
                      pl.BlockSpec((None, D_MODEL, ff), lambda i, e, c: (e, 0, 0)),
                      pl.BlockSpec((None, ff, D_MODEL), lambda i, e, c: (e, 0, 0)),
                      vec, vec],
            out_specs=tok(D_MODEL),
            scratch_shapes=[pltpu.VMEM((tm, D_MODEL), BF16), pltpu.VMEM((tm, D_MODEL), F32)]),
        out_shape=jax.ShapeDtypeStruct((t, D_MODEL), F32),
        compiler_params=_params("parallel", "arbitrary"),
        name="moe_ln",
    )(counts, y, mod, mod, mod, gates, rank, rank_t, wg, wu, wd, ln_g, ln_b)


FF_CUT = 768


def _swiglu(h, wg_ref, wu_ref, wd_ref):
    cuts = (0, FF_CUT, wg_ref.shape[1])
    parts = [(jnp.dot(h, wg_ref[:, lo:hi], preferred_element_type=F32),
              jnp.dot(h, wu_ref[:, lo:hi], preferred_element_type=F32))
             for lo, hi in zip(cuts, cuts[1:])]
    outs = [jnp.dot((_silu(a) * b).astype(BF16), wd_ref[lo:hi, :], preferred_element_type=F32)
            for (a, b), lo, hi in zip(parts, cuts, cuts[1:])]
    return outs[0] + outs[1]


def _ffn_kernel(y_ref, sh_ref, sc_ref, g2_ref, wg_ref, wu_ref, wd_ref, lg_ref, lb_ref,
                o_ref, h_scr, acc_scr, *, n_blocks):
    e = pl.program_id(1)

    @pl.when(e == 0)
    def _():
        h_scr[...] = (y_ref[...] * (1.0 + sc_ref[...]) + sh_ref[...]).astype(BF16)
        acc_scr[...] = jnp.zeros_like(acc_scr)

    acc_scr[...] += _swiglu(h_scr[...], wg_ref, wu_ref, wd_ref)

    @pl.when(e == n_blocks - 1)
    def _():
        o_ref[...] = _layer_norm(ALPHA * y_ref[...] + g2_ref[...] * acc_scr[...],
                                 lg_ref[...], lb_ref[...])


def _ffn(y, mod, wg, wu, wd, ln_g, ln_b, cond_fn, tm=512):
    t = y.shape[0]
    n_blocks, _, ff = wg.shape
    vec = pl.BlockSpec((1, D_MODEL), lambda i, e: (0, 0))
    kern = functools.partial(_ffn_kernel, n_blocks=n_blocks)
    return pl.pallas_call(
        kern,
        grid=(t // tm, n_blocks),
        in_specs=[pl.BlockSpec((tm, D_MODEL), lambda i, e: (i, 0)),
                  _mod_spec(3, cond_fn), _mod_spec(4, cond_fn), _mod_spec(5, cond_fn),
                  pl.BlockSpec((None, D_MODEL, ff), lambda i, e: (e, 0, 0)),
                  pl.BlockSpec((None, D_MODEL, ff), lambda i, e: (e, 0, 0)),
                  pl.BlockSpec((None, ff, D_MODEL), lambda i, e: (e, 0, 0)),
                  vec, vec],
        out_specs=pl.BlockSpec((tm, D_MODEL), lambda i, e: (i, 0)),
        out_shape=jax.ShapeDtypeStruct((t, D_MODEL), F32),
        scratch_shapes=[pltpu.VMEM((tm, D_MODEL), BF16), pltpu.VMEM((tm, D_MODEL), F32)],
        compiler_params=_params("parallel", "arbitrary"),
        name="ffn_ln",
    )(y, mod, mod, mod, wg, wu, wd, ln_g, ln_b)


def _permute_w_in(w):
    offs = np.cumsum((0,) + PROJ_SIZES)
    qa, ka, va, qkv, gate, a, b, qc, kc, vc = (w[:, offs[i]:offs[i + 1]] for i in range(10))
    pad = jnp.zeros((D_MODEL, AB_PAD - 4 * H_DN), w.dtype)
    return jnp.concatenate([qa, ka, va, qkv, gate, qc, kc, vc, a, b, pad], axis=1).astype(BF16)


def _pad_row(v, n=AB_PAD):
    v = v.reshape(1, -1)
    return jnp.pad(v, ((0, 0), (0, n - v.shape[1])))


def _group_matrix(n):
    idx = np.arange(n) // HEAD_DIM
    return jnp.asarray((idx[:, None] == idx[None, :]).astype(np.float32), dtype=BF16)


def _expand_matrix():
    e = np.zeros((2, AB_PAD, 2 * W_DN), np.float32)
    for d in range(2):
        for h in range(H_DN):
            e[d, d * H_DN + h, h * HEAD_DIM:(h + 1) * HEAD_DIM] = 1.0
            e[d, 2 * H_DN + d * H_DN + h, W_DN + h * HEAD_DIM:W_DN + (h + 1) * HEAD_DIM] = 1.0
    return jnp.asarray(e, dtype=BF16)


def _chunk_cumsum_matrices(rows):
    r = np.arange(rows)
    same = (r[:, None] // CHUNK) == (r[None, :] // CHUNK)
    return jnp.asarray(np.stack([same & (r[:, None] >= r[None, :]),
                                 same & (r[:, None] <= r[None, :])]).astype(np.float32), dtype=BF16)


def kernel(x_prompt, x_sample, cache_na_k, cache_na_v, cache_df_k, cache_df_v, state_dn, c, c_ctx,
           ada_w, ada_b, w_in, conv_dn, a_log_dn, dt_bias_dn, onorm_dn, rpb_na, lambda_df, subln_df,
           w_out, ln1_g, ln1_b, ln2_g, ln2_b, ffn_w_gate, ffn_w_up, ffn_w_down, router_w, router_b,
           moe_w_gate, moe_w_up, moe_w_down):
    conds = jnp.concatenate([c_ctx[None, :], c, jnp.zeros((N_COND - 1 - DEC_BATCH, D_MODEL), F32)], axis=0)
    mods = _ada_table(conds, ada_w, ada_b)
    ctx_cond = lambda i: 0
    lat_cond = lambda i: 1 + (i * 512) // DEC_SEQ
    lat_cond_moe = lambda i: 1 + (i * MOE_TM) // DEC_SEQ
    tri = jnp.asarray(np.tril(np.ones((MOE_TM, MOE_TM), np.float32), -1), dtype=BF16)

    grp512, grp256 = _group_matrix(2 * W_DN), _group_matrix(W_DN)
    e_mat = _expand_matrix()
    cum = _chunk_cumsum_matrices(256)
    cos, sin, rot = _rope_tables()
    one_col = jnp.zeros((DEC_BATCH, PAST_LEN, H_DF, V_AUG - HEAD_DIM), BF16).at[..., 0].set(1.0)

    y_ctx = x_prompt.reshape(T_CTX, D_MODEL)
    y_lat = x_sample.reshape(T_LAT, D_MODEL)
    ctx_out = []
    for l in range(DEPTH):
        mod = mods[l]
        w_in_l = _permute_w_in(w_in[l])
        w_out_l = w_out[l].astype(BF16)
        conv_w = jnp.pad(conv_dn[l], ((0, 8 - CONV_K), (0, 0)))
        alog_row = _pad_row(a_log_dn[l])
        dtb_row = _pad_row(dt_bias_dn[l])
        onorm_row = jnp.tile(onorm_dn[l], H_DN).reshape(1, W_DN)
        subln_row = subln_df[l].reshape(1, HEAD_DIM)
        lg1, lb1 = ln1_g[l].reshape(1, D_MODEL), ln1_b[l].reshape(1, D_MODEL)
        lg2, lb2 = ln2_g[l].reshape(1, D_MODEL), ln2_b[l].reshape(1, D_MODEL)

        na, dn, gate, df, ab = _inproj(y_ctx, mod, w_in_l, ctx_cond)
        u, *gates_dn = _dn_prep(dn, ab, conv_w, alog_row, dtb_row, grp512, e_mat, cum, SEQ)
        s0 = jnp.zeros((BATCH, 2, H_DN, HEAD_DIM, HEAD_DIM), F32)
        o_f, o_b, s_fin = _deltanet(u, *gates_dn, s0, SEQ, SEQ)
        oa, oc = _ctx_attn(na, df, lambda_df[l], subln_row, l)
        y1_ctx = _outproj(oa, o_f, o_b, gate, onorm_row, grp256, oc, y_ctx, mod, w_out_l, lg1, lb1, ctx_cond)
        ctx_out.append((na[:, W_NA:2 * W_NA].reshape(BATCH, SEQ, H_NA, HEAD_DIM),
                        na[:, 2 * W_NA:].reshape(BATCH, SEQ, H_NA, HEAD_DIM),
                        df[:, W_DF:2 * W_DF].reshape(BATCH, SEQ, H_DF, HEAD_DIM),
                        df[:, 2 * W_DF:].reshape(BATCH, SEQ, H_DF, HEAD_DIM),
                        s_fin))

        na, dn, gate, df, ab = _inproj(y_lat, mod, w_in_l, lat_cond)
        u, *gates_dn = _dn_prep(dn, ab, conv_w, alog_row, dtb_row, grp512, e_mat, cum, DEC_SEQ)
        o_f, o_b, _ = _deltanet(u, *gates_dn, state_dn[:, l], DEC_SEQ, 512)
        oa = _lat_na(na, cache_na_k[:, l].reshape(DEC_BATCH, PAST_LEN, W_NA).astype(BF16),
                     cache_na_v[:, l].reshape(DEC_BATCH, PAST_LEN, W_NA).astype(BF16),
                     _na_bias_table(rpb_na[l]))
        q_r, kt, v_b = _rope(df, cos, sin, rot)
        ktc = jnp.swapaxes(cache_df_k[:, l].reshape(DEC_BATCH, PAST_LEN, W_DF), 1, 2).astype(BF16)
        vc = jnp.concatenate([cache_df_v[:, l].astype(BF16), one_col], axis=-1)
        kt_all = jnp.concatenate([kt, ktc], axis=2)
        v_all = jnp.concatenate([v_b.reshape(DEC_BATCH, DEC_SEQ, H_DF * V_AUG),
                                 vc.reshape(DEC_BATCH, PAST_LEN, H_DF * V_AUG)], axis=1)
        oc = _lat_df(q_r, kt_all, v_all, lambda_df[l], subln_row, l)
        y1_lat = _outproj(oa, o_f, o_b, gate, onorm_row, grp256, oc, y_lat, mod, w_out_l, lg1, lb1, lat_cond)

        i = l // 2
        if l % 2 == 0:
            split = lambda w: jnp.transpose(w.reshape(D_MODEL, D_FF // MOE_FF, MOE_FF), (1, 0, 2))
            wg, wu = split(ffn_w_gate[i]).astype(BF16), split(ffn_w_up[i]).astype(BF16)
            wd = ffn_w_down[i].reshape(D_FF // MOE_FF, MOE_FF, D_MODEL).astype(BF16)
            y_ctx = _ffn(y1_ctx, mod, wg, wu, wd, lg2, lb2, ctx_cond)
            y_lat = _ffn(y1_lat, mod, wg, wu, wd, lg2, lb2, lat_cond)
        else:
            wg, wu, wd = moe_w_gate[i].astype(BF16), moe_w_up[i].astype(BF16), moe_w_down[i].astype(BF16)
            rw = jnp.pad(router_w[i], ((0, 0), (0, LANES - N_EXPERTS)))
            rb = _pad_row(router_b[i], LANES)
            ys = []
            for y1, cond in ((y1_ctx, ctx_cond), (y1_lat, lat_cond_moe)):
                gates, rank, rank_t, cnt = _router(y1, mod, rw, rb, tri, cond)
                counts = cnt[:, 0, :N_EXPERTS].astype(jnp.int32).reshape(-1)
                ys.append(_moe(y1, mod, gates, rank, rank_t, counts, wg, wu, wd, lg2, lb2, cond))
            y_ctx, y_lat = ys

    stack = lambda j: jnp.stack([t[j] for t in ctx_out], axis=1)
    return (y_ctx.reshape(BATCH, SEQ, D_MODEL), y_lat.reshape(DEC_BATCH, DEC_SEQ, D_MODEL),
            stack(0), stack(1), stack(2), stack(3), stack(4))
```

```python
import functools
import math

import jax
import jax.numpy as jnp
import numpy as np
from jax import lax
from jax.experimental import pallas as pl
from jax.experimental.pallas import tpu as pltpu

F32 = jnp.float32
BF16 = jnp.bfloat16
HIGHEST = lax.Precision.HIGHEST

D_MODEL = 1024
BATCH = 32
SEQ = 256
DEPTH = 2
DEC_BATCH = 2
DEC_SEQ = 4096
PAST_LEN = 512
GRID_W = 64
GRID_H = DEC_SEQ // GRID_W
HEAD_DIM = 64
H_NA = 6
H_DN = 4
H_DF = 6
W_NA = H_NA * HEAD_DIM
W_DN = H_DN * HEAD_DIM
W_DF = H_DF * HEAD_DIM
DF_QK = HEAD_DIM // 2
WIN_R = 8
WIN_C = 16
CONV_K = 5
CHUNK = 64
D_FF = 2816
N_EXPERTS = 8
MOE_FF = 1408
ALPHA = (2 * DEPTH) ** 0.25
LN_EPS = 1e-5
RMS_EPS = 1e-6
ROPE_BASE = 10000.0
NEG_INF = -1e30
PROJ_SIZES = (W_NA, W_NA, W_NA, 3 * W_DN, W_DN, 2 * H_DN, 2 * H_DN, W_DF, W_DF, W_DF)

T_CTX = BATCH * SEQ
T_LAT = DEC_BATCH * DEC_SEQ
N_COND = 8
LANES = 128
AB_PAD = LANES
SEG_NA = (0, 3 * W_NA)
SEG_DN = (SEG_NA[1], SEG_NA[1] + 3 * W_DN)
SEG_GATE = (SEG_DN[1], SEG_DN[1] + W_DN)
SEG_DF = (SEG_GATE[1], SEG_GATE[1] + 3 * W_DF)
SEG_AB = (SEG_DF[1], SEG_DF[1] + AB_PAD)
P_PAD = SEG_AB[1]
SEGS = (SEG_NA, SEG_DN, SEG_GATE, SEG_DF, SEG_AB)
BD = H_DN * CHUNK
V_AUG = LANES
DF_Q_SCALE = DF_QK ** -0.5 * math.log2(math.e)
VMEM_LIMIT = 56 * 1024 * 1024


def _params(*sem):
    return pltpu.CompilerParams(dimension_semantics=sem, vmem_limit_bytes=VMEM_LIMIT)


def _dot(a, b):
    return jnp.dot(a.astype(BF16), b.astype(BF16), preferred_element_type=F32)


def _dot_nt(a, b):
    return lax.dot_general(a.astype(BF16), b.astype(BF16), (((1,), (1,)), ((), ())),
                           preferred_element_type=F32)


def _split3(x):
    x1 = x.astype(BF16)
    r1 = x - x1.astype(F32)
    x2 = r1.astype(BF16)
    return x1, x2, (r1 - x2.astype(F32)).astype(BF16)


def _dot_sel(x, sel):
    m = x.shape[0]
    y = jnp.dot(jnp.concatenate(_split3(x), axis=0), sel, preferred_element_type=F32)
    return y[:m] + y[m:2 * m] + y[2 * m:]


def _sel_dot(sel, x):
    n = x.shape[1]
    y = jnp.dot(sel, jnp.concatenate(_split3(x), axis=1), preferred_element_type=F32)
    return y[:, :n] + y[:, n:2 * n] + y[:, 2 * n:]


def _dot_exact(a, b):
    return jnp.dot(a, b, precision=HIGHEST, preferred_element_type=F32)


def _silu(x):
    return x * jax.nn.sigmoid(x)


def _layer_norm(x, g, b):
    mu = jnp.mean(x, axis=-1, keepdims=True)
    xc = x - mu
    var = jnp.mean(xc * xc, axis=-1, keepdims=True)
    return xc * lax.rsqrt(var + LN_EPS) * g + b


def _ada_kernel(c_ref, w_ref, b_ref, o_ref):
    o_ref[...] = _dot_exact(_silu(c_ref[...]), w_ref[...]) + b_ref[...]


def _ada_table(conds, ada_w, ada_b):
    out = pl.pallas_call(
        _ada_kernel,
        grid=(DEPTH, 6),
        in_specs=[pl.BlockSpec((N_COND, D_MODEL), lambda l, k: (0, 0)),
                  pl.BlockSpec((None, D_MODEL, D_MODEL), lambda l, k: (l, 0, k)),
                  pl.BlockSpec((None, None, 1, D_MODEL), lambda l, k: (l, k, 0, 0))],
        out_specs=pl.BlockSpec((None, None, N_COND, D_MODEL), lambda l, k: (l, k, 0, 0)),
        out_shape=jax.ShapeDtypeStruct((DEPTH, 6, N_COND, D_MODEL), F32),
        compiler_params=_params("parallel", "parallel"),
        name="ada_table",
    )(conds, ada_w, ada_b.reshape(DEPTH, 6, 1, D_MODEL))
    return out.reshape(DEPTH, 6, N_COND, 1, D_MODEL)


def _mod_spec(k, cond_fn):
    return pl.BlockSpec((None, None, 1, D_MODEL), lambda i, *_: (k, cond_fn(i), 0, 0))


def _inproj_kernel(x_ref, sh_ref, sc_ref, w_ref, *o_refs):
    h = (x_ref[...] * (1.0 + sc_ref[...]) + sh_ref[...]).astype(BF16)
    for o_ref, (a, b) in zip(o_refs, SEGS):
        o_ref[...] = jnp.dot(h, w_ref[:, a:b], preferred_element_type=F32)


def _inproj(x, mod, w, cond_fn, tm=512):
    t = x.shape[0]
    return pl.pallas_call(
        _inproj_kernel,
        grid=(t // tm,),
        in_specs=[pl.BlockSpec((tm, D_MODEL), lambda i: (i, 0)),
                  _mod_spec(0, cond_fn), _mod_spec(1, cond_fn),
                  pl.BlockSpec((D_MODEL, P_PAD), lambda i: (0, 0))],
        out_specs=[pl.BlockSpec((tm, b - a), lambda i: (i, 0)) for a, b in SEGS],
        out_shape=[jax.ShapeDtypeStruct((t, b - a), F32) for a, b in SEGS],
        compiler_params=_params("parallel"),
        name="inproj",
    )(x, mod, mod, w)


def _dn_prep_kernel(x_ref, prev_ref, next_ref, ab_ref, w_ref, alog_ref, dtb_ref, grp_ref, e_ref, cum_ref,
                    u_ref, gcf_ref, gcb_ref, bf_ref, bb_ref, *, blocks_per_seq, rows):
    i = pl.program_id(0)
    j = i % blocks_per_seq
    prev = jnp.where(j != 0, prev_ref[...], 0.0)
    nxt = jnp.where(j != blocks_per_seq - 1, next_ref[...], 0.0)
    xe = jnp.concatenate([prev, x_ref[...], nxt], axis=0)
    w = w_ref[...]
    base = 8 - CONV_K // 2
    acc = w[0:1, :] * xe[base:base + rows, :]
    for t in range(1, CONV_K):
        acc = acc + w[t:t + 1, :] * xe[base + t:base + t + rows, :]
    u = _silu(acc)
    qk = u[:, :2 * W_DN]
    ss = _dot_sel(qk * qk, grp_ref[...])
    qk = qk * lax.rsqrt(ss + RMS_EPS)
    u_ref[:, :W_DN] = qk[:, :W_DN] * HEAD_DIM ** -0.5
    u_ref[:, W_DN:2 * W_DN] = qk[:, W_DN:]
    u_ref[:, 2 * W_DN:] = u[:, 2 * W_DN:]
    ab = ab_ref[...]
    z = ab + dtb_ref[...]
    softplus = jnp.maximum(z, 0.0) + jnp.log(1.0 + jnp.exp(-jnp.abs(z)))
    g = -jnp.exp(alog_ref[...]) * softplus
    lane = lax.broadcasted_iota(jnp.int32, ab.shape, 1)
    gb = jnp.where(lane < 2 * H_DN, g, jax.nn.sigmoid(ab))
    for d, (gc_ref, beta_ref) in enumerate(((gcf_ref, bf_ref), (gcb_ref, bb_ref))):
        gbx = _dot_sel(gb, e_ref[d])
        gc_ref[...] = _sel_dot(cum_ref[d], gbx[:, :W_DN])
        beta_ref[...] = gbx[:, W_DN:]


def _dn_prep(qkv, ab, conv_w, alog_row, dtb_row, grp, e_mat, cum, seq_len, rows=256):
    t = qkv.shape[0]
    bps = seq_len // rows
    r8 = rows // 8
    last8 = t // 8 - 1
    kern = functools.partial(_dn_prep_kernel, blocks_per_seq=bps, rows=rows)
    return pl.pallas_call(
        kern,
        grid=(t // rows,),
        in_specs=[pl.BlockSpec((rows, 3 * W_DN), lambda i: (i, 0)),
                  pl.BlockSpec((8, 3 * W_DN), lambda i: (jnp.maximum(i * r8 - 1, 0), 0)),
                  pl.BlockSpec((8, 3 * W_DN), lambda i: (jnp.minimum((i + 1) * r8, last8), 0)),
                  pl.BlockSpec((rows, AB_PAD), lambda i: (i, 0)),
                  pl.BlockSpec((8, 3 * W_DN), lambda i: (0, 0)),
                  pl.BlockSpec((1, AB_PAD), lambda i: (0, 0)),
                  pl.BlockSpec((1, AB_PAD), lambda i: (0, 0)),
                  pl.BlockSpec((2 * W_DN, 2 * W_DN), lambda i: (0, 0)),
                  pl.BlockSpec((2, AB_PAD, 2 * W_DN), lambda i: (0, 0, 0)),
                  pl.BlockSpec((2, rows, rows), lambda i: (0, 0, 0))],
        out_specs=[pl.BlockSpec((rows, 3 * W_DN), lambda i: (i, 0))]
        + [pl.BlockSpec((rows, W_DN), lambda i: (i, 0))] * 4,
        out_shape=[jax.ShapeDtypeStruct((t, 3 * W_DN), F32)]
        + [jax.ShapeDtypeStruct((t, W_DN), F32)] * 4,
        compiler_params=_params("parallel"),
        name="dn_prep",
    )(qkv, qkv, qkv, ab, conv_w, alog_row, dtb_row, grp, e_mat, cum)


def _dn_kernel(uf_ref, ub_ref, gcf_ref, gcb_ref, bf_ref, bb_ref, s0_ref, of_ref, ob_ref, sf_ref, s_scr,
               *, n_chunks, n_sub):
    j = pl.program_id(1)
    r = lax.broadcasted_iota(jnp.int32, (BD, BD), 0)
    c = lax.broadcasted_iota(jnp.int32, (BD, BD), 1)
    same = (r // CHUNK) == (c // CHUNK)
    dt = (r % CHUNK) - (c % CHUNK)
    same_f = jnp.where(same, 1.0, 0.0)
    same_b = same_f.astype(BF16)
    eye_f = jnp.where(r == c, 1.0, 0.0)
    dirs = []
    for sign, last in ((1, CHUNK - 1), (-1, 0)):
        dirs.append((jnp.where(jnp.logical_and(same, dt * sign >= 0), 1.0, 0.0),
                     jnp.where(jnp.logical_and(same, dt * sign > 0), 1.0, 0.0), last))

    def lift(x):
        return jnp.concatenate([x, x, x, x], axis=0) * same_f

    def lift_b(x):
        xb = x.astype(BF16)
        return jnp.concatenate([xb, xb, xb, xb], axis=0) * same_b

    @pl.when(j == 0)
    def _():
        for si in range(n_sub):
            for di in range(2):
                rows = [jnp.concatenate([s0_ref[si, di, h]] * H_DN, axis=1) for h in range(H_DN)]
                s_scr[si, di] = jnp.concatenate(rows, axis=0) * same_f

    def each(f, *xs):
        return [f(*a) for a in zip(*xs)]

    def mm(a, b):
        return jnp.dot(a, b, preferred_element_type=F32)

    def chunks(us, gc, beta, ss, incls, stricts, lasts):
        q = [u[:, :W_DN] for u in us]
        k = [u[:, W_DN:2 * W_DN] for u in us]
        v = [u[:, 2 * W_DN:] for u in us]
        gl = each(lambda x, last: x[last:last + 1, :], gc, lasts)
        eg = [jnp.exp(x) for x in gc]
        kb = each(jnp.multiply, k, beta)
        gcol = [lift(x) for x in gc]
        decay = each(lambda x, m: jnp.exp(jnp.where(m > 0.5, x - x.T, NEG_INF)), gcol, incls)
        qk = each(lambda a, b, kk: lax.dot_general(jnp.concatenate([lift_b(a), lift_b(b)], axis=0),
                                                   lift_b(kk), (((1,), (1,)), ((), ())),
                                                   preferred_element_type=F32), q, kb, k)
        attn = each(lambda x, d: (x[:BD] * d).astype(BF16), qk, decay)
        n_mat = each(lambda x, d, m: -(x[BD:] * d) * m, qk, decay, stricts)
        t_inv = [eye_f + x for x in n_mat]
        n_hi = [x.astype(BF16) for x in n_mat]
        m_b = n_hi
        for _ in range(int(math.log2(CHUNK)) - 1):
            m_b = [mm(x, x).astype(BF16) for x in m_b]
            t_inv = each(lambda t, m: t + mm(t.astype(BF16), m), t_inv, m_b)
        n_lo = each(lambda x, h: (x - h.astype(F32)).astype(BF16), n_mat, n_hi)
        x_hi = [t.astype(BF16) for t in t_inv]
        x_lo = each(lambda t, h: (t - h.astype(F32)).astype(BF16), t_inv, x_hi)
        nx = each(lambda h, lo, x: mm(jnp.concatenate([h, lo], axis=0), x), n_hi, n_lo, x_hi)
        nxl = each(mm, n_hi, x_lo)
        resid = each(lambda t, a, b: (eye_f - t + a[:BD] + a[BD:] + b).astype(BF16), t_inv, nx, nxl)
        t_b = each(lambda t, h, rr: (t + mm(h, rr)).astype(BF16), t_inv, x_hi, resid)
        rhs = each(lambda vv, b, kk, e: jnp.concatenate([lift_b(vv * b), lift_b(kk * e)], axis=1),
                   v, beta, kb, eg)
        sol = each(mm, t_b, rhs)
        s_b = [s.astype(BF16) for s in ss]
        ps = each(lambda so, qq, e, sb: mm(jnp.concatenate([so[:, BD:].astype(BF16), lift_b(qq * e)],
                                                           axis=0), sb), sol, q, eg, s_b)
        v_new = each(lambda so, p: (so[:, :BD] - p[:BD]).astype(BF16), sol, ps)
        o_bd = each(lambda p, a, vn: p[BD:] + mm(a, vn), ps, attn, v_new)
        o_tm = [x[0:CHUNK] + x[CHUNK:2 * CHUNK] + x[2 * CHUNK:3 * CHUNK] + x[3 * CHUNK:] for x in o_bd]
        k_tail = each(lambda kk, a, b: lift(kk * jnp.exp(a - b)).T.astype(BF16), k, gl, gc)
        s_new = each(lambda s, a, kt, vn: s * jnp.exp(a) + mm(kt, vn), ss, gl, k_tail, v_new)
        return o_tm, s_new

    chains = [(si, di) for si in range(n_sub) for di in range(2)]

    def body(ci, carry):
        offs = (pl.multiple_of(ci * CHUNK, CHUNK), pl.multiple_of((n_chunks - 1 - ci) * CHUNK, CHUNK))
        u_refs, gc_refs, b_refs, o_refs = (uf_ref, ub_ref), (gcf_ref, gcb_ref), (bf_ref, bb_ref), (of_ref, ob_ref)
        o_tm, s_new = chunks(
            [u_refs[di][si, pl.ds(offs[di], CHUNK), :] for si, di in chains],
            [gc_refs[di][si, pl.ds(offs[di], CHUNK), :] for si, di in chains],
            [b_refs[di][si, pl.ds(offs[di], CHUNK), :] for si, di in chains],
            [s_scr[si, di] for si, di in chains],
            *[[dirs[di][n] for si, di in chains] for n in range(3)])
        for (si, di), o, s in zip(chains, o_tm, s_new):
            o_refs[di][si, pl.ds(offs[di], CHUNK), :] = o
            s_scr[si, di] = s
        return carry

    lax.fori_loop(0, n_chunks, body, 0)

    @pl.when(j == pl.num_programs(1) - 1)
    def _():
        for si in range(n_sub):
            for di in range(2):
                for h in range(H_DN):
                    sl = slice(h * HEAD_DIM, (h + 1) * HEAD_DIM)
                    sf_ref[si, di, h] = s_scr[si, di, sl, sl]


def _deltanet(u, gc_f, gc_b, beta_f, beta_b, s0_bd, seq_len, rows, n_sub=2):
    t = u.shape[0]
    n_seq = t // seq_len
    nb = seq_len // rows
    kern = functools.partial(_dn_kernel, n_chunks=rows // CHUNK, n_sub=n_sub)
    fwd = lambda s, j: (s, j, 0)
    bwd = lambda s, j: (s, nb - 1 - j, 0)
    state = pl.BlockSpec((n_sub, 2, H_DN, HEAD_DIM, HEAD_DIM), lambda s, j: (s, 0, 0, 0, 0))
    o_f, o_b, s_fin = pl.pallas_call(
        kern,
        grid=(n_seq // n_sub, nb),
        in_specs=[pl.BlockSpec((n_sub, rows, 3 * W_DN), fwd), pl.BlockSpec((n_sub, rows, 3 * W_DN), bwd),
                  pl.BlockSpec((n_sub, rows, W_DN), fwd), pl.BlockSpec((n_sub, rows, W_DN), bwd),
                  pl.BlockSpec((n_sub, rows, W_DN), fwd), pl.BlockSpec((n_sub, rows, W_DN), bwd), state],
        out_specs=[pl.BlockSpec((n_sub, rows, W_DN), fwd), pl.BlockSpec((n_sub, rows, W_DN), bwd), state],
        out_shape=[jax.ShapeDtypeStruct((n_seq, seq_len, W_DN), F32),
                   jax.ShapeDtypeStruct((n_seq, seq_len, W_DN), F32),
                   jax.ShapeDtypeStruct((n_seq, 2, H_DN, HEAD_DIM, HEAD_DIM), F32)],
        scratch_shapes=[pltpu.VMEM((n_sub, 2, BD, BD), F32)],
        compiler_params=_params("parallel", "arbitrary"),
        name="deltanet",
    )(*[x.reshape(n_seq, seq_len, -1) for x in (u, u, gc_f, gc_b, beta_f, beta_b)], s0_bd)
    return o_f.reshape(t, W_DN), o_b.reshape(t, W_DN), s_fin


def _diff_lambda(lam_ref, lam_init):
    lp = lam_ref[...]
    return (jnp.exp(jnp.sum(lp[0:1] * lp[1:2], axis=1, keepdims=True))
            - jnp.exp(jnp.sum(lp[2:3] * lp[3:4], axis=1, keepdims=True)) + lam_init)


def _sub_rms(o, subln, lam_init):
    ms = jnp.mean(o * o, axis=-1, keepdims=True)
    return o * lax.rsqrt(ms + RMS_EPS) * subln * (1.0 - lam_init)


def _ctx_attn_kernel(na_ref, df_ref, lam_ref, subln_ref, oa_ref, oc_ref, *, lam_init):
    lam = _diff_lambda(lam_ref, lam_init)
    subln = subln_ref[...]
    n = na_ref.shape[0]
    qa = (na_ref[:, :W_NA] * NA_Q_SCALE).astype(BF16)
    ka = na_ref[:, W_NA:2 * W_NA].astype(BF16)
    va = na_ref[:, 2 * W_NA:].astype(BF16)
    qc = (df_ref[:, :W_DF] * DF_Q_SCALE).astype(BF16)
    kc = df_ref[:, W_DF:2 * W_DF].astype(BF16)
    vc = df_ref[:, 2 * W_DF:].astype(BF16)
    heads = [slice(h * HEAD_DIM, (h + 1) * HEAD_DIM) for h in range(H_NA)]
    maps = [slice(m * DF_QK, (m + 1) * DF_QK) for m in range(2 * H_DF)]
    s_na = [_dot_nt(qa[:, sl], ka[:, sl]) for sl in heads]
    s_df = [_dot_nt(qc[:, sl], kc[:, sl]) for sl in maps]
    p_na = [jnp.exp2(s - jnp.max(s, axis=-1, keepdims=True)) for s in s_na]
    p_df = [jnp.exp2(s - jnp.max(s, axis=-1, keepdims=True)) for s in s_df]
    d_na = [jnp.sum(p, axis=-1, keepdims=True) for p in p_na]
    d_df = [jnp.sum(p, axis=-1, keepdims=True) for p in p_df]
    o_na = [_dot(p, va[:, sl]) for p, sl in zip(p_na, heads)]
    o_df = [_dot(jnp.concatenate([p_df[2 * h].astype(BF16), p_df[2 * h + 1].astype(BF16)], axis=0),
                 vc[:, heads[h]]) for h in range(H_DF)]
    for h in range(H_NA):
        oa_ref[:, heads[h]] = o_na[h] / d_na[h]
    for h in range(H_DF):
        o = o_df[h][:n] / d_df[2 * h] - lam * (o_df[h][n:] / d_df[2 * h + 1])
        oc_ref[:, heads[h]] = _sub_rms(o, subln, lam_init)


def _ctx_attn(qkv_na, qkv_df, lam_p, subln_row, layer):
    lam_init = 0.8 - 0.6 * math.exp(-0.3 * layer)
    kern = functools.partial(_ctx_attn_kernel, lam_init=lam_init)
    return pl.pallas_call(
        kern,
        grid=(BATCH,),
        in_specs=[pl.BlockSpec((SEQ, 3 * W_NA), lambda b: (b, 0)),
                  pl.BlockSpec((SEQ, 3 * W_DF), lambda b: (b, 0)),
                  pl.BlockSpec((4, DF_QK), lambda b: (0, 0)),
                  pl.BlockSpec((1, HEAD_DIM), lambda b: (0, 0))],
        out_specs=[pl.BlockSpec((SEQ, W_NA), lambda b: (b, 0)),
                   pl.BlockSpec((SEQ, W_DF), lambda b: (b, 0))],
        out_shape=[jax.ShapeDtypeStruct((T_CTX, W_NA), F32),
                   jax.ShapeDtypeStruct((T_CTX, W_DF), F32)],
        compiler_params=_params("parallel"),
        name="ctx_attn",
    )(qkv_na, qkv_df, lam_p, subln_row)


NA_R = 4
NA_U = NA_R + WIN_R - 1
NA_GROUPS = GRID_H // NA_R
NA_Q_SCALE = HEAD_DIM ** -0.5 * math.log2(math.e)


def _na_key_start(g):
    return jnp.clip(g * NA_R - WIN_R // 2, 0, GRID_H - NA_U)


def _na_bias_table(rpb):
    qc = np.arange(GRID_W)[:, None]
    kc = np.arange(GRID_W)[None, :]
    cs = np.clip(qc - WIN_C // 2, 0, GRID_W - WIN_C)
    valid = (kc >= cs) & (kc < cs + WIN_C)
    dc = np.clip(kc - qc + (WIN_C - 1), 0, 2 * WIN_C - 2)
    onehot = (dc[None] == np.arange(2 * WIN_C - 1)[:, None, None]).astype(np.float32)
    x = jnp.einsum('hrd,dqk->hrqk', rpb.astype(F32), jnp.asarray(onehot), precision=HIGHEST)
    return jnp.where(valid[None, None], x * math.log2(math.e), NEG_INF)


def _na_fill_bias(tab_ref, bias_ref, g):
    u0 = int(np.clip(g * NA_R - WIN_R // 2, 0, GRID_H - NA_U))
    masked = jnp.full((H_NA, GRID_W, GRID_W), NEG_INF, F32)
    for i in range(NA_R):
        r = g * NA_R + i
        w0 = int(np.clip(r - WIN_R // 2, 0, GRID_H - WIN_R))
        for u in range(NA_U):
            inside = w0 <= u0 + u < w0 + WIN_R
            block = tab_ref[:, u0 + u - r + WIN_R - 1] if inside else masked
            bias_ref[:, i * GRID_W:(i + 1) * GRID_W, u * GRID_W:(u + 1) * GRID_W] = block


def _lat_na_kernel(q_ref, k_ref, v_ref, kc_ref, vc_ref, tab_ref, o_ref, bias_ref):
    g = pl.program_id(1)
    for g_build in (0, 1, NA_GROUPS - 1):
        @pl.when(g == g_build)
        def _():
            _na_fill_bias(tab_ref, bias_ref, g_build)

    start = pl.multiple_of(_na_key_start(g) * GRID_W, GRID_W)
    q = (q_ref[...] * NA_Q_SCALE).astype(BF16)
    kw = k_ref[pl.ds(start, NA_U * GRID_W), :].astype(BF16)
    vw = v_ref[pl.ds(start, NA_U * GRID_W), :].astype(BF16)
    kc = kc_ref[...]
    vc = vc_ref[...]
    heads = [slice(h * HEAD_DIM, (h + 1) * HEAD_DIM) for h in range(H_NA)]
    s_loc = [_dot_nt(q[:, sl], kw[:, sl]) + bias_ref[h] for h, sl in enumerate(heads)]
    s_ctx = [_dot_nt(q[:, sl], kc[:, sl]) for sl in heads]
    m = [jnp.maximum(jnp.max(a, axis=-1, keepdims=True), jnp.max(b, axis=-1, keepdims=True))
         for a, b in zip(s_loc, s_ctx)]
    p_loc = [jnp.exp2(a - mm) for a, mm in zip(s_loc, m)]
    p_ctx = [jnp.exp2(b - mm) for b, mm in zip(s_ctx, m)]
    den = [jnp.sum(a, axis=-1, keepdims=True) + jnp.sum(b, axis=-1, keepdims=True)
           for a, b in zip(p_loc, p_ctx)]
    o = [_dot(a, vw[:, sl]) + _dot(b, vc[:, sl]) for a, b, sl in zip(p_loc, p_ctx, heads)]
    for h, sl in enumerate(heads):
        o_ref[:, sl] = o[h] / den[h]


def _lat_na(qkv_na, ck, cv, bias_tab):
    rows = NA_R * GRID_W
    return pl.pallas_call(
        _lat_na_kernel,
        grid=(DEC_BATCH, NA_GROUPS),
        in_specs=[pl.BlockSpec((rows, W_NA), lambda b, g: (b * NA_GROUPS + g, 0)),
                  pl.BlockSpec((DEC_SEQ, W_NA), lambda b, g: (b, 1)),
                  pl.BlockSpec((DEC_SEQ, W_NA), lambda b, g: (b, 2)),
                  pl.BlockSpec((None, PAST_LEN, W_NA), lambda b, g: (b, 0, 0)),
                  pl.BlockSpec((None, PAST_LEN, W_NA), lambda b, g: (b, 0, 0)),
                  pl.BlockSpec((H_NA, 2 * WIN_R - 1, GRID_W, GRID_W), lambda b, g: (0, 0, 0, 0))],
        out_specs=pl.BlockSpec((rows, W_NA), lambda b, g: (b * NA_GROUPS + g, 0)),
        out_shape=jax.ShapeDtypeStruct((T_LAT, W_NA), F32),
        scratch_shapes=[pltpu.VMEM((H_NA, rows, NA_U * GRID_W), F32)],
        compiler_params=_params("arbitrary", "arbitrary"),
        name="lat_na",
    )(qkv_na, qkv_na, qkv_na, ck, cv, bias_tab)


def _rope_tables():
    nf = DF_QK // 4
    inv = ROPE_BASE ** (-np.arange(nf, dtype=np.float32) / nf)
    t = np.arange(DEC_SEQ)
    pos = np.stack([t // GRID_W, t % GRID_W], axis=-1).astype(np.float32)
    ang = jnp.asarray(pos[:, :, None] * inv)
    cos, sin = jnp.cos(ang), jnp.sin(ang)
    cos32 = jnp.concatenate([cos, cos], axis=-1).reshape(DEC_SEQ, DF_QK)
    sin32 = jnp.concatenate([-sin, sin], axis=-1).reshape(DEC_SEQ, DF_QK)
    reps = W_DF // DF_QK
    rot = np.zeros((W_DF, W_DF), np.float32)
    for dd in range(W_DF):
        rot[dd + nf if dd % (2 * nf) < nf else dd - nf, dd] = 1.0
    return jnp.tile(cos32, (1, reps)), jnp.tile(sin32, (1, reps)), jnp.asarray(rot, dtype=BF16)


def _rope_kernel(df_ref, cos_ref, sin_ref, rot_ref, q_ref, kt_ref, v_ref):
    cos, sin, rot = cos_ref[...], sin_ref[...], rot_ref[...]
    q = df_ref[:, :W_DF]
    k = df_ref[:, W_DF:2 * W_DF]
    q_ref[...] = ((q * cos + _dot_sel(q, rot) * sin) * DF_Q_SCALE).astype(BF16)
    kt_ref[...] = (k * cos + _dot_sel(k, rot) * sin).T.astype(BF16)
    lane = lax.broadcasted_iota(jnp.int32, (q.shape[0], V_AUG - HEAD_DIM), 1)
    one_col = jnp.where(lane == 0, 1.0, 0.0).astype(BF16)
    for h in range(H_DF):
        a = 2 * W_DF + h * HEAD_DIM
        v_ref[:, h * V_AUG:h * V_AUG + HEAD_DIM] = df_ref[:, a:a + HEAD_DIM].astype(BF16)
        v_ref[:, h * V_AUG + HEAD_DIM:(h + 1) * V_AUG] = one_col


def _rope(qkv_df, cos, sin, rot, tm=512):
    nb = DEC_SEQ // tm
    return pl.pallas_call(
        _rope_kernel,
        grid=(DEC_BATCH, nb),
        in_specs=[pl.BlockSpec((tm, 3 * W_DF), lambda b, i: (b * nb + i, 0)),
                  pl.BlockSpec((tm, W_DF), lambda b, i: (i, 0)),
                  pl.BlockSpec((tm, W_DF), lambda b, i: (i, 0)),
                  pl.BlockSpec((W_DF, W_DF), lambda b, i: (0, 0))],
        out_specs=[pl.BlockSpec((tm, W_DF), lambda b, i: (b * nb + i, 0)),
                   pl.BlockSpec((None, W_DF, tm), lambda b, i: (b, 0, i)),
                   pl.BlockSpec((tm, H_DF * V_AUG), lambda b, i: (b * nb + i, 0))],
        out_shape=[jax.ShapeDtypeStruct((T_LAT, W_DF), BF16),
                   jax.ShapeDtypeStruct((DEC_BATCH, W_DF, DEC_SEQ), BF16),
                   jax.ShapeDtypeStruct((T_LAT, H_DF * V_AUG), BF16)],
        compiler_params=_params("parallel", "parallel"),
        name="rope",
    )(qkv_df, cos, sin, rot)


def _lat_df_kernel(q_ref, kt_ref, v_ref, lam_ref, subln_ref, o_ref, *, lam_init, tq):
    q = q_ref[...]
    lam = _diff_lambda(lam_ref, lam_init)
    subln = subln_ref[...]

    def scores(n):
        a = n * DF_QK
        return jnp.dot(q[:, a:a + DF_QK], kt_ref[a:a + DF_QK, :], preferred_element_type=F32)

    n_maps = 2 * H_DF
    es = []
    s_next = scores(0)
    for n in range(n_maps):
        s = s_next
        if n + 1 < n_maps:
            s_next = scores(n + 1)
        es.append(jnp.exp2(s - jnp.max(s, axis=-1, keepdims=True)).astype(BF16))
        if n % 2 == 1:
            h = n // 2
            ov = jnp.dot(jnp.concatenate(es, axis=0), v_ref[:, h * V_AUG:(h + 1) * V_AUG],
                         preferred_element_type=F32)
            o = (ov[:tq, :HEAD_DIM] / ov[:tq, HEAD_DIM:HEAD_DIM + 1]
                 - lam * (ov[tq:, :HEAD_DIM] / ov[tq:, HEAD_DIM:HEAD_DIM + 1]))
            o_ref[:, h * HEAD_DIM:(h + 1) * HEAD_DIM] = _sub_rms(o, subln, lam_init)
            es = []


def _lat_df(q_r, kt_all, v_all, lam_p, subln_row, layer, tq=256):
    lam_init = 0.8 - 0.6 * math.exp(-0.3 * layer)
    nb = DEC_SEQ // tq
    n_keys = kt_all.shape[2]
    kern = functools.partial(_lat_df_kernel, lam_init=lam_init, tq=tq)
    return pl.pallas_call(
        kern,
        grid=(DEC_BATCH, nb),
        in_specs=[pl.BlockSpec((tq, W_DF), lambda b, i: (b * nb + i, 0)),
                  pl.BlockSpec((None, W_DF, n_keys), lambda b, i: (b, 0, 0)),
                  pl.BlockSpec((None, n_keys, H_DF * V_AUG), lambda b, i: (b, 0, 0)),
                  pl.BlockSpec((4, DF_QK), lambda b, i: (0, 0)),
                  pl.BlockSpec((1, HEAD_DIM), lambda b, i: (0, 0))],
        out_specs=pl.BlockSpec((tq, W_DF), lambda b, i: (b * nb + i, 0)),
        out_shape=jax.ShapeDtypeStruct((T_LAT, W_DF), F32),
        compiler_params=_params("parallel", "arbitrary"),
        name="lat_df",
    )(q_r, kt_all, v_all, lam_p, subln_row)


def _outproj_kernel(oa_ref, of_ref, obk_ref, gate_ref, onorm_ref, grp_ref, oc_ref, y_ref, g1_ref, w_ref,
                    lg_ref, lb_ref, o_ref):
    ob = of_ref[...] + obk_ref[...]
    ms = _dot_sel(ob * ob, grp_ref[...]) * (1.0 / HEAD_DIM)
    ob = ob * lax.rsqrt(ms + RMS_EPS) * onorm_ref[...] * _silu(gate_ref[...])
    o = (jnp.dot(oa_ref[...].astype(BF16), w_ref[:W_NA, :], preferred_element_type=F32)
         + jnp.dot(ob.astype(BF16), w_ref[W_NA:W_NA + W_DN, :], preferred_element_type=F32)
         + jnp.dot(oc_ref[...].astype(BF16), w_ref[W_NA + W_DN:, :], preferred_element_type=F32))
    o_ref[...] = _layer_norm(ALPHA * y_ref[...] + g1_ref[...] * o, lg_ref[...], lb_ref[...])


def _outproj(oa, o_fwd, o_bwd, gate, onorm_row, grp, oc, y, mod, w, ln_g, ln_b, cond_fn, tm=512):
    t = y.shape[0]
    row = lambda n: pl.BlockSpec((tm, n), lambda i: (i, 0))
    vec = pl.BlockSpec((1, D_MODEL), lambda i: (0, 0))
    return pl.pallas_call(
        _outproj_kernel,
        grid=(t // tm,),
        in_specs=[row(W_NA), row(W_DN), row(W_DN),
                  row(W_DN), pl.BlockSpec((1, W_DN), lambda i: (0, 0)),
                  pl.BlockSpec((W_DN, W_DN), lambda i: (0, 0)),
                  row(W_DF), row(D_MODEL), _mod_spec(2, cond_fn),
                  pl.BlockSpec((D_MODEL, D_MODEL), lambda i: (0, 0)), vec, vec],
        out_specs=row(D_MODEL),
        out_shape=jax.ShapeDtypeStruct((t, D_MODEL), F32),
        compiler_params=_params("parallel"),
        name="outproj_ln",
    )(oa, o_fwd, o_bwd, gate, onorm_row, grp, oc, y, mod, w, ln_g, ln_b)


MOE_TM = 1024
MOE_CAP = 288


def _router_kernel(y_ref, sh_ref, sc_ref, w_ref, b_ref, tri_ref, g_ref, rk_ref, rkt_ref, cnt_ref):
    h = y_ref[...] * (1.0 + sc_ref[...]) + sh_ref[...]
    logits = _dot_exact(h, w_ref[...]) + b_ref[...]
    lane = lax.broadcasted_iota(jnp.int32, logits.shape, 1).astype(F32)
    logits = jnp.where(lane < N_EXPERTS, logits, -jnp.inf)
    m1 = jnp.max(logits, axis=-1, keepdims=True)
    i1 = jnp.min(jnp.where(logits == m1, lane, float(LANES)), axis=-1, keepdims=True)
    rest = jnp.where(lane == i1, -jnp.inf, logits)
    m2 = jnp.max(rest, axis=-1, keepdims=True)
    i2 = jnp.min(jnp.where(rest == m2, lane, float(LANES)), axis=-1, keepdims=True)
    e2 = jnp.exp(m2 - m1)
    w1 = 1.0 / (1.0 + e2)
    g_ref[...] = jnp.where(lane == i1, w1, 0.0) + jnp.where(lane == i2, e2 * w1, 0.0)
    routed = jnp.where(lane == i1, 1.0, 0.0) + jnp.where(lane == i2, 1.0, 0.0)
    before = jnp.dot(tri_ref[...], routed.astype(BF16), preferred_element_type=F32)
    rank = jnp.where(routed > 0.5, before, -1.0)
    rk_ref[...] = rank
    rkt_ref[...] = rank.T[:N_EXPERTS, :]
    cnt_ref[...] = jnp.sum(routed, axis=0, keepdims=True)


def _router(y, mod, w_pad, b_pad, tri, cond_fn, tm=MOE_TM):
    t = y.shape[0]
    nt = t // tm
    return pl.pallas_call(
        _router_kernel,
        grid=(nt,),
        in_specs=[pl.BlockSpec((tm, D_MODEL), lambda i: (i, 0)),
                  _mod_spec(3, cond_fn), _mod_spec(4, cond_fn),
                  pl.BlockSpec((D_MODEL, LANES), lambda i: (0, 0)),
                  pl.BlockSpec((1, LANES), lambda i: (0, 0)),
                  pl.BlockSpec((tm, tm), lambda i: (0, 0))],
        out_specs=[pl.BlockSpec((tm, LANES), lambda i: (i, 0)),
                   pl.BlockSpec((tm, LANES), lambda i: (i, 0)),
                   pl.BlockSpec((None, N_EXPERTS, tm), lambda i: (i, 0, 0)),
                   pl.BlockSpec((None, 1, LANES), lambda i: (i, 0, 0))],
        out_shape=[jax.ShapeDtypeStruct((t, LANES), F32),
                   jax.ShapeDtypeStruct((t, LANES), F32),
                   jax.ShapeDtypeStruct((nt, N_EXPERTS, tm), F32),
                   jax.ShapeDtypeStruct((nt, 1, LANES), F32)],
        compiler_params=_params("parallel"),
        name="router",
    )(y, mod, mod, w_pad, b_pad, tri)


def _moe_kernel(cnt_ref, y_ref, sh_ref, sc_ref, g2_ref, gates_ref, rk_ref, rkt_ref, wg_ref, wu_ref, wd_ref,
                lg_ref, lb_ref, o_ref, h_scr, acc_scr, *, tm, cap):
    i = pl.program_id(0)
    e = pl.program_id(1)

    @pl.when(e == 0)
    def _():
        h_scr[...] = (y_ref[...] * (1.0 + sc_ref[...]) + sh_ref[...]).astype(BF16)
        acc_scr[...] = jnp.zeros_like(acc_scr)

    lane = lax.broadcasted_iota(jnp.int32, (tm, LANES), 1)
    gate_col = jnp.sum(jnp.where(lane == e, gates_ref[...], 0.0), axis=-1, keepdims=True)
    rank_col = jnp.sum(jnp.where(lane == e, rk_ref[...], 0.0), axis=-1, keepdims=True)
    rank_row = rkt_ref[pl.ds(e, 1), :]
    n_pass = (cnt_ref[i * N_EXPERTS + e] + cap - 1) // cap

    def body(ps, carry):
        base = (ps * cap).astype(F32)
        slot_r = lax.broadcasted_iota(jnp.int32, (cap, tm), 0).astype(F32) + base
        sel = jnp.where(rank_row == slot_r, 1.0, 0.0).astype(BF16)
        xe = jnp.dot(sel, h_scr[...], preferred_element_type=F32).astype(BF16)
        f = _swiglu(xe, wg_ref, wu_ref, wd_ref).astype(BF16)
        slot_c = lax.broadcasted_iota(jnp.int32, (tm, cap), 1).astype(F32) + base
        sel_t = jnp.where(rank_col == slot_c, 1.0, 0.0).astype(BF16)
        acc_scr[...] += gate_col * jnp.dot(sel_t, f, preferred_element_type=F32)
        return carry

    lax.fori_loop(0, n_pass, body, 0)

    @pl.when(e == N_EXPERTS - 1)
    def _():
        o_ref[...] = _layer_norm(ALPHA * y_ref[...] + g2_ref[...] * acc_scr[...],
                                 lg_ref[...], lb_ref[...])


def _moe(y, mod, gates, rank, rank_t, counts, wg, wu, wd, ln_g, ln_b, cond_fn, tm=MOE_TM, cap=MOE_CAP):
    t = y.shape[0]
    ff = wg.shape[2]
    vec = pl.BlockSpec((1, D_MODEL), lambda i, e, c: (0, 0))
    tok = lambda n: pl.BlockSpec((tm, n), lambda i, e, c: (i, 0))
    kern = functools.partial(_moe_kernel, tm=tm, cap=cap)
    return pl.pallas_call(
        kern,
        grid_spec=pltpu.PrefetchScalarGridSpec(
            num_scalar_prefetch=1,
            grid=(t // tm, N_EXPERTS),
            in_specs=[tok(D_MODEL), _mod_spec(3, cond_fn), _mod_spec(4, cond_fn), _mod_spec(5, cond_fn),
                      tok(LANES), tok(LANES),
                      pl.BlockSpec((None, N_EXPERTS, tm), lambda i, e, c: (i, 0, 0)),
                      pl.BlockSpec((None, D_MODEL, ff), lambda i, e, c: (e, 0, 0)),
                      pl.BlockSpec((None, D_MODEL, ff), lambda i, e, c: (e, 0, 0)),
                      pl.BlockSpec((None, ff, D_MODEL), lambda i, e, c: (e, 0, 0)),
                      vec, vec],
            out_specs=tok(D_MODEL),
            scratch_shapes=[pltpu.VMEM((tm, D_MODEL), BF16), pltpu.VMEM((tm, D_MODEL), F32)]),
        out_shape=jax.ShapeDtypeStruct((t, D_MODEL), F32),
        compiler_params=_params("parallel", "arbitrary"),
        name="moe_ln",
    )(counts, y, mod, mod, mod, gates, rank, rank_t, wg, wu, wd, ln_g, ln_b)


FF_CUT = 768


def _swiglu(h, wg_ref, wu_ref, wd_ref):
    cuts = (0, FF_CUT, wg_ref.shape[1])
    parts = [(jnp.dot(h, wg_ref[:, lo:hi], preferred_element_type=F32),
              jnp.dot(h, wu_ref[:, lo:hi], preferred_element_type=F32))
             for lo, hi in zip(cuts, cuts[1:])]
    outs = [jnp.dot((_silu(a) * b).astype(BF16), wd_ref[lo:hi, :], preferred_element_type=F32)
            for (a, b), lo, hi in zip(parts, cuts, cuts[1:])]
    return outs[0] + outs[1]


def _ffn_kernel(y_ref, sh_ref, sc_ref, g2_ref, wg_ref, wu_ref, wd_ref, lg_ref, lb_ref,
                o_ref, h_scr, acc_scr, *, n_blocks):
    e = pl.program_id(1)

    @pl.when(e == 0)
    def _():
        h_scr[...] = (y_ref[...] * (1.0 + sc_ref[...]) + sh_ref[...]).astype(BF16)
        acc_scr[...] = jnp.zeros_like(acc_scr)

    acc_scr[...] += _swiglu(h_scr[...], wg_ref, wu_ref, wd_ref)

    @pl.when(e == n_blocks - 1)
    def _():
        o_ref[...] = _layer_norm(ALPHA * y_ref[...] + g2_ref[...] * acc_scr[...],
                                 lg_ref[...], lb_ref[...])


def _ffn(y, mod, wg, wu, wd, ln_g, ln_b, cond_fn, tm=512, ff=MOE_FF):
    t = y.shape[0]
    n_blocks = wg.shape[1] // ff
    vec = pl.BlockSpec((1, D_MODEL), lambda i, e: (0, 0))
    kern = functools.partial(_ffn_kernel, n_blocks=n_blocks)
    return pl.pallas_call(
        kern,
        grid=(t // tm, n_blocks),
        in_specs=[pl.BlockSpec((tm, D_MODEL), lambda i, e: (i, 0)),
                  _mod_spec(3, cond_fn), _mod_spec(4, cond_fn), _mod_spec(5, cond_fn),
                  pl.BlockSpec((D_MODEL, ff), lambda i, e: (0, e)),
                  pl.BlockSpec((D_MODEL, ff), lambda i, e: (0, e)),
                  pl.BlockSpec((ff, D_MODEL), lambda i, e: (e, 0)),
                  vec, vec],
        out_specs=pl.BlockSpec((tm, D_MODEL), lambda i, e: (i, 0)),
        out_shape=jax.ShapeDtypeStruct((t, D_MODEL), F32),
        scratch_shapes=[pltpu.VMEM((tm, D_MODEL), BF16), pltpu.VMEM((tm, D_MODEL), F32)],
        compiler_params=_params("parallel", "arbitrary"),
        name="ffn_ln",
    )(y, mod, mod, mod, wg, wu, wd, ln_g, ln_b)


def _permute_w_in(w):
    offs = np.cumsum((0,) + PROJ_SIZES)
    qa, ka, va, qkv, gate, a, b, qc, kc, vc = (w[:, offs[i]:offs[i + 1]] for i in range(10))
    pad = jnp.zeros((D_MODEL, AB_PAD - 4 * H_DN), w.dtype)
    return jnp.concatenate([qa, ka, va, qkv, gate, qc, kc, vc, a, b, pad], axis=1).astype(BF16)


def _pad_row(v, n=AB_PAD):
    v = v.reshape(1, -1)
    return jnp.pad(v, ((0, 0), (0, n - v.shape[1])))


def _group_matrix(n):
    idx = np.arange(n) // HEAD_DIM
    return jnp.asarray((idx[:, None] == idx[None, :]).astype(np.float32), dtype=BF16)


def _expand_matrix():
    e = np.zeros((2, AB_PAD, 2 * W_DN), np.float32)
    for d in range(2):
        for h in range(H_DN):
            e[d, d * H_DN + h, h * HEAD_DIM:(h + 1) * HEAD_DIM] = 1.0
            e[d, 2 * H_DN + d * H_DN + h, W_DN + h * HEAD_DIM:W_DN + (h + 1) * HEAD_DIM] = 1.0
    return jnp.asarray(e, dtype=BF16)


def _chunk_cumsum_matrices(rows):
    r = np.arange(rows)
    same = (r[:, None] // CHUNK) == (r[None, :] // CHUNK)
    return jnp.asarray(np.stack([same & (r[:, None] >= r[None, :]),
                                 same & (r[:, None] <= r[None, :])]).astype(np.float32), dtype=BF16)


def kernel(x_prompt, x_sample, cache_na_k, cache_na_v, cache_df_k, cache_df_v, state_dn, c, c_ctx,
           ada_w, ada_b, w_in, conv_dn, a_log_dn, dt_bias_dn, onorm_dn, rpb_na, lambda_df, subln_df,
           w_out, ln1_g, ln1_b, ln2_g, ln2_b, ffn_w_gate, ffn_w_up, ffn_w_down, router_w, router_b,
           moe_w_gate, moe_w_up, moe_w_down):
    conds = jnp.concatenate([c_ctx[None, :], c, jnp.zeros((N_COND - 1 - DEC_BATCH, D_MODEL), F32)], axis=0)
    mods = _ada_table(conds, ada_w, ada_b)
    ctx_cond = lambda i: 0
    lat_cond = lambda i: 1 + (i * 512) // DEC_SEQ
    lat_cond_moe = lambda i: 1 + (i * MOE_TM) // DEC_SEQ
    tri = jnp.asarray(np.tril(np.ones((MOE_TM, MOE_TM), np.float32), -1), dtype=BF16)

    grp512, grp256 = _group_matrix(2 * W_DN), _group_matrix(W_DN)
    e_mat = _expand_matrix()
    cum = _chunk_cumsum_matrices(256)
    cos, sin, rot = _rope_tables()
    one_col = jnp.zeros((DEC_BATCH, PAST_LEN, H_DF, V_AUG - HEAD_DIM), BF16).at[..., 0].set(1.0)

    y_ctx = x_prompt.reshape(T_CTX, D_MODEL)
    y_lat = x_sample.reshape(T_LAT, D_MODEL)
    ctx_out = []
    for l in range(DEPTH):
        mod = mods[l]
        w_in_l = _permute_w_in(w_in[l])
        w_out_l = w_out[l].astype(BF16)
        conv_w = jnp.pad(conv_dn[l], ((0, 8 - CONV_K), (0, 0)))
        alog_row = _pad_row(a_log_dn[l])
        dtb_row = _pad_row(dt_bias_dn[l])
        onorm_row = jnp.tile(onorm_dn[l], H_DN).reshape(1, W_DN)
        subln_row = subln_df[l].reshape(1, HEAD_DIM)
        lg1, lb1 = ln1_g[l].reshape(1, D_MODEL), ln1_b[l].reshape(1, D_MODEL)
        lg2, lb2 = ln2_g[l].reshape(1, D_MODEL), ln2_b[l].reshape(1, D_MODEL)

        na, dn, gate, df, ab = _inproj(y_ctx, mod, w_in_l, ctx_cond)
        u, *gates_dn = _dn_prep(dn, ab, conv_w, alog_row, dtb_row, grp512, e_mat, cum, SEQ)
        s0 = jnp.zeros((BATCH, 2, H_DN, HEAD_DIM, HEAD_DIM), F32)
        o_f, o_b, s_fin = _deltanet(u, *gates_dn, s0, SEQ, SEQ, n_sub=4)
        oa, oc = _ctx_attn(na, df, lambda_df[l], subln_row, l)
        y1_ctx = _outproj(oa, o_f, o_b, gate, onorm_row, grp256, oc, y_ctx, mod, w_out_l, lg1, lb1, ctx_cond)
        ctx_out.append((na[:, W_NA:2 * W_NA].reshape(BATCH, SEQ, H_NA, HEAD_DIM),
                        na[:, 2 * W_NA:].reshape(BATCH, SEQ, H_NA, HEAD_DIM),
                        df[:, W_DF:2 * W_DF].reshape(BATCH, SEQ, H_DF, HEAD_DIM),
                        df[:, 2 * W_DF:].reshape(BATCH, SEQ, H_DF, HEAD_DIM),
                        s_fin))

        na, dn, gate, df, ab = _inproj(y_lat, mod, w_in_l, lat_cond)
        u, *gates_dn = _dn_prep(dn, ab, conv_w, alog_row, dtb_row, grp512, e_mat, cum, DEC_SEQ)
        o_f, o_b, _ = _deltanet(u, *gates_dn, state_dn[:, l], DEC_SEQ, 512)
        oa = _lat_na(na, cache_na_k[:, l].reshape(DEC_BATCH, PAST_LEN, W_NA).astype(BF16),
                     cache_na_v[:, l].reshape(DEC_BATCH, PAST_LEN, W_NA).astype(BF16),
                     _na_bias_table(rpb_na[l]))
        q_r, kt, v_b = _rope(df, cos, sin, rot)
        ktc = jnp.swapaxes(cache_df_k[:, l].reshape(DEC_BATCH, PAST_LEN, W_DF), 1, 2).astype(BF16)
        vc = jnp.concatenate([cache_df_v[:, l].astype(BF16), one_col], axis=-1)
        kt_all = jnp.concatenate([kt, ktc], axis=2)
        v_all = jnp.concatenate([v_b.reshape(DEC_BATCH, DEC_SEQ, H_DF * V_AUG),
                                 vc.reshape(DEC_BATCH, PAST_LEN, H_DF * V_AUG)], axis=1)
        oc = _lat_df(q_r, kt_all, v_all, lambda_df[l], subln_row, l)
        y1_lat = _outproj(oa, o_f, o_b, gate, onorm_row, grp256, oc, y_lat, mod, w_out_l, lg1, lb1, lat_cond)

        i = l // 2
        if l % 2 == 0:
            wg, wu, wd = ffn_w_gate[i].astype(BF16), ffn_w_up[i].astype(BF16), ffn_w_down[i].astype(BF16)
            y_ctx = _ffn(y1_ctx, mod, wg, wu, wd, lg2, lb2, ctx_cond)
            y_lat = _ffn(y1_lat, mod, wg, wu, wd, lg2, lb2, lat_cond)
        else:
            wg, wu, wd = moe_w_gate[i].astype(BF16), moe_w_up[i].astype(BF16), moe_w_down[i].astype(BF16)
            rw = jnp.pad(router_w[i], ((0, 0), (0, LANES - N_EXPERTS)))
            rb = _pad_row(router_b[i], LANES)
            ys = []
            for y1, cond in ((y1_ctx, ctx_cond), (y1_lat, lat_cond_moe)):
                gates, rank, rank_t, cnt = _router(y1, mod, rw, rb, tri, cond)
                counts = cnt[:, 0, :N_EXPERTS].astype(jnp.int32).reshape(-1)
                ys.append(_moe(y1, mod, gates, rank, rank_t, counts, wg, wu, wd, lg2, lb2, cond))
            y_ctx, y_lat = ys

    stack = lambda j: jnp.stack([t[j] for t in ctx_out], axis=1)
    return (y_ctx.reshape(BATCH, SEQ, D_MODEL), y_lat.reshape(DEC_BATCH, DEC_SEQ, D_MODEL),
            stack(0), stack(1), stack(2), stack(3), stack(4))
```

```python
import functools
import math

import jax
import jax.numpy as jnp
import numpy as np
from jax import lax
from jax.experimental import pallas as pl
from jax.experimental.pallas import tpu as pltpu

F32 = jnp.float32
BF16 = jnp.bfloat16
HIGHEST = lax.Precision.HIGHEST

D_MODEL = 1024
BATCH = 32
SEQ = 256
DEPTH = 2
DEC_BATCH = 2
DEC_SEQ = 4096
PAST_LEN = 512
GRID_W = 64
GRID_H = DEC_SEQ // GRID_W
HEAD_DIM = 64
H_NA = 6
H_DN = 4
H_DF = 6
W_NA = H_NA * HEAD_DIM
W_DN = H_DN * HEAD_DIM
W_DF = H_DF * HEAD_DIM
DF_QK = HEAD_DIM // 2
WIN_R = 8
WIN_C = 16
CONV_K = 5
CHUNK = 64
D_FF = 2816
N_EXPERTS = 8
MOE_FF = 1408
ALPHA = (2 * DEPTH) ** 0.25
LN_EPS = 1e-5
RMS_EPS = 1e-6
ROPE_BASE = 10000.0
NEG_INF = -1e30
PROJ_SIZES = (W_NA, W_NA, W_NA, 3 * W_DN, W_DN, 2 * H_DN, 2 * H_DN, W_DF, W_DF, W_DF)

T_CTX = BATCH * SEQ
T_LAT = DEC_BATCH * DEC_SEQ
N_COND = 8
LANES = 128
AB_PAD = LANES
SEG_WIDTHS = (W_NA, W_NA, W_NA, 3 * W_DN, W_DN, W_DF, W_DF, W_DF, AB_PAD)
SEG_OFFS = tuple(int(o) for o in np.cumsum((0,) + SEG_WIDTHS))
SEGS = tuple(zip(SEG_OFFS[:-1], SEG_OFFS[1:]))
SEG_GROUPS = ((0, 1, 2), (3,), (4,), (5, 6, 7), (8,))
P_PAD = SEG_OFFS[-1]
BD = H_DN * CHUNK
V_AUG = LANES
DF_Q_SCALE = DF_QK ** -0.5 * math.log2(math.e)
VMEM_LIMIT = 56 * 1024 * 1024


def _params(*sem):
    return pltpu.CompilerParams(dimension_semantics=sem, vmem_limit_bytes=VMEM_LIMIT)


def _dot(a, b):
    return jnp.dot(a.astype(BF16), b.astype(BF16), preferred_element_type=F32)


def _dot_nt(a, b):
    return lax.dot_general(a.astype(BF16), b.astype(BF16), (((1,), (1,)), ((), ())),
                           preferred_element_type=F32)


def _split3(x):
    x1 = x.astype(BF16)
    r1 = x - x1.astype(F32)
    x2 = r1.astype(BF16)
    return x1, x2, (r1 - x2.astype(F32)).astype(BF16)


def _dot_sel(x, sel):
    m = x.shape[0]
    y = jnp.dot(jnp.concatenate(_split3(x), axis=0), sel, preferred_element_type=F32)
    return y[:m] + y[m:2 * m] + y[2 * m:]


def _sel_dot(sel, x):
    n = x.shape[1]
    y = jnp.dot(sel, jnp.concatenate(_split3(x), axis=1), preferred_element_type=F32)
    return y[:, :n] + y[:, n:2 * n] + y[:, 2 * n:]


def _dot_exact(a, b):
    return jnp.dot(a, b, precision=HIGHEST, preferred_element_type=F32)


def _silu(x):
    return x * jax.nn.sigmoid(x)


def _layer_norm(x, g, b):
    mu = jnp.mean(x, axis=-1, keepdims=True)
    xc = x - mu
    var = jnp.mean(xc * xc, axis=-1, keepdims=True)
    return xc * lax.rsqrt(var + LN_EPS) * g + b


def _ada_kernel(c_ref, w_ref, b_ref, o_ref):
    o_ref[...] = _dot_exact(_silu(c_ref[...]), w_ref[...]) + b_ref[...]


def _ada_table(conds, ada_w, ada_b):
    out = pl.pallas_call(
        _ada_kernel,
        grid=(DEPTH, 6),
        in_specs=[pl.BlockSpec((N_COND, D_MODEL), lambda l, k: (0, 0)),
                  pl.BlockSpec((None, D_MODEL, D_MODEL), lambda l, k: (l, 0, k)),
                  pl.BlockSpec((None, None, 1, D_MODEL), lambda l, k: (l, k, 0, 0))],
        out_specs=pl.BlockSpec((None, None, N_COND, D_MODEL), lambda l, k: (l, k, 0, 0)),
        out_shape=jax.ShapeDtypeStruct((DEPTH, 6, N_COND, D_MODEL), F32),
        compiler_params=_params("parallel", "parallel"),
        name="ada_table",
    )(conds, ada_w, ada_b.reshape(DEPTH, 6, 1, D_MODEL))
    return out.reshape(DEPTH, 6, N_COND, 1, D_MODEL)


def _mod_spec(k, cond_fn):
    return pl.BlockSpec((None, None, 1, D_MODEL), lambda i, *_: (k, cond_fn(i), 0, 0))


def _inproj_kernel(x_ref, sh_ref, sc_ref, w_ref, *o_refs):
    h = (x_ref[...] * (1.0 + sc_ref[...]) + sh_ref[...]).astype(BF16)
    for group in SEG_GROUPS:
        lo = SEG_OFFS[group[0]]
        res = jnp.dot(h, w_ref[:, lo:SEG_OFFS[group[-1] + 1]], preferred_element_type=F32)
        for n in group:
            o_refs[n][...] = res[:, SEG_OFFS[n] - lo:SEG_OFFS[n + 1] - lo]


def _inproj(x, mod, w, cond_fn, tm=512):
    t = x.shape[0]
    return pl.pallas_call(
        _inproj_kernel,
        grid=(t // tm,),
        in_specs=[pl.BlockSpec((tm, D_MODEL), lambda i: (i, 0)),
                  _mod_spec(0, cond_fn), _mod_spec(1, cond_fn),
                  pl.BlockSpec((D_MODEL, P_PAD), lambda i: (0, 0))],
        out_specs=[pl.BlockSpec((tm, b - a), lambda i: (i, 0)) for a, b in SEGS],
        out_shape=[jax.ShapeDtypeStruct((t, b - a), F32) for a, b in SEGS],
        compiler_params=_params("parallel"),
        name="inproj",
    )(x, mod, mod, w)


def _dn_prep_kernel(x_ref, prev_ref, next_ref, ab_ref, w_ref, alog_ref, dtb_ref, grp_ref, e_ref, cum_ref,
                    u_ref, gcf_ref, gcb_ref, bf_ref, bb_ref, *, blocks_per_seq, rows):
    i = pl.program_id(0)
    j = i % blocks_per_seq
    prev = jnp.where(j != 0, prev_ref[...], 0.0)
    nxt = jnp.where(j != blocks_per_seq - 1, next_ref[...], 0.0)
    xe = jnp.concatenate([prev, x_ref[...], nxt], axis=0)
    w = w_ref[...]
    base = 8 - CONV_K // 2
    acc = w[0:1, :] * xe[base:base + rows, :]
    for t in range(1, CONV_K):
        acc = acc + w[t:t + 1, :] * xe[base + t:base + t + rows, :]
    u = _silu(acc)
    qk = u[:, :2 * W_DN]
    ss = _dot_sel(qk * qk, grp_ref[...])
    qk = qk * lax.rsqrt(ss + RMS_EPS)
    u_ref[:, :W_DN] = qk[:, :W_DN] * HEAD_DIM ** -0.5
    u_ref[:, W_DN:2 * W_DN] = qk[:, W_DN:]
    u_ref[:, 2 * W_DN:] = u[:, 2 * W_DN:]
    ab = ab_ref[...]
    z = ab + dtb_ref[...]
    softplus = jnp.maximum(z, 0.0) + jnp.log(1.0 + jnp.exp(-jnp.abs(z)))
    g = -jnp.exp(alog_ref[...]) * softplus
    lane = lax.broadcasted_iota(jnp.int32, ab.shape, 1)
    gb = jnp.where(lane < 2 * H_DN, g, jax.nn.sigmoid(ab))
    for d, (gc_ref, beta_ref) in enumerate(((gcf_ref, bf_ref), (gcb_ref, bb_ref))):
        gbx = _dot_sel(gb, e_ref[d])
        gc_ref[...] = _sel_dot(cum_ref[d], gbx[:, :W_DN])
        beta_ref[...] = gbx[:, W_DN:]


def _dn_prep(qkv, ab, conv_w, alog_row, dtb_row, grp, e_mat, cum, seq_len, rows=256):
    t = qkv.shape[0]
    bps = seq_len // rows
    r8 = rows // 8
    last8 = t // 8 - 1
    kern = functools.partial(_dn_prep_kernel, blocks_per_seq=bps, rows=rows)
    return pl.pallas_call(
        kern,
        grid=(t // rows,),
        in_specs=[pl.BlockSpec((rows, 3 * W_DN), lambda i: (i, 0)),
                  pl.BlockSpec((8, 3 * W_DN), lambda i: (jnp.maximum(i * r8 - 1, 0), 0)),
                  pl.BlockSpec((8, 3 * W_DN), lambda i: (jnp.minimum((i + 1) * r8, last8), 0)),
                  pl.BlockSpec((rows, AB_PAD), lambda i: (i, 0)),
                  pl.BlockSpec((8, 3 * W_DN), lambda i: (0, 0)),
                  pl.BlockSpec((1, AB_PAD), lambda i: (0, 0)),
                  pl.BlockSpec((1, AB_PAD), lambda i: (0, 0)),
                  pl.BlockSpec((2 * W_DN, 2 * W_DN), lambda i: (0, 0)),
                  pl.BlockSpec((2, AB_PAD, 2 * W_DN), lambda i: (0, 0, 0)),
                  pl.BlockSpec((2, rows, rows), lambda i: (0, 0, 0))],
        out_specs=[pl.BlockSpec((rows, 3 * W_DN), lambda i: (i, 0))]
        + [pl.BlockSpec((rows, W_DN), lambda i: (i, 0))] * 4,
        out_shape=[jax.ShapeDtypeStruct((t, 3 * W_DN), F32)]
        + [jax.ShapeDtypeStruct((t, W_DN), F32)] * 4,
        compiler_params=_params("parallel"),
        name="dn_prep",
    )(qkv, qkv, qkv, ab, conv_w, alog_row, dtb_row, grp, e_mat, cum)


def _dn_kernel(uf_ref, ub_ref, gcf_ref, gcb_ref, bf_ref, bb_ref, s0_ref, of_ref, ob_ref, sf_ref, s_scr,
               *, n_chunks, n_sub):
    j = pl.program_id(1)
    r = lax.broadcasted_iota(jnp.int32, (BD, BD), 0)
    c = lax.broadcasted_iota(jnp.int32, (BD, BD), 1)
    same = (r // CHUNK) == (c // CHUNK)
    dt = (r % CHUNK) - (c % CHUNK)
    same_f = jnp.where(same, 1.0, 0.0)
    same_b = same_f.astype(BF16)
    eye_f = jnp.where(r == c, 1.0, 0.0)
    dirs = []
    for sign, last in ((1, CHUNK - 1), (-1, 0)):
        dirs.append((jnp.where(jnp.logical_and(same, dt * sign >= 0), 1.0, 0.0),
                     jnp.where(jnp.logical_and(same, dt * sign > 0), 1.0, 0.0), last))

    def lift(x):
        return jnp.concatenate([x, x, x, x], axis=0) * same_f

    def lift_b(x):
        xb = x.astype(BF16)
        return jnp.concatenate([xb, xb, xb, xb], axis=0) * same_b

    @pl.when(j == 0)
    def _():
        for si in range(n_sub):
            for di in range(2):
                rows = [jnp.concatenate([s0_ref[si, di, h]] * H_DN, axis=1) for h in range(H_DN)]
                s_scr[si, di] = jnp.concatenate(rows, axis=0) * same_f

    def each(f, *xs):
        return [f(*a) for a in zip(*xs)]

    def mm(a, b):
        return jnp.dot(a, b, preferred_element_type=F32)

    def chunks(us, gc, beta, ss, incls, stricts, lasts):
        q = [u[:, :W_DN] for u in us]
        k = [u[:, W_DN:2 * W_DN] for u in us]
        v = [u[:, 2 * W_DN:] for u in us]
        gl = each(lambda x, last: x[last:last + 1, :], gc, lasts)
        eg = [jnp.exp(x) for x in gc]
        kb = each(jnp.multiply, k, beta)
        gcol = [lift(x) for x in gc]
        decay = each(lambda x, m: jnp.exp(jnp.where(m > 0.5, x - x.T, NEG_INF)), gcol, incls)
        qk = each(lambda a, b, kk: lax.dot_general(jnp.concatenate([lift_b(a), lift_b(b)], axis=0),
                                                   lift_b(kk), (((1,), (1,)), ((), ())),
                                                   preferred_element_type=F32), q, kb, k)
        attn = each(lambda x, d: (x[:BD] * d).astype(BF16), qk, decay)
        n_mat = each(lambda x, d, m: -(x[BD:] * d) * m, qk, decay, stricts)
        t_inv = [eye_f + x for x in n_mat]
        n_hi = [x.astype(BF16) for x in n_mat]
        m_b = n_hi
        for _ in range(int(math.log2(CHUNK)) - 1):
            m_b = [mm(x, x).astype(BF16) for x in m_b]
            t_inv = each(lambda t, m: t + mm(t.astype(BF16), m), t_inv, m_b)
        n_lo = each(lambda x, h: (x - h.astype(F32)).astype(BF16), n_mat, n_hi)
        x_hi = [t.astype(BF16) for t in t_inv]
        x_lo = each(lambda t, h: (t - h.astype(F32)).astype(BF16), t_inv, x_hi)
        nx = each(lambda h, lo, x: mm(jnp.concatenate([h, lo], axis=0), x), n_hi, n_lo, x_hi)
        nxl = each(mm, n_hi, x_lo)
        resid = each(lambda t, a, b: (eye_f - t + a[:BD] + a[BD:] + b).astype(BF16), t_inv, nx, nxl)
        t_b = each(lambda t, h, rr: (t + mm(h, rr)).astype(BF16), t_inv, x_hi, resid)
        rhs = each(lambda vv, b, kk, e: jnp.concatenate([lift_b(vv * b), lift_b(kk * e)], axis=1),
                   v, beta, kb, eg)
        sol = each(mm, t_b, rhs)
        s_b = [s.astype(BF16) for s in ss]
        ps = each(lambda so, qq, e, sb: mm(jnp.concatenate([so[:, BD:].astype(BF16), lift_b(qq * e)],
                                                           axis=0), sb), sol, q, eg, s_b)
        v_new = each(lambda so, p: (so[:, :BD] - p[:BD]).astype(BF16), sol, ps)
        o_bd = each(lambda p, a, vn: p[BD:] + mm(a, vn), ps, attn, v_new)
        o_tm = [x[0:CHUNK] + x[CHUNK:2 * CHUNK] + x[2 * CHUNK:3 * CHUNK] + x[3 * CHUNK:] for x in o_bd]
        k_tail = each(lambda kk, a, b: lift(kk * jnp.exp(a - b)).T.astype(BF16), k, gl, gc)
        s_new = each(lambda s, a, kt, vn: s * jnp.exp(a) + mm(kt, vn), ss, gl, k_tail, v_new)
        return o_tm, s_new

    chains = [(si, di) for si in range(n_sub) for di in range(2)]

    def body(ci, carry):
        offs = (pl.multiple_of(ci * CHUNK, CHUNK), pl.multiple_of((n_chunks - 1 - ci) * CHUNK, CHUNK))
        u_refs, gc_refs, b_refs, o_refs = (uf_ref, ub_ref), (gcf_ref, gcb_ref), (bf_ref, bb_ref), (of_ref, ob_ref)
        o_tm, s_new = chunks(
            [u_refs[di][si, pl.ds(offs[di], CHUNK), :] for si, di in chains],
            [gc_refs[di][si, pl.ds(offs[di], CHUNK), :] for si, di in chains],
            [b_refs[di][si, pl.ds(offs[di], CHUNK), :] for si, di in chains],
            [s_scr[si, di] for si, di in chains],
            *[[dirs[di][n] for si, di in chains] for n in range(3)])
        for (si, di), o, s in zip(chains, o_tm, s_new):
            o_refs[di][si, pl.ds(offs[di], CHUNK), :] = o
            s_scr[si, di] = s
        return carry

    lax.fori_loop(0, n_chunks, body, 0)

    @pl.when(j == pl.num_programs(1) - 1)
    def _():
        for si in range(n_sub):
            for di in range(2):
                for h in range(H_DN):
                    sl = slice(h * HEAD_DIM, (h + 1) * HEAD_DIM)
                    sf_ref[si, di, h] = s_scr[si, di, sl, sl]


def _deltanet(u, gc_f, gc_b, beta_f, beta_b, s0_bd, seq_len, rows, n_sub=2):
    t = u.shape[0]
    n_seq = t // seq_len
    nb = seq_len // rows
    kern = functools.partial(_dn_kernel, n_chunks=rows // CHUNK, n_sub=n_sub)
    fwd = lambda s, j: (s, j, 0)
    bwd = lambda s, j: (s, nb - 1 - j, 0)
    state = pl.BlockSpec((n_sub, 2, H_DN, HEAD_DIM, HEAD_DIM), lambda s, j: (s, 0, 0, 0, 0))
    o_f, o_b, s_fin = pl.pallas_call(
        kern,
        grid=(n_seq // n_sub, nb),
        in_specs=[pl.BlockSpec((n_sub, rows, 3 * W_DN), fwd), pl.BlockSpec((n_sub, rows, 3 * W_DN), bwd),
                  pl.BlockSpec((n_sub, rows, W_DN), fwd), pl.BlockSpec((n_sub, rows, W_DN), bwd),
                  pl.BlockSpec((n_sub, rows, W_DN), fwd), pl.BlockSpec((n_sub, rows, W_DN), bwd), state],
        out_specs=[pl.BlockSpec((n_sub, rows, W_DN), fwd), pl.BlockSpec((n_sub, rows, W_DN), bwd), state],
        out_shape=[jax.ShapeDtypeStruct((n_seq, seq_len, W_DN), F32),
                   jax.ShapeDtypeStruct((n_seq, seq_len, W_DN), F32),
                   jax.ShapeDtypeStruct((n_seq, 2, H_DN, HEAD_DIM, HEAD_DIM), F32)],
        scratch_shapes=[pltpu.VMEM((n_sub, 2, BD, BD), F32)],
        compiler_params=_params("parallel", "arbitrary"),
        name="deltanet",
    )(*[x.reshape(n_seq, seq_len, -1) for x in (u, u, gc_f, gc_b, beta_f, beta_b)], s0_bd)
    return o_f.reshape(t, W_DN), o_b.reshape(t, W_DN), s_fin


def _diff_lambda(lam_ref, lam_init):
    lp = lam_ref[...]
    return (jnp.exp(jnp.sum(lp[0:1] * lp[1:2], axis=1, keepdims=True))
            - jnp.exp(jnp.sum(lp[2:3] * lp[3:4], axis=1, keepdims=True)) + lam_init)


def _sub_rms(o, subln, lam_init):
    ms = jnp.mean(o * o, axis=-1, keepdims=True)
    return o * lax.rsqrt(ms + RMS_EPS) * subln * (1.0 - lam_init)


def _ctx_attn_kernel(qa_ref, ka_ref, va_ref, qc_ref, kc_ref, vc_ref, lam_ref, subln_ref, oa_ref, oc_ref,
                     *, lam_init):
    lam = _diff_lambda(lam_ref, lam_init)
    subln = subln_ref[...]
    n = qa_ref.shape[0]
    qa = (qa_ref[...] * NA_Q_SCALE).astype(BF16)
    ka = ka_ref[...].astype(BF16)
    va = va_ref[...].astype(BF16)
    qc = (qc_ref[...] * DF_Q_SCALE).astype(BF16)
    kc = kc_ref[...].astype(BF16)
    vc = vc_ref[...].astype(BF16)
    heads = [slice(h * HEAD_DIM, (h + 1) * HEAD_DIM) for h in range(H_NA)]
    maps = [slice(m * DF_QK, (m + 1) * DF_QK) for m in range(2 * H_DF)]
    s_na = [_dot_nt(qa[:, sl], ka[:, sl]) for sl in heads]
    s_df = [_dot_nt(qc[:, sl], kc[:, sl]) for sl in maps]
    p_na = [jnp.exp2(s - jnp.max(s, axis=-1, keepdims=True)) for s in s_na]
    p_df = [jnp.exp2(s - jnp.max(s, axis=-1, keepdims=True)) for s in s_df]
    d_na = [jnp.sum(p, axis=-1, keepdims=True) for p in p_na]
    d_df = [jnp.sum(p, axis=-1, keepdims=True) for p in p_df]
    o_na = [_dot(p, va[:, sl]) for p, sl in zip(p_na, heads)]
    o_df = [_dot(jnp.concatenate([p_df[2 * h].astype(BF16), p_df[2 * h + 1].astype(BF16)], axis=0),
                 vc[:, heads[h]]) for h in range(H_DF)]
    for h in range(H_NA):
        oa_ref[:, heads[h]] = o_na[h] / d_na[h]
    for h in range(H_DF):
        o = o_df[h][:n] / d_df[2 * h] - lam * (o_df[h][n:] / d_df[2 * h + 1])
        oc_ref[:, heads[h]] = _sub_rms(o, subln, lam_init)


def _ctx_attn(qkv_na, qkv_df, lam_p, subln_row, layer):
    lam_init = 0.8 - 0.6 * math.exp(-0.3 * layer)
    kern = functools.partial(_ctx_attn_kernel, lam_init=lam_init)
    return pl.pallas_call(
        kern,
        grid=(BATCH,),
        in_specs=[pl.BlockSpec((SEQ, W_NA), lambda b: (b, 0))] * 3
        + [pl.BlockSpec((SEQ, W_DF), lambda b: (b, 0))] * 3
        + [pl.BlockSpec((4, DF_QK), lambda b: (0, 0)),
                  pl.BlockSpec((1, HEAD_DIM), lambda b: (0, 0))],
        out_specs=[pl.BlockSpec((SEQ, W_NA), lambda b: (b, 0)),
                   pl.BlockSpec((SEQ, W_DF), lambda b: (b, 0))],
        out_shape=[jax.ShapeDtypeStruct((T_CTX, W_NA), F32),
                   jax.ShapeDtypeStruct((T_CTX, W_DF), F32)],
        compiler_params=_params("parallel"),
        name="ctx_attn",
    )(*qkv_na, *qkv_df, lam_p, subln_row)


NA_R = 4
NA_U = NA_R + WIN_R - 1
NA_GROUPS = GRID_H // NA_R
NA_Q_SCALE = HEAD_DIM ** -0.5 * math.log2(math.e)


def _na_key_start(g):
    return jnp.clip(g * NA_R - WIN_R // 2, 0, GRID_H - NA_U)


def _na_bias_table(rpb):
    qc = np.arange(GRID_W)[:, None]
    kc = np.arange(GRID_W)[None, :]
    cs = np.clip(qc - WIN_C // 2, 0, GRID_W - WIN_C)
    valid = (kc >= cs) & (kc < cs + WIN_C)
    dc = np.clip(kc - qc + (WIN_C - 1), 0, 2 * WIN_C - 2)
    onehot = (dc[None] == np.arange(2 * WIN_C - 1)[:, None, None]).astype(np.float32)
    x = jnp.einsum('hrd,dqk->hrqk', rpb.astype(F32), jnp.asarray(onehot), precision=HIGHEST)
    return jnp.where(valid[None, None], x * math.log2(math.e), NEG_INF)


def _na_fill_bias(tab_ref, bias_ref, g):
    u0 = int(np.clip(g * NA_R - WIN_R // 2, 0, GRID_H - NA_U))
    masked = jnp.full((H_NA, GRID_W, GRID_W), NEG_INF, F32)
    for i in range(NA_R):
        r = g * NA_R + i
        w0 = int(np.clip(r - WIN_R // 2, 0, GRID_H - WIN_R))
        for u in range(NA_U):
            inside = w0 <= u0 + u < w0 + WIN_R
            block = tab_ref[:, u0 + u - r + WIN_R - 1] if inside else masked
            bias_ref[:, i * GRID_W:(i + 1) * GRID_W, u * GRID_W:(u + 1) * GRID_W] = block


def _lat_na_kernel(q_ref, k_ref, v_ref, kc_ref, vc_ref, tab_ref, o_ref, bias_ref):
    g = pl.program_id(1)
    for g_build in (0, 1, NA_GROUPS - 1):
        @pl.when(g == g_build)
        def _():
            _na_fill_bias(tab_ref, bias_ref, g_build)

    start = pl.multiple_of(_na_key_start(g) * GRID_W, GRID_W)
    q = (q_ref[...] * NA_Q_SCALE).astype(BF16)
    kw = k_ref[pl.ds(start, NA_U * GRID_W), :].astype(BF16)
    vw = v_ref[pl.ds(start, NA_U * GRID_W), :].astype(BF16)
    kc = kc_ref[...]
    vc = vc_ref[...]
    heads = [slice(h * HEAD_DIM, (h + 1) * HEAD_DIM) for h in range(H_NA)]
    s_loc = [_dot_nt(q[:, sl], kw[:, sl]) + bias_ref[h] for h, sl in enumerate(heads)]
    s_ctx = [_dot_nt(q[:, sl], kc[:, sl]) for sl in heads]
    m = [jnp.maximum(jnp.max(a, axis=-1, keepdims=True), jnp.max(b, axis=-1, keepdims=True))
         for a, b in zip(s_loc, s_ctx)]
    p_loc = [jnp.exp2(a - mm) for a, mm in zip(s_loc, m)]
    p_ctx = [jnp.exp2(b - mm) for b, mm in zip(s_ctx, m)]
    den = [jnp.sum(a, axis=-1, keepdims=True) + jnp.sum(b, axis=-1, keepdims=True)
           for a, b in zip(p_loc, p_ctx)]
    o = [_dot(a, vw[:, sl]) + _dot(b, vc[:, sl]) for a, b, sl in zip(p_loc, p_ctx, heads)]
    for h, sl in enumerate(heads):
        o_ref[:, sl] = o[h] / den[h]


def _lat_na(qkv_na, ck, cv, bias_tab):
    rows = NA_R * GRID_W
    return pl.pallas_call(
        _lat_na_kernel,
        grid=(DEC_BATCH, NA_GROUPS),
        in_specs=[pl.BlockSpec((rows, W_NA), lambda b, g: (b * NA_GROUPS + g, 0)),
                  pl.BlockSpec((DEC_SEQ, W_NA), lambda b, g: (b, 0)),
                  pl.BlockSpec((DEC_SEQ, W_NA), lambda b, g: (b, 0)),
                  pl.BlockSpec((None, PAST_LEN, W_NA), lambda b, g: (b, 0, 0)),
                  pl.BlockSpec((None, PAST_LEN, W_NA), lambda b, g: (b, 0, 0)),
                  pl.BlockSpec((H_NA, 2 * WIN_R - 1, GRID_W, GRID_W), lambda b, g: (0, 0, 0, 0))],
        out_specs=pl.BlockSpec((rows, W_NA), lambda b, g: (b * NA_GROUPS + g, 0)),
        out_shape=jax.ShapeDtypeStruct((T_LAT, W_NA), F32),
        scratch_shapes=[pltpu.VMEM((H_NA, rows, NA_U * GRID_W), F32)],
        compiler_params=_params("arbitrary", "arbitrary"),
        name="lat_na",
    )(*qkv_na, ck, cv, bias_tab)


def _rope_tables():
    nf = DF_QK // 4
    inv = ROPE_BASE ** (-np.arange(nf, dtype=np.float32) / nf)
    t = np.arange(DEC_SEQ)
    pos = np.stack([t // GRID_W, t % GRID_W], axis=-1).astype(np.float32)
    ang = jnp.asarray(pos[:, :, None] * inv)
    cos, sin = jnp.cos(ang), jnp.sin(ang)
    cos32 = jnp.concatenate([cos, cos], axis=-1).reshape(DEC_SEQ, DF_QK)
    sin32 = jnp.concatenate([-sin, sin], axis=-1).reshape(DEC_SEQ, DF_QK)
    reps = W_DF // DF_QK
    rot = np.zeros((W_DF, W_DF), np.float32)
    for dd in range(W_DF):
        rot[dd + nf if dd % (2 * nf) < nf else dd - nf, dd] = 1.0
    return jnp.tile(cos32, (1, reps)), jnp.tile(sin32, (1, reps)), jnp.asarray(rot, dtype=BF16)


def _rope_kernel(qin_ref, kin_ref, vin_ref, kc_ref, vc_ref, cos_ref, sin_ref, rot_ref, q_ref, kt_ref, v_ref,
                 *, nb):
    i = pl.program_id(1)
    lane = lax.broadcasted_iota(jnp.int32, (q_ref.shape[0], V_AUG - HEAD_DIM), 1)
    one_col = jnp.where(lane == 0, 1.0, 0.0).astype(BF16)

    def put_values(src_ref):
        for h in range(H_DF):
            a = h * HEAD_DIM
            v_ref[:, h * V_AUG:h * V_AUG + HEAD_DIM] = src_ref[:, a:a + HEAD_DIM].astype(BF16)
            v_ref[:, h * V_AUG + HEAD_DIM:(h + 1) * V_AUG] = one_col

    @pl.when(i < nb)
    def _():
        cos, sin, rot = cos_ref[...], sin_ref[...], rot_ref[...]
        q = qin_ref[...]
        k = kin_ref[...]
        q_ref[...] = ((q * cos + _dot_sel(q, rot) * sin) * DF_Q_SCALE).astype(BF16)
        kt_ref[...] = (k * cos + _dot_sel(k, rot) * sin).T.astype(BF16)
        put_values(vin_ref)

    @pl.when(i == nb)
    def _():
        kt_ref[...] = kc_ref[...].T.astype(BF16)
        put_values(vc_ref)


def _rope(qkv_df, kc, vc, cos, sin, rot, tm=PAST_LEN):
    nb = DEC_SEQ // tm
    lat = lambda b, i: (b * nb + jnp.minimum(i, nb - 1), 0)
    pos = lambda b, i: (jnp.minimum(i, nb - 1), 0)
    ctx = pl.BlockSpec((None, PAST_LEN, W_DF), lambda b, i: (b, 0, 0))
    return pl.pallas_call(
        functools.partial(_rope_kernel, nb=nb),
        grid=(DEC_BATCH, nb + 1),
        in_specs=[pl.BlockSpec((tm, W_DF), lat)] * 3 + [ctx, ctx]
        + [pl.BlockSpec((tm, W_DF), pos), pl.BlockSpec((tm, W_DF), pos),
           pl.BlockSpec((W_DF, W_DF), lambda b, i: (0, 0))],
        out_specs=[pl.BlockSpec((tm, W_DF), lat),
                   pl.BlockSpec((None, W_DF, tm), lambda b, i: (b, 0, i)),
                   pl.BlockSpec((None, tm, H_DF * V_AUG), lambda b, i: (b, i, 0))],
        out_shape=[jax.ShapeDtypeStruct((T_LAT, W_DF), BF16),
                   jax.ShapeDtypeStruct((DEC_BATCH, W_DF, DEC_SEQ + PAST_LEN), BF16),
                   jax.ShapeDtypeStruct((DEC_BATCH, DEC_SEQ + PAST_LEN, H_DF * V_AUG), BF16)],
        compiler_params=_params("parallel", "arbitrary"),
        name="rope",
    )(*qkv_df, kc, vc, cos, sin, rot)


def _lat_df_kernel(q_ref, kt_ref, v_ref, lam_ref, subln_ref, o_ref, *, lam_init, tq):
    q = q_ref[...]
    lam = _diff_lambda(lam_ref, lam_init)
    subln = subln_ref[...]

    def scores(n):
        a = n * DF_QK
        return jnp.dot(q[:, a:a + DF_QK], kt_ref[a:a + DF_QK, :], preferred_element_type=F32)

    n_maps = 2 * H_DF
    es = []
    s_next = scores(0)
    for n in range(n_maps):
        s = s_next
        if n + 1 < n_maps:
            s_next = scores(n + 1)
        es.append(jnp.exp2(s - jnp.max(s, axis=-1, keepdims=True)).astype(BF16))
        if n % 2 == 1:
            h = n // 2
            ov = jnp.dot(jnp.concatenate(es, axis=0), v_ref[:, h * V_AUG:(h + 1) * V_AUG],
                         preferred_element_type=F32)
            o = (ov[:tq, :HEAD_DIM] / ov[:tq, HEAD_DIM:HEAD_DIM + 1]
                 - lam * (ov[tq:, :HEAD_DIM] / ov[tq:, HEAD_DIM:HEAD_DIM + 1]))
            o_ref[:, h * HEAD_DIM:(h + 1) * HEAD_DIM] = _sub_rms(o, subln, lam_init)
            es = []


def _lat_df(q_r, kt_all, v_all, lam_p, subln_row, layer, tq=256):
    lam_init = 0.8 - 0.6 * math.exp(-0.3 * layer)
    nb = DEC_SEQ // tq
    n_keys = kt_all.shape[2]
    kern = functools.partial(_lat_df_kernel, lam_init=lam_init, tq=tq)
    return pl.pallas_call(
        kern,
        grid=(DEC_BATCH, nb),
        in_specs=[pl.BlockSpec((tq, W_DF), lambda b, i: (b * nb + i, 0)),
                  pl.BlockSpec((None, W_DF, n_keys), lambda b, i: (b, 0, 0)),
                  pl.BlockSpec((None, n_keys, H_DF * V_AUG), lambda b, i: (b, 0, 0)),
                  pl.BlockSpec((4, DF_QK), lambda b, i: (0, 0)),
                  pl.BlockSpec((1, HEAD_DIM), lambda b, i: (0, 0))],
        out_specs=pl.BlockSpec((tq, W_DF), lambda b, i: (b * nb + i, 0)),
        out_shape=jax.ShapeDtypeStruct((T_LAT, W_DF), F32),
        compiler_params=_params("parallel", "arbitrary"),
        name="lat_df",
    )(q_r, kt_all, v_all, lam_p, subln_row)


def _outproj_kernel(oa_ref, of_ref, obk_ref, gate_ref, onorm_ref, grp_ref, oc_ref, y_ref, g1_ref, w_ref,
                    lg_ref, lb_ref, o_ref):
    ob = of_ref[...] + obk_ref[...]
    ms = _dot_sel(ob * ob, grp_ref[...]) * (1.0 / HEAD_DIM)
    ob = ob * lax.rsqrt(ms + RMS_EPS) * onorm_ref[...] * _silu(gate_ref[...])
    o = (jnp.dot(oa_ref[...].astype(BF16), w_ref[:W_NA, :], preferred_element_type=F32)
         + jnp.dot(ob.astype(BF16), w_ref[W_NA:W_NA + W_DN, :], preferred_element_type=F32)
         + jnp.dot(oc_ref[...].astype(BF16), w_ref[W_NA + W_DN:, :], preferred_element_type=F32))
    o_ref[...] = _layer_norm(ALPHA * y_ref[...] + g1_ref[...] * o, lg_ref[...], lb_ref[...])


def _outproj(oa, o_fwd, o_bwd, gate, onorm_row, grp, oc, y, mod, w, ln_g, ln_b, cond_fn, tm=512):
    t = y.shape[0]
    row = lambda n: pl.BlockSpec((tm, n), lambda i: (i, 0))
    vec = pl.BlockSpec((1, D_MODEL), lambda i: (0, 0))
    return pl.pallas_call(
        _outproj_kernel,
        grid=(t // tm,),
        in_specs=[row(W_NA), row(W_DN), row(W_DN),
                  row(W_DN), pl.BlockSpec((1, W_DN), lambda i: (0, 0)),
                  pl.BlockSpec((W_DN, W_DN), lambda i: (0, 0)),
                  row(W_DF), row(D_MODEL), _mod_spec(2, cond_fn),
                  pl.BlockSpec((D_MODEL, D_MODEL), lambda i: (0, 0)), vec, vec],
        out_specs=row(D_MODEL),
        out_shape=jax.ShapeDtypeStruct((t, D_MODEL), F32),
        compiler_params=_params("parallel"),
        name="outproj_ln",
    )(oa, o_fwd, o_bwd, gate, onorm_row, grp, oc, y, mod, w, ln_g, ln_b)


MOE_TM = 1024
MOE_CAP = 288


def _router_kernel(y_ref, sh_ref, sc_ref, w_ref, b_ref, tri_ref, g_ref, rk_ref, rkt_ref, cnt_ref):
    h = y_ref[...] * (1.0 + sc_ref[...]) + sh_ref[...]
    logits = _dot_exact(h, w_ref[...]) + b_ref[...]
    lane = lax.broadcasted_iota(jnp.int32, logits.shape, 1).astype(F32)
    logits = jnp.where(lane < N_EXPERTS, logits, -jnp.inf)
    m1 = jnp.max(logits, axis=-1, keepdims=True)
    i1 = jnp.min(jnp.where(logits == m1, lane, float(LANES)), axis=-1, keepdims=True)
    rest = jnp.where(lane == i1, -jnp.inf, logits)
    m2 = jnp.max(rest, axis=-1, keepdims=True)
    i2 = jnp.min(jnp.where(rest == m2, lane, float(LANES)), axis=-1, keepdims=True)
    e2 = jnp.exp(m2 - m1)
    w1 = 1.0 / (1.0 + e2)
    g_ref[...] = jnp.where(lane == i1, w1, 0.0) + jnp.where(lane == i2, e2 * w1, 0.0)
    routed = jnp.where(lane == i1, 1.0, 0.0) + jnp.where(lane == i2, 1.0, 0.0)
    before = jnp.dot(tri_ref[...], routed.astype(BF16), preferred_element_type=F32)
    rank = jnp.where(routed > 0.5, before, -1.0)
    rk_ref[...] = rank
    rkt_ref[...] = rank.T[:N_EXPERTS, :]
    cnt_ref[...] = jnp.sum(routed, axis=0, keepdims=True)


def _router(y, mod, w_pad, b_pad, tri, cond_fn, tm=MOE_TM):
    t = y.shape[0]
    nt = t // tm
    return pl.pallas_call(
        _router_kernel,
        grid=(nt,),
        in_specs=[pl.BlockSpec((tm, D_MODEL), lambda i: (i, 0)),
                  _mod_spec(3, cond_fn), _mod_spec(4, cond_fn),
                  pl.BlockSpec((D_MODEL, LANES), lambda i: (0, 0)),
                  pl.BlockSpec((1, LANES), lambda i: (0, 0)),
                  pl.BlockSpec((tm, tm), lambda i: (0, 0))],
        out_specs=[pl.BlockSpec((tm, LANES), lambda i: (i, 0)),
                   pl.BlockSpec((tm, LANES), lambda i: (i, 0)),
                   pl.BlockSpec((None, N_EXPERTS, tm), lambda i: (i, 0, 0)),
                   pl.BlockSpec((None, 1, LANES), lambda i: (i, 0, 0))],
        out_shape=[jax.ShapeDtypeStruct((t, LANES), F32),
                   jax.ShapeDtypeStruct((t, LANES), F32),
                   jax.ShapeDtypeStruct((nt, N_EXPERTS, tm), F32),
                   jax.ShapeDtypeStruct((nt, 1, LANES), F32)],
        compiler_params=_params("parallel"),
        name="router",
    )(y, mod, mod, w_pad, b_pad, tri)


def _moe_kernel(cnt_ref, y_ref, sh_ref, sc_ref, g2_ref, gates_ref, rk_ref, rkt_ref, wg_ref, wu_ref, wd_ref,
                lg_ref, lb_ref, o_ref, h_scr, acc_scr, *, tm, cap):
    i = pl.program_id(0)
    e = pl.program_id(1)

    @pl.when(e == 0)
    def _():
        h_scr[...] = (y_ref[...] * (1.0 + sc_ref[...]) + sh_ref[...]).astype(BF16)
        acc_scr[...] = jnp.zeros_like(acc_scr)

    lane = lax.broadcasted_iota(jnp.int32, (tm, LANES), 1)
    gate_col = jnp.sum(jnp.where(lane == e, gates_ref[...], 0.0), axis=-1, keepdims=True)
    rank_col = jnp.sum(jnp.where(lane == e, rk_ref[...], 0.0), axis=-1, keepdims=True)
    rank_row = rkt_ref[pl.ds(e, 1), :]
    n_pass = (cnt_ref[i * N_EXPERTS + e] + cap - 1) // cap

    def body(ps, carry):
        base = (ps * cap).astype(F32)
        slot_r = lax.broadcasted_iota(jnp.int32, (cap, tm), 0).astype(F32) + base
        sel = jnp.where(rank_row == slot_r, 1.0, 0.0).astype(BF16)
        xe = jnp.dot(sel, h_scr[...], preferred_element_type=F32).astype(BF16)
        f = _swiglu(xe, wg_ref, wu_ref, wd_ref).astype(BF16)
        slot_c = lax.broadcasted_iota(jnp.int32, (tm, cap), 1).astype(F32) + base
        sel_t = jnp.where(rank_col == slot_c, 1.0, 0.0).astype(BF16)
        acc_scr[...] += gate_col * jnp.dot(sel_t, f, preferred_element_type=F32)
        return carry

    lax.fori_loop(0, n_pass, body, 0)

    @pl.when(e == N_EXPERTS - 1)
    def _():
        o_ref[...] = _layer_norm(ALPHA * y_ref[...] + g2_ref[...] * acc_scr[...],
                                 lg_ref[...], lb_ref[...])


def _moe(y, mod, gates, rank, rank_t, counts, wg, wu, wd, ln_g, ln_b, cond_fn, tm=MOE_TM, cap=MOE_CAP):
    t = y.shape[0]
    ff = wg.shape[2]
    vec = pl.BlockSpec((1, D_MODEL), lambda i, e, c: (0, 0))
    tok = lambda n: pl.BlockSpec((tm, n), lambda i, e, c: (i, 0))
    kern = functools.partial(_moe_kernel, tm=tm, cap=cap)
    return pl.pallas_call(
        kern,
        grid_spec=pltpu.PrefetchScalarGridSpec(
            num_scalar_prefetch=1,
            grid=(t // tm, N_EXPERTS),
            in_specs=[tok(D_MODEL), _mod_spec(3, cond_fn), _mod_spec(4, cond_fn), _mod_spec(5, cond_fn),
                      tok(LANES), tok(LANES),
                      pl.BlockSpec((None, N_EXPERTS, tm), lambda i, e, c: (i, 0, 0)),
                      pl.BlockSpec((None, D_MODEL, ff), lambda i, e, c: (e, 0, 0)),
                      pl.BlockSpec((None, D_MODEL, ff), lambda i, e, c: (e, 0, 0)),
                      pl.BlockSpec((None, ff, D_MODEL), lambda i, e, c: (e, 0, 0)),
                      vec, vec],
            out_specs=tok(D_MODEL),
            scratch_shapes=[pltpu.VMEM((tm, D_MODEL), BF16), pltpu.VMEM((tm, D_MODEL), F32)]),
        out_shape=jax.ShapeDtypeStruct((t, D_MODEL), F32),
        compiler_params=_params("parallel", "arbitrary"),
        name="moe_ln",
    )(counts, y, mod, mod, mod, gates, rank, rank_t, wg, wu, wd, ln_g, ln_b)


FF_CUT = 768


def _swiglu(h, wg_ref, wu_ref, wd_ref):
    cuts = (0, FF_CUT, wg_ref.shape[1])
    parts = [(jnp.dot(h, wg_ref[:, lo:hi], preferred_element_type=F32),
              jnp.dot(h, wu_ref[:, lo:hi], preferred_element_type=F32))
             for lo, hi in zip(cuts, cuts[1:])]
    outs = [jnp.dot((_silu(a) * b).astype(BF16), wd_ref[lo:hi, :], preferred_element_type=F32)
            for (a, b), lo, hi in zip(parts, cuts, cuts[1:])]
    return outs[0] + outs[1]


def _ffn_kernel(y_ref, sh_ref, sc_ref, g2_ref, wg_ref, wu_ref, wd_ref, lg_ref, lb_ref,
                o_ref, h_scr, acc_scr, *, n_blocks):
    e = pl.program_id(1)

    @pl.when(e == 0)
    def _():
        h_scr[...] = (y_ref[...] * (1.0 + sc_ref[...]) + sh_ref[...]).astype(BF16)
        acc_scr[...] = jnp.zeros_like(acc_scr)

    acc_scr[...] += _swiglu(h_scr[...], wg_ref, wu_ref, wd_ref)

    @pl.when(e == n_blocks - 1)
    def _():
        o_ref[...] = _layer_norm(ALPHA * y_ref[...] + g2_ref[...] * acc_scr[...],
                                 lg_ref[...], lb_ref[...])


def _ffn(y, mod, wg, wu, wd, ln_g, ln_b, cond_fn, tm=512, ff=MOE_FF):
    t = y.shape[0]
    n_blocks = wg.shape[1] // ff
    vec = pl.BlockSpec((1, D_MODEL), lambda i, e: (0, 0))
    kern = functools.partial(_ffn_kernel, n_blocks=n_blocks)
    return pl.pallas_call(
        kern,
        grid=(t // tm, n_blocks),
        in_specs=[pl.BlockSpec((tm, D_MODEL), lambda i, e: (i, 0)),
                  _mod_spec(3, cond_fn), _mod_spec(4, cond_fn), _mod_spec(5, cond_fn),
                  pl.BlockSpec((D_MODEL, ff), lambda i, e: (0, e)),
                  pl.BlockSpec((D_MODEL, ff), lambda i, e: (0, e)),
                  pl.BlockSpec((ff, D_MODEL), lambda i, e: (e, 0)),
                  vec, vec],
        out_specs=pl.BlockSpec((tm, D_MODEL), lambda i, e: (i, 0)),
        out_shape=jax.ShapeDtypeStruct((t, D_MODEL), F32),
        scratch_shapes=[pltpu.VMEM((tm, D_MODEL), BF16), pltpu.VMEM((tm, D_MODEL), F32)],
        compiler_params=_params("parallel", "arbitrary"),
        name="ffn_ln",
    )(y, mod, mod, mod, wg, wu, wd, ln_g, ln_b)


def _permute_w_in(w):
    offs = np.cumsum((0,) + PROJ_SIZES)
    qa, ka, va, qkv, gate, a, b, qc, kc, vc = (w[:, offs[i]:offs[i + 1]] for i in range(10))
    pad = jnp.zeros((D_MODEL, AB_PAD - 4 * H_DN), w.dtype)
    return jnp.concatenate([qa, ka, va, qkv, gate, qc, kc, vc, a, b, pad], axis=1).astype(BF16)


def _pad_row(v, n=AB_PAD):
    v = v.reshape(1, -1)
    return jnp.pad(v, ((0, 0), (0, n - v.shape[1])))


def _group_matrix(n):
    idx = np.arange(n) // HEAD_DIM
    return jnp.asarray((idx[:, None] == idx[None, :]).astype(np.float32), dtype=BF16)


def _expand_matrix():
    e = np.zeros((2, AB_PAD, 2 * W_DN), np.float32)
    for d in range(2):
        for h in range(H_DN):
            e[d, d * H_DN + h, h * HEAD_DIM:(h + 1) * HEAD_DIM] = 1.0
            e[d, 2 * H_DN + d * H_DN + h, W_DN + h * HEAD_DIM:W_DN + (h + 1) * HEAD_DIM] = 1.0
    return jnp.asarray(e, dtype=BF16)


def _chunk_cumsum_matrices(rows):
    r = np.arange(rows)
    same = (r[:, None] // CHUNK) == (r[None, :] // CHUNK)
    return jnp.asarray(np.stack([same & (r[:, None] >= r[None, :]),
                                 same & (r[:, None] <= r[None, :])]).astype(np.float32), dtype=BF16)


def kernel(x_prompt, x_sample, cache_na_k, cache_na_v, cache_df_k, cache_df_v, state_dn, c, c_ctx,
           ada_w, ada_b, w_in, conv_dn, a_log_dn, dt_bias_dn, onorm_dn, rpb_na, lambda_df, subln_df,
           w_out, ln1_g, ln1_b, ln2_g, ln2_b, ffn_w_gate, ffn_w_up, ffn_w_down, router_w, router_b,
           moe_w_gate, moe_w_up, moe_w_down):
    conds = jnp.concatenate([c_ctx[None, :], c, jnp.zeros((N_COND - 1 - DEC_BATCH, D_MODEL), F32)], axis=0)
    mods = _ada_table(conds, ada_w, ada_b)
    ctx_cond = lambda i: 0
    lat_cond = lambda i: 1 + (i * 512) // DEC_SEQ
    lat_cond_moe = lambda i: 1 + (i * MOE_TM) // DEC_SEQ
    tri = jnp.asarray(np.tril(np.ones((MOE_TM, MOE_TM), np.float32), -1), dtype=BF16)

    grp512, grp256 = _group_matrix(2 * W_DN), _group_matrix(W_DN)
    e_mat = _expand_matrix()
    cum = _chunk_cumsum_matrices(256)
    cos, sin, rot = _rope_tables()

    y_ctx = x_prompt.reshape(T_CTX, D_MODEL)
    y_lat = x_sample.reshape(T_LAT, D_MODEL)
    ctx_out = []
    for l in range(DEPTH):
        mod = mods[l]
        w_in_l = _permute_w_in(w_in[l])
        w_out_l = w_out[l].astype(BF16)
        conv_w = jnp.pad(conv_dn[l], ((0, 8 - CONV_K), (0, 0)))
        alog_row = _pad_row(a_log_dn[l])
        dtb_row = _pad_row(dt_bias_dn[l])
        onorm_row = jnp.tile(onorm_dn[l], H_DN).reshape(1, W_DN)
        subln_row = subln_df[l].reshape(1, HEAD_DIM)
        lg1, lb1 = ln1_g[l].reshape(1, D_MODEL), ln1_b[l].reshape(1, D_MODEL)
        lg2, lb2 = ln2_g[l].reshape(1, D_MODEL), ln2_b[l].reshape(1, D_MODEL)

        *na, dn, gate, q_df, k_df, v_df, ab = _inproj(y_ctx, mod, w_in_l, ctx_cond)
        df = (q_df, k_df, v_df)
        u, *gates_dn = _dn_prep(dn, ab, conv_w, alog_row, dtb_row, grp512, e_mat, cum, SEQ)
        s0 = jnp.zeros((BATCH, 2, H_DN, HEAD_DIM, HEAD_DIM), F32)
        o_f, o_b, s_fin = _deltanet(u, *gates_dn, s0, SEQ, SEQ, n_sub=4)
        oa, oc = _ctx_attn(na, df, lambda_df[l], subln_row, l)
        y1_ctx = _outproj(oa, o_f, o_b, gate, onorm_row, grp256, oc, y_ctx, mod, w_out_l, lg1, lb1, ctx_cond)
        ctx_out.append((na[1].reshape(BATCH, SEQ, H_NA, HEAD_DIM), na[2].reshape(BATCH, SEQ, H_NA, HEAD_DIM),
                        k_df.reshape(BATCH, SEQ, H_DF, HEAD_DIM), v_df.reshape(BATCH, SEQ, H_DF, HEAD_DIM),
                        s_fin))

        *na, dn, gate, q_df, k_df, v_df, ab = _inproj(y_lat, mod, w_in_l, lat_cond)
        df = (q_df, k_df, v_df)
        u, *gates_dn = _dn_prep(dn, ab, conv_w, alog_row, dtb_row, grp512, e_mat, cum, DEC_SEQ)
        o_f, o_b, _ = _deltanet(u, *gates_dn, state_dn[:, l], DEC_SEQ, 512)
        oa = _lat_na(na, cache_na_k[:, l].reshape(DEC_BATCH, PAST_LEN, W_NA).astype(BF16),
                     cache_na_v[:, l].reshape(DEC_BATCH, PAST_LEN, W_NA).astype(BF16),
                     _na_bias_table(rpb_na[l]))
        q_r, kt_all, v_all = _rope(df, cache_df_k[:, l].reshape(DEC_BATCH, PAST_LEN, W_DF),
                                   cache_df_v[:, l].reshape(DEC_BATCH, PAST_LEN, W_DF), cos, sin, rot)
        oc = _lat_df(q_r, kt_all, v_all, lambda_df[l], subln_row, l)
        y1_lat = _outproj(oa, o_f, o_b, gate, onorm_row, grp256, oc, y_lat, mod, w_out_l, lg1, lb1, lat_cond)

        i = l // 2
        if l % 2 == 0:
            wg, wu, wd = ffn_w_gate[i].astype(BF16), ffn_w_up[i].astype(BF16), ffn_w_down[i].astype(BF16)
            y_ctx = _ffn(y1_ctx, mod, wg, wu, wd, lg2, lb2, ctx_cond)
            y_lat = _ffn(y1_lat, mod, wg, wu, wd, lg2, lb2, lat_cond)
        else:
            wg, wu, wd = moe_w_gate[i].astype(BF16), moe_w_up[i].astype(BF16), moe_w_down[i].astype(BF16)
            rw = jnp.pad(router_w[i], ((0, 0), (0, LANES - N_EXPERTS)))
            rb = _pad_row(router_b[i], LANES)
            ys = []
            for y1, cond in ((y1_ctx, ctx_cond), (y1_lat, lat_cond_moe)):
                gates, rank, rank_t, cnt = _router(y1, mod, rw, rb, tri, cond)
                counts = cnt[:, 0, :N_EXPERTS].astype(jnp.int32).reshape(-1)
                ys.append(_moe(y1, mod, gates, rank, rank_t, counts, wg, wu, wd, lg2, lb2, cond))
            y_ctx, y_lat = ys

    stack = lambda j: jnp.stack([t[j] for t in ctx_out], axis=1)
    return (y_ctx.reshape(BATCH, SEQ, D_MODEL), y_lat.reshape(DEC_BATCH, DEC_SEQ, D_MODEL),
            stack(0), stack(1), stack(2), stack(3), stack(4))
```

```python
import functools
import math

import jax
import jax.numpy as jnp
import numpy as np
from jax import lax
from jax.experimental import pallas as pl
from jax.experimental.pallas import tpu as pltpu

F32 = jnp.float32
BF16 = jnp.bfloat16
HIGHEST = lax.Precision.HIGHEST

D_MODEL = 1024
BATCH = 32
SEQ = 256
DEPTH = 2
DEC_BATCH = 2
DEC_SEQ = 4096
PAST_LEN = 512
GRID_W = 64
GRID_H = DEC_SEQ // GRID_W
HEAD_DIM = 64
H_NA = 6
H_DN = 4
H_DF = 6
W_NA = H_NA * HEAD_DIM
W_DN = H_DN * HEAD_DIM
W_DF = H_DF * HEAD_DIM
DF_QK = HEAD_DIM // 2
WIN_R = 8
WIN_C = 16
CONV_K = 5
CHUNK = 64
D_FF = 2816
N_EXPERTS = 8
MOE_FF = 1408
ALPHA = (2 * DEPTH) ** 0.25
LN_EPS = 1e-5
RMS_EPS = 1e-6
ROPE_BASE = 10000.0
NEG_INF = -1e30
PROJ_SIZES = (W_NA, W_NA, W_NA, 3 * W_DN, W_DN, 2 * H_DN, 2 * H_DN, W_DF, W_DF, W_DF)

T_CTX = BATCH * SEQ
T_LAT = DEC_BATCH * DEC_SEQ
N_COND = 8
LANES = 128
AB_PAD = LANES
SEG_WIDTHS = (W_NA, W_NA, W_NA, 3 * W_DN, W_DN, W_DF, W_DF, W_DF, AB_PAD)
SEG_OFFS = tuple(int(o) for o in np.cumsum((0,) + SEG_WIDTHS))
SEGS = tuple(zip(SEG_OFFS[:-1], SEG_OFFS[1:]))
SEG_GROUPS = ((0, 1, 2), (3,), (4,), (5, 6, 7), (8,))
P_PAD = SEG_OFFS[-1]
BD = H_DN * CHUNK
V_AUG = LANES
DF_Q_SCALE = DF_QK ** -0.5 * math.log2(math.e)
VMEM_LIMIT = 56 * 1024 * 1024


def _params(*sem):
    return pltpu.CompilerParams(dimension_semantics=sem, vmem_limit_bytes=VMEM_LIMIT)


def _dot(a, b):
    return jnp.dot(a.astype(BF16), b.astype(BF16), preferred_element_type=F32)


def _dot_nt(a, b):
    return lax.dot_general(a.astype(BF16), b.astype(BF16), (((1,), (1,)), ((), ())),
                           preferred_element_type=F32)


def _split3(x):
    x1 = x.astype(BF16)
    r1 = x - x1.astype(F32)
    x2 = r1.astype(BF16)
    return x1, x2, (r1 - x2.astype(F32)).astype(BF16)


def _dot_sel(x, sel):
    m = x.shape[0]
    y = jnp.dot(jnp.concatenate(_split3(x), axis=0), sel, preferred_element_type=F32)
    return y[:m] + y[m:2 * m] + y[2 * m:]


def _sel_dot(sel, x):
    n = x.shape[1]
    y = jnp.dot(sel, jnp.concatenate(_split3(x), axis=1), preferred_element_type=F32)
    return y[:, :n] + y[:, n:2 * n] + y[:, 2 * n:]


def _dot_exact(a, b):
    return jnp.dot(a, b, precision=HIGHEST, preferred_element_type=F32)


def _silu(x):
    return x * jax.nn.sigmoid(x)


def _layer_norm(x, g, b):
    mu = jnp.mean(x, axis=-1, keepdims=True)
    xc = x - mu
    var = jnp.mean(xc * xc, axis=-1, keepdims=True)
    return xc * lax.rsqrt(var + LN_EPS) * g + b


def _ada_kernel(c_ref, w_ref, b_ref, o_ref):
    o_ref[...] = _dot_exact(_silu(c_ref[...]), w_ref[...]) + b_ref[...]


def _ada_table(conds, ada_w, ada_b):
    out = pl.pallas_call(
        _ada_kernel,
        grid=(DEPTH, 6),
        in_specs=[pl.BlockSpec((N_COND, D_MODEL), lambda l, k: (0, 0)),
                  pl.BlockSpec((None, D_MODEL, D_MODEL), lambda l, k: (l, 0, k)),
                  pl.BlockSpec((None, None, 1, D_MODEL), lambda l, k: (l, k, 0, 0))],
        out_specs=pl.BlockSpec((None, None, N_COND, D_MODEL), lambda l, k: (l, k, 0, 0)),
        out_shape=jax.ShapeDtypeStruct((DEPTH, 6, N_COND, D_MODEL), F32),
        compiler_params=_params("parallel", "parallel"),
        name="ada_table",
    )(conds, ada_w, ada_b.reshape(DEPTH, 6, 1, D_MODEL))
    return out.reshape(DEPTH, 6, N_COND, 1, D_MODEL)


def _mod_spec(k, cond_fn):
    return pl.BlockSpec((None, None, 1, D_MODEL), lambda i, *_: (k, cond_fn(i), 0, 0))


def _inproj_kernel(x_ref, sh_ref, sc_ref, w_ref, *o_refs):
    h = (x_ref[...] * (1.0 + sc_ref[...]) + sh_ref[...]).astype(BF16)
    for group in SEG_GROUPS:
        lo = SEG_OFFS[group[0]]
        res = jnp.dot(h, w_ref[:, lo:SEG_OFFS[group[-1] + 1]], preferred_element_type=F32)
        for n in group:
            o_refs[n][...] = res[:, SEG_OFFS[n] - lo:SEG_OFFS[n + 1] - lo]


def _inproj(x, mod, w, cond_fn, tm=512):
    t = x.shape[0]
    return pl.pallas_call(
        _inproj_kernel,
        grid=(t // tm,),
        in_specs=[pl.BlockSpec((tm, D_MODEL), lambda i: (i, 0)),
                  _mod_spec(0, cond_fn), _mod_spec(1, cond_fn),
                  pl.BlockSpec((D_MODEL, P_PAD), lambda i: (0, 0))],
        out_specs=[pl.BlockSpec((tm, b - a), lambda i: (i, 0)) for a, b in SEGS],
        out_shape=[jax.ShapeDtypeStruct((t, b - a), F32) for a, b in SEGS],
        compiler_params=_params("parallel"),
        name="inproj",
    )(x, mod, mod, w)


def _dn_prep_kernel(x_ref, prev_ref, next_ref, ab_ref, w_ref, alog_ref, dtb_ref, grp_ref, e_ref, cum_ref,
                    u_ref, gcf_ref, gcb_ref, bf_ref, bb_ref, *, blocks_per_seq, rows):
    i = pl.program_id(0)
    j = i % blocks_per_seq
    prev = jnp.where(j != 0, prev_ref[...], 0.0)
    nxt = jnp.where(j != blocks_per_seq - 1, next_ref[...], 0.0)
    xe = jnp.concatenate([prev, x_ref[...], nxt], axis=0)
    w = w_ref[...]
    base = 8 - CONV_K // 2
    acc = w[0:1, :] * xe[base:base + rows, :]
    for t in range(1, CONV_K):
        acc = acc + w[t:t + 1, :] * xe[base + t:base + t + rows, :]
    u = _silu(acc)
    qk = u[:, :2 * W_DN]
    ss = _dot_sel(qk * qk, grp_ref[...])
    qk = qk * lax.rsqrt(ss + RMS_EPS)
    u_ref[:, :W_DN] = qk[:, :W_DN] * HEAD_DIM ** -0.5
    u_ref[:, W_DN:2 * W_DN] = qk[:, W_DN:]
    u_ref[:, 2 * W_DN:] = u[:, 2 * W_DN:]
    ab = ab_ref[...]
    z = ab + dtb_ref[...]
    softplus = jnp.maximum(z, 0.0) + jnp.log(1.0 + jnp.exp(-jnp.abs(z)))
    g = -jnp.exp(alog_ref[...]) * softplus
    lane = lax.broadcasted_iota(jnp.int32, ab.shape, 1)
    gb = jnp.where(lane < 2 * H_DN, g, jax.nn.sigmoid(ab))
    for d, (gc_ref, beta_ref) in enumerate(((gcf_ref, bf_ref), (gcb_ref, bb_ref))):
        gbx = _dot_sel(gb, e_ref[d])
        gc_ref[...] = _sel_dot(cum_ref[d], gbx[:, :W_DN])
        beta_ref[...] = gbx[:, W_DN:]


def _dn_prep(qkv, ab, conv_w, alog_row, dtb_row, grp, e_mat, cum, seq_len, rows=256):
    t = qkv.shape[0]
    bps = seq_len // rows
    r8 = rows // 8
    last8 = t // 8 - 1
    kern = functools.partial(_dn_prep_kernel, blocks_per_seq=bps, rows=rows)
    return pl.pallas_call(
        kern,
        grid=(t // rows,),
        in_specs=[pl.BlockSpec((rows, 3 * W_DN), lambda i: (i, 0)),
                  pl.BlockSpec((8, 3 * W_DN), lambda i: (jnp.maximum(i * r8 - 1, 0), 0)),
                  pl.BlockSpec((8, 3 * W_DN), lambda i: (jnp.minimum((i + 1) * r8, last8), 0)),
                  pl.BlockSpec((rows, AB_PAD), lambda i: (i, 0)),
                  pl.BlockSpec((8, 3 * W_DN), lambda i: (0, 0)),
                  pl.BlockSpec((1, AB_PAD), lambda i: (0, 0)),
                  pl.BlockSpec((1, AB_PAD), lambda i: (0, 0)),
                  pl.BlockSpec((2 * W_DN, 2 * W_DN), lambda i: (0, 0)),
                  pl.BlockSpec((2, AB_PAD, 2 * W_DN), lambda i: (0, 0, 0)),
                  pl.BlockSpec((2, rows, rows), lambda i: (0, 0, 0))],
        out_specs=[pl.BlockSpec((rows, 3 * W_DN), lambda i: (i, 0))]
        + [pl.BlockSpec((rows, W_DN), lambda i: (i, 0))] * 4,
        out_shape=[jax.ShapeDtypeStruct((t, 3 * W_DN), F32)]
        + [jax.ShapeDtypeStruct((t, W_DN), F32)] * 4,
        compiler_params=_params("parallel"),
        name="dn_prep",
    )(qkv, qkv, qkv, ab, conv_w, alog_row, dtb_row, grp, e_mat, cum)


def _dn_kernel(uf_ref, ub_ref, gcf_ref, gcb_ref, bf_ref, bb_ref, s0_ref, of_ref, ob_ref, sf_ref, s_scr,
               *, n_chunks, n_sub):
    j = pl.program_id(1)
    r = lax.broadcasted_iota(jnp.int32, (BD, BD), 0)
    c = lax.broadcasted_iota(jnp.int32, (BD, BD), 1)
    same = (r // CHUNK) == (c // CHUNK)
    dt = (r % CHUNK) - (c % CHUNK)
    same_f = jnp.where(same, 1.0, 0.0)
    same_b = same_f.astype(BF16)
    eye_f = jnp.where(r == c, 1.0, 0.0)
    dirs = []
    for sign, last in ((1, CHUNK - 1), (-1, 0)):
        dirs.append((jnp.where(jnp.logical_and(same, dt * sign >= 0), 1.0, 0.0),
                     jnp.where(jnp.logical_and(same, dt * sign > 0), 1.0, 0.0), last))

    def lift(x):
        return jnp.concatenate([x, x, x, x], axis=0) * same_f

    def lift_b(x):
        xb = x.astype(BF16)
        return jnp.concatenate([xb, xb, xb, xb], axis=0) * same_b

    @pl.when(j == 0)
    def _():
        for si in range(n_sub):
            for di in range(2):
                rows = [jnp.concatenate([s0_ref[si, di, h]] * H_DN, axis=1) for h in range(H_DN)]
                s_scr[si, di] = jnp.concatenate(rows, axis=0) * same_f

    def each(f, *xs):
        return [f(*a) for a in zip(*xs)]

    def mm(a, b):
        return jnp.dot(a, b, preferred_element_type=F32)

    def chunks(us, gc, beta, ss, incls, stricts, lasts):
        q = [u[:, :W_DN] for u in us]
        k = [u[:, W_DN:2 * W_DN] for u in us]
        v = [u[:, 2 * W_DN:] for u in us]
        gl = each(lambda x, last: x[last:last + 1, :], gc, lasts)
        eg = [jnp.exp(x) for x in gc]
        kb = each(jnp.multiply, k, beta)
        gcol = [lift(x) for x in gc]
        decay = each(lambda x, m: jnp.exp(jnp.where(m > 0.5, x - x.T, NEG_INF)), gcol, incls)
        qk = each(lambda a, b, kk: lax.dot_general(jnp.concatenate([lift_b(a), lift_b(b)], axis=0),
                                                   lift_b(kk), (((1,), (1,)), ((), ())),
                                                   preferred_element_type=F32), q, kb, k)
        attn = each(lambda x, d: (x[:BD] * d).astype(BF16), qk, decay)
        n_mat = each(lambda x, d, m: -(x[BD:] * d) * m, qk, decay, stricts)
        t_inv = [eye_f + x for x in n_mat]
        n_hi = [x.astype(BF16) for x in n_mat]
        m_b = n_hi
        for _ in range(int(math.log2(CHUNK)) - 1):
            m_b = [mm(x, x).astype(BF16) for x in m_b]
            t_inv = each(lambda t, m: t + mm(t.astype(BF16), m), t_inv, m_b)
        n_lo = each(lambda x, h: (x - h.astype(F32)).astype(BF16), n_mat, n_hi)
        x_hi = [t.astype(BF16) for t in t_inv]
        x_lo = each(lambda t, h: (t - h.astype(F32)).astype(BF16), t_inv, x_hi)
        nx = each(lambda h, lo, x: mm(jnp.concatenate([h, lo], axis=0), x), n_hi, n_lo, x_hi)
        nxl = each(mm, n_hi, x_lo)
        resid = each(lambda t, a, b: (eye_f - t + a[:BD] + a[BD:] + b).astype(BF16), t_inv, nx, nxl)
        t_b = each(lambda t, h, rr: (t + mm(h, rr)).astype(BF16), t_inv, x_hi, resid)
        rhs = each(lambda vv, b, kk, e: jnp.concatenate([lift_b(vv * b), lift_b(kk * e)], axis=1),
                   v, beta, kb, eg)
        sol = each(mm, t_b, rhs)
        s_b = [s.astype(BF16) for s in ss]
        ps = each(lambda so, qq, e, sb: mm(jnp.concatenate([so[:, BD:].astype(BF16), lift_b(qq * e)],
                                                           axis=0), sb), sol, q, eg, s_b)
        v_new = each(lambda so, p: (so[:, :BD] - p[:BD]).astype(BF16), sol, ps)
        o_bd = each(lambda p, a, vn: p[BD:] + mm(a, vn), ps, attn, v_new)
        o_tm = [x[0:CHUNK] + x[CHUNK:2 * CHUNK] + x[2 * CHUNK:3 * CHUNK] + x[3 * CHUNK:] for x in o_bd]
        k_tail = each(lambda kk, a, b: lift(kk * jnp.exp(a - b)).T.astype(BF16), k, gl, gc)
        s_new = each(lambda s, a, kt, vn: s * jnp.exp(a) + mm(kt, vn), ss, gl, k_tail, v_new)
        return o_tm, s_new

    chains = [(si, di) for si in range(n_sub) for di in range(2)]

    def body(ci, carry):
        offs = (pl.multiple_of(ci * CHUNK, CHUNK), pl.multiple_of((n_chunks - 1 - ci) * CHUNK, CHUNK))
        u_refs, gc_refs, b_refs, o_refs = (uf_ref, ub_ref), (gcf_ref, gcb_ref), (bf_ref, bb_ref), (of_ref, ob_ref)
        o_tm, s_new = chunks(
            [u_refs[di][si, pl.ds(offs[di], CHUNK), :] for si, di in chains],
            [gc_refs[di][si, pl.ds(offs[di], CHUNK), :] for si, di in chains],
            [b_refs[di][si, pl.ds(offs[di], CHUNK), :] for si, di in chains],
            [s_scr[si, di] for si, di in chains],
            *[[dirs[di][n] for si, di in chains] for n in range(3)])
        for (si, di), o, s in zip(chains, o_tm, s_new):
            o_refs[di][si, pl.ds(offs[di], CHUNK), :] = o
            s_scr[si, di] = s
        return carry

    lax.fori_loop(0, n_chunks, body, 0)

    @pl.when(j == pl.num_programs(1) - 1)
    def _():
        for si in range(n_sub):
            for di in range(2):
                for h in range(H_DN):
                    sl = slice(h * HEAD_DIM, (h + 1) * HEAD_DIM)
                    sf_ref[si, di, h] = s_scr[si, di, sl, sl]


def _deltanet(u, gc_f, gc_b, beta_f, beta_b, s0_bd, seq_len, rows, n_sub=2):
    t = u.shape[0]
    n_seq = t // seq_len
    nb = seq_len // rows
    kern = functools.partial(_dn_kernel, n_chunks=rows // CHUNK, n_sub=n_sub)
    fwd = lambda s, j: (s, j, 0)
    bwd = lambda s, j: (s, nb - 1 - j, 0)
    state = pl.BlockSpec((n_sub, 2, H_DN, HEAD_DIM, HEAD_DIM), lambda s, j: (s, 0, 0, 0, 0))
    o_f, o_b, s_fin = pl.pallas_call(
        kern,
        grid=(n_seq // n_sub, nb),
        in_specs=[pl.BlockSpec((n_sub, rows, 3 * W_DN), fwd), pl.BlockSpec((n_sub, rows, 3 * W_DN), bwd),
                  pl.BlockSpec((n_sub, rows, W_DN), fwd), pl.BlockSpec((n_sub, rows, W_DN), bwd),
                  pl.BlockSpec((n_sub, rows, W_DN), fwd), pl.BlockSpec((n_sub, rows, W_DN), bwd), state],
        out_specs=[pl.BlockSpec((n_sub, rows, W_DN), fwd), pl.BlockSpec((n_sub, rows, W_DN), bwd), state],
        out_shape=[jax.ShapeDtypeStruct((n_seq, seq_len, W_DN), F32),
                   jax.ShapeDtypeStruct((n_seq, seq_len, W_DN), F32),
                   jax.ShapeDtypeStruct((n_seq, 2, H_DN, HEAD_DIM, HEAD_DIM), F32)],
        scratch_shapes=[pltpu.VMEM((n_sub, 2, BD, BD), F32)],
        compiler_params=_params("parallel", "arbitrary"),
        name="deltanet",
    )(*[x.reshape(n_seq, seq_len, -1) for x in (u, u, gc_f, gc_b, beta_f, beta_b)], s0_bd)
    return o_f.reshape(t, W_DN), o_b.reshape(t, W_DN), s_fin


def _diff_lambda(lam_ref, lam_init):
    lp = lam_ref[...]
    return (jnp.exp(jnp.sum(lp[0:1] * lp[1:2], axis=1, keepdims=True))
            - jnp.exp(jnp.sum(lp[2:3] * lp[3:4], axis=1, keepdims=True)) + lam_init)


def _sub_rms(o, subln, lam_init):
    ms = jnp.mean(o * o, axis=-1, keepdims=True)
    return o * lax.rsqrt(ms + RMS_EPS) * subln * (1.0 - lam_init)


def _ctx_attn_kernel(qa_ref, ka_ref, va_ref, qc_ref, kc_ref, vc_ref, lam_ref, subln_ref, oa_ref, oc_ref,
                     *, lam_init):
    lam = _diff_lambda(lam_ref, lam_init)
    subln = subln_ref[...]
    n = qa_ref.shape[0]
    qa = (qa_ref[...] * NA_Q_SCALE).astype(BF16)
    ka = ka_ref[...].astype(BF16)
    va = va_ref[...].astype(BF16)
    qc = (qc_ref[...] * DF_Q_SCALE).astype(BF16)
    kc = kc_ref[...].astype(BF16)
    vc = vc_ref[...].astype(BF16)
    heads = [slice(h * HEAD_DIM, (h + 1) * HEAD_DIM) for h in range(H_NA)]
    maps = [slice(m * DF_QK, (m + 1) * DF_QK) for m in range(2 * H_DF)]
    s_na = [_dot_nt(qa[:, sl], ka[:, sl]) for sl in heads]
    s_df = [_dot_nt(qc[:, sl], kc[:, sl]) for sl in maps]
    p_na = [jnp.exp2(s - jnp.max(s, axis=-1, keepdims=True)) for s in s_na]
    p_df = [jnp.exp2(s - jnp.max(s, axis=-1, keepdims=True)) for s in s_df]
    d_na = [jnp.sum(p, axis=-1, keepdims=True) for p in p_na]
    d_df = [jnp.sum(p, axis=-1, keepdims=True) for p in p_df]
    o_na = [_dot(p, va[:, sl]) for p, sl in zip(p_na, heads)]
    o_df = [_dot(jnp.concatenate([p_df[2 * h].astype(BF16), p_df[2 * h + 1].astype(BF16)], axis=0),
                 vc[:, heads[h]]) for h in range(H_DF)]
    for h in range(H_NA):
        oa_ref[:, heads[h]] = o_na[h] / d_na[h]
    for h in range(H_DF):
        o = o_df[h][:n] / d_df[2 * h] - lam * (o_df[h][n:] / d_df[2 * h + 1])
        oc_ref[:, heads[h]] = _sub_rms(o, subln, lam_init)


def _ctx_attn(qkv_na, qkv_df, lam_p, subln_row, layer):
    lam_init = 0.8 - 0.6 * math.exp(-0.3 * layer)
    kern = functools.partial(_ctx_attn_kernel, lam_init=lam_init)
    return pl.pallas_call(
        kern,
        grid=(BATCH,),
        in_specs=[pl.BlockSpec((SEQ, W_NA), lambda b: (b, 0))] * 3
        + [pl.BlockSpec((SEQ, W_DF), lambda b: (b, 0))] * 3
        + [pl.BlockSpec((4, DF_QK), lambda b: (0, 0)),
                  pl.BlockSpec((1, HEAD_DIM), lambda b: (0, 0))],
        out_specs=[pl.BlockSpec((SEQ, W_NA), lambda b: (b, 0)),
                   pl.BlockSpec((SEQ, W_DF), lambda b: (b, 0))],
        out_shape=[jax.ShapeDtypeStruct((T_CTX, W_NA), F32),
                   jax.ShapeDtypeStruct((T_CTX, W_DF), F32)],
        compiler_params=_params("parallel"),
        name="ctx_attn",
    )(*qkv_na, *qkv_df, lam_p, subln_row)


NA_R = 4
NA_U = NA_R + WIN_R - 1
NA_GROUPS = GRID_H // NA_R
NA_Q_SCALE = HEAD_DIM ** -0.5 * math.log2(math.e)


def _na_key_start(g):
    return jnp.clip(g * NA_R - WIN_R // 2, 0, GRID_H - NA_U)


def _na_bias_table(rpb):
    qc = np.arange(GRID_W)[:, None]
    kc = np.arange(GRID_W)[None, :]
    cs = np.clip(qc - WIN_C // 2, 0, GRID_W - WIN_C)
    valid = (kc >= cs) & (kc < cs + WIN_C)
    dc = np.clip(kc - qc + (WIN_C - 1), 0, 2 * WIN_C - 2)
    onehot = (dc[None] == np.arange(2 * WIN_C - 1)[:, None, None]).astype(np.float32)
    x = jnp.einsum('hrd,dqk->hrqk', rpb.astype(F32), jnp.asarray(onehot), precision=HIGHEST)
    return jnp.where(valid[None, None], x * math.log2(math.e), NEG_INF)


def _na_fill_bias(tab_ref, bias_ref, g):
    u0 = int(np.clip(g * NA_R - WIN_R // 2, 0, GRID_H - NA_U))
    masked = jnp.full((H_NA, GRID_W, GRID_W), NEG_INF, F32)
    for i in range(NA_R):
        r = g * NA_R + i
        w0 = int(np.clip(r - WIN_R // 2, 0, GRID_H - WIN_R))
        for u in range(NA_U):
            inside = w0 <= u0 + u < w0 + WIN_R
            block = tab_ref[:, u0 + u - r + WIN_R - 1] if inside else masked
            bias_ref[:, i * GRID_W:(i + 1) * GRID_W, u * GRID_W:(u + 1) * GRID_W] = block


def _lat_na_kernel(q_ref, k_ref, v_ref, kc_ref, vc_ref, tab_ref, o_ref, bias_ref):
    g = pl.program_id(1)
    for g_build in (0, 1, NA_GROUPS - 1):
        @pl.when(g == g_build)
        def _():
            _na_fill_bias(tab_ref, bias_ref, g_build)

    start = pl.multiple_of(_na_key_start(g) * GRID_W, GRID_W)
    q = (q_ref[...] * NA_Q_SCALE).astype(BF16)
    kw = k_ref[pl.ds(start, NA_U * GRID_W), :].astype(BF16)
    vw = v_ref[pl.ds(start, NA_U * GRID_W), :].astype(BF16)
    kc = kc_ref[...]
    vc = vc_ref[...]
    heads = [slice(h * HEAD_DIM, (h + 1) * HEAD_DIM) for h in range(H_NA)]
    s_loc = [_dot_nt(q[:, sl], kw[:, sl]) + bias_ref[h] for h, sl in enumerate(heads)]
    s_ctx = [_dot_nt(q[:, sl], kc[:, sl]) for sl in heads]
    m = [jnp.maximum(jnp.max(a, axis=-1, keepdims=True), jnp.max(b, axis=-1, keepdims=True))
         for a, b in zip(s_loc, s_ctx)]
    p_loc = [jnp.exp2(a - mm) for a, mm in zip(s_loc, m)]
    p_ctx = [jnp.exp2(b - mm) for b, mm in zip(s_ctx, m)]
    den = [jnp.sum(a, axis=-1, keepdims=True) + jnp.sum(b, axis=-1, keepdims=True)
           for a, b in zip(p_loc, p_ctx)]
    o = [_dot(a, vw[:, sl]) + _dot(b, vc[:, sl]) for a, b, sl in zip(p_loc, p_ctx, heads)]
    for h, sl in enumerate(heads):
        o_ref[:, sl] = o[h] / den[h]


def _lat_na(qkv_na, ck, cv, bias_tab):
    rows = NA_R * GRID_W
    return pl.pallas_call(
        _lat_na_kernel,
        grid=(DEC_BATCH, NA_GROUPS),
        in_specs=[pl.BlockSpec((rows, W_NA), lambda b, g: (b * NA_GROUPS + g, 0)),
                  pl.BlockSpec((DEC_SEQ, W_NA), lambda b, g: (b, 0)),
                  pl.BlockSpec((DEC_SEQ, W_NA), lambda b, g: (b, 0)),
                  pl.BlockSpec((None, PAST_LEN, W_NA), lambda b, g: (b, 0, 0)),
                  pl.BlockSpec((None, PAST_LEN, W_NA), lambda b, g: (b, 0, 0)),
                  pl.BlockSpec((H_NA, 2 * WIN_R - 1, GRID_W, GRID_W), lambda b, g: (0, 0, 0, 0))],
        out_specs=pl.BlockSpec((rows, W_NA), lambda b, g: (b * NA_GROUPS + g, 0)),
        out_shape=jax.ShapeDtypeStruct((T_LAT, W_NA), F32),
        scratch_shapes=[pltpu.VMEM((H_NA, rows, NA_U * GRID_W), F32)],
        compiler_params=_params("arbitrary", "arbitrary"),
        name="lat_na",
    )(*qkv_na, ck, cv, bias_tab)


def _rope_tables():
    nf = DF_QK // 4
    inv = ROPE_BASE ** (-np.arange(nf, dtype=np.float32) / nf)
    t = np.arange(DEC_SEQ)
    pos = np.stack([t // GRID_W, t % GRID_W], axis=-1).astype(np.float32)
    ang = jnp.asarray(pos[:, :, None] * inv)
    cos, sin = jnp.cos(ang), jnp.sin(ang)
    cos32 = jnp.concatenate([cos, cos], axis=-1).reshape(DEC_SEQ, DF_QK)
    sin32 = jnp.concatenate([-sin, sin], axis=-1).reshape(DEC_SEQ, DF_QK)
    reps = W_DF // DF_QK
    rot = np.zeros((W_DF, W_DF), np.float32)
    for dd in range(W_DF):
        rot[dd + nf if dd % (2 * nf) < nf else dd - nf, dd] = 1.0
    return jnp.tile(cos32, (1, reps)), jnp.tile(sin32, (1, reps)), jnp.asarray(rot, dtype=BF16)


def _rope_kernel(qin_ref, kin_ref, vin_ref, kc_ref, vc_ref, cos_ref, sin_ref, rot_ref, q_ref, kt_ref, v_ref,
                 *, nb):
    i = pl.program_id(1)
    lane = lax.broadcasted_iota(jnp.int32, (q_ref.shape[0], V_AUG - HEAD_DIM), 1)
    one_col = jnp.where(lane == 0, 1.0, 0.0).astype(BF16)

    def put_values(src_ref):
        for h in range(H_DF):
            a = h * HEAD_DIM
            v_ref[:, h * V_AUG:h * V_AUG + HEAD_DIM] = src_ref[:, a:a + HEAD_DIM].astype(BF16)
            v_ref[:, h * V_AUG + HEAD_DIM:(h + 1) * V_AUG] = one_col

    @pl.when(i < nb)
    def _():
        cos, sin, rot = cos_ref[...], sin_ref[...], rot_ref[...]
        q = qin_ref[...]
        k = kin_ref[...]
        q_ref[...] = ((q * cos + _dot_sel(q, rot) * sin) * DF_Q_SCALE).astype(BF16)
        kt_ref[...] = (k * cos + _dot_sel(k, rot) * sin).T.astype(BF16)
        put_values(vin_ref)

    @pl.when(i == nb)
    def _():
        kt_ref[...] = kc_ref[...].T.astype(BF16)
        put_values(vc_ref)


def _rope(qkv_df, kc, vc, cos, sin, rot, tm=PAST_LEN):
    nb = DEC_SEQ // tm
    lat = lambda b, i: (b * nb + jnp.minimum(i, nb - 1), 0)
    pos = lambda b, i: (jnp.minimum(i, nb - 1), 0)
    ctx = pl.BlockSpec((None, PAST_LEN, W_DF), lambda b, i: (b, 0, 0))
    return pl.pallas_call(
        functools.partial(_rope_kernel, nb=nb),
        grid=(DEC_BATCH, nb + 1),
        in_specs=[pl.BlockSpec((tm, W_DF), lat)] * 3 + [ctx, ctx]
        + [pl.BlockSpec((tm, W_DF), pos), pl.BlockSpec((tm, W_DF), pos),
           pl.BlockSpec((W_DF, W_DF), lambda b, i: (0, 0))],
        out_specs=[pl.BlockSpec((tm, W_DF), lat),
                   pl.BlockSpec((None, W_DF, tm), lambda b, i: (b, 0, i)),
                   pl.BlockSpec((None, tm, H_DF * V_AUG), lambda b, i: (b, i, 0))],
        out_shape=[jax.ShapeDtypeStruct((T_LAT, W_DF), BF16),
                   jax.ShapeDtypeStruct((DEC_BATCH, W_DF, DEC_SEQ + PAST_LEN), BF16),
                   jax.ShapeDtypeStruct((DEC_BATCH, DEC_SEQ + PAST_LEN, H_DF * V_AUG), BF16)],
        compiler_params=_params("parallel", "arbitrary"),
        name="rope",
    )(*qkv_df, kc, vc, cos, sin, rot)


def _lat_df_kernel(q_ref, kt_ref, v_ref, lam_ref, subln_ref, o_ref, *, lam_init, tq):
    q = q_ref[...]
    lam = _diff_lambda(lam_ref, lam_init)
    subln = subln_ref[...]

    def scores(n):
        a = n * DF_QK
        return jnp.dot(q[:, a:a + DF_QK], kt_ref[a:a + DF_QK, :], preferred_element_type=F32)

    n_maps = 2 * H_DF
    es = []
    s_next = scores(0)
    for n in range(n_maps):
        s = s_next
        if n + 1 < n_maps:
            s_next = scores(n + 1)
        es.append(jnp.exp2(s - jnp.max(s, axis=-1, keepdims=True)).astype(BF16))
        if n % 2 == 1:
            h = n // 2
            ov = jnp.dot(jnp.concatenate(es, axis=0), v_ref[:, h * V_AUG:(h + 1) * V_AUG],
                         preferred_element_type=F32)
            o = (ov[:tq, :HEAD_DIM] / ov[:tq, HEAD_DIM:HEAD_DIM + 1]
                 - lam * (ov[tq:, :HEAD_DIM] / ov[tq:, HEAD_DIM:HEAD_DIM + 1]))
            o_ref[:, h * HEAD_DIM:(h + 1) * HEAD_DIM] = _sub_rms(o, subln, lam_init)
            es = []


def _lat_df(q_r, kt_all, v_all, lam_p, subln_row, layer, tq=256):
    lam_init = 0.8 - 0.6 * math.exp(-0.3 * layer)
    nb = DEC_SEQ // tq
    n_keys = kt_all.shape[2]
    kern = functools.partial(_lat_df_kernel, lam_init=lam_init, tq=tq)
    return pl.pallas_call(
        kern,
        grid=(DEC_BATCH, nb),
        in_specs=[pl.BlockSpec((tq, W_DF), lambda b, i: (b * nb + i, 0)),
                  pl.BlockSpec((None, W_DF, n_keys), lambda b, i: (b, 0, 0)),
                  pl.BlockSpec((None, n_keys, H_DF * V_AUG), lambda b, i: (b, 0, 0)),
                  pl.BlockSpec((4, DF_QK), lambda b, i: (0, 0)),
                  pl.BlockSpec((1, HEAD_DIM), lambda b, i: (0, 0))],
        out_specs=pl.BlockSpec((tq, W_DF), lambda b, i: (b * nb + i, 0)),
        out_shape=jax.ShapeDtypeStruct((T_LAT, W_DF), F32),
        compiler_params=_params("parallel", "arbitrary"),
        name="lat_df",
    )(q_r, kt_all, v_all, lam_p, subln_row)


def _outproj_kernel(oa_ref, of_ref, obk_ref, gate_ref, onorm_ref, grp_ref, oc_ref, y_ref, g1_ref, w_ref,
                    lg_ref, lb_ref, o_ref):
    ob = of_ref[...] + obk_ref[...]
    ms = _dot_sel(ob * ob, grp_ref[...]) * (1.0 / HEAD_DIM)
    ob = ob * lax.rsqrt(ms + RMS_EPS) * onorm_ref[...] * _silu(gate_ref[...])
    o = (jnp.dot(oa_ref[...].astype(BF16), w_ref[:W_NA, :], preferred_element_type=F32)
         + jnp.dot(ob.astype(BF16), w_ref[W_NA:W_NA + W_DN, :], preferred_element_type=F32)
         + jnp.dot(oc_ref[...].astype(BF16), w_ref[W_NA + W_DN:, :], preferred_element_type=F32))
    o_ref[...] = _layer_norm(ALPHA * y_ref[...] + g1_ref[...] * o, lg_ref[...], lb_ref[...])


def _outproj(oa, o_fwd, o_bwd, gate, onorm_row, grp, oc, y, mod, w, ln_g, ln_b, cond_fn, tm=512):
    t = y.shape[0]
    row = lambda n: pl.BlockSpec((tm, n), lambda i: (i, 0))
    vec = pl.BlockSpec((1, D_MODEL), lambda i: (0, 0))
    return pl.pallas_call(
        _outproj_kernel,
        grid=(t // tm,),
        in_specs=[row(W_NA), row(W_DN), row(W_DN),
                  row(W_DN), pl.BlockSpec((1, W_DN), lambda i: (0, 0)),
                  pl.BlockSpec((W_DN, W_DN), lambda i: (0, 0)),
                  row(W_DF), row(D_MODEL), _mod_spec(2, cond_fn),
                  pl.BlockSpec((D_MODEL, D_MODEL), lambda i: (0, 0)), vec, vec],
        out_specs=row(D_MODEL),
        out_shape=jax.ShapeDtypeStruct((t, D_MODEL), F32),
        compiler_params=_params("parallel"),
        name="outproj_ln",
    )(oa, o_fwd, o_bwd, gate, onorm_row, grp, oc, y, mod, w, ln_g, ln_b)


MOE_TM = 1024
MOE_CAP = 256
MOE_CAP_SMALL = 128


def _router_kernel(y_ref, sh_ref, sc_ref, w_ref, b_ref, tri_ref, g_ref, rk_ref, rkt_ref, cnt_ref):
    h = y_ref[...] * (1.0 + sc_ref[...]) + sh_ref[...]
    logits = _dot_exact(h, w_ref[...]) + b_ref[...]
    lane = lax.broadcasted_iota(jnp.int32, logits.shape, 1).astype(F32)
    logits = jnp.where(lane < N_EXPERTS, logits, -jnp.inf)
    m1 = jnp.max(logits, axis=-1, keepdims=True)
    i1 = jnp.min(jnp.where(logits == m1, lane, float(LANES)), axis=-1, keepdims=True)
    rest = jnp.where(lane == i1, -jnp.inf, logits)
    m2 = jnp.max(rest, axis=-1, keepdims=True)
    i2 = jnp.min(jnp.where(rest == m2, lane, float(LANES)), axis=-1, keepdims=True)
    e2 = jnp.exp(m2 - m1)
    w1 = 1.0 / (1.0 + e2)
    g_ref[...] = jnp.where(lane == i1, w1, 0.0) + jnp.where(lane == i2, e2 * w1, 0.0)
    routed = jnp.where(lane == i1, 1.0, 0.0) + jnp.where(lane == i2, 1.0, 0.0)
    before = jnp.dot(tri_ref[...], routed.astype(BF16), preferred_element_type=F32)
    rank = jnp.where(routed > 0.5, before, -1.0)
    rk_ref[...] = rank
    rkt_ref[...] = rank.T[:N_EXPERTS, :]
    cnt_ref[...] = jnp.sum(routed, axis=0, keepdims=True)


def _router(y, mod, w_pad, b_pad, tri, cond_fn, tm=MOE_TM):
    t = y.shape[0]
    nt = t // tm
    return pl.pallas_call(
        _router_kernel,
        grid=(nt,),
        in_specs=[pl.BlockSpec((tm, D_MODEL), lambda i: (i, 0)),
                  _mod_spec(3, cond_fn), _mod_spec(4, cond_fn),
                  pl.BlockSpec((D_MODEL, LANES), lambda i: (0, 0)),
                  pl.BlockSpec((1, LANES), lambda i: (0, 0)),
                  pl.BlockSpec((tm, tm), lambda i: (0, 0))],
        out_specs=[pl.BlockSpec((tm, LANES), lambda i: (i, 0)),
                   pl.BlockSpec((tm, LANES), lambda i: (i, 0)),
                   pl.BlockSpec((None, N_EXPERTS, tm), lambda i: (i, 0, 0)),
                   pl.BlockSpec((None, 1, LANES), lambda i: (i, 0, 0))],
        out_shape=[jax.ShapeDtypeStruct((t, LANES), F32),
                   jax.ShapeDtypeStruct((t, LANES), F32),
                   jax.ShapeDtypeStruct((nt, N_EXPERTS, tm), F32),
                   jax.ShapeDtypeStruct((nt, 1, LANES), F32)],
        compiler_params=_params("parallel"),
        name="router",
    )(y, mod, mod, w_pad, b_pad, tri)


def _moe_kernel(cnt_ref, y_ref, sh_ref, sc_ref, g2_ref, gates_ref, rk_ref, rkt_ref, wg_ref, wu_ref, wd_ref,
                lg_ref, lb_ref, o_ref, h_scr, acc_scr, *, tm, cap, cap_small):
    i = pl.program_id(0)
    e = pl.program_id(1)

    @pl.when(e == 0)
    def _():
        h_scr[...] = (y_ref[...] * (1.0 + sc_ref[...]) + sh_ref[...]).astype(BF16)
        acc_scr[...] = jnp.zeros_like(acc_scr)

    lane = lax.broadcasted_iota(jnp.int32, (tm, LANES), 1)
    gate_col = jnp.sum(jnp.where(lane == e, gates_ref[...], 0.0), axis=-1, keepdims=True)
    rank_col = jnp.sum(jnp.where(lane == e, rk_ref[...], 0.0), axis=-1, keepdims=True)
    rank_row = rkt_ref[pl.ds(e, 1), :]

    def run_pass(first, rows):
        base = first.astype(F32)
        slot_r = lax.broadcasted_iota(jnp.int32, (rows, tm), 0).astype(F32) + base
        sel = jnp.where(rank_row == slot_r, 1.0, 0.0).astype(BF16)
        xe = jnp.dot(sel, h_scr[...], preferred_element_type=F32).astype(BF16)
        f = _swiglu(xe, wg_ref, wu_ref, wd_ref).astype(BF16)
        slot_c = lax.broadcasted_iota(jnp.int32, (tm, rows), 1).astype(F32) + base
        sel_t = jnp.where(rank_col == slot_c, 1.0, 0.0).astype(BF16)
        acc_scr[...] += gate_col * jnp.dot(sel_t, f, preferred_element_type=F32)

    count = cnt_ref[i * N_EXPERTS + e]
    rem = count % cap
    n_big = count // cap + jnp.where(rem > cap_small, 1, 0)

    def body(ps, carry):
        run_pass(ps * cap, cap)
        return carry

    lax.fori_loop(0, n_big, body, 0)

    @pl.when(jnp.logical_and(rem > 0, rem <= cap_small))
    def _():
        run_pass(n_big * cap, cap_small)

    @pl.when(e == N_EXPERTS - 1)
    def _():
        o_ref[...] = _layer_norm(ALPHA * y_ref[...] + g2_ref[...] * acc_scr[...],
                                 lg_ref[...], lb_ref[...])


def _moe(y, mod, gates, rank, rank_t, counts, wg, wu, wd, ln_g, ln_b, cond_fn, tm=MOE_TM, cap=MOE_CAP,
         cap_small=MOE_CAP_SMALL):
    t = y.shape[0]
    ff = wg.shape[2]
    vec = pl.BlockSpec((1, D_MODEL), lambda i, e, c: (0, 0))
    tok = lambda n: pl.BlockSpec((tm, n), lambda i, e, c: (i, 0))
    kern = functools.partial(_moe_kernel, tm=tm, cap=cap, cap_small=cap_small)
    return pl.pallas_call(
        kern,
        grid_spec=pltpu.PrefetchScalarGridSpec(
            num_scalar_prefetch=1,
            grid=(t // tm, N_EXPERTS),
            in_specs=[tok(D_MODEL), _mod_spec(3, cond_fn), _mod_spec(4, cond_fn), _mod_spec(5, cond_fn),
                      tok(LANES), tok(LANES),
                      pl.BlockSpec((None, N_EXPERTS, tm), lambda i, e, c: (i, 0, 0)),
                      pl.BlockSpec((None, D_MODEL, ff), lambda i, e, c: (e, 0, 0)),
                      pl.BlockSpec((None, D_MODEL, ff), lambda i, e, c: (e, 0, 0)),
                      pl.BlockSpec((None, ff, D_MODEL), lambda i, e, c: (e, 0, 0)),
                      vec, vec],
            out_specs=tok(D_MODEL),
            scratch_shapes=[pltpu.VMEM((tm, D_MODEL), BF16), pltpu.VMEM((tm, D_MODEL), F32)]),
        out_shape=jax.ShapeDtypeStruct((t, D_MODEL), F32),
        compiler_params=_params("parallel", "arbitrary"),
        name="moe_ln",
    )(counts, y, mod, mod, mod, gates, rank, rank_t, wg, wu, wd, ln_g, ln_b)


FF_CUT = 768


def _swiglu(h, wg_ref, wu_ref, wd_ref):
    cuts = (0, FF_CUT, wg_ref.shape[1])
    parts = [(jnp.dot(h, wg_ref[:, lo:hi], preferred_element_type=F32),
              jnp.dot(h, wu_ref[:, lo:hi], preferred_element_type=F32))
             for lo, hi in zip(cuts, cuts[1:])]
    outs = [jnp.dot((_silu(a) * b).astype(BF16), wd_ref[lo:hi, :], preferred_element_type=F32)
            for (a, b), lo, hi in zip(parts, cuts, cuts[1:])]
    return outs[0] + outs[1]


def _ffn_kernel(y_ref, sh_ref, sc_ref, g2_ref, wg_ref, wu_ref, wd_ref, lg_ref, lb_ref,
                o_ref, h_scr, acc_scr, *, n_blocks):
    e = pl.program_id(1)

    @pl.when(e == 0)
    def _():
        h_scr[...] = (y_ref[...] * (1.0 + sc_ref[...]) + sh_ref[...]).astype(BF16)
        acc_scr[...] = jnp.zeros_like(acc_scr)

    acc_scr[...] += _swiglu(h_scr[...], wg_ref, wu_ref, wd_ref)

    @pl.when(e == n_blocks - 1)
    def _():
        o_ref[...] = _layer_norm(ALPHA * y_ref[...] + g2_ref[...] * acc_scr[...],
                                 lg_ref[...], lb_ref[...])


def _ffn(y, mod, wg, wu, wd, ln_g, ln_b, cond_fn, tm=512, ff=MOE_FF):
    t = y.shape[0]
    n_blocks = wg.shape[1] // ff
    vec = pl.BlockSpec((1, D_MODEL), lambda i, e: (0, 0))
    kern = functools.partial(_ffn_kernel, n_blocks=n_blocks)
    return pl.pallas_call(
        kern,
        grid=(t // tm, n_blocks),
        in_specs=[pl.BlockSpec((tm, D_MODEL), lambda i, e: (i, 0)),
                  _mod_spec(3, cond_fn), _mod_spec(4, cond_fn), _mod_spec(5, cond_fn),
                  pl.BlockSpec((D_MODEL, ff), lambda i, e: (0, e)),
                  pl.BlockSpec((D_MODEL, ff), lambda i, e: (0, e)),
                  pl.BlockSpec((ff, D_MODEL), lambda i, e: (e, 0)),
                  vec, vec],
        out_specs=pl.BlockSpec((tm, D_MODEL), lambda i, e: (i, 0)),
        out_shape=jax.ShapeDtypeStruct((t, D_MODEL), F32),
        scratch_shapes=[pltpu.VMEM((tm, D_MODEL), BF16), pltpu.VMEM((tm, D_MODEL), F32)],
        compiler_params=_params("parallel", "arbitrary"),
        name="ffn_ln",
    )(y, mod, mod, mod, wg, wu, wd, ln_g, ln_b)


def _permute_w_in(w):
    offs = np.cumsum((0,) + PROJ_SIZES)
    qa, ka, va, qkv, gate, a, b, qc, kc, vc = (w[:, offs[i]:offs[i + 1]] for i in range(10))
    pad = jnp.zeros((D_MODEL, AB_PAD - 4 * H_DN), w.dtype)
    return jnp.concatenate([qa, ka, va, qkv, gate, qc, kc, vc, a, b, pad], axis=1).astype(BF16)


def _pad_row(v, n=AB_PAD):
    v = v.reshape(1, -1)
    return jnp.pad(v, ((0, 0), (0, n - v.shape[1])))


def _group_matrix(n):
    idx = np.arange(n) // HEAD_DIM
    return jnp.asarray((idx[:, None] == idx[None, :]).astype(np.float32), dtype=BF16)


def _expand_matrix():
    e = np.zeros((2, AB_PAD, 2 * W_DN), np.float32)
    for d in range(2):
        for h in range(H_DN):
            e[d, d * H_DN + h, h * HEAD_DIM:(h + 1) * HEAD_DIM] = 1.0
            e[d, 2 * H_DN + d * H_DN + h, W_DN + h * HEAD_DIM:W_DN + (h + 1) * HEAD_DIM] = 1.0
    return jnp.asarray(e, dtype=BF16)


def _chunk_cumsum_matrices(rows):
    r = np.arange(rows)
    same = (r[:, None] // CHUNK) == (r[None, :] // CHUNK)
    return jnp.asarray(np.stack([same & (r[:, None] >= r[None, :]),
                                 same & (r[:, None] <= r[None, :])]).astype(np.float32), dtype=BF16)


def kernel(x_prompt, x_sample, cache_na_k, cache_na_v, cache_df_k, cache_df_v, state_dn, c, c_ctx,
           ada_w, ada_b, w_in, conv_dn, a_log_dn, dt_bias_dn, onorm_dn, rpb_na, lambda_df, subln_df,
           w_out, ln1_g, ln1_b, ln2_g, ln2_b, ffn_w_gate, ffn_w_up, ffn_w_down, router_w, router_b,
           moe_w_gate, moe_w_up, moe_w_down):
    conds = jnp.concatenate([c_ctx[None, :], c, jnp.zeros((N_COND - 1 - DEC_BATCH, D_MODEL), F32)], axis=0)
    mods = _ada_table(conds, ada_w, ada_b)
    ctx_cond = lambda i: 0
    lat_cond = lambda i: 1 + (i * 512) // DEC_SEQ
    lat_cond_moe = lambda i: 1 + (i * MOE_TM) // DEC_SEQ
    tri = jnp.asarray(np.tril(np.ones((MOE_TM, MOE_TM), np.float32), -1), dtype=BF16)

    grp512, grp256 = _group_matrix(2 * W_DN), _group_matrix(W_DN)
    e_mat = _expand_matrix()
    cum = _chunk_cumsum_matrices(256)
    cos, sin, rot = _rope_tables()

    y_ctx = x_prompt.reshape(T_CTX, D_MODEL)
    y_lat = x_sample.reshape(T_LAT, D_MODEL)
    ctx_out = []
    for l in range(DEPTH):
        mod = mods[l]
        w_in_l = _permute_w_in(w_in[l])
        w_out_l = w_out[l].astype(BF16)
        conv_w = jnp.pad(conv_dn[l], ((0, 8 - CONV_K), (0, 0)))
        alog_row = _pad_row(a_log_dn[l])
        dtb_row = _pad_row(dt_bias_dn[l])
        onorm_row = jnp.tile(onorm_dn[l], H_DN).reshape(1, W_DN)
        subln_row = subln_df[l].reshape(1, HEAD_DIM)
        lg1, lb1 = ln1_g[l].reshape(1, D_MODEL), ln1_b[l].reshape(1, D_MODEL)
        lg2, lb2 = ln2_g[l].reshape(1, D_MODEL), ln2_b[l].reshape(1, D_MODEL)

        *na, dn, gate, q_df, k_df, v_df, ab = _inproj(y_ctx, mod, w_in_l, ctx_cond)
        df = (q_df, k_df, v_df)
        u, *gates_dn = _dn_prep(dn, ab, conv_w, alog_row, dtb_row, grp512, e_mat, cum, SEQ)
        s0 = jnp.zeros((BATCH, 2, H_DN, HEAD_DIM, HEAD_DIM), F32)
        o_f, o_b, s_fin = _deltanet(u, *gates_dn, s0, SEQ, SEQ, n_sub=4)
        oa, oc = _ctx_attn(na, df, lambda_df[l], subln_row, l)
        y1_ctx = _outproj(oa, o_f, o_b, gate, onorm_row, grp256, oc, y_ctx, mod, w_out_l, lg1, lb1, ctx_cond)
        ctx_out.append((na[1].reshape(BATCH, SEQ, H_NA, HEAD_DIM), na[2].reshape(BATCH, SEQ, H_NA, HEAD_DIM),
                        k_df.reshape(BATCH, SEQ, H_DF, HEAD_DIM), v_df.reshape(BATCH, SEQ, H_DF, HEAD_DIM),
                        s_fin))

        *na, dn, gate, q_df, k_df, v_df, ab = _inproj(y_lat, mod, w_in_l, lat_cond)
        df = (q_df, k_df, v_df)
        u, *gates_dn = _dn_prep(dn, ab, conv_w, alog_row, dtb_row, grp512, e_mat, cum, DEC_SEQ)
        o_f, o_b, _ = _deltanet(u, *gates_dn, state_dn[:, l], DEC_SEQ, 512)
        oa = _lat_na(na, cache_na_k[:, l].reshape(DEC_BATCH, PAST_LEN, W_NA).astype(BF16),
                     cache_na_v[:, l].reshape(DEC_BATCH, PAST_LEN, W_NA).astype(BF16),
                     _na_bias_table(rpb_na[l]))
        q_r, kt_all, v_all = _rope(df, cache_df_k[:, l].reshape(DEC_BATCH, PAST_LEN, W_DF),
                                   cache_df_v[:, l].reshape(DEC_BATCH, PAST_LEN, W_DF), cos, sin, rot)
        oc = _lat_df(q_r, kt_all, v_all, lambda_df[l], subln_row, l)
        y1_lat = _outproj(oa, o_f, o_b, gate, onorm_row, grp256, oc, y_lat, mod, w_out_l, lg1, lb1, lat_cond)

        i = l // 2
        if l % 2 == 0:
            wg, wu, wd = ffn_w_gate[i].astype(BF16), ffn_w_up[i].astype(BF16), ffn_w_down[i].astype(BF16)
            y_ctx = _ffn(y1_ctx, mod, wg, wu, wd, lg2, lb2, ctx_cond)
            y_lat = _ffn(y1_lat, mod, wg, wu, wd, lg2, lb2, lat_cond)
        else:
            wg, wu, wd = moe_w_gate[i].astype(BF16), moe_w_up[i].astype(BF16), moe_w_down[i].astype(BF16)
            rw = jnp.pad(router_w[i], ((0, 0), (0, LANES - N_EXPERTS)))
            rb = _pad_row(router_b[i], LANES)
            ys = []
            for y1, cond in ((y1_ctx, ctx_cond), (y1_lat, lat_cond_moe)):
                gates, rank, rank_t, cnt = _router(y1, mod, rw, rb, tri, cond)
                counts = cnt[:, 0, :N_EXPERTS].astype(jnp.int32).reshape(-1)
                ys.append(_moe(y1, mod, gates, rank, rank_t, counts, wg, wu, wd, lg2, lb2, cond))
            y_ctx, y_lat = ys

    stack = lambda j: jnp.stack([t[j] for t in ctx_out], axis=1)
    return (y_ctx.reshape(BATCH, SEQ, D_MODEL), y_lat.reshape(DEC_BATCH, DEC_SEQ, D_MODEL),
            stack(0), stack(1), stack(2), stack(3), stack(4))
```

```python
import functools
import math

import jax
import jax.numpy as jnp
import numpy as np
from jax import lax
from jax.experimental import pallas as pl
from jax.experimental.pallas import tpu as pltpu

F32 = jnp.float32
BF16 = jnp.bfloat16
HIGHEST = lax.Precision.HIGHEST

D_MODEL = 1024
BATCH = 32
SEQ = 256
DEPTH = 2
DEC_BATCH = 2
DEC_SEQ = 4096
PAST_LEN = 512
GRID_W = 64
GRID_H = DEC_SEQ // GRID_W
HEAD_DIM = 64
H_NA = 6
H_DN = 4
H_DF = 6
W_NA = H_NA * HEAD_DIM
W_DN = H_DN * HEAD_DIM
W_DF = H_DF * HEAD_DIM
DF_QK = HEAD_DIM // 2
WIN_R = 8
WIN_C = 16
CONV_K = 5
CHUNK = 64
D_FF = 2816
N_EXPERTS = 8
MOE_FF = 1408
ALPHA = (2 * DEPTH) ** 0.25
LN_EPS = 1e-5
RMS_EPS = 1e-6
ROPE_BASE = 10000.0
NEG_INF = -1e30
PROJ_SIZES = (W_NA, W_NA, W_NA, 3 * W_DN, W_DN, 2 * H_DN, 2 * H_DN, W_DF, W_DF, W_DF)

T_CTX = BATCH * SEQ
T_LAT = DEC_BATCH * DEC_SEQ
N_COND = 8
N_MOD = 6
ROW_TILE = 512
DN_ROWS = 512
LANES = 128
AB_PAD = LANES
SEG_WIDTHS = (W_NA, W_NA, W_NA, 3 * W_DN, W_DN, W_DF, W_DF, W_DF, AB_PAD)
SEG_OFFS = tuple(int(o) for o in np.cumsum((0,) + SEG_WIDTHS))
SEGS = tuple(zip(SEG_OFFS[:-1], SEG_OFFS[1:]))
SEG_GROUPS = ((0, 1, 2), (3,), (4,), (5, 6, 7), (8,))
P_PAD = SEG_OFFS[-1]
BD = H_DN * CHUNK
V_AUG = LANES
DF_Q_SCALE = DF_QK ** -0.5 * math.log2(math.e)
VMEM_LIMIT = 56 * 1024 * 1024


def _params(*sem):
    return pltpu.CompilerParams(dimension_semantics=sem, vmem_limit_bytes=VMEM_LIMIT)


def _dot(a, b):
    return jnp.dot(a.astype(BF16), b.astype(BF16), preferred_element_type=F32)


def _dot_nt(a, b):
    return lax.dot_general(a.astype(BF16), b.astype(BF16), (((1,), (1,)), ((), ())),
                           preferred_element_type=F32)


def _split3(x):
    x1 = x.astype(BF16)
    r1 = x - x1.astype(F32)
    x2 = r1.astype(BF16)
    return x1, x2, (r1 - x2.astype(F32)).astype(BF16)


def _dot_sel(x, sel):
    m = x.shape[0]
    y = jnp.dot(jnp.concatenate(_split3(x), axis=0), sel, preferred_element_type=F32)
    return y[:m] + y[m:2 * m] + y[2 * m:]


def _sel_dot(sel, x):
    n = x.shape[1]
    y = jnp.dot(sel, jnp.concatenate(_split3(x), axis=1), preferred_element_type=F32)
    return y[:, :n] + y[:, n:2 * n] + y[:, 2 * n:]


def _dot_exact(a, b):
    return jnp.dot(a, b, precision=HIGHEST, preferred_element_type=F32)


def _silu(x):
    return x * jax.nn.sigmoid(x)


def _layer_norm(x, g, b):
    mu = jnp.mean(x, axis=-1, keepdims=True)
    xc = x - mu
    var = jnp.mean(xc * xc, axis=-1, keepdims=True)
    return xc * lax.rsqrt(var + LN_EPS) * g + b


def _ada_kernel(c_ref, w_ref, b_ref, o_ref):
    o_ref[...] = _dot_exact(_silu(c_ref[...]), w_ref[...]) + b_ref[...]


def _ada_table(conds, ada_w, ada_b):
    out = pl.pallas_call(
        _ada_kernel,
        grid=(DEPTH, N_MOD),
        in_specs=[pl.BlockSpec((N_COND, D_MODEL), lambda l, k: (0, 0)),
                  pl.BlockSpec((None, D_MODEL, D_MODEL), lambda l, k: (l, 0, k)),
                  pl.BlockSpec((None, None, 1, D_MODEL), lambda l, k: (l, k, 0, 0))],
        out_specs=pl.BlockSpec((None, None, N_COND, D_MODEL), lambda l, k: (l, k, 0, 0)),
        out_shape=jax.ShapeDtypeStruct((DEPTH, N_MOD, N_COND, D_MODEL), F32),
        compiler_params=_params("parallel", "parallel"),
        name="ada_table",
    )(conds, ada_w, ada_b.reshape(DEPTH, N_MOD, 1, D_MODEL))
    return out.reshape(DEPTH, N_MOD, N_COND, 1, D_MODEL)


def _mod_spec(k, cond_fn):
    return pl.BlockSpec((None, None, 1, D_MODEL), lambda i, *_: (k, cond_fn(i), 0, 0))


def _inproj_kernel(x_ref, sh_ref, sc_ref, w_ref, *o_refs):
    h = (x_ref[...] * (1.0 + sc_ref[...]) + sh_ref[...]).astype(BF16)
    for group in SEG_GROUPS:
        lo = SEG_OFFS[group[0]]
        res = jnp.dot(h, w_ref[:, lo:SEG_OFFS[group[-1] + 1]], preferred_element_type=F32)
        for n in group:
            o_refs[n][...] = res[:, SEG_OFFS[n] - lo:SEG_OFFS[n + 1] - lo]


def _inproj(x, mod, w, cond_fn, tm=ROW_TILE):
    t = x.shape[0]
    return pl.pallas_call(
        _inproj_kernel,
        grid=(t // tm,),
        in_specs=[pl.BlockSpec((tm, D_MODEL), lambda i: (i, 0)),
                  _mod_spec(0, cond_fn), _mod_spec(1, cond_fn),
                  pl.BlockSpec((D_MODEL, P_PAD), lambda i: (0, 0))],
        out_specs=[pl.BlockSpec((tm, b - a), lambda i: (i, 0)) for a, b in SEGS],
        out_shape=[jax.ShapeDtypeStruct((t, b - a), F32) for a, b in SEGS],
        compiler_params=_params("parallel"),
        name="inproj",
    )(x, mod, mod, w)


def _dn_prep_kernel(x_ref, prev_ref, next_ref, ab_ref, w_ref, alog_ref, dtb_ref, grp_ref, e_ref, cum_ref,
                    u_ref, gcf_ref, gcb_ref, bf_ref, bb_ref, *, blocks_per_seq, rows):
    i = pl.program_id(0)
    j = i % blocks_per_seq
    prev = jnp.where(j != 0, prev_ref[...], 0.0)
    nxt = jnp.where(j != blocks_per_seq - 1, next_ref[...], 0.0)
    xe = jnp.concatenate([prev, x_ref[...], nxt], axis=0)
    w = w_ref[...]
    base = 8 - CONV_K // 2
    acc = w[0:1, :] * xe[base:base + rows, :]
    for t in range(1, CONV_K):
        acc = acc + w[t:t + 1, :] * xe[base + t:base + t + rows, :]
    u = _silu(acc)
    qk = u[:, :2 * W_DN]
    ss = _dot_sel(qk * qk, grp_ref[...])
    qk = qk * lax.rsqrt(ss + RMS_EPS)
    u_ref[:, :W_DN] = qk[:, :W_DN] * HEAD_DIM ** -0.5
    u_ref[:, W_DN:2 * W_DN] = qk[:, W_DN:]
    u_ref[:, 2 * W_DN:] = u[:, 2 * W_DN:]
    ab = ab_ref[...]
    z = ab + dtb_ref[...]
    softplus = jnp.maximum(z, 0.0) + jnp.log(1.0 + jnp.exp(-jnp.abs(z)))
    g = -jnp.exp(alog_ref[...]) * softplus
    lane = lax.broadcasted_iota(jnp.int32, ab.shape, 1)
    gb = jnp.where(lane < 2 * H_DN, g, jax.nn.sigmoid(ab))
    for d, (gc_ref, beta_ref) in enumerate(((gcf_ref, bf_ref), (gcb_ref, bb_ref))):
        gbx = _dot_sel(gb, e_ref[d])
        gc_ref[...] = _sel_dot(cum_ref[d], gbx[:, :W_DN])
        beta_ref[...] = gbx[:, W_DN:]


def _dn_prep(qkv, ab, conv_w, alog_row, dtb_row, grp, e_mat, cum, seq_len, rows=SEQ):
    t = qkv.shape[0]
    bps = seq_len // rows
    r8 = rows // 8
    last8 = t // 8 - 1
    kern = functools.partial(_dn_prep_kernel, blocks_per_seq=bps, rows=rows)
    return pl.pallas_call(
        kern,
        grid=(t // rows,),
        in_specs=[pl.BlockSpec((rows, 3 * W_DN), lambda i: (i, 0)),
                  pl.BlockSpec((8, 3 * W_DN), lambda i: (jnp.maximum(i * r8 - 1, 0), 0)),
                  pl.BlockSpec((8, 3 * W_DN), lambda i: (jnp.minimum((i + 1) * r8, last8), 0)),
                  pl.BlockSpec((rows, AB_PAD), lambda i: (i, 0)),
                  pl.BlockSpec((8, 3 * W_DN), lambda i: (0, 0)),
                  pl.BlockSpec((1, AB_PAD), lambda i: (0, 0)),
                  pl.BlockSpec((1, AB_PAD), lambda i: (0, 0)),
                  pl.BlockSpec((2 * W_DN, 2 * W_DN), lambda i: (0, 0)),
                  pl.BlockSpec((2, AB_PAD, 2 * W_DN), lambda i: (0, 0, 0)),
                  pl.BlockSpec((2, rows, rows), lambda i: (0, 0, 0))],
        out_specs=[pl.BlockSpec((rows, 3 * W_DN), lambda i: (i, 0))]
        + [pl.BlockSpec((rows, W_DN), lambda i: (i, 0))] * 4,
        out_shape=[jax.ShapeDtypeStruct((t, 3 * W_DN), F32)]
        + [jax.ShapeDtypeStruct((t, W_DN), F32)] * 4,
        compiler_params=_params("parallel"),
        name="dn_prep",
    )(qkv, qkv, qkv, ab, conv_w, alog_row, dtb_row, grp, e_mat, cum)


def _dn_kernel(uf_ref, ub_ref, gcf_ref, gcb_ref, bf_ref, bb_ref, s0_ref, of_ref, ob_ref, sf_ref, s_scr,
               *, n_chunks, n_sub):
    j = pl.program_id(1)
    r = lax.broadcasted_iota(jnp.int32, (BD, BD), 0)
    c = lax.broadcasted_iota(jnp.int32, (BD, BD), 1)
    same = (r // CHUNK) == (c // CHUNK)
    dt = (r % CHUNK) - (c % CHUNK)
    same_f = jnp.where(same, 1.0, 0.0)
    same_b = same_f.astype(BF16)
    eye_f = jnp.where(r == c, 1.0, 0.0)
    dirs = []
    for sign, last in ((1, CHUNK - 1), (-1, 0)):
        dirs.append((jnp.where(jnp.logical_and(same, dt * sign >= 0), 1.0, 0.0),
                     jnp.where(jnp.logical_and(same, dt * sign > 0), 1.0, 0.0), last))

    def lift(x):
        return jnp.concatenate([x, x, x, x], axis=0) * same_f

    def lift_b(x):
        xb = x.astype(BF16)
        return jnp.concatenate([xb, xb, xb, xb], axis=0) * same_b

    @pl.when(j == 0)
    def _():
        for si in range(n_sub):
            for di in range(2):
                rows = [jnp.concatenate([s0_ref[si, di, h]] * H_DN, axis=1) for h in range(H_DN)]
                s_scr[si, di] = jnp.concatenate(rows, axis=0) * same_f

    def each(f, *xs):
        return [f(*a) for a in zip(*xs)]

    def mm(a, b):
        return jnp.dot(a, b, preferred_element_type=F32)

    def chunks(us, gc, beta, ss, incls, stricts, lasts):
        q = [u[:, :W_DN] for u in us]
        k = [u[:, W_DN:2 * W_DN] for u in us]
        v = [u[:, 2 * W_DN:] for u in us]
        gl = each(lambda x, last: x[last:last + 1, :], gc, lasts)
        eg = [jnp.exp(x) for x in gc]
        kb = each(jnp.multiply, k, beta)
        gcol = [lift(x) for x in gc]
        decay = each(lambda x, m: jnp.exp(jnp.where(m > 0.5, x - x.T, NEG_INF)), gcol, incls)
        qk = each(lambda a, b, kk: lax.dot_general(jnp.concatenate([lift_b(a), lift_b(b)], axis=0),
                                                   lift_b(kk), (((1,), (1,)), ((), ())),
                                                   preferred_element_type=F32), q, kb, k)
        attn = each(lambda x, d: (x[:BD] * d).astype(BF16), qk, decay)
        n_mat = each(lambda x, d, m: -(x[BD:] * d) * m, qk, decay, stricts)
        t_inv = [eye_f + x for x in n_mat]
        n_hi = [x.astype(BF16) for x in n_mat]
        m_b = n_hi
        for _ in range(int(math.log2(CHUNK)) - 1):
            m_b = [mm(x, x).astype(BF16) for x in m_b]
            t_inv = each(lambda t, m: t + mm(t.astype(BF16), m), t_inv, m_b)
        n_lo = each(lambda x, h: (x - h.astype(F32)).astype(BF16), n_mat, n_hi)
        x_hi = [t.astype(BF16) for t in t_inv]
        x_lo = each(lambda t, h: (t - h.astype(F32)).astype(BF16), t_inv, x_hi)
        nx = each(lambda h, lo, x: mm(jnp.concatenate([h, lo], axis=0), x), n_hi, n_lo, x_hi)
        nxl = each(mm, n_hi, x_lo)
        resid = each(lambda t, a, b: (eye_f - t + a[:BD] + a[BD:] + b).astype(BF16), t_inv, nx, nxl)
        t_b = each(lambda t, h, rr: (t + mm(h, rr)).astype(BF16), t_inv, x_hi, resid)
        rhs = each(lambda vv, b, kk, e: jnp.concatenate([lift_b(vv * b), lift_b(kk * e)], axis=1),
                   v, beta, kb, eg)
        sol = each(mm, t_b, rhs)
        s_b = [s.astype(BF16) for s in ss]
        ps = each(lambda so, qq, e, sb: mm(jnp.concatenate([so[:, BD:].astype(BF16), lift_b(qq * e)],
                                                           axis=0), sb), sol, q, eg, s_b)
        v_new = each(lambda so, p: (so[:, :BD] - p[:BD]).astype(BF16), sol, ps)
        o_bd = each(lambda p, a, vn: p[BD:] + mm(a, vn), ps, attn, v_new)
        o_tm = [x[0:CHUNK] + x[CHUNK:2 * CHUNK] + x[2 * CHUNK:3 * CHUNK] + x[3 * CHUNK:] for x in o_bd]
        k_tail = each(lambda kk, a, b: lift(kk * jnp.exp(a - b)).T.astype(BF16), k, gl, gc)
        s_new = each(lambda s, a, kt, vn: s * jnp.exp(a) + mm(kt, vn), ss, gl, k_tail, v_new)
        return o_tm, s_new

    chains = [(si, di) for si in range(n_sub) for di in range(2)]

    def body(ci, carry):
        offs = (pl.multiple_of(ci * CHUNK, CHUNK), pl.multiple_of((n_chunks - 1 - ci) * CHUNK, CHUNK))
        u_refs, gc_refs, b_refs, o_refs = (uf_ref, ub_ref), (gcf_ref, gcb_ref), (bf_ref, bb_ref), (of_ref, ob_ref)
        o_tm, s_new = chunks(
            [u_refs[di][si, pl.ds(offs[di], CHUNK), :] for si, di in chains],
            [gc_refs[di][si, pl.ds(offs[di], CHUNK), :] for si, di in chains],
            [b_refs[di][si, pl.ds(offs[di], CHUNK), :] for si, di in chains],
            [s_scr[si, di] for si, di in chains],
            *[[dirs[di][n] for si, di in chains] for n in range(3)])
        for (si, di), o, s in zip(chains, o_tm, s_new):
            o_refs[di][si, pl.ds(offs[di], CHUNK), :] = o
            s_scr[si, di] = s
        return carry

    lax.fori_loop(0, n_chunks, body, 0)

    @pl.when(j == pl.num_programs(1) - 1)
    def _():
        for si in range(n_sub):
            for di in range(2):
                for h in range(H_DN):
                    sl = slice(h * HEAD_DIM, (h + 1) * HEAD_DIM)
                    sf_ref[si, di, h] = s_scr[si, di, sl, sl]


def _deltanet(u, gc_f, gc_b, beta_f, beta_b, s0_bd, seq_len, rows, n_sub=2):
    t = u.shape[0]
    n_seq = t // seq_len
    nb = seq_len // rows
    kern = functools.partial(_dn_kernel, n_chunks=rows // CHUNK, n_sub=n_sub)
    fwd = lambda s, j: (s, j, 0)
    bwd = lambda s, j: (s, nb - 1 - j, 0)
    state = pl.BlockSpec((n_sub, 2, H_DN, HEAD_DIM, HEAD_DIM), lambda s, j: (s, 0, 0, 0, 0))
    o_f, o_b, s_fin = pl.pallas_call(
        kern,
        grid=(n_seq // n_sub, nb),
        in_specs=[pl.BlockSpec((n_sub, rows, 3 * W_DN), fwd), pl.BlockSpec((n_sub, rows, 3 * W_DN), bwd),
                  pl.BlockSpec((n_sub, rows, W_DN), fwd), pl.BlockSpec((n_sub, rows, W_DN), bwd),
                  pl.BlockSpec((n_sub, rows, W_DN), fwd), pl.BlockSpec((n_sub, rows, W_DN), bwd), state],
        out_specs=[pl.BlockSpec((n_sub, rows, W_DN), fwd), pl.BlockSpec((n_sub, rows, W_DN), bwd), state],
        out_shape=[jax.ShapeDtypeStruct((n_seq, seq_len, W_DN), F32),
                   jax.ShapeDtypeStruct((n_seq, seq_len, W_DN), F32),
                   jax.ShapeDtypeStruct((n_seq, 2, H_DN, HEAD_DIM, HEAD_DIM), F32)],
        scratch_shapes=[pltpu.VMEM((n_sub, 2, BD, BD), F32)],
        compiler_params=_params("parallel", "arbitrary"),
        name="deltanet",
    )(*[x.reshape(n_seq, seq_len, -1) for x in (u, u, gc_f, gc_b, beta_f, beta_b)], s0_bd)
    return o_f.reshape(t, W_DN), o_b.reshape(t, W_DN), s_fin


def _diff_lambda(lam_ref, lam_init):
    lp = lam_ref[...]
    return (jnp.exp(jnp.sum(lp[0:1] * lp[1:2], axis=1, keepdims=True))
            - jnp.exp(jnp.sum(lp[2:3] * lp[3:4], axis=1, keepdims=True)) + lam_init)


def _sub_rms(o, subln, lam_init):
    ms = jnp.mean(o * o, axis=-1, keepdims=True)
    return o * lax.rsqrt(ms + RMS_EPS) * subln * (1.0 - lam_init)


def _ctx_attn_kernel(qa_ref, ka_ref, va_ref, qc_ref, kc_ref, vc_ref, lam_ref, subln_ref, oa_ref, oc_ref,
                     *, lam_init):
    lam = _diff_lambda(lam_ref, lam_init)
    subln = subln_ref[...]
    n = qa_ref.shape[0]
    qa = (qa_ref[...] * NA_Q_SCALE).astype(BF16)
    ka = ka_ref[...].astype(BF16)
    va = va_ref[...].astype(BF16)
    qc = (qc_ref[...] * DF_Q_SCALE).astype(BF16)
    kc = kc_ref[...].astype(BF16)
    vc = vc_ref[...].astype(BF16)
    heads = [slice(h * HEAD_DIM, (h + 1) * HEAD_DIM) for h in range(H_NA)]
    maps = [slice(m * DF_QK, (m + 1) * DF_QK) for m in range(2 * H_DF)]
    s_na = [_dot_nt(qa[:, sl], ka[:, sl]) for sl in heads]
    s_df = [_dot_nt(qc[:, sl], kc[:, sl]) for sl in maps]
    p_na = [jnp.exp2(s - jnp.max(s, axis=-1, keepdims=True)) for s in s_na]
    p_df = [jnp.exp2(s - jnp.max(s, axis=-1, keepdims=True)) for s in s_df]
    d_na = [jnp.sum(p, axis=-1, keepdims=True) for p in p_na]
    d_df = [jnp.sum(p, axis=-1, keepdims=True) for p in p_df]
    o_na = [_dot(p, va[:, sl]) for p, sl in zip(p_na, heads)]
    o_df = [_dot(jnp.concatenate([p_df[2 * h].astype(BF16), p_df[2 * h + 1].astype(BF16)], axis=0),
                 vc[:, heads[h]]) for h in range(H_DF)]
    for h in range(H_NA):
        oa_ref[:, heads[h]] = (o_na[h] / d_na[h]).astype(BF16)
    for h in range(H_DF):
        o = o_df[h][:n] / d_df[2 * h] - lam * (o_df[h][n:] / d_df[2 * h + 1])
        oc_ref[:, heads[h]] = _sub_rms(o, subln, lam_init).astype(BF16)


def _ctx_attn(qkv_na, qkv_df, lam_p, subln_row, layer):
    lam_init = 0.8 - 0.6 * math.exp(-0.3 * layer)
    kern = functools.partial(_ctx_attn_kernel, lam_init=lam_init)
    return pl.pallas_call(
        kern,
        grid=(BATCH,),
        in_specs=[pl.BlockSpec((SEQ, W_NA), lambda b: (b, 0))] * 3
        + [pl.BlockSpec((SEQ, W_DF), lambda b: (b, 0))] * 3
        + [pl.BlockSpec((4, DF_QK), lambda b: (0, 0)),
                  pl.BlockSpec((1, HEAD_DIM), lambda b: (0, 0))],
        out_specs=[pl.BlockSpec((SEQ, W_NA), lambda b: (b, 0)),
                   pl.BlockSpec((SEQ, W_DF), lambda b: (b, 0))],
        out_shape=[jax.ShapeDtypeStruct((T_CTX, W_NA), BF16),
                   jax.ShapeDtypeStruct((T_CTX, W_DF), BF16)],
        compiler_params=_params("parallel"),
        name="ctx_attn",
    )(*qkv_na, *qkv_df, lam_p, subln_row)


NA_R = 4
NA_U = NA_R + WIN_R - 1
NA_GROUPS = GRID_H // NA_R
NA_Q_SCALE = HEAD_DIM ** -0.5 * math.log2(math.e)


def _na_key_start(g):
    return jnp.clip(g * NA_R - WIN_R // 2, 0, GRID_H - NA_U)


def _na_bias_table(rpb):
    qc = np.arange(GRID_W)[:, None]
    kc = np.arange(GRID_W)[None, :]
    cs = np.clip(qc - WIN_C // 2, 0, GRID_W - WIN_C)
    valid = (kc >= cs) & (kc < cs + WIN_C)
    dc = np.clip(kc - qc + (WIN_C - 1), 0, 2 * WIN_C - 2)
    onehot = (dc[None] == np.arange(2 * WIN_C - 1)[:, None, None]).astype(np.float32)
    x = jnp.einsum('hrd,dqk->hrqk', rpb.astype(F32), jnp.asarray(onehot), precision=HIGHEST)
    return jnp.where(valid[None, None], x * math.log2(math.e), NEG_INF)


def _na_fill_bias(tab_ref, bias_ref, g):
    u0 = int(np.clip(g * NA_R - WIN_R // 2, 0, GRID_H - NA_U))
    masked = jnp.full((H_NA, GRID_W, GRID_W), NEG_INF, F32)
    for i in range(NA_R):
        r = g * NA_R + i
        w0 = int(np.clip(r - WIN_R // 2, 0, GRID_H - WIN_R))
        for u in range(NA_U):
            inside = w0 <= u0 + u < w0 + WIN_R
            block = tab_ref[:, u0 + u - r + WIN_R - 1] if inside else masked
            bias_ref[:, i * GRID_W:(i + 1) * GRID_W, u * GRID_W:(u + 1) * GRID_W] = block


def _lat_na_kernel(q_ref, k_ref, v_ref, kc_ref, vc_ref, tab_ref, o_ref, bias_ref):
    g = pl.program_id(1)
    for g_build in (0, 1, NA_GROUPS - 1):
        @pl.when(g == g_build)
        def _():
            _na_fill_bias(tab_ref, bias_ref, g_build)

    start = pl.multiple_of(_na_key_start(g) * GRID_W, GRID_W)
    q = (q_ref[...] * NA_Q_SCALE).astype(BF16)
    kw = k_ref[pl.ds(start, NA_U * GRID_W), :].astype(BF16)
    vw = v_ref[pl.ds(start, NA_U * GRID_W), :].astype(BF16)
    kc = kc_ref[...]
    vc = vc_ref[...]
    heads = [slice(h * HEAD_DIM, (h + 1) * HEAD_DIM) for h in range(H_NA)]
    s_loc = [_dot_nt(q[:, sl], kw[:, sl]) + bias_ref[h] for h, sl in enumerate(heads)]
    s_ctx = [_dot_nt(q[:, sl], kc[:, sl]) for sl in heads]
    m = [jnp.maximum(jnp.max(a, axis=-1, keepdims=True), jnp.max(b, axis=-1, keepdims=True))
         for a, b in zip(s_loc, s_ctx)]
    p_loc = [jnp.exp2(a - mm) for a, mm in zip(s_loc, m)]
    p_ctx = [jnp.exp2(b - mm) for b, mm in zip(s_ctx, m)]
    den = [jnp.sum(a, axis=-1, keepdims=True) + jnp.sum(b, axis=-1, keepdims=True)
           for a, b in zip(p_loc, p_ctx)]
    o = [_dot(a, vw[:, sl]) + _dot(b, vc[:, sl]) for a, b, sl in zip(p_loc, p_ctx, heads)]
    for h, sl in enumerate(heads):
        o_ref[:, sl] = (o[h] / den[h]).astype(BF16)


def _lat_na(qkv_na, ck, cv, bias_tab):
    rows = NA_R * GRID_W
    return pl.pallas_call(
        _lat_na_kernel,
        grid=(DEC_BATCH, NA_GROUPS),
        in_specs=[pl.BlockSpec((rows, W_NA), lambda b, g: (b * NA_GROUPS + g, 0)),
                  pl.BlockSpec((DEC_SEQ, W_NA), lambda b, g: (b, 0)),
                  pl.BlockSpec((DEC_SEQ, W_NA), lambda b, g: (b, 0)),
                  pl.BlockSpec((None, PAST_LEN, W_NA), lambda b, g: (b, 0, 0)),
                  pl.BlockSpec((None, PAST_LEN, W_NA), lambda b, g: (b, 0, 0)),
                  pl.BlockSpec((H_NA, 2 * WIN_R - 1, GRID_W, GRID_W), lambda b, g: (0, 0, 0, 0))],
        out_specs=pl.BlockSpec((rows, W_NA), lambda b, g: (b * NA_GROUPS + g, 0)),
        out_shape=jax.ShapeDtypeStruct((T_LAT, W_NA), BF16),
        scratch_shapes=[pltpu.VMEM((H_NA, rows, NA_U * GRID_W), F32)],
        compiler_params=_params("arbitrary", "arbitrary"),
        name="lat_na",
    )(*qkv_na, ck, cv, bias_tab)


def _rope_tables():
    nf = DF_QK // 4
    inv = ROPE_BASE ** (-np.arange(nf, dtype=np.float32) / nf)
    t = np.arange(DEC_SEQ)
    pos = np.stack([t // GRID_W, t % GRID_W], axis=-1).astype(np.float32)
    ang = jnp.asarray(pos[:, :, None] * inv)
    cos, sin = jnp.cos(ang), jnp.sin(ang)
    cos32 = jnp.concatenate([cos, cos], axis=-1).reshape(DEC_SEQ, DF_QK)
    sin32 = jnp.concatenate([-sin, sin], axis=-1).reshape(DEC_SEQ, DF_QK)
    reps = W_DF // DF_QK
    rot = np.zeros((W_DF, W_DF), np.float32)
    for dd in range(W_DF):
        rot[dd + nf if dd % (2 * nf) < nf else dd - nf, dd] = 1.0
    return jnp.tile(cos32, (1, reps)), jnp.tile(sin32, (1, reps)), jnp.asarray(rot, dtype=BF16)


def _rope_kernel(qin_ref, kin_ref, vin_ref, kc_ref, vc_ref, cos_ref, sin_ref, rot_ref, q_ref, kt_ref, v_ref,
                 *, nb):
    i = pl.program_id(1)
    lane = lax.broadcasted_iota(jnp.int32, (q_ref.shape[0], V_AUG - HEAD_DIM), 1)
    one_col = jnp.where(lane == 0, 1.0, 0.0).astype(BF16)

    def put_values(src_ref):
        for h in range(H_DF):
            a = h * HEAD_DIM
            v_ref[:, h * V_AUG:h * V_AUG + HEAD_DIM] = src_ref[:, a:a + HEAD_DIM].astype(BF16)
            v_ref[:, h * V_AUG + HEAD_DIM:(h + 1) * V_AUG] = one_col

    @pl.when(i < nb)
    def _():
        cos, sin, rot = cos_ref[...], sin_ref[...], rot_ref[...]
        q = qin_ref[...]
        k = kin_ref[...]
        q_ref[...] = ((q * cos + _dot_sel(q, rot) * sin) * DF_Q_SCALE).astype(BF16)
        kt_ref[...] = (k * cos + _dot_sel(k, rot) * sin).T.astype(BF16)
        put_values(vin_ref)

    @pl.when(i == nb)
    def _():
        kt_ref[...] = kc_ref[...].T.astype(BF16)
        put_values(vc_ref)


def _rope(qkv_df, kc, vc, cos, sin, rot, tm=PAST_LEN):
    nb = DEC_SEQ // tm
    lat = lambda b, i: (b * nb + jnp.minimum(i, nb - 1), 0)
    pos = lambda b, i: (jnp.minimum(i, nb - 1), 0)
    ctx = pl.BlockSpec((None, PAST_LEN, W_DF), lambda b, i: (b, 0, 0))
    return pl.pallas_call(
        functools.partial(_rope_kernel, nb=nb),
        grid=(DEC_BATCH, nb + 1),
        in_specs=[pl.BlockSpec((tm, W_DF), lat)] * 3 + [ctx, ctx]
        + [pl.BlockSpec((tm, W_DF), pos), pl.BlockSpec((tm, W_DF), pos),
           pl.BlockSpec((W_DF, W_DF), lambda b, i: (0, 0))],
        out_specs=[pl.BlockSpec((tm, W_DF), lat),
                   pl.BlockSpec((None, W_DF, tm), lambda b, i: (b, 0, i)),
                   pl.BlockSpec((None, tm, H_DF * V_AUG), lambda b, i: (b, i, 0))],
        out_shape=[jax.ShapeDtypeStruct((T_LAT, W_DF), BF16),
                   jax.ShapeDtypeStruct((DEC_BATCH, W_DF, DEC_SEQ + PAST_LEN), BF16),
                   jax.ShapeDtypeStruct((DEC_BATCH, DEC_SEQ + PAST_LEN, H_DF * V_AUG), BF16)],
        compiler_params=_params("parallel", "arbitrary"),
        name="rope",
    )(*qkv_df, kc, vc, cos, sin, rot)


def _lat_df_kernel(q_ref, kt_ref, v_ref, lam_ref, subln_ref, o_ref, *, lam_init, tq):
    q = q_ref[...]
    lam = _diff_lambda(lam_ref, lam_init)
    subln = subln_ref[...]

    def scores(n):
        a = n * DF_QK
        return jnp.dot(q[:, a:a + DF_QK], kt_ref[a:a + DF_QK, :], preferred_element_type=F32)

    n_maps = 2 * H_DF
    es = []
    s_next = scores(0)
    for n in range(n_maps):
        s = s_next
        if n + 1 < n_maps:
            s_next = scores(n + 1)
        es.append(jnp.exp2(s - jnp.max(s, axis=-1, keepdims=True)).astype(BF16))
        if n % 2 == 1:
            h = n // 2
            ov = jnp.dot(jnp.concatenate(es, axis=0), v_ref[:, h * V_AUG:(h + 1) * V_AUG],
                         preferred_element_type=F32)
            o = (ov[:tq, :HEAD_DIM] / ov[:tq, HEAD_DIM:HEAD_DIM + 1]
                 - lam * (ov[tq:, :HEAD_DIM] / ov[tq:, HEAD_DIM:HEAD_DIM + 1]))
            o_ref[:, h * HEAD_DIM:(h + 1) * HEAD_DIM] = _sub_rms(o, subln, lam_init).astype(BF16)
            es = []


def _lat_df(q_r, kt_all, v_all, lam_p, subln_row, layer, tq=2 * LANES):
    lam_init = 0.8 - 0.6 * math.exp(-0.3 * layer)
    nb = DEC_SEQ // tq
    n_keys = kt_all.shape[2]
    kern = functools.partial(_lat_df_kernel, lam_init=lam_init, tq=tq)
    return pl.pallas_call(
        kern,
        grid=(DEC_BATCH, nb),
        in_specs=[pl.BlockSpec((tq, W_DF), lambda b, i: (b * nb + i, 0)),
                  pl.BlockSpec((None, W_DF, n_keys), lambda b, i: (b, 0, 0)),
                  pl.BlockSpec((None, n_keys, H_DF * V_AUG), lambda b, i: (b, 0, 0)),
                  pl.BlockSpec((4, DF_QK), lambda b, i: (0, 0)),
                  pl.BlockSpec((1, HEAD_DIM), lambda b, i: (0, 0))],
        out_specs=pl.BlockSpec((tq, W_DF), lambda b, i: (b * nb + i, 0)),
        out_shape=jax.ShapeDtypeStruct((T_LAT, W_DF), BF16),
        compiler_params=_params("parallel", "arbitrary"),
        name="lat_df",
    )(q_r, kt_all, v_all, lam_p, subln_row)


def _outproj_kernel(oa_ref, of_ref, obk_ref, gate_ref, onorm_ref, grp_ref, oc_ref, y_ref, g1_ref, w_ref,
                    lg_ref, lb_ref, o_ref):
    ob = of_ref[...] + obk_ref[...]
    ms = _dot_sel(ob * ob, grp_ref[...]) * (1.0 / HEAD_DIM)
    ob = ob * lax.rsqrt(ms + RMS_EPS) * onorm_ref[...] * _silu(gate_ref[...])
    o = (jnp.dot(oa_ref[...].astype(BF16), w_ref[:W_NA, :], preferred_element_type=F32)
         + jnp.dot(ob.astype(BF16), w_ref[W_NA:W_NA + W_DN, :], preferred_element_type=F32)
         + jnp.dot(oc_ref[...].astype(BF16), w_ref[W_NA + W_DN:, :], preferred_element_type=F32))
    o_ref[...] = _layer_norm(ALPHA * y_ref[...] + g1_ref[...] * o, lg_ref[...], lb_ref[...])


def _outproj(oa, o_fwd, o_bwd, gate, onorm_row, grp, oc, y, mod, w, ln_g, ln_b, cond_fn, tm=ROW_TILE):
    t = y.shape[0]
    row = lambda n: pl.BlockSpec((tm, n), lambda i: (i, 0))
    vec = pl.BlockSpec((1, D_MODEL), lambda i: (0, 0))
    return pl.pallas_call(
        _outproj_kernel,
        grid=(t // tm,),
        in_specs=[row(W_NA), row(W_DN), row(W_DN),
                  row(W_DN), pl.BlockSpec((1, W_DN), lambda i: (0, 0)),
                  pl.BlockSpec((W_DN, W_DN), lambda i: (0, 0)),
                  row(W_DF), row(D_MODEL), _mod_spec(2, cond_fn),
                  pl.BlockSpec((D_MODEL, D_MODEL), lambda i: (0, 0)), vec, vec],
        out_specs=row(D_MODEL),
        out_shape=jax.ShapeDtypeStruct((t, D_MODEL), F32),
        compiler_params=_params("parallel"),
        name="outproj_ln",
    )(oa, o_fwd, o_bwd, gate, onorm_row, grp, oc, y, mod, w, ln_g, ln_b)


MOE_TM = 1024
MOE_CAP = 256
MOE_CAP_SMALL = 128


def _router_kernel(y_ref, sh_ref, sc_ref, w_ref, b_ref, tri_ref, g_ref, rk_ref, rkt_ref, cnt_ref):
    h = y_ref[...] * (1.0 + sc_ref[...]) + sh_ref[...]
    logits = _dot_exact(h, w_ref[...]) + b_ref[...]
    lane = lax.broadcasted_iota(jnp.int32, logits.shape, 1).astype(F32)
    logits = jnp.where(lane < N_EXPERTS, logits, -jnp.inf)
    m1 = jnp.max(logits, axis=-1, keepdims=True)
    i1 = jnp.min(jnp.where(logits == m1, lane, float(LANES)), axis=-1, keepdims=True)
    rest = jnp.where(lane == i1, -jnp.inf, logits)
    m2 = jnp.max(rest, axis=-1, keepdims=True)
    i2 = jnp.min(jnp.where(rest == m2, lane, float(LANES)), axis=-1, keepdims=True)
    e2 = jnp.exp(m2 - m1)
    w1 = 1.0 / (1.0 + e2)
    g_ref[...] = jnp.where(lane == i1, w1, 0.0) + jnp.where(lane == i2, e2 * w1, 0.0)
    routed = jnp.where(lane == i1, 1.0, 0.0) + jnp.where(lane == i2, 1.0, 0.0)
    before = jnp.dot(tri_ref[...], routed.astype(BF16), preferred_element_type=F32)
    rank = jnp.where(routed > 0.5, before, -1.0)
    rk_ref[...] = rank
    rkt_ref[...] = rank.T[:N_EXPERTS, :]
    cnt_ref[...] = jnp.sum(routed, axis=0, keepdims=True)


def _router(y, mod, w_pad, b_pad, tri, cond_fn, tm=MOE_TM):
    t = y.shape[0]
    nt = t // tm
    return pl.pallas_call(
        _router_kernel,
        grid=(nt,),
        in_specs=[pl.BlockSpec((tm, D_MODEL), lambda i: (i, 0)),
                  _mod_spec(3, cond_fn), _mod_spec(4, cond_fn),
                  pl.BlockSpec((D_MODEL, LANES), lambda i: (0, 0)),
                  pl.BlockSpec((1, LANES), lambda i: (0, 0)),
                  pl.BlockSpec((tm, tm), lambda i: (0, 0))],
        out_specs=[pl.BlockSpec((tm, LANES), lambda i: (i, 0)),
                   pl.BlockSpec((tm, LANES), lambda i: (i, 0)),
                   pl.BlockSpec((None, N_EXPERTS, tm), lambda i: (i, 0, 0)),
                   pl.BlockSpec((None, 1, LANES), lambda i: (i, 0, 0))],
        out_shape=[jax.ShapeDtypeStruct((t, LANES), F32),
                   jax.ShapeDtypeStruct((t, LANES), F32),
                   jax.ShapeDtypeStruct((nt, N_EXPERTS, tm), F32),
                   jax.ShapeDtypeStruct((nt, 1, LANES), F32)],
        compiler_params=_params("parallel"),
        name="router",
    )(y, mod, mod, w_pad, b_pad, tri)


def _moe_kernel(cnt_ref, y_ref, sh_ref, sc_ref, g2_ref, gates_ref, rk_ref, rkt_ref, wg_ref, wu_ref, wd_ref,
                lg_ref, lb_ref, o_ref, h_scr, acc_scr, *, tm, cap, cap_small):
    i = pl.program_id(0)
    e = pl.program_id(1)

    @pl.when(e == 0)
    def _():
        h_scr[...] = (y_ref[...] * (1.0 + sc_ref[...]) + sh_ref[...]).astype(BF16)
        acc_scr[...] = jnp.zeros_like(acc_scr)

    lane = lax.broadcasted_iota(jnp.int32, (tm, LANES), 1)
    gate_col = jnp.sum(jnp.where(lane == e, gates_ref[...], 0.0), axis=-1, keepdims=True)
    rank_col = jnp.sum(jnp.where(lane == e, rk_ref[...], 0.0), axis=-1, keepdims=True)
    rank_row = rkt_ref[pl.ds(e, 1), :]

    def run_pass(first, rows):
        base = first.astype(F32)
        slot_r = lax.broadcasted_iota(jnp.int32, (rows, tm), 0).astype(F32) + base
        sel = jnp.where(rank_row == slot_r, 1.0, 0.0).astype(BF16)
        xe = jnp.dot(sel, h_scr[...], preferred_element_type=F32).astype(BF16)
        f = _swiglu(xe, wg_ref, wu_ref, wd_ref).astype(BF16)
        slot_c = lax.broadcasted_iota(jnp.int32, (tm, rows), 1).astype(F32) + base
        sel_t = jnp.where(rank_col == slot_c, 1.0, 0.0).astype(BF16)
        acc_scr[...] += gate_col * jnp.dot(sel_t, f, preferred_element_type=F32)

    count = cnt_ref[i * N_EXPERTS + e]
    rem = count % cap
    n_big = count // cap + jnp.where(rem > cap_small, 1, 0)

    def body(ps, carry):
        run_pass(ps * cap, cap)
        return carry

    lax.fori_loop(0, n_big, body, 0)

    @pl.when(jnp.logical_and(rem > 0, rem <= cap_small))
    def _():
        run_pass(n_big * cap, cap_small)

    @pl.when(e == N_EXPERTS - 1)
    def _():
        o_ref[...] = _layer_norm(ALPHA * y_ref[...] + g2_ref[...] * acc_scr[...],
                                 lg_ref[...], lb_ref[...])


def _moe(y, mod, gates, rank, rank_t, counts, wg, wu, wd, ln_g, ln_b, cond_fn, tm=MOE_TM, cap=MOE_CAP,
         cap_small=MOE_CAP_SMALL):
    t = y.shape[0]
    ff = wg.shape[2]
    vec = pl.BlockSpec((1, D_MODEL), lambda i, e, c: (0, 0))
    tok = lambda n: pl.BlockSpec((tm, n), lambda i, e, c: (i, 0))
    kern = functools.partial(_moe_kernel, tm=tm, cap=cap, cap_small=cap_small)
    return pl.pallas_call(
        kern,
        grid_spec=pltpu.PrefetchScalarGridSpec(
            num_scalar_prefetch=1,
            grid=(t // tm, N_EXPERTS),
            in_specs=[tok(D_MODEL), _mod_spec(3, cond_fn), _mod_spec(4, cond_fn), _mod_spec(5, cond_fn),
                      tok(LANES), tok(LANES),
                      pl.BlockSpec((None, N_EXPERTS, tm), lambda i, e, c: (i, 0, 0)),
                      pl.BlockSpec((None, D_MODEL, ff), lambda i, e, c: (e, 0, 0)),
                      pl.BlockSpec((None, D_MODEL, ff), lambda i, e, c: (e, 0, 0)),
                      pl.BlockSpec((None, ff, D_MODEL), lambda i, e, c: (e, 0, 0)),
                      vec, vec],
            out_specs=tok(D_MODEL),
            scratch_shapes=[pltpu.VMEM((tm, D_MODEL), BF16), pltpu.VMEM((tm, D_MODEL), F32)]),
        out_shape=jax.ShapeDtypeStruct((t, D_MODEL), F32),
        compiler_params=_params("parallel", "arbitrary"),
        name="moe_ln",
    )(counts, y, mod, mod, mod, gates, rank, rank_t, wg, wu, wd, ln_g, ln_b)


FF_CUT = 768


def _swiglu(h, wg_ref, wu_ref, wd_ref):
    cuts = (0, FF_CUT, wg_ref.shape[1])
    parts = [(jnp.dot(h, wg_ref[:, lo:hi], preferred_element_type=F32),
              jnp.dot(h, wu_ref[:, lo:hi], preferred_element_type=F32))
             for lo, hi in zip(cuts, cuts[1:])]
    outs = [jnp.dot((_silu(a) * b).astype(BF16), wd_ref[lo:hi, :], preferred_element_type=F32)
            for (a, b), lo, hi in zip(parts, cuts, cuts[1:])]
    return outs[0] + outs[1]


def _ffn_kernel(y_ref, sh_ref, sc_ref, g2_ref, wg_ref, wu_ref, wd_ref, lg_ref, lb_ref,
                o_ref, h_scr, acc_scr, *, n_blocks):
    e = pl.program_id(1)

    @pl.when(e == 0)
    def _():
        h_scr[...] = (y_ref[...] * (1.0 + sc_ref[...]) + sh_ref[...]).astype(BF16)
        acc_scr[...] = jnp.zeros_like(acc_scr)

    acc_scr[...] += _swiglu(h_scr[...], wg_ref, wu_ref, wd_ref)

    @pl.when(e == n_blocks - 1)
    def _():
        o_ref[...] = _layer_norm(ALPHA * y_ref[...] + g2_ref[...] * acc_scr[...],
                                 lg_ref[...], lb_ref[...])


def _ffn(y, mod, wg, wu, wd, ln_g, ln_b, cond_fn, tm=ROW_TILE, ff=MOE_FF):
    t = y.shape[0]
    n_blocks = wg.shape[1] // ff
    vec = pl.BlockSpec((1, D_MODEL), lambda i, e: (0, 0))
    kern = functools.partial(_ffn_kernel, n_blocks=n_blocks)
    return pl.pallas_call(
        kern,
        grid=(t // tm, n_blocks),
        in_specs=[pl.BlockSpec((tm, D_MODEL), lambda i, e: (i, 0)),
                  _mod_spec(3, cond_fn), _mod_spec(4, cond_fn), _mod_spec(5, cond_fn),
                  pl.BlockSpec((D_MODEL, ff), lambda i, e: (0, e)),
                  pl.BlockSpec((D_MODEL, ff), lambda i, e: (0, e)),
                  pl.BlockSpec((ff, D_MODEL), lambda i, e: (e, 0)),
                  vec, vec],
        out_specs=pl.BlockSpec((tm, D_MODEL), lambda i, e: (i, 0)),
        out_shape=jax.ShapeDtypeStruct((t, D_MODEL), F32),
        scratch_shapes=[pltpu.VMEM((tm, D_MODEL), BF16), pltpu.VMEM((tm, D_MODEL), F32)],
        compiler_params=_params("parallel", "arbitrary"),
        name="ffn_ln",
    )(y, mod, mod, mod, wg, wu, wd, ln_g, ln_b)


def _permute_w_in(w):
    offs = np.cumsum((0,) + PROJ_SIZES)
    qa, ka, va, qkv, gate, a, b, qc, kc, vc = (w[:, offs[i]:offs[i + 1]] for i in range(10))
    pad = jnp.zeros((D_MODEL, AB_PAD - 4 * H_DN), w.dtype)
    return jnp.concatenate([qa, ka, va, qkv, gate, qc, kc, vc, a, b, pad], axis=1).astype(BF16)


def _pad_row(v, n=AB_PAD):
    v = v.reshape(1, -1)
    return jnp.pad(v, ((0, 0), (0, n - v.shape[1])))


def _group_matrix(n):
    idx = np.arange(n) // HEAD_DIM
    return jnp.asarray((idx[:, None] == idx[None, :]).astype(np.float32), dtype=BF16)


def _expand_matrix():
    e = np.zeros((2, AB_PAD, 2 * W_DN), np.float32)
    for d in range(2):
        for h in range(H_DN):
            e[d, d * H_DN + h, h * HEAD_DIM:(h + 1) * HEAD_DIM] = 1.0
            e[d, 2 * H_DN + d * H_DN + h, W_DN + h * HEAD_DIM:W_DN + (h + 1) * HEAD_DIM] = 1.0
    return jnp.asarray(e, dtype=BF16)


def _chunk_cumsum_matrices(rows):
    r = np.arange(rows)
    same = (r[:, None] // CHUNK) == (r[None, :] // CHUNK)
    return jnp.asarray(np.stack([same & (r[:, None] >= r[None, :]),
                                 same & (r[:, None] <= r[None, :])]).astype(np.float32), dtype=BF16)


def kernel(x_prompt, x_sample, cache_na_k, cache_na_v, cache_df_k, cache_df_v, state_dn, c, c_ctx,
           ada_w, ada_b, w_in, conv_dn, a_log_dn, dt_bias_dn, onorm_dn, rpb_na, lambda_df, subln_df,
           w_out, ln1_g, ln1_b, ln2_g, ln2_b, ffn_w_gate, ffn_w_up, ffn_w_down, router_w, router_b,
           moe_w_gate, moe_w_up, moe_w_down):
    conds = jnp.concatenate([c_ctx[None, :], c, jnp.zeros((N_COND - 1 - DEC_BATCH, D_MODEL), F32)], axis=0)
    mods = _ada_table(conds, ada_w, ada_b)
    ctx_cond = lambda i: 0
    lat_cond = lambda i: 1 + (i * ROW_TILE) // DEC_SEQ
    lat_cond_moe = lambda i: 1 + (i * MOE_TM) // DEC_SEQ
    tri = jnp.asarray(np.tril(np.ones((MOE_TM, MOE_TM), np.float32), -1), dtype=BF16)

    grp512, grp256 = _group_matrix(2 * W_DN), _group_matrix(W_DN)
    e_mat = _expand_matrix()
    cum = _chunk_cumsum_matrices(SEQ)
    cos, sin, rot = _rope_tables()

    y_ctx = x_prompt.reshape(T_CTX, D_MODEL)
    y_lat = x_sample.reshape(T_LAT, D_MODEL)
    ctx_out = []
    for l in range(DEPTH):
        mod = mods[l]
        w_in_l = _permute_w_in(w_in[l])
        w_out_l = w_out[l].astype(BF16)
        conv_w = jnp.pad(conv_dn[l], ((0, 8 - CONV_K), (0, 0)))
        alog_row = _pad_row(a_log_dn[l])
        dtb_row = _pad_row(dt_bias_dn[l])
        onorm_row = jnp.tile(onorm_dn[l], H_DN).reshape(1, W_DN)
        subln_row = subln_df[l].reshape(1, HEAD_DIM)
        lg1, lb1 = ln1_g[l].reshape(1, D_MODEL), ln1_b[l].reshape(1, D_MODEL)
        lg2, lb2 = ln2_g[l].reshape(1, D_MODEL), ln2_b[l].reshape(1, D_MODEL)

        *na, dn, gate, q_df, k_df, v_df, ab = _inproj(y_ctx, mod, w_in_l, ctx_cond)
        df = (q_df, k_df, v_df)
        u, *gates_dn = _dn_prep(dn, ab, conv_w, alog_row, dtb_row, grp512, e_mat, cum, SEQ)
        s0 = jnp.zeros((BATCH, 2, H_DN, HEAD_DIM, HEAD_DIM), F32)
        o_f, o_b, s_fin = _deltanet(u, *gates_dn, s0, SEQ, SEQ, n_sub=4)
        oa, oc = _ctx_attn(na, df, lambda_df[l], subln_row, l)
        y1_ctx = _outproj(oa, o_f, o_b, gate, onorm_row, grp256, oc, y_ctx, mod, w_out_l, lg1, lb1, ctx_cond)
        ctx_out.append((na[1].reshape(BATCH, SEQ, H_NA, HEAD_DIM), na[2].reshape(BATCH, SEQ, H_NA, HEAD_DIM),
                        k_df.reshape(BATCH, SEQ, H_DF, HEAD_DIM), v_df.reshape(BATCH, SEQ, H_DF, HEAD_DIM),
                        s_fin))

        *na, dn, gate, q_df, k_df, v_df, ab = _inproj(y_lat, mod, w_in_l, lat_cond)
        df = (q_df, k_df, v_df)
        u, *gates_dn = _dn_prep(dn, ab, conv_w, alog_row, dtb_row, grp512, e_mat, cum, DEC_SEQ)
        o_f, o_b, _ = _deltanet(u, *gates_dn, state_dn[:, l], DEC_SEQ, DN_ROWS)
        oa = _lat_na(na, cache_na_k[:, l].reshape(DEC_BATCH, PAST_LEN, W_NA).astype(BF16),
                     cache_na_v[:, l].reshape(DEC_BATCH, PAST_LEN, W_NA).astype(BF16),
                     _na_bias_table(rpb_na[l]))
        q_r, kt_all, v_all = _rope(df, cache_df_k[:, l].reshape(DEC_BATCH, PAST_LEN, W_DF),
                                   cache_df_v[:, l].reshape(DEC_BATCH, PAST_LEN, W_DF), cos, sin, rot)
        oc = _lat_df(q_r, kt_all, v_all, lambda_df[l], subln_row, l)
        y1_lat = _outproj(oa, o_f, o_b, gate, onorm_row, grp256, oc, y_lat, mod, w_out_l, lg1, lb1, lat_cond)

        i = l // 2
        if l % 2 == 0:
            wg, wu, wd = ffn_w_gate[i].astype(BF16), ffn_w_up[i].astype(BF16), ffn_w_down[i].astype(BF16)
            y_ctx = _ffn(y1_ctx, mod, wg, wu, wd, lg2, lb2, ctx_cond)
            y_lat = _ffn(y1_lat, mod, wg, wu, wd, lg2, lb2, lat_cond)
        else:
            wg, wu, wd = moe_w_gate[i].astype(BF16), moe_w_up[i].astype(BF16), moe_w_down[i].astype(BF16)
            rw = jnp.pad(router_w[i], ((0, 0), (0, LANES - N_EXPERTS)))
            rb = _pad_row(router_b[i], LANES)
            ys = []
            for y1, cond in ((y1_ctx, ctx_cond), (y1_lat, lat_cond_moe)):
                gates, rank, rank_t, cnt = _router(y1, mod, rw, rb, tri, cond)
                counts = cnt[:, 0, :N_EXPERTS].astype(jnp.int32).reshape(-1)
                ys.append(_moe(y1, mod, gates, rank, rank_t, counts, wg, wu, wd, lg2, lb2, cond))
            y_ctx, y_lat = ys

    stack = lambda j: jnp.stack([t[j] for t in ctx_out], axis=1)
    return (y_ctx.reshape(BATCH, SEQ, D_MODEL), y_lat.reshape(DEC_BATCH, DEC_SEQ, D_MODEL),
            stack(0), stack(1), stack(2), stack(3), stack(4))
```

```python
import functools
import math

import jax
import jax.numpy as jnp
import numpy as np
from jax import lax
from jax.experimental import pallas as pl
from jax.experimental.pallas import tpu as pltpu

F32 = jnp.float32
BF16 = jnp.bfloat16
HIGHEST = lax.Precision.HIGHEST

D_MODEL = 1024
BATCH = 32
SEQ = 256
DEPTH = 2
DEC_BATCH = 2
DEC_SEQ = 4096
PAST_LEN = 512
GRID_W = 64
GRID_H = DEC_SEQ // GRID_W
HEAD_DIM = 64
H_NA = 6
H_DN = 4
H_DF = 6
W_NA = H_NA * HEAD_DIM
W_DN = H_DN * HEAD_DIM
W_DF = H_DF * HEAD_DIM
DF_QK = HEAD_DIM // 2
WIN_R = 8
WIN_C = 16
CONV_K = 5
CHUNK = 64
D_FF = 2816
N_EXPERTS = 8
MOE_FF = 1408
ALPHA = (2 * DEPTH) ** 0.25
LN_EPS = 1e-5
RMS_EPS = 1e-6
ROPE_BASE = 10000.0
NEG_INF = -1e30
PROJ_SIZES = (W_NA, W_NA, W_NA, 3 * W_DN, W_DN, 2 * H_DN, 2 * H_DN, W_DF, W_DF, W_DF)

T_CTX = BATCH * SEQ
T_LAT = DEC_BATCH * DEC_SEQ
N_COND = 8
N_MOD = 6
ROW_TILE = 512
DN_ROWS = 512
LANES = 128
AB_PAD = LANES
SEG_WIDTHS = (W_NA, W_NA, W_NA, 3 * W_DN, W_DN, W_DF, W_DF, W_DF, AB_PAD)
SEG_OFFS = tuple(int(o) for o in np.cumsum((0,) + SEG_WIDTHS))
SEGS = tuple(zip(SEG_OFFS[:-1], SEG_OFFS[1:]))
SEG_GROUPS = ((0, 1, 2), (3,), (4,), (5, 6, 7), (8,))
P_PAD = SEG_OFFS[-1]
BD = H_DN * CHUNK
V_AUG = LANES
DF_Q_SCALE = DF_QK ** -0.5 * math.log2(math.e)
VMEM_LIMIT = 56 * 1024 * 1024


def _params(*sem):
    return pltpu.CompilerParams(dimension_semantics=sem, vmem_limit_bytes=VMEM_LIMIT)


def _dot(a, b):
    return jnp.dot(a.astype(BF16), b.astype(BF16), preferred_element_type=F32)


def _dot_nt(a, b):
    return lax.dot_general(a.astype(BF16), b.astype(BF16), (((1,), (1,)), ((), ())),
                           preferred_element_type=F32)


def _split3(x):
    x1 = x.astype(BF16)
    r1 = x - x1.astype(F32)
    x2 = r1.astype(BF16)
    return x1, x2, (r1 - x2.astype(F32)).astype(BF16)


def _dot_sel(x, sel):
    m = x.shape[0]
    y = jnp.dot(jnp.concatenate(_split3(x), axis=0), sel, preferred_element_type=F32)
    return y[:m] + y[m:2 * m] + y[2 * m:]


def _sel_dot(sel, x):
    n = x.shape[1]
    y = jnp.dot(sel, jnp.concatenate(_split3(x), axis=1), preferred_element_type=F32)
    return y[:, :n] + y[:, n:2 * n] + y[:, 2 * n:]


def _dot_x3(a, b):
    m = a.shape[0]
    a_hi = a.astype(BF16)
    a_lo = (a - a_hi.astype(F32)).astype(BF16)
    b_hi = b.astype(BF16)
    b_lo = (b - b_hi.astype(F32)).astype(BF16)
    y = jnp.dot(jnp.concatenate([a_hi, a_lo], axis=0), b_hi, preferred_element_type=F32)
    return y[:m] + y[m:] + jnp.dot(a_hi, b_lo, preferred_element_type=F32)


def _silu(x):
    return x * jax.nn.sigmoid(x)


def _layer_norm(x, g, b):
    mu = jnp.mean(x, axis=-1, keepdims=True)
    xc = x - mu
    var = jnp.mean(xc * xc, axis=-1, keepdims=True)
    return xc * lax.rsqrt(var + LN_EPS) * g + b


def _ada_kernel(c_ref, w_ref, b_ref, o_ref):
    o_ref[...] = _dot_x3(_silu(c_ref[...]), w_ref[...]) + b_ref[...]


def _ada_table(conds, ada_w, ada_b):
    out = pl.pallas_call(
        _ada_kernel,
        grid=(DEPTH, N_MOD),
        in_specs=[pl.BlockSpec((N_COND, D_MODEL), lambda l, k: (0, 0)),
                  pl.BlockSpec((None, D_MODEL, D_MODEL), lambda l, k: (l, 0, k)),
                  pl.BlockSpec((None, None, 1, D_MODEL), lambda l, k: (l, k, 0, 0))],
        out_specs=pl.BlockSpec((None, None, N_COND, D_MODEL), lambda l, k: (l, k, 0, 0)),
        out_shape=jax.ShapeDtypeStruct((DEPTH, N_MOD, N_COND, D_MODEL), F32),
        compiler_params=_params("parallel", "parallel"),
        name="ada_table",
    )(conds, ada_w, ada_b.reshape(DEPTH, N_MOD, 1, D_MODEL))
    return out.reshape(DEPTH, N_MOD, N_COND, 1, D_MODEL)


def _mod_spec(k, cond_fn):
    return pl.BlockSpec((None, None, 1, D_MODEL), lambda i, *_: (k, cond_fn(i), 0, 0))


def _inproj_kernel(x_ref, sh_ref, sc_ref, w_ref, *o_refs):
    h = (x_ref[...] * (1.0 + sc_ref[...]) + sh_ref[...]).astype(BF16)
    for group in SEG_GROUPS:
        lo = SEG_OFFS[group[0]]
        res = jnp.dot(h, w_ref[:, lo:SEG_OFFS[group[-1] + 1]], preferred_element_type=F32)
        for n in group:
            o_refs[n][...] = res[:, SEG_OFFS[n] - lo:SEG_OFFS[n + 1] - lo]


def _inproj(x, mod, w, cond_fn, tm=ROW_TILE):
    t = x.shape[0]
    return pl.pallas_call(
        _inproj_kernel,
        grid=(t // tm,),
        in_specs=[pl.BlockSpec((tm, D_MODEL), lambda i: (i, 0)),
                  _mod_spec(0, cond_fn), _mod_spec(1, cond_fn),
                  pl.BlockSpec((D_MODEL, P_PAD), lambda i: (0, 0))],
        out_specs=[pl.BlockSpec((tm, b - a), lambda i: (i, 0)) for a, b in SEGS],
        out_shape=[jax.ShapeDtypeStruct((t, b - a), F32) for a, b in SEGS],
        compiler_params=_params("parallel"),
        name="inproj",
    )(x, mod, mod, w)


def _dn_prep_kernel(x_ref, prev_ref, next_ref, ab_ref, w_ref, alog_ref, dtb_ref, grp_ref, e_ref, cum_ref,
                    u_ref, gcf_ref, gcb_ref, bf_ref, bb_ref, *, blocks_per_seq, rows):
    i = pl.program_id(0)
    j = i % blocks_per_seq
    prev = jnp.where(j != 0, prev_ref[...], 0.0)
    nxt = jnp.where(j != blocks_per_seq - 1, next_ref[...], 0.0)
    xe = jnp.concatenate([prev, x_ref[...], nxt], axis=0)
    w = w_ref[...]
    base = 8 - CONV_K // 2
    acc = w[0:1, :] * xe[base:base + rows, :]
    for t in range(1, CONV_K):
        acc = acc + w[t:t + 1, :] * xe[base + t:base + t + rows, :]
    u = _silu(acc)
    qk = u[:, :2 * W_DN]
    ss = _dot_sel(qk * qk, grp_ref[...])
    qk = qk * lax.rsqrt(ss + RMS_EPS)
    u_ref[:, :W_DN] = qk[:, :W_DN] * HEAD_DIM ** -0.5
    u_ref[:, W_DN:2 * W_DN] = qk[:, W_DN:]
    u_ref[:, 2 * W_DN:] = u[:, 2 * W_DN:]
    ab = ab_ref[...]
    z = ab + dtb_ref[...]
    softplus = jnp.maximum(z, 0.0) + jnp.log(1.0 + jnp.exp(-jnp.abs(z)))
    g = -jnp.exp(alog_ref[...]) * softplus
    lane = lax.broadcasted_iota(jnp.int32, ab.shape, 1)
    gb = jnp.where(lane < 2 * H_DN, g, jax.nn.sigmoid(ab))
    for d, (gc_ref, beta_ref) in enumerate(((gcf_ref, bf_ref), (gcb_ref, bb_ref))):
        gbx = _dot_sel(gb, e_ref[d])
        gc_ref[...] = _sel_dot(cum_ref[d], gbx[:, :W_DN])
        beta_ref[...] = gbx[:, W_DN:]


def _dn_prep(qkv, ab, conv_w, alog_row, dtb_row, grp, e_mat, cum, seq_len, rows=SEQ):
    t = qkv.shape[0]
    bps = seq_len // rows
    r8 = rows // 8
    last8 = t // 8 - 1
    kern = functools.partial(_dn_prep_kernel, blocks_per_seq=bps, rows=rows)
    return pl.pallas_call(
        kern,
        grid=(t // rows,),
        in_specs=[pl.BlockSpec((rows, 3 * W_DN), lambda i: (i, 0)),
                  pl.BlockSpec((8, 3 * W_DN), lambda i: (jnp.maximum(i * r8 - 1, 0), 0)),
                  pl.BlockSpec((8, 3 * W_DN), lambda i: (jnp.minimum((i + 1) * r8, last8), 0)),
                  pl.BlockSpec((rows, AB_PAD), lambda i: (i, 0)),
                  pl.BlockSpec((8, 3 * W_DN), lambda i: (0, 0)),
                  pl.BlockSpec((1, AB_PAD), lambda i: (0, 0)),
                  pl.BlockSpec((1, AB_PAD), lambda i: (0, 0)),
                  pl.BlockSpec((2 * W_DN, 2 * W_DN), lambda i: (0, 0)),
                  pl.BlockSpec((2, AB_PAD, 2 * W_DN), lambda i: (0, 0, 0)),
                  pl.BlockSpec((2, rows, rows), lambda i: (0, 0, 0))],
        out_specs=[pl.BlockSpec((rows, 3 * W_DN), lambda i: (i, 0))]
        + [pl.BlockSpec((rows, W_DN), lambda i: (i, 0))] * 4,
        out_shape=[jax.ShapeDtypeStruct((t, 3 * W_DN), F32)]
        + [jax.ShapeDtypeStruct((t, W_DN), F32)] * 4,
        compiler_params=_params("parallel"),
        name="dn_prep",
    )(qkv, qkv, qkv, ab, conv_w, alog_row, dtb_row, grp, e_mat, cum)


def _dn_kernel(uf_ref, ub_ref, gcf_ref, gcb_ref, bf_ref, bb_ref, s0_ref, of_ref, ob_ref, sf_ref, s_scr,
               *, n_chunks, n_sub):
    j = pl.program_id(1)
    r = lax.broadcasted_iota(jnp.int32, (BD, BD), 0)
    c = lax.broadcasted_iota(jnp.int32, (BD, BD), 1)
    same = (r // CHUNK) == (c // CHUNK)
    dt = (r % CHUNK) - (c % CHUNK)
    same_f = jnp.where(same, 1.0, 0.0)
    same_b = same_f.astype(BF16)
    eye_f = jnp.where(r == c, 1.0, 0.0)
    dirs = []
    for sign, last in ((1, CHUNK - 1), (-1, 0)):
        dirs.append((jnp.where(jnp.logical_and(same, dt * sign >= 0), 1.0, 0.0),
                     jnp.where(jnp.logical_and(same, dt * sign > 0), 1.0, 0.0), last))

    def lift(x):
        return jnp.concatenate([x, x, x, x], axis=0) * same_f

    def lift_b(x):
        xb = x.astype(BF16)
        return jnp.concatenate([xb, xb, xb, xb], axis=0) * same_b

    @pl.when(j == 0)
    def _():
        for si in range(n_sub):
            for di in range(2):
                rows = [jnp.concatenate([s0_ref[si, di, h]] * H_DN, axis=1) for h in range(H_DN)]
                s_scr[si, di] = jnp.concatenate(rows, axis=0) * same_f

    def each(f, *xs):
        return [f(*a) for a in zip(*xs)]

    def mm(a, b):
        return jnp.dot(a, b, preferred_element_type=F32)

    def chunks(us, gc, beta, ss, incls, stricts, lasts):
        q = [u[:, :W_DN] for u in us]
        k = [u[:, W_DN:2 * W_DN] for u in us]
        v = [u[:, 2 * W_DN:] for u in us]
        gl = each(lambda x, last: x[last:last + 1, :], gc, lasts)
        eg = [jnp.exp(x) for x in gc]
        kb = each(jnp.multiply, k, beta)
        gcol = [lift(x) for x in gc]
        decay = each(lambda x, m: jnp.exp(jnp.where(m > 0.5, x - x.T, NEG_INF)), gcol, incls)
        qk = each(lambda a, b, kk: lax.dot_general(jnp.concatenate([lift_b(a), lift_b(b)], axis=0),
                                                   lift_b(kk), (((1,), (1,)), ((), ())),
                                                   preferred_element_type=F32), q, kb, k)
        attn = each(lambda x, d: (x[:BD] * d).astype(BF16), qk, decay)
        n_mat = each(lambda x, d, m: -(x[BD:] * d) * m, qk, decay, stricts)
        t_inv = [eye_f + x for x in n_mat]
        n_hi = [x.astype(BF16) for x in n_mat]
        m_b = n_hi
        for _ in range(int(math.log2(CHUNK)) - 1):
            m_b = [mm(x, x).astype(BF16) for x in m_b]
            t_inv = each(lambda t, m: t + mm(t.astype(BF16), m), t_inv, m_b)
        n_lo = each(lambda x, h: (x - h.astype(F32)).astype(BF16), n_mat, n_hi)
        x_hi = [t.astype(BF16) for t in t_inv]
        x_lo = each(lambda t, h: (t - h.astype(F32)).astype(BF16), t_inv, x_hi)
        nx = each(lambda h, lo, x: mm(jnp.concatenate([h, lo], axis=0), x), n_hi, n_lo, x_hi)
        nxl = each(mm, n_hi, x_lo)
        resid = each(lambda t, a, b: (eye_f - t + a[:BD] + a[BD:] + b).astype(BF16), t_inv, nx, nxl)
        t_b = each(lambda t, h, rr: (t + mm(h, rr)).astype(BF16), t_inv, x_hi, resid)
        rhs = each(lambda vv, b, kk, e: jnp.concatenate([lift_b(vv * b), lift_b(kk * e)], axis=1),
                   v, beta, kb, eg)
        sol = each(mm, t_b, rhs)
        s_b = [s.astype(BF16) for s in ss]
        ps = each(lambda so, qq, e, sb: mm(jnp.concatenate([so[:, BD:].astype(BF16), lift_b(qq * e)],
                                                           axis=0), sb), sol, q, eg, s_b)
        v_new = each(lambda so, p: (so[:, :BD] - p[:BD]).astype(BF16), sol, ps)
        o_bd = each(lambda p, a, vn: p[BD:] + mm(a, vn), ps, attn, v_new)
        o_tm = [x[0:CHUNK] + x[CHUNK:2 * CHUNK] + x[2 * CHUNK:3 * CHUNK] + x[3 * CHUNK:] for x in o_bd]
        k_tail = each(lambda kk, a, b: lift(kk * jnp.exp(a - b)).T.astype(BF16), k, gl, gc)
        s_new = each(lambda s, a, kt, vn: s * jnp.exp(a) + mm(kt, vn), ss, gl, k_tail, v_new)
        return o_tm, s_new

    chains = [(si, di) for si in range(n_sub) for di in range(2)]

    def body(ci, carry):
        offs = (pl.multiple_of(ci * CHUNK, CHUNK), pl.multiple_of((n_chunks - 1 - ci) * CHUNK, CHUNK))
        u_refs, gc_refs, b_refs, o_refs = (uf_ref, ub_ref), (gcf_ref, gcb_ref), (bf_ref, bb_ref), (of_ref, ob_ref)
        o_tm, s_new = chunks(
            [u_refs[di][si, pl.ds(offs[di], CHUNK), :] for si, di in chains],
            [gc_refs[di][si, pl.ds(offs[di], CHUNK), :] for si, di in chains],
            [b_refs[di][si, pl.ds(offs[di], CHUNK), :] for si, di in chains],
            [s_scr[si, di] for si, di in chains],
            *[[dirs[di][n] for si, di in chains] for n in range(3)])
        for (si, di), o, s in zip(chains, o_tm, s_new):
            o_refs[di][si, pl.ds(offs[di], CHUNK), :] = o
            s_scr[si, di] = s
        return carry

    lax.fori_loop(0, n_chunks, body, 0)

    @pl.when(j == pl.num_programs(1) - 1)
    def _():
        for si in range(n_sub):
            for di in range(2):
                for h in range(H_DN):
                    sl = slice(h * HEAD_DIM, (h + 1) * HEAD_DIM)
                    sf_ref[si, di, h] = s_scr[si, di, sl, sl]


def _deltanet(u, gc_f, gc_b, beta_f, beta_b, s0_bd, seq_len, rows, n_sub=2):
    t = u.shape[0]
    n_seq = t // seq_len
    nb = seq_len // rows
    kern = functools.partial(_dn_kernel, n_chunks=rows // CHUNK, n_sub=n_sub)
    fwd = lambda s, j: (s, j, 0)
    bwd = lambda s, j: (s, nb - 1 - j, 0)
    state = pl.BlockSpec((n_sub, 2, H_DN, HEAD_DIM, HEAD_DIM), lambda s, j: (s, 0, 0, 0, 0))
    o_f, o_b, s_fin = pl.pallas_call(
        kern,
        grid=(n_seq // n_sub, nb),
        in_specs=[pl.BlockSpec((n_sub, rows, 3 * W_DN), fwd), pl.BlockSpec((n_sub, rows, 3 * W_DN), bwd),
                  pl.BlockSpec((n_sub, rows, W_DN), fwd), pl.BlockSpec((n_sub, rows, W_DN), bwd),
                  pl.BlockSpec((n_sub, rows, W_DN), fwd), pl.BlockSpec((n_sub, rows, W_DN), bwd), state],
        out_specs=[pl.BlockSpec((n_sub, rows, W_DN), fwd), pl.BlockSpec((n_sub, rows, W_DN), bwd), state],
        out_shape=[jax.ShapeDtypeStruct((n_seq, seq_len, W_DN), F32),
                   jax.ShapeDtypeStruct((n_seq, seq_len, W_DN), F32),
                   jax.ShapeDtypeStruct((n_seq, 2, H_DN, HEAD_DIM, HEAD_DIM), F32)],
        scratch_shapes=[pltpu.VMEM((n_sub, 2, BD, BD), F32)],
        compiler_params=_params("parallel", "arbitrary"),
        name="deltanet",
    )(*[x.reshape(n_seq, seq_len, -1) for x in (u, u, gc_f, gc_b, beta_f, beta_b)], s0_bd)
    return o_f.reshape(t, W_DN), o_b.reshape(t, W_DN), s_fin


def _diff_lambda(lam_ref, lam_init):
    lp = lam_ref[...]
    return (jnp.exp(jnp.sum(lp[0:1] * lp[1:2], axis=1, keepdims=True))
            - jnp.exp(jnp.sum(lp[2:3] * lp[3:4], axis=1, keepdims=True)) + lam_init)


def _sub_rms(o, subln, lam_init):
    ms = jnp.mean(o * o, axis=-1, keepdims=True)
    return o * lax.rsqrt(ms + RMS_EPS) * subln * (1.0 - lam_init)


def _ctx_attn_kernel(qa_ref, ka_ref, va_ref, qc_ref, kc_ref, vc_ref, lam_ref, subln_ref, oa_ref, oc_ref,
                     *, lam_init):
    lam = _diff_lambda(lam_ref, lam_init)
    subln = subln_ref[...]
    n = qa_ref.shape[0]
    qa = (qa_ref[...] * NA_Q_SCALE).astype(BF16)
    ka = ka_ref[...].astype(BF16)
    va = va_ref[...].astype(BF16)
    qc = (qc_ref[...] * DF_Q_SCALE).astype(BF16)
    kc = kc_ref[...].astype(BF16)
    vc = vc_ref[...].astype(BF16)
    heads = [slice(h * HEAD_DIM, (h + 1) * HEAD_DIM) for h in range(H_NA)]
    maps = [slice(m * DF_QK, (m + 1) * DF_QK) for m in range(2 * H_DF)]
    s_na = [_dot_nt(qa[:, sl], ka[:, sl]) for sl in heads]
    s_df = [_dot_nt(qc[:, sl], kc[:, sl]) for sl in maps]
    p_na = [jnp.exp2(s - jnp.max(s, axis=-1, keepdims=True)) for s in s_na]
    p_df = [jnp.exp2(s - jnp.max(s, axis=-1, keepdims=True)) for s in s_df]
    d_na = [jnp.sum(p, axis=-1, keepdims=True) for p in p_na]
    d_df = [jnp.sum(p, axis=-1, keepdims=True) for p in p_df]
    o_na = [_dot(p, va[:, sl]) for p, sl in zip(p_na, heads)]
    o_df = [_dot(jnp.concatenate([p_df[2 * h].astype(BF16), p_df[2 * h + 1].astype(BF16)], axis=0),
                 vc[:, heads[h]]) for h in range(H_DF)]
    for h in range(H_NA):
        oa_ref[:, heads[h]] = (o_na[h] / d_na[h]).astype(BF16)
    for h in range(H_DF):
        o = o_df[h][:n] / d_df[2 * h] - lam * (o_df[h][n:] / d_df[2 * h + 1])
        oc_ref[:, heads[h]] = _sub_rms(o, subln, lam_init).astype(BF16)


def _ctx_attn(qkv_na, qkv_df, lam_p, subln_row, layer):
    lam_init = 0.8 - 0.6 * math.exp(-0.3 * layer)
    kern = functools.partial(_ctx_attn_kernel, lam_init=lam_init)
    return pl.pallas_call(
        kern,
        grid=(BATCH,),
        in_specs=[pl.BlockSpec((SEQ, W_NA), lambda b: (b, 0))] * 3
        + [pl.BlockSpec((SEQ, W_DF), lambda b: (b, 0))] * 3
        + [pl.BlockSpec((4, DF_QK), lambda b: (0, 0)),
                  pl.BlockSpec((1, HEAD_DIM), lambda b: (0, 0))],
        out_specs=[pl.BlockSpec((SEQ, W_NA), lambda b: (b, 0)),
                   pl.BlockSpec((SEQ, W_DF), lambda b: (b, 0))],
        out_shape=[jax.ShapeDtypeStruct((T_CTX, W_NA), BF16),
                   jax.ShapeDtypeStruct((T_CTX, W_DF), BF16)],
        compiler_params=_params("parallel"),
        name="ctx_attn",
    )(*qkv_na, *qkv_df, lam_p, subln_row)


NA_R = 4
NA_U = NA_R + WIN_R - 1
NA_GROUPS = GRID_H // NA_R
NA_Q_SCALE = HEAD_DIM ** -0.5 * math.log2(math.e)


def _na_key_start(g):
    return jnp.clip(g * NA_R - WIN_R // 2, 0, GRID_H - NA_U)


def _na_bias_table(rpb):
    qc = np.arange(GRID_W)[:, None]
    kc = np.arange(GRID_W)[None, :]
    cs = np.clip(qc - WIN_C // 2, 0, GRID_W - WIN_C)
    valid = (kc >= cs) & (kc < cs + WIN_C)
    dc = np.clip(kc - qc + (WIN_C - 1), 0, 2 * WIN_C - 2)
    onehot = (dc[None] == np.arange(2 * WIN_C - 1)[:, None, None]).astype(np.float32)
    x = jnp.einsum('hrd,dqk->hrqk', rpb.astype(F32), jnp.asarray(onehot), precision=HIGHEST)
    return jnp.where(valid[None, None], x * math.log2(math.e), NEG_INF)


def _na_fill_bias(tab_ref, bias_ref, g):
    u0 = int(np.clip(g * NA_R - WIN_R // 2, 0, GRID_H - NA_U))
    masked = jnp.full((H_NA, GRID_W, GRID_W), NEG_INF, F32)
    for i in range(NA_R):
        r = g * NA_R + i
        w0 = int(np.clip(r - WIN_R // 2, 0, GRID_H - WIN_R))
        for u in range(NA_U):
            inside = w0 <= u0 + u < w0 + WIN_R
            block = tab_ref[:, u0 + u - r + WIN_R - 1] if inside else masked
            bias_ref[:, i * GRID_W:(i + 1) * GRID_W, u * GRID_W:(u + 1) * GRID_W] = block


def _lat_na_kernel(q_ref, k_ref, v_ref, kc_ref, vc_ref, tab_ref, o_ref, bias_ref):
    g = pl.program_id(1)
    for g_build in (0, 1, NA_GROUPS - 1):
        @pl.when(g == g_build)
        def _():
            _na_fill_bias(tab_ref, bias_ref, g_build)

    start = pl.multiple_of(_na_key_start(g) * GRID_W, GRID_W)
    q = (q_ref[...] * NA_Q_SCALE).astype(BF16)
    kw = k_ref[pl.ds(start, NA_U * GRID_W), :].astype(BF16)
    vw = v_ref[pl.ds(start, NA_U * GRID_W), :].astype(BF16)
    kc = kc_ref[...]
    vc = vc_ref[...]
    heads = [slice(h * HEAD_DIM, (h + 1) * HEAD_DIM) for h in range(H_NA)]
    s_loc = [_dot_nt(q[:, sl], kw[:, sl]) + bias_ref[h] for h, sl in enumerate(heads)]
    s_ctx = [_dot_nt(q[:, sl], kc[:, sl]) for sl in heads]
    m = [jnp.maximum(jnp.max(a, axis=-1, keepdims=True), jnp.max(b, axis=-1, keepdims=True))
         for a, b in zip(s_loc, s_ctx)]
    p_loc = [jnp.exp2(a - mm) for a, mm in zip(s_loc, m)]
    p_ctx = [jnp.exp2(b - mm) for b, mm in zip(s_ctx, m)]
    den = [jnp.sum(a, axis=-1, keepdims=True) + jnp.sum(b, axis=-1, keepdims=True)
           for a, b in zip(p_loc, p_ctx)]
    o = [_dot(a, vw[:, sl]) + _dot(b, vc[:, sl]) for a, b, sl in zip(p_loc, p_ctx, heads)]
    for h, sl in enumerate(heads):
        o_ref[:, sl] = (o[h] / den[h]).astype(BF16)


def _lat_na(qkv_na, ck, cv, bias_tab):
    rows = NA_R * GRID_W
    return pl.pallas_call(
        _lat_na_kernel,
        grid=(DEC_BATCH, NA_GROUPS),
        in_specs=[pl.BlockSpec((rows, W_NA), lambda b, g: (b * NA_GROUPS + g, 0)),
                  pl.BlockSpec((DEC_SEQ, W_NA), lambda b, g: (b, 0)),
                  pl.BlockSpec((DEC_SEQ, W_NA), lambda b, g: (b, 0)),
                  pl.BlockSpec((None, PAST_LEN, W_NA), lambda b, g: (b, 0, 0)),
                  pl.BlockSpec((None, PAST_LEN, W_NA), lambda b, g: (b, 0, 0)),
                  pl.BlockSpec((H_NA, 2 * WIN_R - 1, GRID_W, GRID_W), lambda b, g: (0, 0, 0, 0))],
        out_specs=pl.BlockSpec((rows, W_NA), lambda b, g: (b * NA_GROUPS + g, 0)),
        out_shape=jax.ShapeDtypeStruct((T_LAT, W_NA), BF16),
        scratch_shapes=[pltpu.VMEM((H_NA, rows, NA_U * GRID_W), F32)],
        compiler_params=_params("arbitrary", "arbitrary"),
        name="lat_na",
    )(*qkv_na, ck, cv, bias_tab)


def _rope_tables():
    nf = DF_QK // 4
    inv = ROPE_BASE ** (-np.arange(nf, dtype=np.float32) / nf)
    t = np.arange(DEC_SEQ)
    pos = np.stack([t // GRID_W, t % GRID_W], axis=-1).astype(np.float32)
    ang = jnp.asarray(pos[:, :, None] * inv)
    cos, sin = jnp.cos(ang), jnp.sin(ang)
    cos32 = jnp.concatenate([cos, cos], axis=-1).reshape(DEC_SEQ, DF_QK)
    sin32 = jnp.concatenate([-sin, sin], axis=-1).reshape(DEC_SEQ, DF_QK)
    reps = W_DF // DF_QK
    rot = np.zeros((W_DF, W_DF), np.float32)
    for dd in range(W_DF):
        rot[dd + nf if dd % (2 * nf) < nf else dd - nf, dd] = 1.0
    return jnp.tile(cos32, (1, reps)), jnp.tile(sin32, (1, reps)), jnp.asarray(rot, dtype=BF16)


def _rope_kernel(qin_ref, kin_ref, vin_ref, kc_ref, vc_ref, cos_ref, sin_ref, rot_ref, q_ref, kt_ref, v_ref,
                 *, nb):
    i = pl.program_id(1)
    lane = lax.broadcasted_iota(jnp.int32, (q_ref.shape[0], V_AUG - HEAD_DIM), 1)
    one_col = jnp.where(lane == 0, 1.0, 0.0).astype(BF16)

    def put_values(src_ref):
        for h in range(H_DF):
            a = h * HEAD_DIM
            v_ref[:, h * V_AUG:h * V_AUG + HEAD_DIM] = src_ref[:, a:a + HEAD_DIM].astype(BF16)
            v_ref[:, h * V_AUG + HEAD_DIM:(h + 1) * V_AUG] = one_col

    @pl.when(i < nb)
    def _():
        cos, sin, rot = cos_ref[...], sin_ref[...], rot_ref[...]
        q = qin_ref[...]
        k = kin_ref[...]
        q_ref[...] = ((q * cos + _dot_sel(q, rot) * sin) * DF_Q_SCALE).astype(BF16)
        kt_ref[...] = (k * cos + _dot_sel(k, rot) * sin).T.astype(BF16)
        put_values(vin_ref)

    @pl.when(i == nb)
    def _():
        kt_ref[...] = kc_ref[...].T.astype(BF16)
        put_values(vc_ref)


def _rope(qkv_df, kc, vc, cos, sin, rot, tm=PAST_LEN):
    nb = DEC_SEQ // tm
    lat = lambda b, i: (b * nb + jnp.minimum(i, nb - 1), 0)
    pos = lambda b, i: (jnp.minimum(i, nb - 1), 0)
    ctx = pl.BlockSpec((None, PAST_LEN, W_DF), lambda b, i: (b, 0, 0))
    return pl.pallas_call(
        functools.partial(_rope_kernel, nb=nb),
        grid=(DEC_BATCH, nb + 1),
        in_specs=[pl.BlockSpec((tm, W_DF), lat)] * 3 + [ctx, ctx]
        + [pl.BlockSpec((tm, W_DF), pos), pl.BlockSpec((tm, W_DF), pos),
           pl.BlockSpec((W_DF, W_DF), lambda b, i: (0, 0))],
        out_specs=[pl.BlockSpec((tm, W_DF), lat),
                   pl.BlockSpec((None, W_DF, tm), lambda b, i: (b, 0, i)),
                   pl.BlockSpec((None, tm, H_DF * V_AUG), lambda b, i: (b, i, 0))],
        out_shape=[jax.ShapeDtypeStruct((T_LAT, W_DF), BF16),
                   jax.ShapeDtypeStruct((DEC_BATCH, W_DF, DEC_SEQ + PAST_LEN), BF16),
                   jax.ShapeDtypeStruct((DEC_BATCH, DEC_SEQ + PAST_LEN, H_DF * V_AUG), BF16)],
        compiler_params=_params("parallel", "arbitrary"),
        name="rope",
    )(*qkv_df, kc, vc, cos, sin, rot)


def _lat_df_kernel(q_ref, kt_ref, v_ref, lam_ref, subln_ref, o_ref, *, lam_init, tq):
    q = q_ref[...]
    lam = _diff_lambda(lam_ref, lam_init)
    subln = subln_ref[...]

    def scores(n):
        a = n * DF_QK
        return jnp.dot(q[:, a:a + DF_QK], kt_ref[a:a + DF_QK, :], preferred_element_type=F32)

    n_maps = 2 * H_DF
    es = []
    s_next = scores(0)
    for n in range(n_maps):
        s = s_next
        if n + 1 < n_maps:
            s_next = scores(n + 1)
        es.append(jnp.exp2(s - jnp.max(s, axis=-1, keepdims=True)).astype(BF16))
        if n % 2 == 1:
            h = n // 2
            ov = jnp.dot(jnp.concatenate(es, axis=0), v_ref[:, h * V_AUG:(h + 1) * V_AUG],
                         preferred_element_type=F32)
            o = (ov[:tq, :HEAD_DIM] / ov[:tq, HEAD_DIM:HEAD_DIM + 1]
                 - lam * (ov[tq:, :HEAD_DIM] / ov[tq:, HEAD_DIM:HEAD_DIM + 1]))
            o_ref[:, h * HEAD_DIM:(h + 1) * HEAD_DIM] = _sub_rms(o, subln, lam_init).astype(BF16)
            es = []


def _lat_df(q_r, kt_all, v_all, lam_p, subln_row, layer, tq=2 * LANES):
    lam_init = 0.8 - 0.6 * math.exp(-0.3 * layer)
    nb = DEC_SEQ // tq
    n_keys = kt_all.shape[2]
    kern = functools.partial(_lat_df_kernel, lam_init=lam_init, tq=tq)
    return pl.pallas_call(
        kern,
        grid=(DEC_BATCH, nb),
        in_specs=[pl.BlockSpec((tq, W_DF), lambda b, i: (b * nb + i, 0)),
                  pl.BlockSpec((None, W_DF, n_keys), lambda b, i: (b, 0, 0)),
                  pl.BlockSpec((None, n_keys, H_DF * V_AUG), lambda b, i: (b, 0, 0)),
                  pl.BlockSpec((4, DF_QK), lambda b, i: (0, 0)),
                  pl.BlockSpec((1, HEAD_DIM), lambda b, i: (0, 0))],
        out_specs=pl.BlockSpec((tq, W_DF), lambda b, i: (b * nb + i, 0)),
        out_shape=jax.ShapeDtypeStruct((T_LAT, W_DF), BF16),
        compiler_params=_params("parallel", "arbitrary"),
        name="lat_df",
    )(q_r, kt_all, v_all, lam_p, subln_row)


def _outproj_kernel(oa_ref, of_ref, obk_ref, gate_ref, onorm_ref, grp_ref, oc_ref, y_ref, g1_ref, w_ref,
                    lg_ref, lb_ref, o_ref):
    ob = of_ref[...] + obk_ref[...]
    ms = _dot_sel(ob * ob, grp_ref[...]) * (1.0 / HEAD_DIM)
    ob = ob * lax.rsqrt(ms + RMS_EPS) * onorm_ref[...] * _silu(gate_ref[...])
    o = (jnp.dot(oa_ref[...].astype(BF16), w_ref[:W_NA, :], preferred_element_type=F32)
         + jnp.dot(ob.astype(BF16), w_ref[W_NA:W_NA + W_DN, :], preferred_element_type=F32)
         + jnp.dot(oc_ref[...].astype(BF16), w_ref[W_NA + W_DN:, :], preferred_element_type=F32))
    o_ref[...] = _layer_norm(ALPHA * y_ref[...] + g1_ref[...] * o, lg_ref[...], lb_ref[...])


def _outproj(oa, o_fwd, o_bwd, gate, onorm_row, grp, oc, y, mod, w, ln_g, ln_b, cond_fn, tm=ROW_TILE):
    t = y.shape[0]
    row = lambda n: pl.BlockSpec((tm, n), lambda i: (i, 0))
    vec = pl.BlockSpec((1, D_MODEL), lambda i: (0, 0))
    return pl.pallas_call(
        _outproj_kernel,
        grid=(t // tm,),
        in_specs=[row(W_NA), row(W_DN), row(W_DN),
                  row(W_DN), pl.BlockSpec((1, W_DN), lambda i: (0, 0)),
                  pl.BlockSpec((W_DN, W_DN), lambda i: (0, 0)),
                  row(W_DF), row(D_MODEL), _mod_spec(2, cond_fn),
                  pl.BlockSpec((D_MODEL, D_MODEL), lambda i: (0, 0)), vec, vec],
        out_specs=row(D_MODEL),
        out_shape=jax.ShapeDtypeStruct((t, D_MODEL), F32),
        compiler_params=_params("parallel"),
        name="outproj_ln",
    )(oa, o_fwd, o_bwd, gate, onorm_row, grp, oc, y, mod, w, ln_g, ln_b)


MOE_TM = 1024
MOE_CAP = 256
MOE_CAP_SMALL = 128


def _router_kernel(y_ref, sh_ref, sc_ref, w_ref, b_ref, tri_ref, g_ref, rk_ref, rkt_ref, cnt_ref):
    h = y_ref[...] * (1.0 + sc_ref[...]) + sh_ref[...]
    logits = _dot_x3(h, w_ref[...]) + b_ref[...]
    lane = lax.broadcasted_iota(jnp.int32, logits.shape, 1).astype(F32)
    logits = jnp.where(lane < N_EXPERTS, logits, -jnp.inf)
    m1 = jnp.max(logits, axis=-1, keepdims=True)
    i1 = jnp.min(jnp.where(logits == m1, lane, float(LANES)), axis=-1, keepdims=True)
    rest = jnp.where(lane == i1, -jnp.inf, logits)
    m2 = jnp.max(rest, axis=-1, keepdims=True)
    i2 = jnp.min(jnp.where(rest == m2, lane, float(LANES)), axis=-1, keepdims=True)
    e2 = jnp.exp(m2 - m1)
    w1 = 1.0 / (1.0 + e2)
    g_ref[...] = jnp.where(lane == i1, w1, 0.0) + jnp.where(lane == i2, e2 * w1, 0.0)
    routed = jnp.where(lane == i1, 1.0, 0.0) + jnp.where(lane == i2, 1.0, 0.0)
    before = jnp.dot(tri_ref[...], routed.astype(BF16), preferred_element_type=F32)
    rank = jnp.where(routed > 0.5, before, -1.0)
    rk_ref[...] = rank
    rkt_ref[...] = rank.T[:N_EXPERTS, :]
    cnt_ref[...] = jnp.sum(routed, axis=0, keepdims=True)


def _router(y, mod, w_pad, b_pad, tri, cond_fn, tm=MOE_TM):
    t = y.shape[0]
    nt = t // tm
    return pl.pallas_call(
        _router_kernel,
        grid=(nt,),
        in_specs=[pl.BlockSpec((tm, D_MODEL), lambda i: (i, 0)),
                  _mod_spec(3, cond_fn), _mod_spec(4, cond_fn),
                  pl.BlockSpec((D_MODEL, LANES), lambda i: (0, 0)),
                  pl.BlockSpec((1, LANES), lambda i: (0, 0)),
                  pl.BlockSpec((tm, tm), lambda i: (0, 0))],
        out_specs=[pl.BlockSpec((tm, LANES), lambda i: (i, 0)),
                   pl.BlockSpec((tm, LANES), lambda i: (i, 0)),
                   pl.BlockSpec((None, N_EXPERTS, tm), lambda i: (i, 0, 0)),
                   pl.BlockSpec((None, 1, LANES), lambda i: (i, 0, 0))],
        out_shape=[jax.ShapeDtypeStruct((t, LANES), F32),
                   jax.ShapeDtypeStruct((t, LANES), F32),
                   jax.ShapeDtypeStruct((nt, N_EXPERTS, tm), F32),
                   jax.ShapeDtypeStruct((nt, 1, LANES), F32)],
        compiler_params=_params("parallel"),
        name="router",
    )(y, mod, mod, w_pad, b_pad, tri)


def _moe_kernel(cnt_ref, y_ref, sh_ref, sc_ref, g2_ref, gates_ref, rk_ref, rkt_ref, wg_ref, wu_ref, wd_ref,
                lg_ref, lb_ref, o_ref, h_scr, acc_scr, *, tm, cap, cap_small):
    i = pl.program_id(0)
    e = pl.program_id(1)

    @pl.when(e == 0)
    def _():
        h_scr[...] = (y_ref[...] * (1.0 + sc_ref[...]) + sh_ref[...]).astype(BF16)
        acc_scr[...] = jnp.zeros_like(acc_scr)

    lane = lax.broadcasted_iota(jnp.int32, (tm, LANES), 1)
    gate_col = jnp.sum(jnp.where(lane == e, gates_ref[...], 0.0), axis=-1, keepdims=True)
    rank_col = jnp.sum(jnp.where(lane == e, rk_ref[...], 0.0), axis=-1, keepdims=True)
    rank_row = rkt_ref[pl.ds(e, 1), :]

    def run_pass(first, rows):
        base = first.astype(F32)
        slot_r = lax.broadcasted_iota(jnp.int32, (rows, tm), 0).astype(F32) + base
        sel = jnp.where(rank_row == slot_r, 1.0, 0.0).astype(BF16)
        xe = jnp.dot(sel, h_scr[...], preferred_element_type=F32).astype(BF16)
        f = _swiglu(xe, wg_ref, wu_ref, wd_ref).astype(BF16)
        slot_c = lax.broadcasted_iota(jnp.int32, (tm, rows), 1).astype(F32) + base
        sel_t = jnp.where(rank_col == slot_c, 1.0, 0.0).astype(BF16)
        acc_scr[...] += gate_col * jnp.dot(sel_t, f, preferred_element_type=F32)

    count = cnt_ref[i * N_EXPERTS + e]
    rem = count % cap
    n_big = count // cap + jnp.where(rem > cap_small, 1, 0)

    def body(ps, carry):
        run_pass(ps * cap, cap)
        return carry

    lax.fori_loop(0, n_big, body, 0)

    @pl.when(jnp.logical_and(rem > 0, rem <= cap_small))
    def _():
        run_pass(n_big * cap, cap_small)

    @pl.when(e == N_EXPERTS - 1)
    def _():
        o_ref[...] = _layer_norm(ALPHA * y_ref[...] + g2_ref[...] * acc_scr[...],
                                 lg_ref[...], lb_ref[...])


def _moe(y, mod, gates, rank, rank_t, counts, wg, wu, wd, ln_g, ln_b, cond_fn, tm=MOE_TM, cap=MOE_CAP,
         cap_small=MOE_CAP_SMALL):
    t = y.shape[0]
    ff = wg.shape[2]
    vec = pl.BlockSpec((1, D_MODEL), lambda i, e, c: (0, 0))
    tok = lambda n: pl.BlockSpec((tm, n), lambda i, e, c: (i, 0))
    kern = functools.partial(_moe_kernel, tm=tm, cap=cap, cap_small=cap_small)
    return pl.pallas_call(
        kern,
        grid_spec=pltpu.PrefetchScalarGridSpec(
            num_scalar_prefetch=1,
            grid=(t // tm, N_EXPERTS),
            in_specs=[tok(D_MODEL), _mod_spec(3, cond_fn), _mod_spec(4, cond_fn), _mod_spec(5, cond_fn),
                      tok(LANES), tok(LANES),
                      pl.BlockSpec((None, N_EXPERTS, tm), lambda i, e, c: (i, 0, 0)),
                      pl.BlockSpec((None, D_MODEL, ff), lambda i, e, c: (e, 0, 0)),
                      pl.BlockSpec((None, D_MODEL, ff), lambda i, e, c: (e, 0, 0)),
                      pl.BlockSpec((None, ff, D_MODEL), lambda i, e, c: (e, 0, 0)),
                      vec, vec],
            out_specs=tok(D_MODEL),
            scratch_shapes=[pltpu.VMEM((tm, D_MODEL), BF16), pltpu.VMEM((tm, D_MODEL), F32)]),
        out_shape=jax.ShapeDtypeStruct((t, D_MODEL), F32),
        compiler_params=_params("parallel", "arbitrary"),
        name="moe_ln",
    )(counts, y, mod, mod, mod, gates, rank, rank_t, wg, wu, wd, ln_g, ln_b)


FF_CUT = 768


def _swiglu(h, wg_ref, wu_ref, wd_ref):
    cuts = (0, FF_CUT, wg_ref.shape[1])
    parts = [(jnp.dot(h, wg_ref[:, lo:hi], preferred_element_type=F32),
              jnp.dot(h, wu_ref[:, lo:hi], preferred_element_type=F32))
             for lo, hi in zip(cuts, cuts[1:])]
    outs = [jnp.dot((_silu(a) * b).astype(BF16), wd_ref[lo:hi, :], preferred_element_type=F32)
            for (a, b), lo, hi in zip(parts, cuts, cuts[1:])]
    return outs[0] + outs[1]


def _ffn_kernel(y_ref, sh_ref, sc_ref, g2_ref, wg_ref, wu_ref, wd_ref, lg_ref, lb_ref,
                o_ref, h_scr, acc_scr, *, n_blocks):
    e = pl.program_id(1)

    @pl.when(e == 0)
    def _():
        h_scr[...] = (y_ref[...] * (1.0 + sc_ref[...]) + sh_ref[...]).astype(BF16)
        acc_scr[...] = jnp.zeros_like(acc_scr)

    acc_scr[...] += _swiglu(h_scr[...], wg_ref, wu_ref, wd_ref)

    @pl.when(e == n_blocks - 1)
    def _():
        o_ref[...] = _layer_norm(ALPHA * y_ref[...] + g2_ref[...] * acc_scr[...],
                                 lg_ref[...], lb_ref[...])


def _ffn(y, mod, wg, wu, wd, ln_g, ln_b, cond_fn, tm=ROW_TILE, ff=MOE_FF):
    t = y.shape[0]
    n_blocks = wg.shape[1] // ff
    vec = pl.BlockSpec((1, D_MODEL), lambda i, e: (0, 0))
    kern = functools.partial(_ffn_kernel, n_blocks=n_blocks)
    return pl.pallas_call(
        kern,
        grid=(t // tm, n_blocks),
        in_specs=[pl.BlockSpec((tm, D_MODEL), lambda i, e: (i, 0)),
                  _mod_spec(3, cond_fn), _mod_spec(4, cond_fn), _mod_spec(5, cond_fn),
                  pl.BlockSpec((D_MODEL, ff), lambda i, e: (0, e)),
                  pl.BlockSpec((D_MODEL, ff), lambda i, e: (0, e)),
                  pl.BlockSpec((ff, D_MODEL), lambda i, e: (e, 0)),
                  vec, vec],
        out_specs=pl.BlockSpec((tm, D_MODEL), lambda i, e: (i, 0)),
        out_shape=jax.ShapeDtypeStruct((t, D_MODEL), F32),
        scratch_shapes=[pltpu.VMEM((tm, D_MODEL), BF16), pltpu.VMEM((tm, D_MODEL), F32)],
        compiler_params=_params("parallel", "arbitrary"),
        name="ffn_ln",
    )(y, mod, mod, mod, wg, wu, wd, ln_g, ln_b)


def _permute_w_in(w):
    offs = np.cumsum((0,) + PROJ_SIZES)
    qa, ka, va, qkv, gate, a, b, qc, kc, vc = (w[:, offs[i]:offs[i + 1]] for i in range(10))
    pad = jnp.zeros((D_MODEL, AB_PAD - 4 * H_DN), w.dtype)
    return jnp.concatenate([qa, ka, va, qkv, gate, qc, kc, vc, a, b, pad], axis=1).astype(BF16)


def _pad_row(v, n=AB_PAD):
    v = v.reshape(1, -1)
    return jnp.pad(v, ((0, 0), (0, n - v.shape[1])))


def _group_matrix(n):
    idx = np.arange(n) // HEAD_DIM
    return jnp.asarray((idx[:, None] == idx[None, :]).astype(np.float32), dtype=BF16)


def _expand_matrix():
    e = np.zeros((2, AB_PAD, 2 * W_DN), np.float32)
    for d in range(2):
        for h in range(H_DN):
            e[d, d * H_DN + h, h * HEAD_DIM:(h + 1) * HEAD_DIM] = 1.0
            e[d, 2 * H_DN + d * H_DN + h, W_DN + h * HEAD_DIM:W_DN + (h + 1) * HEAD_DIM] = 1.0
    return jnp.asarray(e, dtype=BF16)


def _chunk_cumsum_matrices(rows):
    r = np.arange(rows)
    same = (r[:, None] // CHUNK) == (r[None, :] // CHUNK)
    return jnp.asarray(np.stack([same & (r[:, None] >= r[None, :]),
                                 same & (r[:, None] <= r[None, :])]).astype(np.float32), dtype=BF16)


def kernel(x_prompt, x_sample, cache_na_k, cache_na_v, cache_df_k, cache_df_v, state_dn, c, c_ctx,
           ada_w, ada_b, w_in, conv_dn, a_log_dn, dt_bias_dn, onorm_dn, rpb_na, lambda_df, subln_df,
           w_out, ln1_g, ln1_b, ln2_g, ln2_b, ffn_w_gate, ffn_w_up, ffn_w_down, router_w, router_b,
           moe_w_gate, moe_w_up, moe_w_down):
    conds = jnp.concatenate([c_ctx[None, :], c, jnp.zeros((N_COND - 1 - DEC_BATCH, D_MODEL), F32)], axis=0)
    mods = _ada_table(conds, ada_w, ada_b)
    ctx_cond = lambda i: 0
    lat_cond = lambda i: 1 + (i * ROW_TILE) // DEC_SEQ
    lat_cond_moe = lambda i: 1 + (i * MOE_TM) // DEC_SEQ
    tri = jnp.asarray(np.tril(np.ones((MOE_TM, MOE_TM), np.float32), -1), dtype=BF16)

    grp512, grp256 = _group_matrix(2 * W_DN), _group_matrix(W_DN)
    e_mat = _expand_matrix()
    cum = _chunk_cumsum_matrices(SEQ)
    cos, sin, rot = _rope_tables()

    y_ctx = x_prompt.reshape(T_CTX, D_MODEL)
    y_lat = x_sample.reshape(T_LAT, D_MODEL)
    ctx_out = []
    for l in range(DEPTH):
        mod = mods[l]
        w_in_l = _permute_w_in(w_in[l])
        w_out_l = w_out[l].astype(BF16)
        conv_w = jnp.pad(conv_dn[l], ((0, 8 - CONV_K), (0, 0)))
        alog_row = _pad_row(a_log_dn[l])
        dtb_row = _pad_row(dt_bias_dn[l])
        onorm_row = jnp.tile(onorm_dn[l], H_DN).reshape(1, W_DN)
        subln_row = subln_df[l].reshape(1, HEAD_DIM)
        lg1, lb1 = ln1_g[l].reshape(1, D_MODEL), ln1_b[l].reshape(1, D_MODEL)
        lg2, lb2 = ln2_g[l].reshape(1, D_MODEL), ln2_b[l].reshape(1, D_MODEL)

        *na, dn, gate, q_df, k_df, v_df, ab = _inproj(y_ctx, mod, w_in_l, ctx_cond)
        df = (q_df, k_df, v_df)
        u, *gates_dn = _dn_prep(dn, ab, conv_w, alog_row, dtb_row, grp512, e_mat, cum, SEQ)
        s0 = jnp.zeros((BATCH, 2, H_DN, HEAD_DIM, HEAD_DIM), F32)
        o_f, o_b, s_fin = _deltanet(u, *gates_dn, s0, SEQ, SEQ, n_sub=4)
        oa, oc = _ctx_attn(na, df, lambda_df[l], subln_row, l)
        y1_ctx = _outproj(oa, o_f, o_b, gate, onorm_row, grp256, oc, y_ctx, mod, w_out_l, lg1, lb1, ctx_cond)
        ctx_out.append((na[1].reshape(BATCH, SEQ, H_NA, HEAD_DIM), na[2].reshape(BATCH, SEQ, H_NA, HEAD_DIM),
                        k_df.reshape(BATCH, SEQ, H_DF, HEAD_DIM), v_df.reshape(BATCH, SEQ, H_DF, HEAD_DIM),
                        s_fin))

        *na, dn, gate, q_df, k_df, v_df, ab = _inproj(y_lat, mod, w_in_l, lat_cond)
        df = (q_df, k_df, v_df)
        u, *gates_dn = _dn_prep(dn, ab, conv_w, alog_row, dtb_row, grp512, e_mat, cum, DEC_SEQ)
        o_f, o_b, _ = _deltanet(u, *gates_dn, state_dn[:, l], DEC_SEQ, DN_ROWS)
        oa = _lat_na(na, cache_na_k[:, l].reshape(DEC_BATCH, PAST_LEN, W_NA).astype(BF16),
                     cache_na_v[:, l].reshape(DEC_BATCH, PAST_LEN, W_NA).astype(BF16),
                     _na_bias_table(rpb_na[l]))
        q_r, kt_all, v_all = _rope(df, cache_df_k[:, l].reshape(DEC_BATCH, PAST_LEN, W_DF),
                                   cache_df_v[:, l].reshape(DEC_BATCH, PAST_LEN, W_DF), cos, sin, rot)
        oc = _lat_df(q_r, kt_all, v_all, lambda_df[l], subln_row, l)
        y1_lat = _outproj(oa, o_f, o_b, gate, onorm_row, grp256, oc, y_lat, mod, w_out_l, lg1, lb1, lat_cond)

        i = l // 2
        if l % 2 == 0:
            wg, wu, wd = ffn_w_gate[i].astype(BF16), ffn_w_up[i].astype(BF16), ffn_w_down[i].astype(BF16)
            y_ctx = _ffn(y1_ctx, mod, wg, wu, wd, lg2, lb2, ctx_cond)
            y_lat = _ffn(y1_lat, mod, wg, wu, wd, lg2, lb2, lat_cond)
        else:
            wg, wu, wd = moe_w_gate[i].astype(BF16), moe_w_up[i].astype(BF16), moe_w_down[i].astype(BF16)
            rw = jnp.pad(router_w[i], ((0, 0), (0, LANES - N_EXPERTS)))
            rb = _pad_row(router_b[i], LANES)
            ys = []
            for y1, cond in ((y1_ctx, ctx_cond), (y1_lat, lat_cond_moe)):
                gates, rank, rank_t, cnt = _router(y1, mod, rw, rb, tri, cond)
                counts = cnt[:, 0, :N_EXPERTS].astype(jnp.int32).reshape(-1)
                ys.append(_moe(y1, mod, gates, rank, rank_t, counts, wg, wu, wd, lg2, lb2, cond))
            y_ctx, y_lat = ys

    stack = lambda j: jnp.stack([t[j] for t in ctx_out], axis=1)
    return (y_ctx.reshape(BATCH, SEQ, D_MODEL), y_lat.reshape(DEC_BATCH, DEC_SEQ, D_MODEL),
            stack(0), stack(1), stack(2), stack(3), stack(4))
```

```python
import functools
import math

import jax
import jax.numpy as jnp
import numpy as np
from jax import lax
from jax.experimental import pallas as pl
from jax.experimental.pallas import tpu as pltpu

F32 = jnp.float32
BF16 = jnp.bfloat16
HIGHEST = lax.Precision.HIGHEST

D_MODEL = 1024
BATCH = 32
SEQ = 256
DEPTH = 2
DEC_BATCH = 2
DEC_SEQ = 4096
PAST_LEN = 512
GRID_W = 64
GRID_H = DEC_SEQ // GRID_W
HEAD_DIM = 64
H_NA = 6
H_DN = 4
H_DF = 6
W_NA = H_NA * HEAD_DIM
W_DN = H_DN * HEAD_DIM
W_DF = H_DF * HEAD_DIM
DF_QK = HEAD_DIM // 2
WIN_R = 8
WIN_C = 16
CONV_K = 5
CHUNK = 64
D_FF = 2816
N_EXPERTS = 8
MOE_FF = 1408
ALPHA = (2 * DEPTH) ** 0.25
LN_EPS = 1e-5
RMS_EPS = 1e-6
ROPE_BASE = 10000.0
NEG_INF = -1e30
PROJ_SIZES = (W_NA, W_NA, W_NA, 3 * W_DN, W_DN, 2 * H_DN, 2 * H_DN, W_DF, W_DF, W_DF)

T_CTX = BATCH * SEQ
T_LAT = DEC_BATCH * DEC_SEQ
N_COND = 8
N_MOD = 6
ROW_TILE = 1024
DN_ROWS = 512
LANES = 128
AB_PAD = LANES
SEG_WIDTHS = (W_NA, W_NA, W_NA, 3 * W_DN, W_DN, W_DF, W_DF, W_DF, AB_PAD)
SEG_OFFS = tuple(int(o) for o in np.cumsum((0,) + SEG_WIDTHS))
SEGS = tuple(zip(SEG_OFFS[:-1], SEG_OFFS[1:]))
SEG_GROUPS = ((0, 1, 2), (3,), (4,), (5, 6, 7), (8,))
P_PAD = SEG_OFFS[-1]
BD = H_DN * CHUNK
V_AUG = LANES
DF_Q_SCALE = DF_QK ** -0.5 * math.log2(math.e)
VMEM_LIMIT = 56 * 1024 * 1024


def _params(*sem):
    return pltpu.CompilerParams(dimension_semantics=sem, vmem_limit_bytes=VMEM_LIMIT)


def _dot(a, b):
    return jnp.dot(a.astype(BF16), b.astype(BF16), preferred_element_type=F32)


def _dot_nt(a, b):
    return lax.dot_general(a.astype(BF16), b.astype(BF16), (((1,), (1,)), ((), ())),
                           preferred_element_type=F32)


def _split3(x):
    x1 = x.astype(BF16)
    r1 = x - x1.astype(F32)
    x2 = r1.astype(BF16)
    return x1, x2, (r1 - x2.astype(F32)).astype(BF16)


def _dot_sel(x, sel):
    m = x.shape[0]
    y = jnp.dot(jnp.concatenate(_split3(x), axis=0), sel, preferred_element_type=F32)
    return y[:m] + y[m:2 * m] + y[2 * m:]


def _sel_dot(sel, x):
    n = x.shape[1]
    y = jnp.dot(sel, jnp.concatenate(_split3(x), axis=1), preferred_element_type=F32)
    return y[:, :n] + y[:, n:2 * n] + y[:, 2 * n:]


def _dot_x3(a, b):
    m = a.shape[0]
    a_hi = a.astype(BF16)
    a_lo = (a - a_hi.astype(F32)).astype(BF16)
    b_hi = b.astype(BF16)
    b_lo = (b - b_hi.astype(F32)).astype(BF16)
    y = jnp.dot(jnp.concatenate([a_hi, a_lo], axis=0), b_hi, preferred_element_type=F32)
    return y[:m] + y[m:] + jnp.dot(a_hi, b_lo, preferred_element_type=F32)


def _silu(x):
    return x * jax.nn.sigmoid(x)


def _layer_norm(x, g, b):
    mu = jnp.mean(x, axis=-1, keepdims=True)
    xc = x - mu
    var = jnp.mean(xc * xc, axis=-1, keepdims=True)
    return xc * lax.rsqrt(var + LN_EPS) * g + b


def _ada_kernel(c_ref, w_ref, b_ref, o_ref):
    o_ref[...] = _dot_x3(_silu(c_ref[...]), w_ref[...]) + b_ref[...]


def _ada_table(conds, ada_w, ada_b):
    out = pl.pallas_call(
        _ada_kernel,
        grid=(DEPTH, N_MOD),
        in_specs=[pl.BlockSpec((N_COND, D_MODEL), lambda l, k: (0, 0)),
                  pl.BlockSpec((None, D_MODEL, D_MODEL), lambda l, k: (l, 0, k)),
                  pl.BlockSpec((None, None, 1, D_MODEL), lambda l, k: (l, k, 0, 0))],
        out_specs=pl.BlockSpec((None, None, N_COND, D_MODEL), lambda l, k: (l, k, 0, 0)),
        out_shape=jax.ShapeDtypeStruct((DEPTH, N_MOD, N_COND, D_MODEL), F32),
        compiler_params=_params("parallel", "parallel"),
        name="ada_table",
    )(conds, ada_w, ada_b.reshape(DEPTH, N_MOD, 1, D_MODEL))
    return out.reshape(DEPTH, N_MOD, N_COND, 1, D_MODEL)


def _mod_spec(k, cond_fn):
    return pl.BlockSpec((None, None, 1, D_MODEL), lambda i, *_: (k, cond_fn(i), 0, 0))


def _inproj_kernel(x_ref, sh_ref, sc_ref, w_ref, *o_refs):
    h = (x_ref[...] * (1.0 + sc_ref[...]) + sh_ref[...]).astype(BF16)
    for group in SEG_GROUPS:
        lo = SEG_OFFS[group[0]]
        res = jnp.dot(h, w_ref[:, lo:SEG_OFFS[group[-1] + 1]], preferred_element_type=F32)
        for n in group:
            o_refs[n][...] = res[:, SEG_OFFS[n] - lo:SEG_OFFS[n + 1] - lo]


def _inproj(x, mod, w, cond_fn, tm=ROW_TILE):
    t = x.shape[0]
    return pl.pallas_call(
        _inproj_kernel,
        grid=(t // tm,),
        in_specs=[pl.BlockSpec((tm, D_MODEL), lambda i: (i, 0)),
                  _mod_spec(0, cond_fn), _mod_spec(1, cond_fn),
                  pl.BlockSpec((D_MODEL, P_PAD), lambda i: (0, 0))],
        out_specs=[pl.BlockSpec((tm, b - a), lambda i: (i, 0)) for a, b in SEGS],
        out_shape=[jax.ShapeDtypeStruct((t, b - a), F32) for a, b in SEGS],
        compiler_params=_params("parallel"),
        name="inproj",
    )(x, mod, mod, w)


def _dn_prep_kernel(x_ref, prev_ref, next_ref, ab_ref, w_ref, alog_ref, dtb_ref, grp_ref, e_ref, cum_ref,
                    u_ref, gcf_ref, gcb_ref, bf_ref, bb_ref, *, blocks_per_seq, rows):
    i = pl.program_id(0)
    j = i % blocks_per_seq
    prev = jnp.where(j != 0, prev_ref[...], 0.0)
    nxt = jnp.where(j != blocks_per_seq - 1, next_ref[...], 0.0)
    xe = jnp.concatenate([prev, x_ref[...], nxt], axis=0)
    w = w_ref[...]
    base = 8 - CONV_K // 2
    acc = w[0:1, :] * xe[base:base + rows, :]
    for t in range(1, CONV_K):
        acc = acc + w[t:t + 1, :] * xe[base + t:base + t + rows, :]
    u = _silu(acc)
    qk = u[:, :2 * W_DN]
    ss = _dot_sel(qk * qk, grp_ref[...])
    qk = qk * lax.rsqrt(ss + RMS_EPS)
    u_ref[:, :W_DN] = qk[:, :W_DN] * HEAD_DIM ** -0.5
    u_ref[:, W_DN:2 * W_DN] = qk[:, W_DN:]
    u_ref[:, 2 * W_DN:] = u[:, 2 * W_DN:]
    ab = ab_ref[...]
    z = ab + dtb_ref[...]
    softplus = jnp.maximum(z, 0.0) + jnp.log(1.0 + jnp.exp(-jnp.abs(z)))
    g = -jnp.exp(alog_ref[...]) * softplus
    lane = lax.broadcasted_iota(jnp.int32, ab.shape, 1)
    gb = jnp.where(lane < 2 * H_DN, g, jax.nn.sigmoid(ab))
    for d, (gc_ref, beta_ref) in enumerate(((gcf_ref, bf_ref), (gcb_ref, bb_ref))):
        gbx = _dot_sel(gb, e_ref[d])
        gc_ref[...] = _sel_dot(cum_ref[d], gbx[:, :W_DN])
        beta_ref[...] = gbx[:, W_DN:]


def _dn_prep(qkv, ab, conv_w, alog_row, dtb_row, grp, e_mat, cum, seq_len, rows=SEQ):
    t = qkv.shape[0]
    bps = seq_len // rows
    r8 = rows // 8
    last8 = t // 8 - 1
    kern = functools.partial(_dn_prep_kernel, blocks_per_seq=bps, rows=rows)
    return pl.pallas_call(
        kern,
        grid=(t // rows,),
        in_specs=[pl.BlockSpec((rows, 3 * W_DN), lambda i: (i, 0)),
                  pl.BlockSpec((8, 3 * W_DN), lambda i: (jnp.maximum(i * r8 - 1, 0), 0)),
                  pl.BlockSpec((8, 3 * W_DN), lambda i: (jnp.minimum((i + 1) * r8, last8), 0)),
                  pl.BlockSpec((rows, AB_PAD), lambda i: (i, 0)),
                  pl.BlockSpec((8, 3 * W_DN), lambda i: (0, 0)),
                  pl.BlockSpec((1, AB_PAD), lambda i: (0, 0)),
                  pl.BlockSpec((1, AB_PAD), lambda i: (0, 0)),
                  pl.BlockSpec((2 * W_DN, 2 * W_DN), lambda i: (0, 0)),
                  pl.BlockSpec((2, AB_PAD, 2 * W_DN), lambda i: (0, 0, 0)),
                  pl.BlockSpec((2, rows, rows), lambda i: (0, 0, 0))],
        out_specs=[pl.BlockSpec((rows, 3 * W_DN), lambda i: (i, 0))]
        + [pl.BlockSpec((rows, W_DN), lambda i: (i, 0))] * 4,
        out_shape=[jax.ShapeDtypeStruct((t, 3 * W_DN), F32)]
        + [jax.ShapeDtypeStruct((t, W_DN), F32)] * 4,
        compiler_params=_params("parallel"),
        name="dn_prep",
    )(qkv, qkv, qkv, ab, conv_w, alog_row, dtb_row, grp, e_mat, cum)


def _dn_kernel(uf_ref, ub_ref, gcf_ref, gcb_ref, bf_ref, bb_ref, s0_ref, of_ref, ob_ref, sf_ref, s_scr,
               *, n_chunks, n_sub):
    j = pl.program_id(1)
    r = lax.broadcasted_iota(jnp.int32, (BD, BD), 0)
    c = lax.broadcasted_iota(jnp.int32, (BD, BD), 1)
    same = (r // CHUNK) == (c // CHUNK)
    dt = (r % CHUNK) - (c % CHUNK)
    same_f = jnp.where(same, 1.0, 0.0)
    same_b = same_f.astype(BF16)
    eye_f = jnp.where(r == c, 1.0, 0.0)
    dirs = []
    for sign, last in ((1, CHUNK - 1), (-1, 0)):
        dirs.append((jnp.where(jnp.logical_and(same, dt * sign >= 0), 1.0, 0.0),
                     jnp.where(jnp.logical_and(same, dt * sign > 0), 1.0, 0.0), last))

    def lift(x):
        return jnp.concatenate([x, x, x, x], axis=0) * same_f

    def lift_b(x):
        xb = x.astype(BF16)
        return jnp.concatenate([xb, xb, xb, xb], axis=0) * same_b

    @pl.when(j == 0)
    def _():
        for si in range(n_sub):
            for di in range(2):
                rows = [jnp.concatenate([s0_ref[si, di, h]] * H_DN, axis=1) for h in range(H_DN)]
                s_scr[si, di] = jnp.concatenate(rows, axis=0) * same_f

    def each(f, *xs):
        return [f(*a) for a in zip(*xs)]

    def mm(a, b):
        return jnp.dot(a, b, preferred_element_type=F32)

    def chunks(us, gc, beta, ss, incls, stricts, lasts):
        q = [u[:, :W_DN] for u in us]
        k = [u[:, W_DN:2 * W_DN] for u in us]
        v = [u[:, 2 * W_DN:] for u in us]
        gl = each(lambda x, last: x[last:last + 1, :], gc, lasts)
        eg = [jnp.exp(x) for x in gc]
        kb = each(jnp.multiply, k, beta)
        gcol = [lift(x) for x in gc]
        decay = each(lambda x, m: jnp.exp(jnp.where(m > 0.5, x - x.T, NEG_INF)), gcol, incls)
        qk = each(lambda a, b, kk: lax.dot_general(jnp.concatenate([lift_b(a), lift_b(b)], axis=0),
                                                   lift_b(kk), (((1,), (1,)), ((), ())),
                                                   preferred_element_type=F32), q, kb, k)
        attn = each(lambda x, d: (x[:BD] * d).astype(BF16), qk, decay)
        n_mat = each(lambda x, d, m: -(x[BD:] * d) * m, qk, decay, stricts)
        t_inv = [eye_f + x for x in n_mat]
        n_hi = [x.astype(BF16) for x in n_mat]
        m_b = n_hi
        for _ in range(int(math.log2(CHUNK)) - 1):
            m_b = [mm(x, x).astype(BF16) for x in m_b]
            t_inv = each(lambda t, m: t + mm(t.astype(BF16), m), t_inv, m_b)
        n_lo = each(lambda x, h: (x - h.astype(F32)).astype(BF16), n_mat, n_hi)
        x_hi = [t.astype(BF16) for t in t_inv]
        x_lo = each(lambda t, h: (t - h.astype(F32)).astype(BF16), t_inv, x_hi)
        nx = each(lambda h, lo, x: mm(jnp.concatenate([h, lo], axis=0), x), n_hi, n_lo, x_hi)
        nxl = each(mm, n_hi, x_lo)
        resid = each(lambda t, a, b: (eye_f - t + a[:BD] + a[BD:] + b).astype(BF16), t_inv, nx, nxl)
        t_b = each(lambda t, h, rr: (t + mm(h, rr)).astype(BF16), t_inv, x_hi, resid)
        rhs = each(lambda vv, b, kk, e: jnp.concatenate([lift_b(vv * b), lift_b(kk * e)], axis=1),
                   v, beta, kb, eg)
        sol = each(mm, t_b, rhs)
        s_b = [s.astype(BF16) for s in ss]
        ps = each(lambda so, qq, e, sb: mm(jnp.concatenate([so[:, BD:].astype(BF16), lift_b(qq * e)],
                                                           axis=0), sb), sol, q, eg, s_b)
        v_new = each(lambda so, p: (so[:, :BD] - p[:BD]).astype(BF16), sol, ps)
        o_bd = each(lambda p, a, vn: p[BD:] + mm(a, vn), ps, attn, v_new)
        o_tm = [x[0:CHUNK] + x[CHUNK:2 * CHUNK] + x[2 * CHUNK:3 * CHUNK] + x[3 * CHUNK:] for x in o_bd]
        k_tail = each(lambda kk, a, b: lift(kk * jnp.exp(a - b)).T.astype(BF16), k, gl, gc)
        s_new = each(lambda s, a, kt, vn: s * jnp.exp(a) + mm(kt, vn), ss, gl, k_tail, v_new)
        return o_tm, s_new

    chains = [(si, di) for si in range(n_sub) for di in range(2)]

    def body(ci, carry):
        offs = (pl.multiple_of(ci * CHUNK, CHUNK), pl.multiple_of((n_chunks - 1 - ci) * CHUNK, CHUNK))
        u_refs, gc_refs, b_refs, o_refs = (uf_ref, ub_ref), (gcf_ref, gcb_ref), (bf_ref, bb_ref), (of_ref, ob_ref)
        o_tm, s_new = chunks(
            [u_refs[di][si, pl.ds(offs[di], CHUNK), :] for si, di in chains],
            [gc_refs[di][si, pl.ds(offs[di], CHUNK), :] for si, di in chains],
            [b_refs[di][si, pl.ds(offs[di], CHUNK), :] for si, di in chains],
            [s_scr[si, di] for si, di in chains],
            *[[dirs[di][n] for si, di in chains] for n in range(3)])
        for (si, di), o, s in zip(chains, o_tm, s_new):
            o_refs[di][si, pl.ds(offs[di], CHUNK), :] = o
            s_scr[si, di] = s
        return carry

    lax.fori_loop(0, n_chunks, body, 0)

    @pl.when(j == pl.num_programs(1) - 1)
    def _():
        for si in range(n_sub):
            for di in range(2):
                for h in range(H_DN):
                    sl = slice(h * HEAD_DIM, (h + 1) * HEAD_DIM)
                    sf_ref[si, di, h] = s_scr[si, di, sl, sl]


def _deltanet(u, gc_f, gc_b, beta_f, beta_b, s0_bd, seq_len, rows, n_sub=2):
    t = u.shape[0]
    n_seq = t // seq_len
    nb = seq_len // rows
    kern = functools.partial(_dn_kernel, n_chunks=rows // CHUNK, n_sub=n_sub)
    fwd = lambda s, j: (s, j, 0)
    bwd = lambda s, j: (s, nb - 1 - j, 0)
    state = pl.BlockSpec((n_sub, 2, H_DN, HEAD_DIM, HEAD_DIM), lambda s, j: (s, 0, 0, 0, 0))
    o_f, o_b, s_fin = pl.pallas_call(
        kern,
        grid=(n_seq // n_sub, nb),
        in_specs=[pl.BlockSpec((n_sub, rows, 3 * W_DN), fwd), pl.BlockSpec((n_sub, rows, 3 * W_DN), bwd),
                  pl.BlockSpec((n_sub, rows, W_DN), fwd), pl.BlockSpec((n_sub, rows, W_DN), bwd),
                  pl.BlockSpec((n_sub, rows, W_DN), fwd), pl.BlockSpec((n_sub, rows, W_DN), bwd), state],
        out_specs=[pl.BlockSpec((n_sub, rows, W_DN), fwd), pl.BlockSpec((n_sub, rows, W_DN), bwd), state],
        out_shape=[jax.ShapeDtypeStruct((n_seq, seq_len, W_DN), F32),
                   jax.ShapeDtypeStruct((n_seq, seq_len, W_DN), F32),
                   jax.ShapeDtypeStruct((n_seq, 2, H_DN, HEAD_DIM, HEAD_DIM), F32)],
        scratch_shapes=[pltpu.VMEM((n_sub, 2, BD, BD), F32)],
        compiler_params=_params("parallel", "arbitrary"),
        name="deltanet",
    )(*[x.reshape(n_seq, seq_len, -1) for x in (u, u, gc_f, gc_b, beta_f, beta_b)], s0_bd)
    return o_f.reshape(t, W_DN), o_b.reshape(t, W_DN), s_fin


def _diff_lambda(lam_ref, lam_init):
    lp = lam_ref[...]
    return (jnp.exp(jnp.sum(lp[0:1] * lp[1:2], axis=1, keepdims=True))
            - jnp.exp(jnp.sum(lp[2:3] * lp[3:4], axis=1, keepdims=True)) + lam_init)


def _sub_rms(o, subln, lam_init):
    ms = jnp.mean(o * o, axis=-1, keepdims=True)
    return o * lax.rsqrt(ms + RMS_EPS) * subln * (1.0 - lam_init)


def _ctx_attn_kernel(qa_ref, ka_ref, va_ref, qc_ref, kc_ref, vc_ref, lam_ref, subln_ref, oa_ref, oc_ref,
                     *, lam_init):
    lam = _diff_lambda(lam_ref, lam_init)
    subln = subln_ref[...]
    n = qa_ref.shape[0]
    qa = (qa_ref[...] * NA_Q_SCALE).astype(BF16)
    ka = ka_ref[...].astype(BF16)
    va = va_ref[...].astype(BF16)
    qc = (qc_ref[...] * DF_Q_SCALE).astype(BF16)
    kc = kc_ref[...].astype(BF16)
    vc = vc_ref[...].astype(BF16)
    heads = [slice(h * HEAD_DIM, (h + 1) * HEAD_DIM) for h in range(H_NA)]
    maps = [slice(m * DF_QK, (m + 1) * DF_QK) for m in range(2 * H_DF)]
    s_na = [_dot_nt(qa[:, sl], ka[:, sl]) for sl in heads]
    s_df = [_dot_nt(qc[:, sl], kc[:, sl]) for sl in maps]
    p_na = [jnp.exp2(s - jnp.max(s, axis=-1, keepdims=True)) for s in s_na]
    p_df = [jnp.exp2(s - jnp.max(s, axis=-1, keepdims=True)) for s in s_df]
    d_na = [jnp.sum(p, axis=-1, keepdims=True) for p in p_na]
    d_df = [jnp.sum(p, axis=-1, keepdims=True) for p in p_df]
    o_na = [_dot(p, va[:, sl]) for p, sl in zip(p_na, heads)]
    o_df = [_dot(jnp.concatenate([p_df[2 * h].astype(BF16), p_df[2 * h + 1].astype(BF16)], axis=0),
                 vc[:, heads[h]]) for h in range(H_DF)]
    for h in range(H_NA):
        oa_ref[:, heads[h]] = (o_na[h] / d_na[h]).astype(BF16)
    for h in range(H_DF):
        o = o_df[h][:n] / d_df[2 * h] - lam * (o_df[h][n:] / d_df[2 * h + 1])
        oc_ref[:, heads[h]] = _sub_rms(o, subln, lam_init).astype(BF16)


def _ctx_attn(qkv_na, qkv_df, lam_p, subln_row, layer):
    lam_init = 0.8 - 0.6 * math.exp(-0.3 * layer)
    kern = functools.partial(_ctx_attn_kernel, lam_init=lam_init)
    return pl.pallas_call(
        kern,
        grid=(BATCH,),
        in_specs=[pl.BlockSpec((SEQ, W_NA), lambda b: (b, 0))] * 3
        + [pl.BlockSpec((SEQ, W_DF), lambda b: (b, 0))] * 3
        + [pl.BlockSpec((4, DF_QK), lambda b: (0, 0)),
                  pl.BlockSpec((1, HEAD_DIM), lambda b: (0, 0))],
        out_specs=[pl.BlockSpec((SEQ, W_NA), lambda b: (b, 0)),
                   pl.BlockSpec((SEQ, W_DF), lambda b: (b, 0))],
        out_shape=[jax.ShapeDtypeStruct((T_CTX, W_NA), BF16),
                   jax.ShapeDtypeStruct((T_CTX, W_DF), BF16)],
        compiler_params=_params("parallel"),
        name="ctx_attn",
    )(*qkv_na, *qkv_df, lam_p, subln_row)


NA_R = 4
NA_U = NA_R + WIN_R - 1
NA_GROUPS = GRID_H // NA_R
NA_Q_SCALE = HEAD_DIM ** -0.5 * math.log2(math.e)


def _na_key_start(g):
    return jnp.clip(g * NA_R - WIN_R // 2, 0, GRID_H - NA_U)


def _na_bias_table(rpb):
    qc = np.arange(GRID_W)[:, None]
    kc = np.arange(GRID_W)[None, :]
    cs = np.clip(qc - WIN_C // 2, 0, GRID_W - WIN_C)
    valid = (kc >= cs) & (kc < cs + WIN_C)
    dc = np.clip(kc - qc + (WIN_C - 1), 0, 2 * WIN_C - 2)
    onehot = (dc[None] == np.arange(2 * WIN_C - 1)[:, None, None]).astype(np.float32)
    x = jnp.einsum('hrd,dqk->hrqk', rpb.astype(F32), jnp.asarray(onehot), precision=HIGHEST)
    return jnp.where(valid[None, None], x * math.log2(math.e), NEG_INF)


def _na_fill_bias(tab_ref, bias_ref, g):
    u0 = int(np.clip(g * NA_R - WIN_R // 2, 0, GRID_H - NA_U))
    masked = jnp.full((H_NA, GRID_W, GRID_W), NEG_INF, F32)
    for i in range(NA_R):
        r = g * NA_R + i
        w0 = int(np.clip(r - WIN_R // 2, 0, GRID_H - WIN_R))
        for u in range(NA_U):
            inside = w0 <= u0 + u < w0 + WIN_R
            block = tab_ref[:, u0 + u - r + WIN_R - 1] if inside else masked
            bias_ref[:, i * GRID_W:(i + 1) * GRID_W, u * GRID_W:(u + 1) * GRID_W] = block


def _lat_na_kernel(q_ref, k_ref, v_ref, kc_ref, vc_ref, tab_ref, o_ref, bias_ref):
    g = pl.program_id(1)
    for g_build in (0, 1, NA_GROUPS - 1):
        @pl.when(g == g_build)
        def _():
            _na_fill_bias(tab_ref, bias_ref, g_build)

    start = pl.multiple_of(_na_key_start(g) * GRID_W, GRID_W)
    q = (q_ref[...] * NA_Q_SCALE).astype(BF16)
    kw = k_ref[pl.ds(start, NA_U * GRID_W), :].astype(BF16)
    vw = v_ref[pl.ds(start, NA_U * GRID_W), :].astype(BF16)
    kc = kc_ref[...]
    vc = vc_ref[...]
    heads = [slice(h * HEAD_DIM, (h + 1) * HEAD_DIM) for h in range(H_NA)]
    s_loc = [_dot_nt(q[:, sl], kw[:, sl]) + bias_ref[h] for h, sl in enumerate(heads)]
    s_ctx = [_dot_nt(q[:, sl], kc[:, sl]) for sl in heads]
    m = [jnp.maximum(jnp.max(a, axis=-1, keepdims=True), jnp.max(b, axis=-1, keepdims=True))
         for a, b in zip(s_loc, s_ctx)]
    p_loc = [jnp.exp2(a - mm) for a, mm in zip(s_loc, m)]
    p_ctx = [jnp.exp2(b - mm) for b, mm in zip(s_ctx, m)]
    den = [jnp.sum(a, axis=-1, keepdims=True) + jnp.sum(b, axis=-1, keepdims=True)
           for a, b in zip(p_loc, p_ctx)]
    o = [_dot(a, vw[:, sl]) + _dot(b, vc[:, sl]) for a, b, sl in zip(p_loc, p_ctx, heads)]
    for h, sl in enumerate(heads):
        o_ref[:, sl] = (o[h] / den[h]).astype(BF16)


def _lat_na(qkv_na, ck, cv, bias_tab):
    rows = NA_R * GRID_W
    return pl.pallas_call(
        _lat_na_kernel,
        grid=(DEC_BATCH, NA_GROUPS),
        in_specs=[pl.BlockSpec((rows, W_NA), lambda b, g: (b * NA_GROUPS + g, 0)),
                  pl.BlockSpec((DEC_SEQ, W_NA), lambda b, g: (b, 0)),
                  pl.BlockSpec((DEC_SEQ, W_NA), lambda b, g: (b, 0)),
                  pl.BlockSpec((None, PAST_LEN, W_NA), lambda b, g: (b, 0, 0)),
                  pl.BlockSpec((None, PAST_LEN, W_NA), lambda b, g: (b, 0, 0)),
                  pl.BlockSpec((H_NA, 2 * WIN_R - 1, GRID_W, GRID_W), lambda b, g: (0, 0, 0, 0))],
        out_specs=pl.BlockSpec((rows, W_NA), lambda b, g: (b * NA_GROUPS + g, 0)),
        out_shape=jax.ShapeDtypeStruct((T_LAT, W_NA), BF16),
        scratch_shapes=[pltpu.VMEM((H_NA, rows, NA_U * GRID_W), F32)],
        compiler_params=_params("arbitrary", "arbitrary"),
        name="lat_na",
    )(*qkv_na, ck, cv, bias_tab)


def _rope_tables():
    nf = DF_QK // 4
    inv = ROPE_BASE ** (-np.arange(nf, dtype=np.float32) / nf)
    t = np.arange(DEC_SEQ)
    pos = np.stack([t // GRID_W, t % GRID_W], axis=-1).astype(np.float32)
    ang = jnp.asarray(pos[:, :, None] * inv)
    cos, sin = jnp.cos(ang), jnp.sin(ang)
    cos32 = jnp.concatenate([cos, cos], axis=-1).reshape(DEC_SEQ, DF_QK)
    sin32 = jnp.concatenate([-sin, sin], axis=-1).reshape(DEC_SEQ, DF_QK)
    reps = W_DF // DF_QK
    rot = np.zeros((W_DF, W_DF), np.float32)
    for dd in range(W_DF):
        rot[dd + nf if dd % (2 * nf) < nf else dd - nf, dd] = 1.0
    return jnp.tile(cos32, (1, reps)), jnp.tile(sin32, (1, reps)), jnp.asarray(rot, dtype=BF16)


def _rope_kernel(qin_ref, kin_ref, vin_ref, kc_ref, vc_ref, cos_ref, sin_ref, rot_ref, q_ref, kt_ref, v_ref,
                 *, nb):
    i = pl.program_id(1)
    lane = lax.broadcasted_iota(jnp.int32, (q_ref.shape[0], V_AUG - HEAD_DIM), 1)
    one_col = jnp.where(lane == 0, 1.0, 0.0).astype(BF16)

    def put_values(src_ref):
        for h in range(H_DF):
            a = h * HEAD_DIM
            v_ref[:, h * V_AUG:h * V_AUG + HEAD_DIM] = src_ref[:, a:a + HEAD_DIM].astype(BF16)
            v_ref[:, h * V_AUG + HEAD_DIM:(h + 1) * V_AUG] = one_col

    @pl.when(i < nb)
    def _():
        cos, sin, rot = cos_ref[...], sin_ref[...], rot_ref[...]
        q = qin_ref[...]
        k = kin_ref[...]
        q_ref[...] = ((q * cos + _dot_sel(q, rot) * sin) * DF_Q_SCALE).astype(BF16)
        kt_ref[...] = (k * cos + _dot_sel(k, rot) * sin).T.astype(BF16)
        put_values(vin_ref)

    @pl.when(i == nb)
    def _():
        kt_ref[...] = kc_ref[...].T.astype(BF16)
        put_values(vc_ref)


def _rope(qkv_df, kc, vc, cos, sin, rot, tm=PAST_LEN):
    nb = DEC_SEQ // tm
    lat = lambda b, i: (b * nb + jnp.minimum(i, nb - 1), 0)
    pos = lambda b, i: (jnp.minimum(i, nb - 1), 0)
    ctx = pl.BlockSpec((None, PAST_LEN, W_DF), lambda b, i: (b, 0, 0))
    return pl.pallas_call(
        functools.partial(_rope_kernel, nb=nb),
        grid=(DEC_BATCH, nb + 1),
        in_specs=[pl.BlockSpec((tm, W_DF), lat)] * 3 + [ctx, ctx]
        + [pl.BlockSpec((tm, W_DF), pos), pl.BlockSpec((tm, W_DF), pos),
           pl.BlockSpec((W_DF, W_DF), lambda b, i: (0, 0))],
        out_specs=[pl.BlockSpec((tm, W_DF), lat),
                   pl.BlockSpec((None, W_DF, tm), lambda b, i: (b, 0, i)),
                   pl.BlockSpec((None, tm, H_DF * V_AUG), lambda b, i: (b, i, 0))],
        out_shape=[jax.ShapeDtypeStruct((T_LAT, W_DF), BF16),
                   jax.ShapeDtypeStruct((DEC_BATCH, W_DF, DEC_SEQ + PAST_LEN), BF16),
                   jax.ShapeDtypeStruct((DEC_BATCH, DEC_SEQ + PAST_LEN, H_DF * V_AUG), BF16)],
        compiler_params=_params("parallel", "arbitrary"),
        name="rope",
    )(*qkv_df, kc, vc, cos, sin, rot)


def _lat_df_kernel(q_ref, kt_ref, v_ref, lam_ref, subln_ref, o_ref, *, lam_init, tq):
    q = q_ref[...]
    lam = _diff_lambda(lam_ref, lam_init)
    subln = subln_ref[...]

    def scores(n):
        a = n * DF_QK
        return jnp.dot(q[:, a:a + DF_QK], kt_ref[a:a + DF_QK, :], preferred_element_type=F32)

    n_maps = 2 * H_DF
    es = []
    s_next = scores(0)
    for n in range(n_maps):
        s = s_next
        if n + 1 < n_maps:
            s_next = scores(n + 1)
        es.append(jnp.exp2(s - jnp.max(s, axis=-1, keepdims=True)).astype(BF16))
        if n % 2 == 1:
            h = n // 2
            ov = jnp.dot(jnp.concatenate(es, axis=0), v_ref[:, h * V_AUG:(h + 1) * V_AUG],
                         preferred_element_type=F32)
            o = (ov[:tq, :HEAD_DIM] / ov[:tq, HEAD_DIM:HEAD_DIM + 1]
                 - lam * (ov[tq:, :HEAD_DIM] / ov[tq:, HEAD_DIM:HEAD_DIM + 1]))
            o_ref[:, h * HEAD_DIM:(h + 1) * HEAD_DIM] = _sub_rms(o, subln, lam_init).astype(BF16)
            es = []


def _lat_df(q_r, kt_all, v_all, lam_p, subln_row, layer, tq=2 * LANES):
    lam_init = 0.8 - 0.6 * math.exp(-0.3 * layer)
    nb = DEC_SEQ // tq
    n_keys = kt_all.shape[2]
    kern = functools.partial(_lat_df_kernel, lam_init=lam_init, tq=tq)
    return pl.pallas_call(
        kern,
        grid=(DEC_BATCH, nb),
        in_specs=[pl.BlockSpec((tq, W_DF), lambda b, i: (b * nb + i, 0)),
                  pl.BlockSpec((None, W_DF, n_keys), lambda b, i: (b, 0, 0)),
                  pl.BlockSpec((None, n_keys, H_DF * V_AUG), lambda b, i: (b, 0, 0)),
                  pl.BlockSpec((4, DF_QK), lambda b, i: (0, 0)),
                  pl.BlockSpec((1, HEAD_DIM), lambda b, i: (0, 0))],
        out_specs=pl.BlockSpec((tq, W_DF), lambda b, i: (b * nb + i, 0)),
        out_shape=jax.ShapeDtypeStruct((T_LAT, W_DF), BF16),
        compiler_params=_params("parallel", "arbitrary"),
        name="lat_df",
    )(q_r, kt_all, v_all, lam_p, subln_row)


def _outproj_kernel(oa_ref, of_ref, obk_ref, gate_ref, onorm_ref, grp_ref, oc_ref, y_ref, g1_ref, w_ref,
                    lg_ref, lb_ref, o_ref):
    ob = of_ref[...] + obk_ref[...]
    ms = _dot_sel(ob * ob, grp_ref[...]) * (1.0 / HEAD_DIM)
    ob = ob * lax.rsqrt(ms + RMS_EPS) * onorm_ref[...] * _silu(gate_ref[...])
    o = (jnp.dot(oa_ref[...].astype(BF16), w_ref[:W_NA, :], preferred_element_type=F32)
         + jnp.dot(ob.astype(BF16), w_ref[W_NA:W_NA + W_DN, :], preferred_element_type=F32)
         + jnp.dot(oc_ref[...].astype(BF16), w_ref[W_NA + W_DN:, :], preferred_element_type=F32))
    o_ref[...] = _layer_norm(ALPHA * y_ref[...] + g1_ref[...] * o, lg_ref[...], lb_ref[...])


def _outproj(oa, o_fwd, o_bwd, gate, onorm_row, grp, oc, y, mod, w, ln_g, ln_b, cond_fn, tm=ROW_TILE):
    t = y.shape[0]
    row = lambda n: pl.BlockSpec((tm, n), lambda i: (i, 0))
    vec = pl.BlockSpec((1, D_MODEL), lambda i: (0, 0))
    return pl.pallas_call(
        _outproj_kernel,
        grid=(t // tm,),
        in_specs=[row(W_NA), row(W_DN), row(W_DN),
                  row(W_DN), pl.BlockSpec((1, W_DN), lambda i: (0, 0)),
                  pl.BlockSpec((W_DN, W_DN), lambda i: (0, 0)),
                  row(W_DF), row(D_MODEL), _mod_spec(2, cond_fn),
                  pl.BlockSpec((D_MODEL, D_MODEL), lambda i: (0, 0)), vec, vec],
        out_specs=row(D_MODEL),
        out_shape=jax.ShapeDtypeStruct((t, D_MODEL), F32),
        compiler_params=_params("parallel"),
        name="outproj_ln",
    )(oa, o_fwd, o_bwd, gate, onorm_row, grp, oc, y, mod, w, ln_g, ln_b)


MOE_TM = 1024
MOE_CAP = 256
MOE_CAP_SMALL = 128


def _router_kernel(y_ref, sh_ref, sc_ref, w_ref, b_ref, tri_ref, g_ref, rk_ref, rkt_ref, cnt_ref):
    h = y_ref[...] * (1.0 + sc_ref[...]) + sh_ref[...]
    logits = _dot_x3(h, w_ref[...]) + b_ref[...]
    lane = lax.broadcasted_iota(jnp.int32, logits.shape, 1).astype(F32)
    logits = jnp.where(lane < N_EXPERTS, logits, -jnp.inf)
    m1 = jnp.max(logits, axis=-1, keepdims=True)
    i1 = jnp.min(jnp.where(logits == m1, lane, float(LANES)), axis=-1, keepdims=True)
    rest = jnp.where(lane == i1, -jnp.inf, logits)
    m2 = jnp.max(rest, axis=-1, keepdims=True)
    i2 = jnp.min(jnp.where(rest == m2, lane, float(LANES)), axis=-1, keepdims=True)
    e2 = jnp.exp(m2 - m1)
    w1 = 1.0 / (1.0 + e2)
    g_ref[...] = jnp.where(lane == i1, w1, 0.0) + jnp.where(lane == i2, e2 * w1, 0.0)
    routed = jnp.where(lane == i1, 1.0, 0.0) + jnp.where(lane == i2, 1.0, 0.0)
    before = jnp.dot(tri_ref[...], routed.astype(BF16), preferred_element_type=F32)
    rank = jnp.where(routed > 0.5, before, -1.0)
    rk_ref[...] = rank
    rkt_ref[...] = rank.T[:N_EXPERTS, :]
    cnt_ref[...] = jnp.sum(routed, axis=0, keepdims=True)


def _router(y, mod, w_pad, b_pad, tri, cond_fn, tm=MOE_TM):
    t = y.shape[0]
    nt = t // tm
    return pl.pallas_call(
        _router_kernel,
        grid=(nt,),
        in_specs=[pl.BlockSpec((tm, D_MODEL), lambda i: (i, 0)),
                  _mod_spec(3, cond_fn), _mod_spec(4, cond_fn),
                  pl.BlockSpec((D_MODEL, LANES), lambda i: (0, 0)),
                  pl.BlockSpec((1, LANES), lambda i: (0, 0)),
                  pl.BlockSpec((tm, tm), lambda i: (0, 0))],
        out_specs=[pl.BlockSpec((tm, LANES), lambda i: (i, 0)),
                   pl.BlockSpec((tm, LANES), lambda i: (i, 0)),
                   pl.BlockSpec((None, N_EXPERTS, tm), lambda i: (i, 0, 0)),
                   pl.BlockSpec((None, 1, LANES), lambda i: (i, 0, 0))],
        out_shape=[jax.ShapeDtypeStruct((t, LANES), F32),
                   jax.ShapeDtypeStruct((t, LANES), F32),
                   jax.ShapeDtypeStruct((nt, N_EXPERTS, tm), F32),
                   jax.ShapeDtypeStruct((nt, 1, LANES), F32)],
        compiler_params=_params("parallel"),
        name="router",
    )(y, mod, mod, w_pad, b_pad, tri)


def _moe_kernel(cnt_ref, y_ref, sh_ref, sc_ref, g2_ref, gates_ref, rk_ref, rkt_ref, wg_ref, wu_ref, wd_ref,
                lg_ref, lb_ref, o_ref, h_scr, acc_scr, *, tm, cap, cap_small):
    i = pl.program_id(0)
    e = pl.program_id(1)

    @pl.when(e == 0)
    def _():
        h_scr[...] = (y_ref[...] * (1.0 + sc_ref[...]) + sh_ref[...]).astype(BF16)
        acc_scr[...] = jnp.zeros_like(acc_scr)

    lane = lax.broadcasted_iota(jnp.int32, (tm, LANES), 1)
    gate_col = jnp.sum(jnp.where(lane == e, gates_ref[...], 0.0), axis=-1, keepdims=True)
    rank_col = jnp.sum(jnp.where(lane == e, rk_ref[...], 0.0), axis=-1, keepdims=True)
    rank_row = rkt_ref[pl.ds(e, 1), :]

    def run_pass(first, rows):
        base = first.astype(F32)
        slot_r = lax.broadcasted_iota(jnp.int32, (rows, tm), 0).astype(F32) + base
        sel = jnp.where(rank_row == slot_r, 1.0, 0.0).astype(BF16)
        xe = jnp.dot(sel, h_scr[...], preferred_element_type=F32).astype(BF16)
        f = _swiglu(xe, wg_ref, wu_ref, wd_ref).astype(BF16)
        slot_c = lax.broadcasted_iota(jnp.int32, (tm, rows), 1).astype(F32) + base
        sel_t = jnp.where(rank_col == slot_c, 1.0, 0.0).astype(BF16)
        acc_scr[...] += gate_col * jnp.dot(sel_t, f, preferred_element_type=F32)

    count = cnt_ref[i * N_EXPERTS + e]
    rem = count % cap
    n_big = count // cap + jnp.where(rem > cap_small, 1, 0)

    def body(ps, carry):
        run_pass(ps * cap, cap)
        return carry

    lax.fori_loop(0, n_big, body, 0)

    @pl.when(jnp.logical_and(rem > 0, rem <= cap_small))
    def _():
        run_pass(n_big * cap, cap_small)

    @pl.when(e == N_EXPERTS - 1)
    def _():
        o_ref[...] = _layer_norm(ALPHA * y_ref[...] + g2_ref[...] * acc_scr[...],
                                 lg_ref[...], lb_ref[...])


def _moe(y, mod, gates, rank, rank_t, counts, wg, wu, wd, ln_g, ln_b, cond_fn, tm=MOE_TM, cap=MOE_CAP,
         cap_small=MOE_CAP_SMALL):
    t = y.shape[0]
    ff = wg.shape[2]
    vec = pl.BlockSpec((1, D_MODEL), lambda i, e, c: (0, 0))
    tok = lambda n: pl.BlockSpec((tm, n), lambda i, e, c: (i, 0))
    kern = functools.partial(_moe_kernel, tm=tm, cap=cap, cap_small=cap_small)
    return pl.pallas_call(
        kern,
        grid_spec=pltpu.PrefetchScalarGridSpec(
            num_scalar_prefetch=1,
            grid=(t // tm, N_EXPERTS),
            in_specs=[tok(D_MODEL), _mod_spec(3, cond_fn), _mod_spec(4, cond_fn), _mod_spec(5, cond_fn),
                      tok(LANES), tok(LANES),
                      pl.BlockSpec((None, N_EXPERTS, tm), lambda i, e, c: (i, 0, 0)),
                      pl.BlockSpec((None, D_MODEL, ff), lambda i, e, c: (e, 0, 0)),
                      pl.BlockSpec((None, D_MODEL, ff), lambda i, e, c: (e, 0, 0)),
                      pl.BlockSpec((None, ff, D_MODEL), lambda i, e, c: (e, 0, 0)),
                      vec, vec],
            out_specs=tok(D_MODEL),
            scratch_shapes=[pltpu.VMEM((tm, D_MODEL), BF16), pltpu.VMEM((tm, D_MODEL), F32)]),
        out_shape=jax.ShapeDtypeStruct((t, D_MODEL), F32),
        compiler_params=_params("parallel", "arbitrary"),
        name="moe_ln",
    )(counts, y, mod, mod, mod, gates, rank, rank_t, wg, wu, wd, ln_g, ln_b)


FF_CUT = 768


def _swiglu(h, wg_ref, wu_ref, wd_ref):
    cuts = (0, FF_CUT, wg_ref.shape[1])
    parts = [(jnp.dot(h, wg_ref[:, lo:hi], preferred_element_type=F32),
              jnp.dot(h, wu_ref[:, lo:hi], preferred_element_type=F32))
             for lo, hi in zip(cuts, cuts[1:])]
    outs = [jnp.dot((_silu(a) * b).astype(BF16), wd_ref[lo:hi, :], preferred_element_type=F32)
            for (a, b), lo, hi in zip(parts, cuts, cuts[1:])]
    return outs[0] + outs[1]


def _ffn_kernel(y_ref, sh_ref, sc_ref, g2_ref, wg_ref, wu_ref, wd_ref, lg_ref, lb_ref,
                o_ref, h_scr, acc_scr, *, n_blocks):
    e = pl.program_id(1)

    @pl.when(e == 0)
    def _():
        h_scr[...] = (y_ref[...] * (1.0 + sc_ref[...]) + sh_ref[...]).astype(BF16)
        acc_scr[...] = jnp.zeros_like(acc_scr)

    acc_scr[...] += _swiglu(h_scr[...], wg_ref, wu_ref, wd_ref)

    @pl.when(e == n_blocks - 1)
    def _():
        o_ref[...] = _layer_norm(ALPHA * y_ref[...] + g2_ref[...] * acc_scr[...],
                                 lg_ref[...], lb_ref[...])


def _ffn(y, mod, wg, wu, wd, ln_g, ln_b, cond_fn, tm=ROW_TILE, ff=MOE_FF):
    t = y.shape[0]
    n_blocks = wg.shape[1] // ff
    vec = pl.BlockSpec((1, D_MODEL), lambda i, e: (0, 0))
    kern = functools.partial(_ffn_kernel, n_blocks=n_blocks)
    return pl.pallas_call(
        kern,
        grid=(t // tm, n_blocks),
        in_specs=[pl.BlockSpec((tm, D_MODEL), lambda i, e: (i, 0)),
                  _mod_spec(3, cond_fn), _mod_spec(4, cond_fn), _mod_spec(5, cond_fn),
                  pl.BlockSpec((D_MODEL, ff), lambda i, e: (0, e)),
                  pl.BlockSpec((D_MODEL, ff), lambda i, e: (0, e)),
                  pl.BlockSpec((ff, D_MODEL), lambda i, e: (e, 0)),
                  vec, vec],
        out_specs=pl.BlockSpec((tm, D_MODEL), lambda i, e: (i, 0)),
        out_shape=jax.ShapeDtypeStruct((t, D_MODEL), F32),
        scratch_shapes=[pltpu.VMEM((tm, D_MODEL), BF16), pltpu.VMEM((tm, D_MODEL), F32)],
        compiler_params=_params("parallel", "arbitrary"),
        name="ffn_ln",
    )(y, mod, mod, mod, wg, wu, wd, ln_g, ln_b)


def _permute_w_in(w):
    offs = np.cumsum((0,) + PROJ_SIZES)
    qa, ka, va, qkv, gate, a, b, qc, kc, vc = (w[:, offs[i]:offs[i + 1]] for i in range(10))
    pad = jnp.zeros((D_MODEL, AB_PAD - 4 * H_DN), w.dtype)
    return jnp.concatenate([qa, ka, va, qkv, gate, qc, kc, vc, a, b, pad], axis=1).astype(BF16)


def _pad_row(v, n=AB_PAD):
    v = v.reshape(1, -1)
    return jnp.pad(v, ((0, 0), (0, n - v.shape[1])))


def _group_matrix(n):
    idx = np.arange(n) // HEAD_DIM
    return jnp.asarray((idx[:, None] == idx[None, :]).astype(np.float32), dtype=BF16)


def _expand_matrix():
    e = np.zeros((2, AB_PAD, 2 * W_DN), np.float32)
    for d in range(2):
        for h in range(H_DN):
            e[d, d * H_DN + h, h * HEAD_DIM:(h + 1) * HEAD_DIM] = 1.0
            e[d, 2 * H_DN + d * H_DN + h, W_DN + h * HEAD_DIM:W_DN + (h + 1) * HEAD_DIM] = 1.0
    return jnp.asarray(e, dtype=BF16)


def _chunk_cumsum_matrices(rows):
    r = np.arange(rows)
    same = (r[:, None] // CHUNK) == (r[None, :] // CHUNK)
    return jnp.asarray(np.stack([same & (r[:, None] >= r[None, :]),
                                 same & (r[:, None] <= r[None, :])]).astype(np.float32), dtype=BF16)


def kernel(x_prompt, x_sample, cache_na_k, cache_na_v, cache_df_k, cache_df_v, state_dn, c, c_ctx,
           ada_w, ada_b, w_in, conv_dn, a_log_dn, dt_bias_dn, onorm_dn, rpb_na, lambda_df, subln_df,
           w_out, ln1_g, ln1_b, ln2_g, ln2_b, ffn_w_gate, ffn_w_up, ffn_w_down, router_w, router_b,
           moe_w_gate, moe_w_up, moe_w_down):
    conds = jnp.concatenate([c_ctx[None, :], c, jnp.zeros((N_COND - 1 - DEC_BATCH, D_MODEL), F32)], axis=0)
    mods = _ada_table(conds, ada_w, ada_b)
    ctx_cond = lambda i: 0
    lat_cond = lambda i: 1 + (i * ROW_TILE) // DEC_SEQ
    lat_cond_moe = lambda i: 1 + (i * MOE_TM) // DEC_SEQ
    tri = jnp.asarray(np.tril(np.ones((MOE_TM, MOE_TM), np.float32), -1), dtype=BF16)

    grp512, grp256 = _group_matrix(2 * W_DN), _group_matrix(W_DN)
    e_mat = _expand_matrix()
    cum = _chunk_cumsum_matrices(SEQ)
    cos, sin, rot = _rope_tables()

    y_ctx = x_prompt.reshape(T_CTX, D_MODEL)
    y_lat = x_sample.reshape(T_LAT, D_MODEL)
    ctx_out = []
    for l in range(DEPTH):
        mod = mods[l]
        w_in_l = _permute_w_in(w_in[l])
        w_out_l = w_out[l].astype(BF16)
        conv_w = jnp.pad(conv_dn[l], ((0, 8 - CONV_K), (0, 0)))
        alog_row = _pad_row(a_log_dn[l])
        dtb_row = _pad_row(dt_bias_dn[l])
        onorm_row = jnp.tile(onorm_dn[l], H_DN).reshape(1, W_DN)
        subln_row = subln_df[l].reshape(1, HEAD_DIM)
        lg1, lb1 = ln1_g[l].reshape(1, D_MODEL), ln1_b[l].reshape(1, D_MODEL)
        lg2, lb2 = ln2_g[l].reshape(1, D_MODEL), ln2_b[l].reshape(1, D_MODEL)

        *na, dn, gate, q_df, k_df, v_df, ab = _inproj(y_ctx, mod, w_in_l, ctx_cond)
        df = (q_df, k_df, v_df)
        u, *gates_dn = _dn_prep(dn, ab, conv_w, alog_row, dtb_row, grp512, e_mat, cum, SEQ)
        s0 = jnp.zeros((BATCH, 2, H_DN, HEAD_DIM, HEAD_DIM), F32)
        o_f, o_b, s_fin = _deltanet(u, *gates_dn, s0, SEQ, SEQ, n_sub=4)
        oa, oc = _ctx_attn(na, df, lambda_df[l], subln_row, l)
        y1_ctx = _outproj(oa, o_f, o_b, gate, onorm_row, grp256, oc, y_ctx, mod, w_out_l, lg1, lb1, ctx_cond)
        ctx_out.append((na[1].reshape(BATCH, SEQ, H_NA, HEAD_DIM), na[2].reshape(BATCH, SEQ, H_NA, HEAD_DIM),
                        k_df.reshape(BATCH, SEQ, H_DF, HEAD_DIM), v_df.reshape(BATCH, SEQ, H_DF, HEAD_DIM),
                        s_fin))

        *na, dn, gate, q_df, k_df, v_df, ab = _inproj(y_lat, mod, w_in_l, lat_cond)
        df = (q_df, k_df, v_df)
        u, *gates_dn = _dn_prep(dn, ab, conv_w, alog_row, dtb_row, grp512, e_mat, cum, DEC_SEQ)
        o_f, o_b, _ = _deltanet(u, *gates_dn, state_dn[:, l], DEC_SEQ, DN_ROWS)
        oa = _lat_na(na, cache_na_k[:, l].reshape(DEC_BATCH, PAST_LEN, W_NA).astype(BF16),
                     cache_na_v[:, l].reshape(DEC_BATCH, PAST_LEN, W_NA).astype(BF16),
                     _na_bias_table(rpb_na[l]))
        q_r, kt_all, v_all = _rope(df, cache_df_k[:, l].reshape(DEC_BATCH, PAST_LEN, W_DF),
                                   cache_df_v[:, l].reshape(DEC_BATCH, PAST_LEN, W_DF), cos, sin, rot)
        oc = _lat_df(q_r, kt_all, v_all, lambda_df[l], subln_row, l)
        y1_lat = _outproj(oa, o_f, o_b, gate, onorm_row, grp256, oc, y_lat, mod, w_out_l, lg1, lb1, lat_cond)

        i = l // 2
        if l % 2 == 0:
            wg, wu, wd = ffn_w_gate[i].astype(BF16), ffn_w_up[i].astype(BF16), ffn_w_down[i].astype(BF16)
            y_ctx = _ffn(y1_ctx, mod, wg, wu, wd, lg2, lb2, ctx_cond)
            y_lat = _ffn(y1_lat, mod, wg, wu, wd, lg2, lb2, lat_cond)
        else:
            wg, wu, wd = moe_w_gate[i].astype(BF16), moe_w_up[i].astype(BF16), moe_w_down[i].astype(BF16)
            rw = jnp.pad(router_w[i], ((0, 0), (0, LANES - N_EXPERTS)))
            rb = _pad_row(router_b[i], LANES)
            ys = []
            for y1, cond in ((y1_ctx, ctx_cond), (y1_lat, lat_cond_moe)):
                gates, rank, rank_t, cnt = _router(y1, mod, rw, rb, tri, cond)
                counts = cnt[:, 0, :N_EXPERTS].astype(jnp.int32).reshape(-1)
                ys.append(_moe(y1, mod, gates, rank, rank_t, counts, wg, wu, wd, lg2, lb2, cond))
            y_ctx, y_lat = ys

    stack = lambda j: jnp.stack([t[j] for t in ctx_out], axis=1)
    return (y_ctx.reshape(BATCH, SEQ, D_MODEL), y_lat.reshape(DEC_BATCH, DEC_SEQ, D_MODEL),
            stack(0), stack(1), stack(2), stack(3), stack(4))
```
